```python
import math
import jax
import jax.numpy as jnp
from jax import lax
import numpy as np

D_MODEL = 1024
BATCH = 1
SEQ = 16384
DEPTH = 4

GRID_W = 64
CTX_LEN = 256
H_A = 4
HD_A = 128
A_WIDTH = H_A * HD_A
CONV_W = 5
H_B = 4
DK_B = 64
DV_B = 128
B_KWIDTH = H_B * DK_B
B_VWIDTH = H_B * DV_B
GLA_RANK = 16
GLA_NORMALIZER = 16.0
H_C = 8
KV_C = 2
REP_C = H_C // KV_C
HD_C = 64
C_WIDTH = H_C * HD_C
WINDOW = 128
ATT_BLOCK = 128
ROPE_THETA = 10000.0
N_BRANCH = 3
CHUNK = 64
N_EXPERTS = 32
TOP_K = 4
D_FF = 1024
SWIGLU_LIMIT = 7.0
SWIGLU_ALPHA = 1.702
MOE_BLOCK = 128
EPS = 1e-6

IN_SPLITS = (A_WIDTH, A_WIDTH, A_WIDTH, A_WIDTH, 2 * H_A, 2 * H_A,
             B_KWIDTH, B_KWIDTH, B_VWIDTH, B_VWIDTH, 2 * GLA_RANK,
             C_WIDTH, KV_C * HD_C, KV_C * HD_C, N_BRANCH * D_MODEL)
IN_COLS = sum(IN_SPLITS)

kernel_name = 'hybrid_deltanet_gla_swa_moe_flow_block'


def rms_norm(x, g):
    xf = x.astype(jnp.float32)
    y = xf * lax.rsqrt(jnp.mean(xf * xf, axis=-1, keepdims=True) + EPS)
    return (y * g.astype(jnp.float32)).astype(x.dtype)


def l2_norm(x):
    return x * lax.rsqrt(jnp.sum(x * x, axis=-1, keepdims=True) + EPS)


def short_conv(x, w):
    return lax.conv_general_dilated(
        x, w[:, None, :].astype(x.dtype), window_strides=(1,),
        padding=[(CONV_W // 2, CONV_W // 2)], dimension_numbers=('NWC', 'WIO', 'NWC'),
        feature_group_count=x.shape[-1])


def dir_order(x):
    return jnp.concatenate([jnp.flip(x[:, :CTX_LEN], 1), jnp.flip(x[:, CTX_LEN:], 1)], 1)


def to_chunks(x):
    b, l, h = x.shape[:3]
    x = x.reshape((b, l // CHUNK, CHUNK, h) + x.shape[3:])
    return x.transpose((1, 0, 3, 2) + tuple(range(4, x.ndim)))


def from_chunks(x):
    n, b, h, c, d = x.shape
    return x.transpose(1, 0, 3, 2, 4).reshape(b, n * c, h, d)


def gated_delta_rule(q, k, v, g, beta):
    b, _, h, dk = q.shape
    dv = v.shape[-1]
    qc, kc, vc = to_chunks(q), to_chunks(k), to_chunks(v)
    gc = jnp.cumsum(to_chunks(g), axis=-1)
    bc = to_chunks(beta)
    idx = jnp.arange(CHUNK)
    incl = idx[:, None] >= idx[None, :]
    strict = idx[:, None] > idx[None, :]
    decay = jnp.exp(jnp.where(incl, gc[..., :, None] - gc[..., None, :], -jnp.inf))
    kb = kc * bc[..., None]
    lower = jnp.einsum('nbhid,nbhjd->nbhij', kb, kc) * jnp.where(strict, decay, 0.0)
    eye = jnp.eye(CHUNK, dtype=q.dtype)
    rhs = jnp.concatenate([vc * bc[..., None], kb * jnp.exp(gc)[..., None]], -1)
    sol = lax.linalg.triangular_solve(lower + eye, rhs, left_side=True, lower=True,
                                      unit_diagonal=True)
    u, w = sol[..., :dv], sol[..., dv:]
    qk = jnp.einsum('nbhid,nbhjd->nbhij', qc, kc) * decay
    q_dec = qc * jnp.exp(gc)[..., None]
    k_dec = kc * jnp.exp(gc[..., -1:] - gc)[..., None]
    g_tot = jnp.exp(gc[..., -1])[..., None, None]

    def step(s, xs):
        u_n, w_n, qk_n, qd_n, kd_n, gt_n = xs
        v_new = u_n - jnp.einsum('bhck,bhkv->bhcv', w_n, s)
        o = jnp.einsum('bhck,bhkv->bhcv', qd_n, s) + jnp.einsum('bhij,bhjv->bhiv', qk_n, v_new)
        s = s * gt_n + jnp.einsum('bhck,bhcv->bhkv', kd_n, v_new)
        return s, o

    _, o = lax.scan(step, jnp.zeros((b, h, dk, dv), q.dtype), (u, w, qk, q_dec, k_dec, g_tot))
    return from_chunks(o)


def gla_chunked(q, k, v, gk):
    b, _, h, dk = q.shape
    dv = v.shape[-1]
    qc, kc, vc = to_chunks(q), to_chunks(k), to_chunks(v)
    bc = jnp.cumsum(to_chunks(gk), axis=-2)
    idx = jnp.arange(CHUNK)
    incl = (idx[:, None] >= idx[None, :])[:, :, None]

    def step(s, xs):
        q_n, k_n, v_n, b_n = xs
        o_inter = jnp.einsum('bhck,bhkv->bhcv', q_n * jnp.exp(b_n), s)
        wd = jnp.exp(jnp.where(incl, b_n[:, :, :, None, :] - b_n[:, :, None, :, :], -jnp.inf))
        a = jnp.einsum('bhik,bhjk,bhijk->bhij', q_n, k_n, wd)
        o = o_inter + jnp.einsum('bhij,bhjv->bhiv', a, v_n)
        b_last = b_n[:, :, -1:, :]
        s = s * jnp.exp(b_last)[:, :, 0, :, None] + jnp.einsum(
            'bhck,bhcv->bhkv', k_n * jnp.exp(b_last - b_n), v_n)
        return s, o

    _, o = lax.scan(step, jnp.zeros((b, h, dk, dv), q.dtype), (qc, kc, vc, bc))
    return from_chunks(o)


def axial_rope(x, rows, cols):
    half = x.shape[-1] // 2
    inv_freq = ROPE_THETA ** (-jnp.arange(0, half, 2, dtype=jnp.float32) / half)

    def rot(xp, pos):
        ang = pos.astype(jnp.float32)[:, None] * inv_freq[None, :]
        cos = jnp.cos(ang)[None, :, None, :].astype(x.dtype)
        sin = jnp.sin(ang)[None, :, None, :].astype(x.dtype)
        x1, x2 = xp[..., :half // 2], xp[..., half // 2:]
        return jnp.concatenate([x1 * cos - x2 * sin, x2 * cos + x1 * sin], -1)

    return jnp.concatenate([rot(x[..., :half], rows), rot(x[..., half:], cols)], -1)


def window_gqa(q, k, v, sink):
    b = q.shape[0]
    scale = HD_C ** -0.5
    sink_l = sink.astype(jnp.float32).reshape(KV_C, REP_C)
    qx, ql = q[:, :CTX_LEN], q[:, CTX_LEN:]
    kx, kl = k[:, :CTX_LEN], k[:, CTX_LEN:]
    vx, vl = v[:, :CTX_LEN], v[:, CTX_LEN:]
    qx = qx.reshape(b, CTX_LEN, KV_C, REP_C, HD_C)
    s = jnp.einsum('bqgrd,bkgd->bgrqk', qx, kx).astype(jnp.float32) * scale
    sink_col = jnp.broadcast_to(sink_l[None, :, :, None, None], s.shape[:-1] + (1,))
    p = jax.nn.softmax(jnp.concatenate([s, sink_col], -1), axis=-1)[..., :CTX_LEN]
    o_ctx = jnp.einsum('bgrqk,bkgd->bqgrd', p.astype(v.dtype), vx).reshape(b, CTX_LEN, C_WIDTH)
    s_len = ql.shape[1]
    nb = s_len // ATT_BLOCK
    qb = ql.reshape(b, nb, ATT_BLOCK, KV_C, REP_C, HD_C)

    def band(t):
        tp = jnp.pad(t, ((0, 0), (ATT_BLOCK, ATT_BLOCK), (0, 0), (0, 0)))
        tp = tp.reshape(b, nb + 2, ATT_BLOCK, KV_C, HD_C)
        return jnp.concatenate([tp[:, :-2], tp[:, 1:-1], tp[:, 2:]], axis=2)

    kb, vb = band(kl), band(vl)
    s_loc = jnp.einsum('bnqgrd,bnkgd->bngrqk', qb, kb).astype(jnp.float32) * scale
    blk = jnp.arange(nb)[:, None, None]
    qpos = blk * ATT_BLOCK + jnp.arange(ATT_BLOCK)[None, :, None]
    kpos = (blk - 1) * ATT_BLOCK + jnp.arange(3 * ATT_BLOCK)[None, None, :]
    valid = (jnp.abs(qpos - kpos) <= WINDOW) & (kpos >= 0) & (kpos < s_len)
    s_loc = jnp.where(valid[None, :, None, None], s_loc, -jnp.inf)
    s_ctx = jnp.einsum('bnqgrd,bcgd->bngrqc', qb, kx).astype(jnp.float32) * scale
    sink_col = jnp.broadcast_to(sink_l[None, None, :, :, None, None], s_ctx.shape[:-1] + (1,))
    p = jax.nn.softmax(jnp.concatenate([s_loc, s_ctx, sink_col], -1), axis=-1).astype(v.dtype)
    p_loc = p[..., :3 * ATT_BLOCK]
    p_ctx = p[..., 3 * ATT_BLOCK:3 * ATT_BLOCK + CTX_LEN]
    o_lat = (jnp.einsum('bngrqk,bnkgd->bnqgrd', p_loc, vb)
             + jnp.einsum('bngrqc,bcgd->bnqgrd', p_ctx, vx)).reshape(b, s_len, C_WIDTH)
    return jnp.concatenate([o_ctx, o_lat], 1)


def token_mixer(h, rows, cols, w_in, conv_w, a_log, dt_bias, dn_g, gla_w2, gla_b2, gla_g,
                q_g, k_g, sink, wpa, wpb, wpc, wo):
    b, lt, _ = h.shape
    f32 = jnp.float32
    points = np.cumsum(IN_SPLITS)[:-1].tolist()
    (aq, ak, av, az, aa, ab, bq, bk, bv, bz, bg, cq, ck, cv, gl) = jnp.split(h @ w_in, points, axis=-1)

    qkv = jnp.concatenate([aq, ak, av], -1)
    qkv = jnp.concatenate([short_conv(qkv[:, :CTX_LEN], conv_w), short_conv(qkv[:, CTX_LEN:], conv_w)], 1)
    qkv = jax.nn.silu(qkv).astype(f32)
    qa, ka, va = [t.reshape(b, lt, H_A, HD_A) for t in jnp.split(qkv, 3, axis=-1)]
    qa = l2_norm(qa) * HD_A ** -0.5
    ka = l2_norm(ka)
    g_dn = -jnp.exp(a_log.astype(f32)) * jax.nn.softplus(
        aa.astype(f32).reshape(b, lt, 2, H_A) + dt_bias.astype(f32))
    beta = jax.nn.sigmoid(ab.astype(f32).reshape(b, lt, 2, H_A))
    o_a = gated_delta_rule(qa, ka, va, g_dn[:, :, 0], beta[:, :, 0]) + dir_order(
        gated_delta_rule(dir_order(qa), dir_order(ka), dir_order(va),
                         dir_order(g_dn[:, :, 1]), dir_order(beta[:, :, 1])))
    y_a = (rms_norm(o_a, dn_g) * jax.nn.silu(az.astype(f32).reshape(b, lt, H_A, HD_A)))
    y_a = y_a.reshape(b, lt, A_WIDTH).astype(h.dtype)

    qb_ = bq.astype(f32).reshape(b, lt, H_B, DK_B) * DK_B ** -0.5
    kb_ = bk.astype(f32).reshape(b, lt, H_B, DK_B)
    vb_ = bv.astype(f32).reshape(b, lt, H_B, DV_B)
    gk = jax.nn.log_sigmoid(jnp.einsum('bldr,drk->bldk', bg.astype(f32).reshape(b, lt, 2, GLA_RANK),
                                       gla_w2.astype(f32)) + gla_b2.astype(f32)) / GLA_NORMALIZER
    gk = gk.reshape(b, lt, 2, H_B, DK_B)
    o_b = gla_chunked(qb_, kb_, vb_, gk[:, :, 0]) + dir_order(
        gla_chunked(dir_order(qb_), dir_order(kb_), dir_order(vb_), dir_order(gk[:, :, 1])))
    y_b = (rms_norm(o_b, gla_g) * jax.nn.silu(bz.astype(f32).reshape(b, lt, H_B, DV_B)))
    y_b = y_b.reshape(b, lt, B_VWIDTH).astype(h.dtype)

    qc = rms_norm(cq.reshape(b, lt, H_C, HD_C), q_g)
    kc = rms_norm(ck.reshape(b, lt, KV_C, HD_C), k_g)
    vc = cv.reshape(b, lt, KV_C, HD_C)
    qc = jnp.concatenate([qc[:, :CTX_LEN], axial_rope(qc[:, CTX_LEN:], rows, cols)], 1)
    kc = jnp.concatenate([kc[:, :CTX_LEN], axial_rope(kc[:, CTX_LEN:], rows, cols)], 1)
    y_c = window_gqa(qc, kc, vc, sink)

    ga, gb, gcc = jnp.split(jax.nn.sigmoid(gl), 3, axis=-1)
    merged = ga * (y_a @ wpa) + gb * (y_b @ wpb) + gcc * (y_c @ wpc)
    return merged @ wo


def moe_ffn(h, router_w, router_b, w_gu, b_gu, w_dn, b_dn):
    b, l, d = h.shape
    x = h.reshape(-1, d)
    t = x.shape[0]
    tk = t * TOP_K
    logits = (x @ router_w + router_b).astype(jnp.float32)
    top_v, top_e = lax.top_k(logits, TOP_K)
    gate = jax.nn.softmax(top_v, axis=-1)
    flat_e = top_e.reshape(-1)
    order = jnp.argsort(flat_e)
    sorted_e = flat_e[order]
    counts = jnp.zeros((N_EXPERTS,), jnp.int32).at[flat_e].add(1)
    padded = (counts + MOE_BLOCK - 1) // MOE_BLOCK * MOE_BLOCK
    start = jnp.cumsum(counts) - counts
    pend = jnp.cumsum(padded)
    pstart = pend - padded
    dest_sorted = pstart[sorted_e] + jnp.arange(tk, dtype=jnp.int32) - start[sorted_e]
    n_blocks = (tk + N_EXPERTS * (MOE_BLOCK - 1) + MOE_BLOCK - 1) // MOE_BLOCK
    p_rows = n_blocks * MOE_BLOCK
    src_tok = jnp.full((p_rows,), t, jnp.int32).at[dest_sorted].set((order // TOP_K).astype(jnp.int32))
    x_src = jnp.concatenate([x, jnp.zeros((1, d), x.dtype)], 0)
    xb = x_src[src_tok].reshape(n_blocks, MOE_BLOCK, d)
    blk_e = jnp.minimum(jnp.searchsorted(pend, jnp.arange(n_blocks, dtype=jnp.int32) * MOE_BLOCK,
                                         side='right'), N_EXPERTS - 1)

    def expert(args):
        xe, e = args
        gu = xe @ w_gu[e] + b_gu[e]
        g_, u_ = gu[:, :D_FF], gu[:, D_FF:]
        g_ = jnp.minimum(g_, SWIGLU_LIMIT)
        u_ = jnp.clip(u_, -SWIGLU_LIMIT, SWIGLU_LIMIT)
        glu = g_ * jax.nn.sigmoid(g_ * SWIGLU_ALPHA)
        return ((u_ + 1.0) * glu) @ w_dn[e] + b_dn[e]

    y = lax.map(expert, (xb, blk_e)).reshape(p_rows, d)
    dest = jnp.zeros((tk,), jnp.int32).at[order].set(dest_sorted)
    y_assign = y[dest].reshape(t, TOP_K, d)
    out = jnp.einsum('tk,tkd->td', gate.astype(y.dtype), y_assign)
    return out.reshape(b, l, d)


def modulate(xn, shift_c, scale_c, shift_l, scale_l):
    return jnp.concatenate([xn[:, :CTX_LEN] * (1.0 + scale_c) + shift_c,
                            xn[:, CTX_LEN:] * (1.0 + scale_l) + shift_l], 1)


def gate_streams(y, g_c, g_l):
    return jnp.concatenate([y[:, :CTX_LEN] * g_c, y[:, CTX_LEN:] * g_l], 1)


def setup_inputs(seed: int = 0) -> dict:
    key = jax.random.key(seed)
    ks = jax.random.split(key, 32)
    d = D_MODEL

    def nrm(k, shape, s):
        return jax.random.normal(k, shape, jnp.float32) * s

    dt = jnp.exp(jax.random.uniform(ks[11], (DEPTH, 2, H_A), jnp.float32,
                                    math.log(1e-3), math.log(1e-1)))
    return {
        'x': nrm(ks[0], (BATCH, SEQ, d), 1.0),
        'c': nrm(ks[1], (BATCH, d), 1.0),
        'ctx': nrm(ks[2], (BATCH, CTX_LEN, d), 1.0),
        'c_ctx': nrm(ks[3], (d,), 1.0),
        'ada_w': nrm(ks[4], (DEPTH, d, 6 * d), 0.5 * d ** -0.5),
        'ada_b': nrm(ks[5], (DEPTH, 6 * d), 0.02),
        'norm_mix_g': 1.0 + nrm(ks[6], (DEPTH, d), 0.01),
        'norm_ffn_g': 1.0 + nrm(ks[7], (DEPTH, d), 0.01),
        'w_in': nrm(ks[8], (DEPTH, d, IN_COLS), d ** -0.5),
        'dn_conv_w': nrm(ks[9], (DEPTH, CONV_W, 3 * A_WIDTH), CONV_W ** -0.5),
        'dn_a_log': jnp.log(jax.random.uniform(ks[10], (DEPTH, 2, H_A), jnp.float32, 1.0, 16.0)),
        'dn_dt_bias': dt + jnp.log(-jnp.expm1(-dt)),
        'dn_norm_g': 1.0 + nrm(ks[12], (DEPTH, HD_A), 0.01),
        'gla_w2': nrm(ks[13], (DEPTH, 2, GLA_RANK, B_KWIDTH), GLA_RANK ** -0.5),
        'gla_b2': nrm(ks[14], (DEPTH, 2, B_KWIDTH), 0.1),
        'gla_norm_g': 1.0 + nrm(ks[15], (DEPTH, DV_B), 0.01),
        'attn_q_norm_g': 1.0 + nrm(ks[16], (DEPTH, HD_C), 0.01),
        'attn_k_norm_g': 1.0 + nrm(ks[17], (DEPTH, HD_C), 0.01),
        'attn_sink': nrm(ks[18], (DEPTH, H_C), 0.5),
        'w_branch_a': nrm(ks[19], (DEPTH, A_WIDTH, d), A_WIDTH ** -0.5),
        'w_branch_b': nrm(ks[20], (DEPTH, B_VWIDTH, d), B_VWIDTH ** -0.5),
        'w_branch_c': nrm(ks[21], (DEPTH, C_WIDTH, d), C_WIDTH ** -0.5),
        'w_out': nrm(ks[22], (DEPTH, d, d), d ** -0.5),
        'router_w': nrm(ks[23], (DEPTH, d, N_EXPERTS), d ** -0.5),
        'router_b': nrm(ks[24], (DEPTH, N_EXPERTS), 0.01),
        'w_gate_up': nrm(ks[25], (DEPTH, N_EXPERTS, d, 2 * D_FF), d ** -0.5),
        'b_gate_up': nrm(ks[26], (DEPTH, N_EXPERTS, 2 * D_FF), 0.01),
        'w_down': nrm(ks[27], (DEPTH, N_EXPERTS, D_FF, d), D_FF ** -0.5),
        'b_down': nrm(ks[28], (DEPTH, N_EXPERTS, d), 0.01),
    }


def reference(x, c, ctx, c_ctx, ada_w, ada_b, norm_mix_g, norm_ffn_g, w_in, dn_conv_w,
              dn_a_log, dn_dt_bias, dn_norm_g, gla_w2, gla_b2, gla_norm_g, attn_q_norm_g,
              attn_k_norm_g, attn_sink, w_branch_a, w_branch_b, w_branch_c, w_out, router_w,
              router_b, w_gate_up, b_gate_up, w_down, b_down):
    s_len = x.shape[1]
    ROWS = s_len // GRID_W
    rows = jnp.repeat(jnp.arange(ROWS), GRID_W)
    cols = jnp.tile(jnp.arange(GRID_W), ROWS)
    xs = jnp.concatenate([ctx, x], 1)
    for l in range(DEPTH):
        mod_l = jax.nn.silu(c) @ ada_w[l] + ada_b[l]
        mod_c = jax.nn.silu(c_ctx) @ ada_w[l] + ada_b[l]
        ml = jnp.split(mod_l[:, None, :], 6, axis=-1)
        mc = jnp.split(mod_c, 6, axis=-1)
        h = modulate(rms_norm(xs, norm_mix_g[l]), mc[0], mc[1], ml[0], ml[1])
        y = token_mixer(h, rows, cols, w_in[l], dn_conv_w[l], dn_a_log[l], dn_dt_bias[l],
                        dn_norm_g[l], gla_w2[l], gla_b2[l], gla_norm_g[l], attn_q_norm_g[l],
                        attn_k_norm_g[l], attn_sink[l], w_branch_a[l], w_branch_b[l],
                        w_branch_c[l], w_out[l])
        xs = xs + gate_streams(y, mc[2], ml[2])
        h = modulate(rms_norm(xs, norm_ffn_g[l]), mc[3], mc[4], ml[3], ml[4])
        y = moe_ffn(h, router_w[l], router_b[l], w_gate_up[l], b_gate_up[l], w_down[l], b_down[l])
        xs = xs + gate_streams(y, mc[5], ml[5])
    return xs[:, CTX_LEN:]
```

```python
import functools
import math

import jax
import jax.numpy as jnp
import numpy as np
from jax import lax
from jax.experimental import pallas as pl
from jax.experimental.pallas import tpu as pltpu

F32 = jnp.float32
BF16 = jnp.bfloat16
I32 = jnp.int32

D_MODEL = 1024
DEPTH = 4
GRID_W = 64
CTX = 256
H_A = 4
HD_A = 128
A_WIDTH = H_A * HD_A
CONV_W = 5
H_B = 4
DK_B = 64
DV_B = 128
B_KWIDTH = H_B * DK_B
B_VWIDTH = H_B * DV_B
GLA_RANK = 16
GLA_NORMALIZER = 16.0
H_C = 8
KV_C = 2
REP_C = H_C // KV_C
HD_C = 64
HD_CP = 128
C_WIDTH = H_C * HD_C
ATT_BLOCK = 128
ROPE_THETA = 10000.0
CHUNK = 64
SUB = 16
N_EXPERTS = 32
TOP_K = 4
D_FF = 1024
SWIGLU_LIMIT = 7.0
SWIGLU_ALPHA = 1.702
EPS = 1e-6
NEG = -1e30

BLK = 256
MOE_BM = 256
VMEM_LIMIT = 56 * 1024 * 1024

COL_GL = 0
COL_QKV = 3072
COL_AZ = 4608
COL_CQ = 5120
COL_CKV = 6144
COL_BQK = 6656
COL_BV = 7168
COL_BZ = 7680
COL_SMALL = 8192
IN_COLS_P = 8320
IN_TN = 640


def _pick(n, cands):
    for c in cands:
        if n % c == 0:
            return c
    raise ValueError(f"no tile for {n}")


def _cparams(sem):
    return pltpu.CompilerParams(dimension_semantics=sem, vmem_limit_bytes=VMEM_LIMIT)


def _bdot(a, b):
    return jnp.dot(a.astype(BF16), b.astype(BF16), preferred_element_type=F32)


def _bdot_nt(a, b):
    return lax.dot_general(a.astype(BF16), b.astype(BF16), (((1,), (1,)), ((), ())),
                           preferred_element_type=F32)


def _bdot_tn(a, b):
    return lax.dot_general(a.astype(BF16), b.astype(BF16), (((0,), (0,)), ((), ())),
                           preferred_element_type=F32)


def _fdot(a, b):
    return jnp.dot(a, b, precision=lax.Precision.HIGHEST, preferred_element_type=F32)


def _split2(a):
    hi = a.astype(BF16)
    lo = (a - hi.astype(F32)).astype(BF16)
    return hi, lo


def _dot3(a, b):
    ah, al = _split2(a)
    bh, bl = _split2(b)
    d = functools.partial(jnp.dot, preferred_element_type=F32)
    return d(ah, bh) + (d(ah, bl) + d(al, bh))


def _sigmoid(x):
    return 1.0 / (1.0 + jnp.exp(-x))


def _silu(x):
    return x * _sigmoid(x)


def _softplus(x):
    return jnp.maximum(x, 0.0) + jnp.log(1.0 + jnp.exp(-jnp.abs(x)))


def _ada_kernel(cc_ref, w_ref, b_ref, o_ref):
    o_ref[...] = _fdot(_silu(cc_ref[...]), w_ref[...]) + b_ref[...]


def _ada_mod(cc, ada_w, ada_b):
    depth = ada_w.shape[0]
    tn = 1536
    return pl.pallas_call(
        _ada_kernel,
        grid=(depth, 6 * D_MODEL // tn),
        in_specs=[pl.BlockSpec((8, D_MODEL), lambda l, j: (0, 0)),
                  pl.BlockSpec((None, D_MODEL, tn), lambda l, j: (l, 0, j)),
                  pl.BlockSpec((None, 1, tn), lambda l, j: (l, 0, j))],
        out_specs=pl.BlockSpec((None, 8, tn), lambda l, j: (l, 0, j)),
        out_shape=jax.ShapeDtypeStruct((depth, 8, 6 * D_MODEL), F32),
        compiler_params=_cparams(("arbitrary", "arbitrary")),
        name="ada_mod",
    )(cc, ada_w, ada_b.reshape(depth, 1, 6 * D_MODEL))


def _norm_mod(x, g, mod_ref, moff, row0):
    tm = x.shape[0]
    y = x * lax.rsqrt(jnp.mean(x * x, axis=-1, keepdims=True) + EPS) * g
    isc = (row0 + lax.broadcasted_iota(I32, (tm, 1), 0)) < CTX
    shift = jnp.where(isc, mod_ref[1:2, moff:moff + D_MODEL], mod_ref[0:1, moff:moff + D_MODEL])
    scale = jnp.where(isc, mod_ref[1:2, moff + D_MODEL:moff + 2 * D_MODEL],
                      mod_ref[0:1, moff + D_MODEL:moff + 2 * D_MODEL])
    return y * (1.0 + scale) + shift


def _res_gate(mod_ref, moff, tm, row0):
    isc = (row0 + lax.broadcasted_iota(I32, (tm, 1), 0)) < CTX
    return jnp.where(isc, mod_ref[1:2, moff + 2 * D_MODEL:moff + 3 * D_MODEL],
                     mod_ref[0:1, moff + 2 * D_MODEL:moff + 3 * D_MODEL])


def _inproj_kernel(x_ref, mod_ref, g_ref, w_ref, o_ref, h_ref, *, tm):
    i = pl.program_id(0)

    @pl.when(pl.program_id(1) == 0)
    def _():
        h_ref[...] = _norm_mod(x_ref[...], g_ref[...], mod_ref, 0, i * tm).astype(BF16)

    o_ref[...] = jnp.dot(h_ref[...], w_ref[...], preferred_element_type=F32)


def _inproj(xs, mod, g, w):
    t = xs.shape[0]
    tm = _pick(t, (1280, 640, 256))
    return pl.pallas_call(
        functools.partial(_inproj_kernel, tm=tm),
        grid=(t // tm, IN_COLS_P // IN_TN),
        in_specs=[pl.BlockSpec((tm, D_MODEL), lambda i, j: (i, 0)),
                  pl.BlockSpec((8, 6 * D_MODEL), lambda i, j: (0, 0)),
                  pl.BlockSpec((1, D_MODEL), lambda i, j: (0, 0)),
                  pl.BlockSpec((D_MODEL, IN_TN), lambda i, j: (0, j))],
        out_specs=pl.BlockSpec((tm, IN_TN), lambda i, j: (i, j)),
        out_shape=jax.ShapeDtypeStruct((t, IN_COLS_P), F32),
        scratch_shapes=[pltpu.VMEM((tm, D_MODEL), BF16)],
        compiler_params=_cparams(("arbitrary", "arbitrary")),
        name="inproj",
    )(xs, mod, g, w)


def _tri_blockdiag(n, lower):
    ii = lax.broadcasted_iota(I32, (n, n), 0)
    jj = lax.broadcasted_iota(I32, (n, n), 1)
    same = (ii >> 6) == (jj >> 6)
    tri = (ii >= jj) if lower else (ii <= jj)
    return jnp.where(same, jnp.where(tri, 1.0, 0.0), 0.0).astype(F32)


def _dnprep_kernel(main_ref, prev_ref, next_ref, small_ref, cw_ref, alog_ref, dtb_ref,
                   q_ref, k_ref, v_ref, gcol_ref, ext_ref, *, nb):
    i = pl.program_id(0)
    use_prev = i >= 2
    use_next = jnp.logical_and(i >= 1, i <= nb - 2)
    ext_ref[0:8, :] = jnp.where(use_prev, prev_ref[...], 0.0)
    ext_ref[8:8 + BLK, :] = main_ref[...]
    ext_ref[8 + BLK:16 + BLK, :] = jnp.where(use_next, next_ref[...], 0.0)
    acc = ext_ref[6:6 + BLK, :] * cw_ref[0:1, :]
    for d in range(1, CONV_W):
        acc = acc + ext_ref[6 + d:6 + d + BLK, :] * cw_ref[d:d + 1, :]
    s = _silu(acc)
    for h in range(H_A):
        for part, ref, mul in ((0, q_ref, HD_A ** -0.5), (1, k_ref, 1.0)):
            seg = s[:, part * A_WIDTH + h * HD_A: part * A_WIDTH + (h + 1) * HD_A]
            nrm = seg * lax.rsqrt(jnp.sum(seg * seg, axis=-1, keepdims=True) + EPS)
            ref[:, h * HD_A:(h + 1) * HD_A] = nrm * mul
    v_ref[...] = s[:, 2 * A_WIDTH:3 * A_WIDTH]
    sm = small_ref[...]
    lane = lax.broadcasted_iota(I32, sm.shape, 1)
    g = -jnp.exp(alog_ref[...]) * _softplus(sm + dtb_ref[...])
    gb = jnp.where(lane < 2 * H_A, g, jnp.where(lane < 4 * H_A, _sigmoid(sm), 0.0))
    cf = _fdot(_tri_blockdiag(BLK, True), gb)
    cr = _fdot(_tri_blockdiag(BLK, False), gb)
    gc = jnp.where(lane < H_A, cf, jnp.where(lane < 2 * H_A, cr, gb))
    gcol_ref[...] = gc


def _dnprep(proj, conv_w, alog_vec, dtb_vec):
    t = proj.shape[0]
    nb = t // BLK
    qkv_blk = COL_QKV // (3 * A_WIDTH)
    last8 = t // 8 - 1
    out_sds = jax.ShapeDtypeStruct((t, A_WIDTH), F32)
    return pl.pallas_call(
        functools.partial(_dnprep_kernel, nb=nb),
        grid=(nb,),
        in_specs=[pl.BlockSpec((BLK, 3 * A_WIDTH), lambda i: (i, qkv_blk)),
                  pl.BlockSpec((8, 3 * A_WIDTH), lambda i: (jnp.maximum(i * (BLK // 8) - 1, 0), qkv_blk)),
                  pl.BlockSpec((8, 3 * A_WIDTH), lambda i: (jnp.minimum((i + 1) * (BLK // 8), last8), qkv_blk)),
                  pl.BlockSpec((BLK, 128), lambda i: (i, COL_SMALL // 128)),
                  pl.BlockSpec((CONV_W, 3 * A_WIDTH), lambda i: (0, 0)),
                  pl.BlockSpec((1, 128), lambda i: (0, 0)),
                  pl.BlockSpec((1, 128), lambda i: (0, 0))],
        out_specs=[pl.BlockSpec((BLK, A_WIDTH), lambda i: (i, 0)),
                   pl.BlockSpec((BLK, A_WIDTH), lambda i: (i, 0)),
                   pl.BlockSpec((BLK, A_WIDTH), lambda i: (i, 0)),
                   pl.BlockSpec((BLK, 128), lambda i: (i, 0))],
        out_shape=[out_sds, out_sds, out_sds,
                   jax.ShapeDtypeStruct((t, 128), F32)],
        scratch_shapes=[pltpu.VMEM((BLK + 16, 3 * A_WIDTH), F32)],
        compiler_params=_cparams(("arbitrary",)),
        name="dn_prep",
    )(proj, proj, proj, proj, conv_w, alog_vec, dtb_vec)


def _unit_tri_inverse(l_mat, masks):
    eye, m_diag, m_l1, m_l2 = masks
    ld = l_mat * m_diag
    x = eye - ld
    p = _dot3(ld, ld)
    x = x + _dot3(x, p)
    p = _dot3(p, p)
    x = x + _dot3(x, p)
    p = _dot3(p, p)
    x = x + _dot3(x, p)
    x = x - _dot3(x, _dot3(l_mat * m_l1, x))
    x = x - _dot3(x, _dot3(l_mat * m_l2, x))
    return x


def _dn_masks():
    ii = lax.broadcasted_iota(I32, (CHUNK, CHUNK), 0)
    jj = lax.broadcasted_iota(I32, (CHUNK, CHUNK), 1)
    one = lambda c: jnp.where(c, 1.0, 0.0).astype(F32)
    eye = one(ii == jj)
    m_diag = one((ii >> 4) == (jj >> 4))
    m_l2 = one((ii >> 5) != (jj >> 5))
    m_l1 = 1.0 - m_diag - m_l2
    return ii, jj, (eye, m_diag, m_l1, m_l2)


def _dn_chunk_head(q, k, v, gcol, grow, bcol, s_mat, fwd, ii, jj, masks):
    incl = (ii >= jj) if fwd else (ii <= jj)
    strict = (ii > jj) if fwd else (ii < jj)
    dec = jnp.exp(jnp.where(incl, gcol - grow, NEG))
    kb = k * bcol
    kh = k.astype(BF16)
    l_mat = _bdot_nt(kb, kh) * jnp.where(strict, dec, 0.0)
    qk = _bdot_nt(q, kh) * dec
    eg = jnp.exp(gcol)
    rhs = jnp.concatenate([v * bcol, kb * eg], axis=1)
    sol = _dot3(_unit_tri_inverse(l_mat, masks), rhs)
    u, w = sol[:, :HD_A], sol[:, HD_A:]
    glast = gcol[CHUNK - 1:CHUNK, :] if fwd else gcol[0:1, :]
    sb = s_mat.astype(BF16)
    v_new = u - _bdot(w, sb)
    o = _bdot(q * eg, sb) + _bdot(qk, v_new)
    s_new = s_mat * jnp.exp(glast) + _bdot_tn(k * jnp.exp(glast - gcol), v_new)
    return o, s_new


def _dnscan_kernel(qf, kf, vf, gcf, qb, kb, vb, gcb, of_ref, ob_ref, s_ref):
    @pl.when(pl.program_id(0) == 0)
    def _():
        s_ref[...] = jnp.zeros_like(s_ref)

    ii, jj, masks = _dn_masks()
    nch = BLK // CHUNK
    pick = jnp.where(lax.broadcasted_iota(I32, (16, 128), 0) == lax.broadcasted_iota(I32, (16, 128), 1),
                     1.0, 0.0).astype(F32)

    def body(step, carry):
        for fwd, (q_r, k_r, v_r, gc_r, o_r) in ((True, (qf, kf, vf, gcf, of_ref)),
                                               (False, (qb, kb, vb, gcb, ob_ref))):
            c = step if fwd else nch - 1 - step
            rows = pl.ds(pl.multiple_of(c * CHUNK, CHUNK), CHUNK)
            d = 0 if fwd else 1
            gct = gc_r[rows, :]
            grows = lax.dot_general(pick, gct, (((1,), (1,)), ((), ())),
                                    precision=lax.Precision.HIGHEST, preferred_element_type=F32)
            for h in range(H_A):
                lanes = slice(h * HD_A, (h + 1) * HD_A)
                gi = d * H_A + h
                o, s_new = _dn_chunk_head(
                    q_r[rows, lanes], k_r[rows, lanes], v_r[rows, lanes],
                    gct[:, gi:gi + 1], grows[gi:gi + 1, :],
                    gct[:, 2 * H_A + gi:2 * H_A + gi + 1],
                    s_ref[gi], fwd, ii, jj, masks)
                o_r[rows, lanes] = o
                s_ref[gi] = s_new
        return carry

    lax.fori_loop(0, nch, body, 0)


def _rev_block(nb):
    return lambda i: jnp.where(i == 0, 0, nb - i)


def _dnscan(q, k, v, gcol):
    t = q.shape[0]
    nb = t // BLK
    rev = _rev_block(nb)
    wide = lambda f: pl.BlockSpec((BLK, A_WIDTH), lambda i: (f(i), 0))
    col = lambda f: pl.BlockSpec((BLK, 128), lambda i: (f(i), 0))
    ident = lambda i: i
    out_sds = jax.ShapeDtypeStruct((t, A_WIDTH), F32)
    return pl.pallas_call(
        _dnscan_kernel,
        grid=(nb,),
        in_specs=[wide(ident), wide(ident), wide(ident), col(ident),
                  wide(rev), wide(rev), wide(rev), col(rev)],
        out_specs=[wide(ident), wide(rev)],
        out_shape=[out_sds, out_sds],
        scratch_shapes=[pltpu.VMEM((2 * H_A, HD_A, HD_A), F32)],
        compiler_params=_cparams(("arbitrary",)),
        name="dn_scan",
    )(q, k, v, gcol, q, k, v, gcol)


def _gla_gates(small_ref, w2_ref, b2_ref, b_ref, d, fwd):
    cols = slice(d * B_KWIDTH, (d + 1) * B_KWIDTH)
    pre = _fdot(small_ref[...], w2_ref[:, cols]) + b2_ref[:, cols]
    gk = -_softplus(-pre) * (1.0 / GLA_NORMALIZER)
    b_ref[d] = _fdot(_tri_blockdiag(BLK, fwd), gk)


def _gla_chunk(qk_ref, v_ref, b_ref, o_ref, st_ref, d, fwd, consts, step):
    sel, headmask_k, st_mask = consts
    nch = BLK // CHUNK
    nsub = CHUNK // SUB
    sub_i = lax.broadcasted_iota(I32, (SUB, 1), 0)
    row_c = lax.broadcasted_iota(I32, (CHUNK, 1), 0)
    if True:
        c = step if fwd else nch - 1 - step
        rows = pl.ds(pl.multiple_of(c * CHUNK, CHUNK), CHUNK)
        q = qk_ref[rows, 0:B_KWIDTH] * (DK_B ** -0.5)
        k = qk_ref[rows, B_KWIDTH:2 * B_KWIDTH]
        v = v_ref[rows, :]
        b = b_ref[d, rows, :]
        vh = v.astype(BF16)
        st = st_ref[d]
        o = _bdot_nt(q * jnp.exp(b), st)
        refs = []
        for sb in range(nsub):
            if fwd:
                r = b[sb * SUB - 1:sb * SUB, :] if sb > 0 else jnp.zeros((1, B_KWIDTH), F32)
            else:
                r = b[(sb + 1) * SUB:(sb + 1) * SUB + 1, :] if sb < nsub - 1 else jnp.zeros((1, B_KWIDTH), F32)
            refs.append(r)
        rfull = jnp.concatenate([jnp.broadcast_to(r, (SUB, B_KWIDTH)) for r in refs], axis=0)
        qs = q * jnp.exp(b - rfull)
        a_off = [None] * H_B
        for sb in (range(1, nsub) if fwd else range(0, nsub - 1)):
            jmask = (row_c < sb * SUB) if fwd else (row_c >= (sb + 1) * SUB)
            ks = (k * jnp.exp(jnp.where(jmask, refs[sb] - b, NEG))).astype(BF16)
            rowmask = jnp.where((row_c >> 4) == sb, 1.0, 0.0)
            for h in range(H_B):
                a = _bdot_nt(qs * headmask_k[h], ks) * rowmask
                a_off[h] = a if a_off[h] is None else a_off[h] + a
        o = o + jnp.concatenate(
            [_bdot(a_off[h], vh[:, h * DV_B:(h + 1) * DV_B]) for h in range(H_B)], axis=1)
        diag = []
        for sb in range(nsub):
            s0 = sb * SUB
            bs, qsb, ksb = b[s0:s0 + SUB, :], q[s0:s0 + SUB, :], k[s0:s0 + SUB, :]
            tiles = []
            for jl in range(SUB):
                causal = (sub_i >= jl) if fwd else (sub_i <= jl)
                e = jnp.exp(jnp.where(causal, bs - bs[jl:jl + 1, :], NEG))
                tiles.append((qsb * ksb[jl:jl + 1, :] * e).astype(BF16))
            red = jnp.dot(jnp.concatenate(tiles, axis=0), sel, preferred_element_type=F32)
            acc = red[0:SUB, :] * v[s0:s0 + 1, :]
            for jl in range(1, SUB):
                acc = acc + red[jl * SUB:(jl + 1) * SUB, :] * v[s0 + jl:s0 + jl + 1, :]
            diag.append(acc)
        o_ref[rows, :] = o + jnp.concatenate(diag, axis=0)
        blast = b[CHUNK - 1:CHUNK, :] if fwd else b[0:1, :]
        kd = k * jnp.exp(blast - b)
        st_ref[d] = st * jnp.exp(blast) + _bdot_tn(v, kd) * st_mask


def _glascan_kernel(qkf, vf, smf, qkb, vb, smb, w2_ref, b2_ref, of_ref, ob_ref, st_ref, b_ref):
    @pl.when(pl.program_id(0) == 0)
    def _():
        st_ref[...] = jnp.zeros_like(st_ref)

    kk = lax.broadcasted_iota(I32, (B_KWIDTH, B_VWIDTH), 0)
    cc = lax.broadcasted_iota(I32, (B_KWIDTH, B_VWIDTH), 1)
    sel = jnp.where((kk >> 6) == (cc >> 7), 1.0, 0.0).astype(BF16)
    lane = lax.broadcasted_iota(I32, (1, B_KWIDTH), 1)
    headmask_k = [jnp.where((lane >> 6) == h, 1.0, 0.0).astype(F32) for h in range(H_B)]
    rr = lax.broadcasted_iota(I32, (B_VWIDTH, B_KWIDTH), 0)
    kc = lax.broadcasted_iota(I32, (B_VWIDTH, B_KWIDTH), 1)
    st_mask = jnp.where((rr >> 7) == (kc >> 6), 1.0, 0.0).astype(F32)
    consts = (sel, headmask_k, st_mask)
    _gla_gates(smf, w2_ref, b2_ref, b_ref, 0, True)
    _gla_gates(smb, w2_ref, b2_ref, b_ref, 1, False)

    def body(step, carry):
        _gla_chunk(qkf, vf, b_ref, of_ref, st_ref, 0, True, consts, step)
        _gla_chunk(qkb, vb, b_ref, ob_ref, st_ref, 1, False, consts, step)
        return carry

    lax.fori_loop(0, BLK // CHUNK, body, 0)


def _glascan(proj, w2full, b2full):
    t = proj.shape[0]
    nb = t // BLK
    rev = _rev_block(nb)
    ident = lambda i: i
    qk = lambda f: pl.BlockSpec((BLK, 2 * B_KWIDTH), lambda i: (f(i), COL_BQK // (2 * B_KWIDTH)))
    vv = lambda f: pl.BlockSpec((BLK, B_VWIDTH), lambda i: (f(i), COL_BV // B_VWIDTH))
    sm = lambda f: pl.BlockSpec((BLK, 128), lambda i: (f(i), COL_SMALL // 128))
    outs = lambda f: pl.BlockSpec((BLK, B_VWIDTH), lambda i: (f(i), 0))
    out_sds = jax.ShapeDtypeStruct((t, B_VWIDTH), F32)
    return pl.pallas_call(
        _glascan_kernel,
        grid=(nb,),
        in_specs=[qk(ident), vv(ident), sm(ident), qk(rev), vv(rev), sm(rev),
                  pl.BlockSpec((128, 2 * B_KWIDTH), lambda i: (0, 0)),
                  pl.BlockSpec((1, 2 * B_KWIDTH), lambda i: (0, 0))],
        out_specs=[outs(ident), outs(rev)],
        out_shape=[out_sds, out_sds],
        scratch_shapes=[pltpu.VMEM((2, B_VWIDTH, B_KWIDTH), F32), pltpu.VMEM((2, BLK, B_KWIDTH), F32)],
        compiler_params=_cparams(("arbitrary",)),
        name="gla_scan",
    )(proj, proj, proj, proj, proj, proj, w2full, b2full)


def _attnprep_kernel(cq_ref, ckv_ref, qg_ref, kg_ref, cos_ref, sin_ref, q_ref, k_ref, v_ref):
    cos = cos_ref[...]
    sin = sin_ref[...]
    lane = lax.broadcasted_iota(I32, cos.shape, 1)
    first = (lane % 32) < 16

    def norm_rope(x, g):
        y = x * lax.rsqrt(jnp.sum(x * x, axis=-1, keepdims=True) * (1.0 / HD_C) + EPS) * g
        partner = jnp.where(first, pltpu.roll(y, HD_CP - 16, 1), pltpu.roll(y, 16, 1))
        return y * cos + partner * sin

    for h in range(H_C):
        seg = slice(h * HD_CP, (h + 1) * HD_CP)
        q_ref[:, seg] = (norm_rope(cq_ref[:, seg], qg_ref[...]) * (HD_C ** -0.5)).astype(BF16)
    for g in range(KV_C):
        seg = slice(g * HD_CP, (g + 1) * HD_CP)
        k_ref[:, seg] = norm_rope(ckv_ref[:, seg], kg_ref[...]).astype(BF16)
    v_ref[...] = ckv_ref[:, KV_C * HD_CP:2 * KV_C * HD_CP].astype(BF16)


def _attnprep(proj, qg, kg, cos_t, sin_t):
    t = proj.shape[0]
    tm = BLK
    qw, kw = H_C * HD_CP, KV_C * HD_CP
    return pl.pallas_call(
        _attnprep_kernel,
        grid=(t // tm,),
        in_specs=[pl.BlockSpec((tm, qw), lambda i: (i, COL_CQ // qw)),
                  pl.BlockSpec((tm, 2 * kw), lambda i: (i, COL_CKV // (2 * kw))),
                  pl.BlockSpec((1, HD_CP), lambda i: (0, 0)),
                  pl.BlockSpec((1, HD_CP), lambda i: (0, 0)),
                  pl.BlockSpec((tm, HD_CP), lambda i: (i, 0)),
                  pl.BlockSpec((tm, HD_CP), lambda i: (i, 0))],
        out_specs=[pl.BlockSpec((tm, qw), lambda i: (i, 0)),
                   pl.BlockSpec((tm, kw), lambda i: (i, 0)),
                   pl.BlockSpec((tm, kw), lambda i: (i, 0))],
        out_shape=[jax.ShapeDtypeStruct((t, qw), BF16),
                   jax.ShapeDtypeStruct((t, kw), BF16),
                   jax.ShapeDtypeStruct((t, kw), BF16)],
        compiler_params=_cparams(("arbitrary",)),
        name="attn_prep",
    )(proj, proj, qg, kg, cos_t, sin_t)


def _attn_kernel(sink_ref, q_ref, kp_ref, kc_ref, kn_ref, kx_ref, vp_ref, vc_ref, vn_ref, vx_ref,
                 o_ref, *, nq):
    qi = pl.program_id(0)
    nctx = CTX // ATT_BLOCK
    latent = qi >= nctx
    ql = lax.broadcasted_iota(I32, (ATT_BLOCK, ATT_BLOCK), 0)
    kl = lax.broadcasted_iota(I32, (ATT_BLOCK, ATT_BLOCK), 1)
    ok_prev = jnp.logical_and(qi - 1 >= nctx, kl >= ql)
    ok_next = jnp.logical_and(jnp.logical_and(latent, qi + 1 <= nq - 1), kl <= ql)
    ok_cur = jnp.logical_and(latent, kl >= 0)
    bias = jnp.concatenate([jnp.where(ok_prev, 0.0, NEG), jnp.where(ok_cur, 0.0, NEG),
                            jnp.where(ok_next, 0.0, NEG),
                            jnp.zeros((ATT_BLOCK, CTX), F32)], axis=1)
    for g in range(KV_C):
        seg = slice(g * HD_CP, (g + 1) * HD_CP)
        kcat = jnp.concatenate([kp_ref[:, seg], kc_ref[:, seg], kn_ref[:, seg], kx_ref[:, seg]], axis=0)
        vcat = jnp.concatenate([vp_ref[:, seg], vc_ref[:, seg], vn_ref[:, seg], vx_ref[:, seg]], axis=0)
        for r in range(REP_C):
            h = g * REP_C + r
            hs = slice(h * HD_CP, (h + 1) * HD_CP)
            s = lax.dot_general(q_ref[:, hs], kcat, (((1,), (1,)), ((), ())),
                                preferred_element_type=F32) + bias
            sink = sink_ref[h]
            m = jnp.maximum(jnp.max(s, axis=-1, keepdims=True), sink)
            p = jnp.exp(s - m)
            den = jnp.sum(p, axis=-1, keepdims=True) + jnp.exp(sink - m)
            o = jnp.dot(p.astype(BF16), vcat, preferred_element_type=F32)
            o_ref[:, hs] = (o / den).astype(BF16)


def _attn(qr, kr, vr, sink):
    t = qr.shape[0]
    nq = t // ATT_BLOCK
    nctx = CTX // ATT_BLOCK
    qw, kw = H_C * HD_CP, KV_C * HD_CP
    prev = lambda i: (jnp.maximum(i - 1, nctx), 0)
    cur = lambda i: (i, 0)
    nxt = lambda i: (jnp.minimum(jnp.maximum(i + 1, nctx), nq - 1), 0)
    kv = lambda f: pl.BlockSpec((ATT_BLOCK, kw), f)
    ctxs = pl.BlockSpec((CTX, kw), lambda i: (0, 0))
    return pl.pallas_call(
        functools.partial(_attn_kernel, nq=nq),
        grid=(nq,),
        in_specs=[pl.BlockSpec(memory_space=pltpu.SMEM),
                  pl.BlockSpec((ATT_BLOCK, qw), cur),
                  kv(prev), kv(cur), kv(nxt), ctxs, kv(prev), kv(cur), kv(nxt), ctxs],
        out_specs=pl.BlockSpec((ATT_BLOCK, qw), cur),
        out_shape=jax.ShapeDtypeStruct((t, qw), BF16),
        compiler_params=_cparams(("arbitrary",)),
        name="attn",
    )(sink, qr, kr, kr, kr, kr, vr, vr, vr, vr)


def _head_rms_gate(o, z, g):
    outs = []
    for h in range(o.shape[1] // 128):
        seg = o[:, h * 128:(h + 1) * 128]
        nrm = seg * lax.rsqrt(jnp.mean(seg * seg, axis=-1, keepdims=True) + EPS) * g
        outs.append(nrm * _silu(z[:, h * 128:(h + 1) * 128]))
    return jnp.concatenate(outs, axis=1)


def _merge_kernel(oaf, oab, az, obf, obb, bz, yc, gl, xs, dng, glag, wpa, wpb, wpc, wo, mod_ref,
                  o_ref, *, tm):
    i = pl.program_id(0)
    ya = _head_rms_gate(oaf[...] + oab[...], az[...], dng[...])
    yb = _head_rms_gate(obf[...] + obb[...], bz[...], glag[...])
    pa = _bdot(ya, wpa[...])
    pb = _bdot(yb, wpb[...])
    pc = jnp.dot(yc[...], wpc[...], preferred_element_type=F32)
    merged = (_sigmoid(gl[:, 0:D_MODEL]) * pa + _sigmoid(gl[:, D_MODEL:2 * D_MODEL]) * pb
              + _sigmoid(gl[:, 2 * D_MODEL:3 * D_MODEL]) * pc)
    y = _bdot(merged, wo[...])
    o_ref[...] = xs[...] + y * _res_gate(mod_ref, 0, tm, i * tm)


def _merge(oaf, oab, obf, obb, yc, proj, xs, dng, glag, wpa, wpb, wpc, wo, mod):
    t = xs.shape[0]
    tm = BLK
    row = lambda w, c=0: pl.BlockSpec((tm, w), lambda i: (i, c))
    full = lambda a: pl.BlockSpec(a.shape, lambda i: (0,) * a.ndim)
    return pl.pallas_call(
        functools.partial(_merge_kernel, tm=tm),
        grid=(t // tm,),
        in_specs=[row(A_WIDTH), row(A_WIDTH), row(A_WIDTH, COL_AZ // A_WIDTH),
                  row(B_VWIDTH), row(B_VWIDTH), row(B_VWIDTH, COL_BZ // B_VWIDTH),
                  row(H_C * HD_CP), row(3 * D_MODEL, COL_GL // (3 * D_MODEL)), row(D_MODEL),
                  full(dng), full(glag), full(wpa), full(wpb), full(wpc), full(wo), full(mod)],
        out_specs=row(D_MODEL),
        out_shape=jax.ShapeDtypeStruct((t, D_MODEL), F32),
        compiler_params=_cparams(("arbitrary",)),
        name="merge",
    )(oaf, oab, proj, obf, obb, proj, yc, proj, xs, dng, glag, wpa, wpb, wpc, wo, mod)


def _router_kernel(x_ref, mod_ref, g_ref, rwt_ref, rb_ref, h_ref, e_ref, rank_ref, gate_ref, cnt_ref,
                   carry_ref, *, tm):
    i = pl.program_id(0)

    @pl.when(i == 0)
    def _():
        carry_ref[...] = jnp.zeros_like(carry_ref)

    h = _norm_mod(x_ref[...], g_ref[...], mod_ref, 3 * D_MODEL, i * tm)
    h_ref[...] = h
    logit = lax.dot_general(rwt_ref[...], h, (((1,), (1,)), ((), ())),
                            precision=lax.Precision.HIGHEST, preferred_element_type=F32) + rb_ref[...]
    erow = lax.broadcasted_iota(I32, (N_EXPERTS, tm), 0)
    vals, idxs, sels = [], [], []
    cur = logit
    for _ in range(TOP_K):
        m = jnp.max(cur, axis=0, keepdims=True)
        idx = jnp.min(jnp.where(cur == m, erow, N_EXPERTS), axis=0, keepdims=True)
        sel = erow == idx
        vals.append(m)
        idxs.append(idx)
        sels.append(sel)
        cur = jnp.where(sel, -jnp.inf, cur)
    ex = [jnp.exp(v - vals[0]) for v in vals]
    den = ex[0] + ex[1] + ex[2] + ex[3]
    onehot = jnp.where(sels[0] | sels[1] | sels[2] | sels[3], 1.0, 0.0).astype(F32)
    ss = lax.broadcasted_iota(I32, (tm, tm), 0)
    tt = lax.broadcasted_iota(I32, (tm, tm), 1)
    before = jnp.where(ss < tt, 1.0, 0.0).astype(BF16)
    cnt = jnp.dot(onehot.astype(BF16), before, preferred_element_type=F32) + carry_ref[:, 0:1]
    ranks = [jnp.sum(jnp.where(s, cnt, 0.0), axis=0, keepdims=True) for s in sels]
    pad_i = jnp.zeros((8 - TOP_K, tm), I32)
    e_ref[...] = jnp.concatenate(idxs + [pad_i], axis=0)
    rank_ref[...] = jnp.concatenate([r.astype(I32) for r in ranks] + [pad_i], axis=0)
    gates = jnp.concatenate([e / den for e in ex] + [jnp.zeros((128 - TOP_K, tm), F32)], axis=0)
    gate_ref[...] = gates.T
    carry_ref[...] = carry_ref[...] + jnp.sum(onehot, axis=1, keepdims=True)
    cnt_ref[...] = carry_ref[...]


def _router(xs, mod, g, rwt, rb):
    t = xs.shape[0]
    tm = BLK
    return pl.pallas_call(
        functools.partial(_router_kernel, tm=tm),
        grid=(t // tm,),
        in_specs=[pl.BlockSpec((tm, D_MODEL), lambda i: (i, 0)),
                  pl.BlockSpec((8, 6 * D_MODEL), lambda i: (0, 0)),
                  pl.BlockSpec((1, D_MODEL), lambda i: (0, 0)),
                  pl.BlockSpec((N_EXPERTS, D_MODEL), lambda i: (0, 0)),
                  pl.BlockSpec((N_EXPERTS, 1), lambda i: (0, 0))],
        out_specs=[pl.BlockSpec((tm, D_MODEL), lambda i: (i, 0)),
                   pl.BlockSpec((8, tm), lambda i: (0, i)),
                   pl.BlockSpec((8, tm), lambda i: (0, i)),
                   pl.BlockSpec((tm, 128), lambda i: (i, 0)),
                   pl.BlockSpec((N_EXPERTS, 128), lambda i: (0, 0))],
        out_shape=[jax.ShapeDtypeStruct((t, D_MODEL), F32),
                   jax.ShapeDtypeStruct((8, t), I32),
                   jax.ShapeDtypeStruct((8, t), I32),
                   jax.ShapeDtypeStruct((t, 128), F32),
                   jax.ShapeDtypeStruct((N_EXPERTS, 128), F32)],
        scratch_shapes=[pltpu.VMEM((N_EXPERTS, 128), F32)],
        compiler_params=_cparams(("arbitrary",)),
        name="router",
    )(xs, mod, g, rwt, rb)


def _dest_kernel(pstart_ref, e_ref, rank_ref, d_ref):
    e = e_ref[...]
    acc = rank_ref[...]
    for ee in range(N_EXPERTS):
        acc = acc + jnp.where(e == ee, pstart_ref[ee], 0)
    d_ref[...] = acc


def _dest(pstart, e, rank):
    t = e.shape[1]
    tl = _pick(t, (3328, 1280, 768, 256, 128))
    blk = pl.BlockSpec((8, tl), lambda i: (0, i))
    return pl.pallas_call(
        _dest_kernel,
        grid=(t // tl,),
        in_specs=[pl.BlockSpec(memory_space=pltpu.SMEM), blk, blk],
        out_specs=blk,
        out_shape=jax.ShapeDtypeStruct((8, t), I32),
        compiler_params=_cparams(("arbitrary",)),
        name="moe_dest",
    )(pstart, e, rank)


def _dispatch_kernel(dest_ref, h_ref, zero_ref, xs_ref, sem, *, tm):
    del zero_ref

    def body(tok, carry):
        for kk in range(TOP_K):
            pltpu.make_async_copy(h_ref.at[pl.ds(tok, 1), :],
                                  xs_ref.at[pl.ds(dest_ref[kk, tok], 1), :], sem).start()
        return carry

    lax.fori_loop(0, tm, body, 0)
    for _ in range(TOP_K):
        pltpu.make_async_copy(h_ref, xs_ref.at[pl.ds(0, tm), :], sem).wait()


def _dispatch(dest, h, p_rows):
    t = h.shape[0]
    tm = BLK
    zeros = jnp.zeros((p_rows, D_MODEL), F32)
    return pl.pallas_call(
        functools.partial(_dispatch_kernel, tm=tm),
        grid=(t // tm,),
        in_specs=[pl.BlockSpec((8, tm), lambda i: (0, i), memory_space=pltpu.SMEM),
                  pl.BlockSpec((tm, D_MODEL), lambda i: (i, 0)),
                  pl.BlockSpec(memory_space=pl.ANY)],
        out_specs=pl.BlockSpec(memory_space=pl.ANY),
        out_shape=jax.ShapeDtypeStruct((p_rows, D_MODEL), F32),
        scratch_shapes=[pltpu.SemaphoreType.DMA(())],
        input_output_aliases={2: 0},
        compiler_params=_cparams(("arbitrary",)),
        name="moe_dispatch",
    )(dest, h, zeros)


def _expert_kernel(be_ref, nused_ref, x_ref, wgu_ref, bgu_ref, wdn_ref, bdn_ref, y_ref, wgu_bf, wdn_bf):
    b = pl.program_id(0)
    changed = jnp.logical_or(b == 0, be_ref[b] != be_ref[jnp.maximum(b - 1, 0)])

    @pl.when(changed)
    def _():
        wgu_bf[...] = wgu_ref[...].astype(BF16)
        wdn_bf[...] = wdn_ref[...].astype(BF16)

    @pl.when(b < nused_ref[0])
    def _():
        gu = jnp.dot(x_ref[...].astype(BF16), wgu_bf[...], preferred_element_type=F32) + bgu_ref[...]
        g_ = jnp.minimum(gu[:, :D_FF], SWIGLU_LIMIT)
        u_ = jnp.clip(gu[:, D_FF:], -SWIGLU_LIMIT, SWIGLU_LIMIT)
        act = (u_ + 1.0) * (g_ * _sigmoid(g_ * SWIGLU_ALPHA))
        y_ref[...] = jnp.dot(act.astype(BF16), wdn_bf[...], preferred_element_type=F32) + bdn_ref[...]

    @pl.when(b >= nused_ref[0])
    def _():
        y_ref[...] = jnp.zeros_like(y_ref)


def _experts(blk_e, nused, xsorted, w_gu, b_gu, w_dn, b_dn):
    p_rows = xsorted.shape[0]
    nblk = p_rows // MOE_BM
    grid_spec = pltpu.PrefetchScalarGridSpec(
        num_scalar_prefetch=2,
        grid=(nblk,),
        in_specs=[pl.BlockSpec((MOE_BM, D_MODEL), lambda b, be, nu: (b, 0)),
                  pl.BlockSpec((None, D_MODEL, 2 * D_FF), lambda b, be, nu: (be[b], 0, 0)),
                  pl.BlockSpec((None, 1, 2 * D_FF), lambda b, be, nu: (be[b], 0, 0)),
                  pl.BlockSpec((None, D_FF, D_MODEL), lambda b, be, nu: (be[b], 0, 0)),
                  pl.BlockSpec((None, 1, D_MODEL), lambda b, be, nu: (be[b], 0, 0))],
        out_specs=pl.BlockSpec((MOE_BM, D_MODEL), lambda b, be, nu: (b, 0)),
        scratch_shapes=[pltpu.VMEM((D_MODEL, 2 * D_FF), BF16), pltpu.VMEM((D_FF, D_MODEL), BF16)],
    )
    return pl.pallas_call(
        _expert_kernel,
        grid_spec=grid_spec,
        out_shape=jax.ShapeDtypeStruct((p_rows, D_MODEL), F32),
        compiler_params=_cparams(("arbitrary",)),
        name="moe_experts",
    )(blk_e, nused, xsorted, w_gu, b_gu.reshape(N_EXPERTS, 1, 2 * D_FF), w_dn,
      b_dn.reshape(N_EXPERTS, 1, D_MODEL))


def _combine_kernel(dest_ref, y_ref, gate_ref, xs_ref, mod_ref, o_ref, buf, sem, *, tm):
    i = pl.program_id(0)

    def body(tok, carry):
        for kk in range(TOP_K):
            pltpu.make_async_copy(y_ref.at[pl.ds(dest_ref[kk, tok], 1), :],
                                  buf.at[kk, pl.ds(tok, 1), :], sem).start()
        return carry

    lax.fori_loop(0, tm, body, 0)
    for kk in range(TOP_K):
        pltpu.make_async_copy(y_ref.at[pl.ds(0, tm), :], buf.at[kk], sem).wait()
    gate = gate_ref[...]
    acc = gate[:, 0:1] * buf[0]
    for kk in range(1, TOP_K):
        acc = acc + gate[:, kk:kk + 1] * buf[kk]
    o_ref[...] = xs_ref[...] + acc * _res_gate(mod_ref, 3 * D_MODEL, tm, i * tm)


def _combine(dest, y, gate_col, xs, mod):
    t = xs.shape[0]
    tm = 128
    return pl.pallas_call(
        functools.partial(_combine_kernel, tm=tm),
        grid=(t // tm,),
        in_specs=[pl.BlockSpec((8, tm), lambda i: (0, i), memory_space=pltpu.SMEM),
                  pl.BlockSpec(memory_space=pl.ANY),
                  pl.BlockSpec((tm, 128), lambda i: (i, 0)),
                  pl.BlockSpec((tm, D_MODEL), lambda i: (i, 0)),
                  pl.BlockSpec((8, 6 * D_MODEL), lambda i: (0, 0))],
        out_specs=pl.BlockSpec((tm, D_MODEL), lambda i: (i, 0)),
        out_shape=jax.ShapeDtypeStruct((t, D_MODEL), F32),
        scratch_shapes=[pltpu.VMEM((TOP_K, tm, D_MODEL), F32), pltpu.SemaphoreType.DMA(())],
        compiler_params=_cparams(("arbitrary",)),
        name="moe_combine",
    )(dest, y, gate_col, xs, mod)


def _pad_heads(w, nh, hd, hdp):
    d = w.shape[0]
    return jnp.pad(w.reshape(d, nh, hd), ((0, 0), (0, 0), (0, hdp - hd))).reshape(d, nh * hdp)


def _layout_w_in(w):
    return _layout_w_in_f32(w).astype(BF16)


def _layout_w_in_f32(w):
    pts = np.cumsum([A_WIDTH] * 4 + [2 * H_A, 2 * H_A, B_KWIDTH, B_KWIDTH, B_VWIDTH, B_VWIDTH,
                                     2 * GLA_RANK, C_WIDTH, KV_C * HD_C, KV_C * HD_C])
    (aq, ak, av, az, aa, ab, bq, bk, bv, bz, bg, cq, ck, cv, gl) = jnp.split(w, pts.tolist(), axis=1)
    small = jnp.concatenate([aa, ab, bg], axis=1)
    small = jnp.pad(small, ((0, 0), (0, 128 - small.shape[1])))
    cols = [gl, aq, ak, av, az, _pad_heads(cq, H_C, HD_C, HD_CP), _pad_heads(ck, KV_C, HD_C, HD_CP),
            _pad_heads(cv, KV_C, HD_C, HD_CP), bq, bk, bv, bz, small]
    out = jnp.concatenate(cols, axis=1)
    assert out.shape[1] == IN_COLS_P
    return out


def _rope_tables(t):
    s_len = t - CTX
    half = HD_C // 2
    inv_freq = ROPE_THETA ** (-jnp.arange(0, half, 2, dtype=F32) / half)
    pos = jnp.arange(s_len)
    rows = (pos // GRID_W).astype(F32)[:, None] * inv_freq[None, :]
    cols = (pos % GRID_W).astype(F32)[:, None] * inv_freq[None, :]
    cr, sr, cc, sc = jnp.cos(rows), jnp.sin(rows), jnp.cos(cols), jnp.sin(cols)
    zpad = jnp.zeros((s_len, HD_CP - HD_C), F32)
    cos_l = jnp.concatenate([cr, cr, cc, cc, zpad], axis=1)
    sin_l = jnp.concatenate([-sr, sr, -sc, sc, zpad], axis=1)
    cos_t = jnp.concatenate([jnp.ones((CTX, HD_CP), F32), cos_l], axis=0)
    sin_t = jnp.concatenate([jnp.zeros((CTX, HD_CP), F32), sin_l], axis=0)
    return cos_t, sin_t


def _lane_vec(v, width=128):
    v = v.reshape(1, -1).astype(F32)
    return jnp.pad(v, ((0, 0), (0, width - v.shape[1])))


def _moe_plan(counts, tk):
    padded = (counts + MOE_BM - 1) // MOE_BM * MOE_BM
    pend = jnp.cumsum(padded)
    pstart = pend - padded
    nblk = (tk + N_EXPERTS * (MOE_BM - 1) + MOE_BM - 1) // MOE_BM
    blk_e = jnp.minimum(jnp.searchsorted(pend, jnp.arange(nblk, dtype=I32) * MOE_BM, side='right'),
                        N_EXPERTS - 1).astype(I32)
    nused = (pend[-1] // MOE_BM).astype(I32).reshape(1)
    return pstart.astype(I32), blk_e, nused, nblk * MOE_BM


def kernel(x, c, ctx, c_ctx, ada_w, ada_b, norm_mix_g, norm_ffn_g, w_in, dn_conv_w, dn_a_log, dn_dt_bias,
           dn_norm_g, gla_w2, gla_b2, gla_norm_g, attn_q_norm_g, attn_k_norm_g, attn_sink, w_branch_a,
           w_branch_b, w_branch_c, w_out, router_w, router_b, w_gate_up, b_gate_up, w_down, b_down):
    assert x.shape[0] == 1 and c.shape[0] == 1 and ctx.shape[1] == CTX
    depth = ada_w.shape[0]
    xs = jnp.concatenate([ctx[0], x[0]], axis=0)
    t = xs.shape[0]
    assert t % BLK == 0 and (t - CTX) % GRID_W == 0
    cc = jnp.concatenate([c, c_ctx[None, :], jnp.zeros((6, D_MODEL), F32)], axis=0)
    mods = _ada_mod(cc, ada_w, ada_b)
    cos_t, sin_t = _rope_tables(t)
    for l in range(depth):
        mod = mods[l]
        proj = _inproj(xs, mod, norm_mix_g[l][None, :], _layout_w_in(w_in[l]))
        qa, ka, va, gcol = _dnprep(proj, dn_conv_w[l], _lane_vec(dn_a_log[l]), _lane_vec(dn_dt_bias[l]))
        oaf, oab = _dnscan(qa, ka, va, gcol)
        w2 = gla_w2[l].astype(F32)
        w2full = jnp.zeros((128, 2 * B_KWIDTH), F32)
        for d in range(2):
            r0 = 4 * H_A + d * GLA_RANK
            w2full = w2full.at[r0:r0 + GLA_RANK, d * B_KWIDTH:(d + 1) * B_KWIDTH].set(w2[d])
        obf, obb = _glascan(proj, w2full, gla_b2[l].reshape(1, 2 * B_KWIDTH).astype(F32))
        qg = _lane_vec(attn_q_norm_g[l])
        kg = _lane_vec(attn_k_norm_g[l])
        qr, kr, vr = _attnprep(proj, qg, kg, cos_t, sin_t)
        yc = _attn(qr, kr, vr, attn_sink[l].astype(F32))
        wpc = jnp.pad(w_branch_c[l].reshape(H_C, HD_C, D_MODEL),
                      ((0, 0), (0, HD_CP - HD_C), (0, 0))).reshape(H_C * HD_CP, D_MODEL)
        xs = _merge(oaf, oab, obf, obb, yc, proj, xs,
                    dn_norm_g[l][None, :], gla_norm_g[l][None, :],
                    w_branch_a[l].astype(BF16), w_branch_b[l].astype(BF16), wpc.astype(BF16),
                    w_out[l].astype(BF16), mod)
        h2, top_e, rank, gate_col, cnt = _router(xs, mod, norm_ffn_g[l][None, :], router_w[l].T,
                                                 router_b[l][:, None])
        pstart, blk_e, nused, p_rows = _moe_plan(cnt[:, 0].astype(I32), t * TOP_K)
        dest = _dest(pstart, top_e, rank)
        xsorted = _dispatch(dest, h2, p_rows)
        y = _experts(blk_e, nused, xsorted, w_gate_up[l], b_gate_up[l], w_down[l], b_down[l])
        xs = _combine(dest, y, gate_col, xs, mod)
    return xs[CTX:][None]
```

```python
import functools
import math

import jax
import jax.numpy as jnp
import numpy as np
from jax import lax
from jax.experimental import pallas as pl
from jax.experimental.pallas import tpu as pltpu

F32 = jnp.float32
BF16 = jnp.bfloat16
I32 = jnp.int32

D_MODEL = 1024
DEPTH = 4
GRID_W = 64
CTX = 256
H_A = 4
HD_A = 128
A_WIDTH = H_A * HD_A
CONV_W = 5
H_B = 4
DK_B = 64
DV_B = 128
B_KWIDTH = H_B * DK_B
B_VWIDTH = H_B * DV_B
GLA_RANK = 16
GLA_NORMALIZER = 16.0
H_C = 8
KV_C = 2
REP_C = H_C // KV_C
HD_C = 64
HD_CP = 128
C_WIDTH = H_C * HD_C
ATT_BLOCK = 128
ROPE_THETA = 10000.0
CHUNK = 64
SUB = 16
N_EXPERTS = 32
TOP_K = 4
D_FF = 1024
SWIGLU_LIMIT = 7.0
SWIGLU_ALPHA = 1.702
EPS = 1e-6
NEG = -1e30

BLK = 256
MOE_BM = 256
N_ZERO_FILLS = 2 * N_EXPERTS + 1
VMEM_LIMIT = 56 * 1024 * 1024

COL_GL = 0
COL_QKV = 3072
COL_AZ = 4608
COL_CQ = 5120
COL_CKV = 6144
COL_BQK = 6656
COL_BV = 7168
COL_BZ = 7680
COL_SMALL = 8192
IN_COLS_P = 8448
IN_TN = 768


def _pick(n, cands):
    for c in cands:
        if n % c == 0:
            return c
    raise ValueError(f"no tile for {n}")


def _cparams(sem):
    return pltpu.CompilerParams(dimension_semantics=sem, vmem_limit_bytes=VMEM_LIMIT)


def _bdot(a, b):
    return jnp.dot(a.astype(BF16), b.astype(BF16), preferred_element_type=F32)


def _bdot_nt(a, b):
    return lax.dot_general(a.astype(BF16), b.astype(BF16), (((1,), (1,)), ((), ())),
                           preferred_element_type=F32)


def _bdot_tn(a, b):
    return lax.dot_general(a.astype(BF16), b.astype(BF16), (((0,), (0,)), ((), ())),
                           preferred_element_type=F32)


def _fdot(a, b):
    return jnp.dot(a, b, precision=lax.Precision.HIGHEST, preferred_element_type=F32)


def _split2(a):
    hi = a.astype(BF16)
    lo = (a - hi.astype(F32)).astype(BF16)
    return hi, lo


def _dot3(a, b):
    ah, al = _split2(a)
    bh, bl = _split2(b)
    d = functools.partial(jnp.dot, preferred_element_type=F32)
    return d(ah, bh) + (d(ah, bl) + d(al, bh))


def _sigmoid(x):
    return 1.0 / (1.0 + jnp.exp(-x))


def _silu(x):
    return x * _sigmoid(x)


def _softplus(x):
    return jnp.maximum(x, 0.0) + jnp.log(1.0 + jnp.exp(-jnp.abs(x)))


def _ada_kernel(cc_ref, w_ref, b_ref, o_ref):
    o_ref[...] = _fdot(_silu(cc_ref[...]), w_ref[...]) + b_ref[...]


def _ada_mod(cc, ada_w, ada_b):
    depth = ada_w.shape[0]
    tn = 1536
    return pl.pallas_call(
        _ada_kernel,
        grid=(depth, 6 * D_MODEL // tn),
        in_specs=[pl.BlockSpec((8, D_MODEL), lambda l, j: (0, 0)),
                  pl.BlockSpec((None, D_MODEL, tn), lambda l, j: (l, 0, j)),
                  pl.BlockSpec((None, 1, tn), lambda l, j: (l, 0, j))],
        out_specs=pl.BlockSpec((None, 8, tn), lambda l, j: (l, 0, j)),
        out_shape=jax.ShapeDtypeStruct((depth, 8, 6 * D_MODEL), F32),
        compiler_params=_cparams(("arbitrary", "arbitrary")),
        name="ada_mod",
    )(cc, ada_w, ada_b.reshape(depth, 1, 6 * D_MODEL))


def _norm_mod(x, g, mod_ref, moff, row0):
    tm = x.shape[0]
    y = x * lax.rsqrt(jnp.mean(x * x, axis=-1, keepdims=True) + EPS) * g
    isc = (row0 + lax.broadcasted_iota(I32, (tm, 1), 0)) < CTX
    shift = jnp.where(isc, mod_ref[1:2, moff:moff + D_MODEL], mod_ref[0:1, moff:moff + D_MODEL])
    scale = jnp.where(isc, mod_ref[1:2, moff + D_MODEL:moff + 2 * D_MODEL],
                      mod_ref[0:1, moff + D_MODEL:moff + 2 * D_MODEL])
    return y * (1.0 + scale) + shift


def _res_gate(mod_ref, moff, tm, row0):
    isc = (row0 + lax.broadcasted_iota(I32, (tm, 1), 0)) < CTX
    return jnp.where(isc, mod_ref[1:2, moff + 2 * D_MODEL:moff + 3 * D_MODEL],
                     mod_ref[0:1, moff + 2 * D_MODEL:moff + 3 * D_MODEL])


def _inproj_kernel(x_ref, mod_ref, g_ref, w_ref, o_ref, h_ref, *, tm):
    i = pl.program_id(0)

    @pl.when(pl.program_id(1) == 0)
    def _():
        h_ref[...] = _norm_mod(x_ref[...], g_ref[...], mod_ref, 0, i * tm).astype(BF16)

    o_ref[...] = jnp.dot(h_ref[...], w_ref[...], preferred_element_type=F32)


def _inproj(xs, mod, g, w):
    t = xs.shape[0]
    tm = _pick(t, (1280, 640, 256))
    return pl.pallas_call(
        functools.partial(_inproj_kernel, tm=tm),
        grid=(t // tm, IN_COLS_P // IN_TN),
        in_specs=[pl.BlockSpec((tm, D_MODEL), lambda i, j: (i, 0)),
                  pl.BlockSpec((8, 6 * D_MODEL), lambda i, j: (0, 0)),
                  pl.BlockSpec((1, D_MODEL), lambda i, j: (0, 0)),
                  pl.BlockSpec((D_MODEL, IN_TN), lambda i, j: (0, j))],
        out_specs=pl.BlockSpec((tm, IN_TN), lambda i, j: (i, j)),
        out_shape=jax.ShapeDtypeStruct((t, IN_COLS_P), F32),
        scratch_shapes=[pltpu.VMEM((tm, D_MODEL), BF16)],
        compiler_params=_cparams(("arbitrary", "arbitrary")),
        name="inproj",
    )(xs, mod, g, w)


def _tri_blockdiag(n, lower):
    ii = lax.broadcasted_iota(I32, (n, n), 0)
    jj = lax.broadcasted_iota(I32, (n, n), 1)
    same = (ii >> 6) == (jj >> 6)
    tri = (ii >= jj) if lower else (ii <= jj)
    return jnp.where(same, jnp.where(tri, 1.0, 0.0), 0.0).astype(F32)


def _dnprep_kernel(main_ref, prev_ref, next_ref, small_ref, cw_ref, alog_ref, dtb_ref,
                   q_ref, k_ref, v_ref, gcol_ref, ext_ref, *, nb):
    i = pl.program_id(0)
    use_prev = i >= 2
    use_next = jnp.logical_and(i >= 1, i <= nb - 2)
    ext_ref[0:8, :] = jnp.where(use_prev, prev_ref[...], 0.0)
    ext_ref[8:8 + BLK, :] = main_ref[...]
    ext_ref[8 + BLK:16 + BLK, :] = jnp.where(use_next, next_ref[...], 0.0)
    acc = ext_ref[6:6 + BLK, :] * cw_ref[0:1, :]
    for d in range(1, CONV_W):
        acc = acc + ext_ref[6 + d:6 + d + BLK, :] * cw_ref[d:d + 1, :]
    s = _silu(acc)
    for h in range(H_A):
        for part, ref, mul in ((0, q_ref, HD_A ** -0.5), (1, k_ref, 1.0)):
            seg = s[:, part * A_WIDTH + h * HD_A: part * A_WIDTH + (h + 1) * HD_A]
            nrm = seg * lax.rsqrt(jnp.sum(seg * seg, axis=-1, keepdims=True) + EPS)
            ref[:, h * HD_A:(h + 1) * HD_A] = nrm * mul
    v_ref[...] = s[:, 2 * A_WIDTH:3 * A_WIDTH]
    sm = small_ref[...]
    lane = lax.broadcasted_iota(I32, sm.shape, 1)
    g = -jnp.exp(alog_ref[...]) * _softplus(sm + dtb_ref[...])
    gb = jnp.where(lane < 2 * H_A, g, jnp.where(lane < 4 * H_A, _sigmoid(sm), 0.0))
    cf = _fdot(_tri_blockdiag(BLK, True), gb)
    cr = _fdot(_tri_blockdiag(BLK, False), gb)
    gc = jnp.where(lane < H_A, cf, jnp.where(lane < 2 * H_A, cr, gb))
    gcol_ref[...] = gc


def _dnprep(proj, conv_w, alog_vec, dtb_vec):
    t = proj.shape[0]
    nb = t // BLK
    qkv_blk = COL_QKV // (3 * A_WIDTH)
    last8 = t // 8 - 1
    out_sds = jax.ShapeDtypeStruct((t, A_WIDTH), F32)
    return pl.pallas_call(
        functools.partial(_dnprep_kernel, nb=nb),
        grid=(nb,),
        in_specs=[pl.BlockSpec((BLK, 3 * A_WIDTH), lambda i: (i, qkv_blk)),
                  pl.BlockSpec((8, 3 * A_WIDTH), lambda i: (jnp.maximum(i * (BLK // 8) - 1, 0), qkv_blk)),
                  pl.BlockSpec((8, 3 * A_WIDTH), lambda i: (jnp.minimum((i + 1) * (BLK // 8), last8), qkv_blk)),
                  pl.BlockSpec((BLK, 128), lambda i: (i, COL_SMALL // 128)),
                  pl.BlockSpec((CONV_W, 3 * A_WIDTH), lambda i: (0, 0)),
                  pl.BlockSpec((1, 128), lambda i: (0, 0)),
                  pl.BlockSpec((1, 128), lambda i: (0, 0))],
        out_specs=[pl.BlockSpec((BLK, A_WIDTH), lambda i: (i, 0)),
                   pl.BlockSpec((BLK, A_WIDTH), lambda i: (i, 0)),
                   pl.BlockSpec((BLK, A_WIDTH), lambda i: (i, 0)),
                   pl.BlockSpec((BLK, 128), lambda i: (i, 0))],
        out_shape=[out_sds, out_sds, out_sds,
                   jax.ShapeDtypeStruct((t, 128), F32)],
        scratch_shapes=[pltpu.VMEM((BLK + 16, 3 * A_WIDTH), F32)],
        compiler_params=_cparams(("arbitrary",)),
        name="dn_prep",
    )(proj, proj, proj, proj, conv_w, alog_vec, dtb_vec)


def _dot3_all(a_list, b_list):
    d = functools.partial(jnp.dot, preferred_element_type=F32)
    sa = [_split2(a) for a in a_list]
    sb = [_split2(b) for b in b_list]
    hh = [d(a[0], b[0]) for a, b in zip(sa, sb)]
    hl = [d(a[0], b[1]) for a, b in zip(sa, sb)]
    lh = [d(a[1], b[0]) for a, b in zip(sa, sb)]
    return [x + (y + z) for x, y, z in zip(hh, hl, lh)]


def _unit_tri_inverse_all(l_mats, masks):
    eye, m_diag, m_l1, m_l2 = masks
    ld = [l * m_diag for l in l_mats]
    x = [eye - a for a in ld]
    p = _dot3_all(ld, ld)
    for it in range(3):
        xp = _dot3_all(x, p)
        if it < 2:
            p = _dot3_all(p, p)
        x = [a + b for a, b in zip(x, xp)]
    for m in (m_l1, m_l2):
        cx = _dot3_all([l * m for l in l_mats], x)
        xcx = _dot3_all(x, cx)
        x = [a - b for a, b in zip(x, xcx)]
    return x


def _dn_masks():
    ii = lax.broadcasted_iota(I32, (CHUNK, CHUNK), 0)
    jj = lax.broadcasted_iota(I32, (CHUNK, CHUNK), 1)
    one = lambda c: jnp.where(c, 1.0, 0.0).astype(F32)
    eye = one(ii == jj)
    m_diag = one((ii >> 4) == (jj >> 4))
    m_l2 = one((ii >> 5) != (jj >> 5))
    m_l1 = 1.0 - m_diag - m_l2
    return ii, jj, (eye, m_diag, m_l1, m_l2)


def _dn_local(items, ii, jj, masks):
    n = len(items)
    dec, lmat, qk, kb, eg = [], [], [], [], []
    for q, k, v, gcol, grow, bcol, fwd in items:
        incl = (ii >= jj) if fwd else (ii <= jj)
        dec.append(jnp.exp(jnp.where(incl, gcol - grow, NEG)))
        kb.append(k * bcol)
        eg.append(jnp.exp(gcol))
    kh = [it[1].astype(BF16) for it in items]
    kk = [_bdot_nt(kb[i], kh[i]) for i in range(n)]
    qkr = [_bdot_nt(items[i][0], kh[i]) for i in range(n)]
    for i in range(n):
        fwd = items[i][6]
        strict = (ii > jj) if fwd else (ii < jj)
        lmat.append(kk[i] * jnp.where(strict, dec[i], 0.0))
        qk.append((qkr[i] * dec[i]).astype(BF16))
    rhs = [jnp.concatenate([items[i][2] * items[i][5], kb[i] * eg[i]], axis=1) for i in range(n)]
    sol = _dot3_all(_unit_tri_inverse_all(lmat, masks), rhs)
    out = []
    for i in range(n):
        q, k, _, gcol, _, _, fwd = items[i]
        glast = gcol[CHUNK - 1:CHUNK, :] if fwd else gcol[0:1, :]
        out.append((sol[i][:, :HD_A], sol[i][:, HD_A:].astype(BF16), qk[i], (q * eg[i]).astype(BF16),
                    (k * jnp.exp(glast - gcol)).astype(BF16), jnp.exp(glast)))
    return out


def _dn_step(local, states):
    n = len(local)
    sb = [s.astype(BF16) for s in states]
    d = functools.partial(jnp.dot, preferred_element_type=F32)
    ws = [d(local[i][1], sb[i]) for i in range(n)]
    qs = [d(local[i][3], sb[i]) for i in range(n)]
    v_new = [(local[i][0] - ws[i]).astype(BF16) for i in range(n)]
    o2 = [d(local[i][2], v_new[i]) for i in range(n)]
    kv = [lax.dot_general(local[i][4], v_new[i], (((0,), (0,)), ((), ())), preferred_element_type=F32)
          for i in range(n)]
    return [qs[i] + o2[i] for i in range(n)], [states[i] * local[i][5] + kv[i] for i in range(n)]


def _dnscan_kernel(qf, kf, vf, gcf, qb, kb, vb, gcb, of_ref, ob_ref, s_ref):
    @pl.when(pl.program_id(0) == 0)
    def _():
        s_ref[...] = jnp.zeros_like(s_ref)

    ii, jj, masks = _dn_masks()
    nch = BLK // CHUNK
    pick = jnp.where(lax.broadcasted_iota(I32, (16, 128), 0) == lax.broadcasted_iota(I32, (16, 128), 1),
                     1.0, 0.0).astype(F32)

    dirs = ((True, (qf, kf, vf, gcf, of_ref)), (False, (qb, kb, vb, gcb, ob_ref)))
    items, sinks = [], []
    for step in range(nch):
        for fwd, (q_r, k_r, v_r, gc_r, o_r) in dirs:
            c = step if fwd else nch - 1 - step
            rows = slice(c * CHUNK, (c + 1) * CHUNK)
            d = 0 if fwd else 1
            gct = gc_r[rows, :]
            grows = lax.dot_general(pick, gct, (((1,), (1,)), ((), ())),
                                    precision=lax.Precision.HIGHEST, preferred_element_type=F32)
            for h in range(H_A):
                lanes = slice(h * HD_A, (h + 1) * HD_A)
                gi = d * H_A + h
                items.append((q_r[rows, lanes], k_r[rows, lanes], v_r[rows, lanes],
                              gct[:, gi:gi + 1], grows[gi:gi + 1, :],
                              gct[:, 2 * H_A + gi:2 * H_A + gi + 1], fwd))
                sinks.append((o_r, rows, lanes))
    local = _dn_local(items, ii, jj, masks)
    nchain = 2 * H_A
    states = [s_ref[gi] for gi in range(nchain)]
    for step in range(nch):
        outs, states = _dn_step(local[step * nchain:(step + 1) * nchain], states)
        for (o_r, rows, lanes), o in zip(sinks[step * nchain:(step + 1) * nchain], outs):
            o_r[rows, lanes] = o
    for gi in range(nchain):
        s_ref[gi] = states[gi]


def _rev_block(nb):
    return lambda i: jnp.where(i == 0, 0, nb - i)


def _dnscan(q, k, v, gcol):
    t = q.shape[0]
    nb = t // BLK
    rev = _rev_block(nb)
    wide = lambda f: pl.BlockSpec((BLK, A_WIDTH), lambda i: (f(i), 0))
    col = lambda f: pl.BlockSpec((BLK, 128), lambda i: (f(i), 0))
    ident = lambda i: i
    out_sds = jax.ShapeDtypeStruct((t, A_WIDTH), F32)
    return pl.pallas_call(
        _dnscan_kernel,
        grid=(nb,),
        in_specs=[wide(ident), wide(ident), wide(ident), col(ident),
                  wide(rev), wide(rev), wide(rev), col(rev)],
        out_specs=[wide(ident), wide(rev)],
        out_shape=[out_sds, out_sds],
        scratch_shapes=[pltpu.VMEM((2 * H_A, HD_A, HD_A), F32)],
        compiler_params=_cparams(("arbitrary",)),
        name="dn_scan",
    )(q, k, v, gcol, q, k, v, gcol)


def _gla_gates(small_ref, w2_ref, b2_ref, b_ref, d, fwd):
    cols = slice(d * B_KWIDTH, (d + 1) * B_KWIDTH)
    pre = _fdot(small_ref[...], w2_ref[:, cols]) + b2_ref[:, cols]
    gk = -_softplus(-pre) * (1.0 / GLA_NORMALIZER)
    b_ref[d] = _fdot(_tri_blockdiag(BLK, fwd), gk)


def _gla_chunk(qk_ref, v_ref, b_ref, o_ref, st_ref, d, fwd, consts, step):
    sel, headmask_k, st_mask = consts
    nch = BLK // CHUNK
    nsub = CHUNK // SUB
    sub_i = lax.broadcasted_iota(I32, (SUB, 1), 0)
    row_c = lax.broadcasted_iota(I32, (CHUNK, 1), 0)
    if True:
        c = step if fwd else nch - 1 - step
        rows = pl.ds(pl.multiple_of(c * CHUNK, CHUNK), CHUNK)
        q = qk_ref[rows, 0:B_KWIDTH] * (DK_B ** -0.5)
        k = qk_ref[rows, B_KWIDTH:2 * B_KWIDTH]
        v = v_ref[rows, :]
        b = b_ref[d, rows, :]
        vh = v.astype(BF16)
        st = st_ref[d]
        o = _bdot_nt(q * jnp.exp(b), st)
        refs = []
        for sb in range(nsub):
            if fwd:
                r = b[sb * SUB - 1:sb * SUB, :] if sb > 0 else jnp.zeros((1, B_KWIDTH), F32)
            else:
                r = b[(sb + 1) * SUB:(sb + 1) * SUB + 1, :] if sb < nsub - 1 else jnp.zeros((1, B_KWIDTH), F32)
            refs.append(r)
        rfull = jnp.concatenate([jnp.broadcast_to(r, (SUB, B_KWIDTH)) for r in refs], axis=0)
        qs = q * jnp.exp(b - rfull)
        a_off = [None] * H_B
        for sb in (range(1, nsub) if fwd else range(0, nsub - 1)):
            jmask = (row_c < sb * SUB) if fwd else (row_c >= (sb + 1) * SUB)
            ks = (k * jnp.exp(jnp.where(jmask, refs[sb] - b, NEG))).astype(BF16)
            rowmask = jnp.where((row_c >> 4) == sb, 1.0, 0.0)
            for h in range(H_B):
                a = _bdot_nt(qs * headmask_k[h], ks) * rowmask
                a_off[h] = a if a_off[h] is None else a_off[h] + a
        o = o + jnp.concatenate(
            [_bdot(a_off[h], vh[:, h * DV_B:(h + 1) * DV_B]) for h in range(H_B)], axis=1)
        diag = []
        for sb in range(nsub):
            s0 = sb * SUB
            bs, qsb, ksb = b[s0:s0 + SUB, :], q[s0:s0 + SUB, :], k[s0:s0 + SUB, :]
            tiles = []
            for jl in range(SUB):
                causal = (sub_i >= jl) if fwd else (sub_i <= jl)
                e = jnp.exp(jnp.where(causal, bs - bs[jl:jl + 1, :], NEG))
                tiles.append((qsb * ksb[jl:jl + 1, :] * e).astype(BF16))
            red = jnp.dot(jnp.concatenate(tiles, axis=0), sel, preferred_element_type=F32)
            acc = red[0:SUB, :] * v[s0:s0 + 1, :]
            for jl in range(1, SUB):
                acc = acc + red[jl * SUB:(jl + 1) * SUB, :] * v[s0 + jl:s0 + jl + 1, :]
            diag.append(acc)
        o_ref[rows, :] = o + jnp.concatenate(diag, axis=0)
        blast = b[CHUNK - 1:CHUNK, :] if fwd else b[0:1, :]
        kd = k * jnp.exp(blast - b)
        st_ref[d] = st * jnp.exp(blast) + _bdot_tn(v, kd) * st_mask


def _glascan_kernel(qkf, vf, smf, qkb, vb, smb, w2_ref, b2_ref, of_ref, ob_ref, st_ref, b_ref):
    @pl.when(pl.program_id(0) == 0)
    def _():
        st_ref[...] = jnp.zeros_like(st_ref)

    kk = lax.broadcasted_iota(I32, (B_KWIDTH, B_VWIDTH), 0)
    cc = lax.broadcasted_iota(I32, (B_KWIDTH, B_VWIDTH), 1)
    sel = jnp.where((kk >> 6) == (cc >> 7), 1.0, 0.0).astype(BF16)
    lane = lax.broadcasted_iota(I32, (1, B_KWIDTH), 1)
    headmask_k = [jnp.where((lane >> 6) == h, 1.0, 0.0).astype(F32) for h in range(H_B)]
    rr = lax.broadcasted_iota(I32, (B_VWIDTH, B_KWIDTH), 0)
    kc = lax.broadcasted_iota(I32, (B_VWIDTH, B_KWIDTH), 1)
    st_mask = jnp.where((rr >> 7) == (kc >> 6), 1.0, 0.0).astype(F32)
    consts = (sel, headmask_k, st_mask)
    _gla_gates(smf, w2_ref, b2_ref, b_ref, 0, True)
    _gla_gates(smb, w2_ref, b2_ref, b_ref, 1, False)

    def body(step, carry):
        _gla_chunk(qkf, vf, b_ref, of_ref, st_ref, 0, True, consts, step)
        _gla_chunk(qkb, vb, b_ref, ob_ref, st_ref, 1, False, consts, step)
        return carry

    lax.fori_loop(0, BLK // CHUNK, body, 0)


def _glascan(proj, w2full, b2full):
    t = proj.shape[0]
    nb = t // BLK
    rev = _rev_block(nb)
    ident = lambda i: i
    qk = lambda f: pl.BlockSpec((BLK, 2 * B_KWIDTH), lambda i: (f(i), COL_BQK // (2 * B_KWIDTH)))
    vv = lambda f: pl.BlockSpec((BLK, B_VWIDTH), lambda i: (f(i), COL_BV // B_VWIDTH))
    sm = lambda f: pl.BlockSpec((BLK, 128), lambda i: (f(i), COL_SMALL // 128))
    outs = lambda f: pl.BlockSpec((BLK, B_VWIDTH), lambda i: (f(i), 0))
    out_sds = jax.ShapeDtypeStruct((t, B_VWIDTH), F32)
    return pl.pallas_call(
        _glascan_kernel,
        grid=(nb,),
        in_specs=[qk(ident), vv(ident), sm(ident), qk(rev), vv(rev), sm(rev),
                  pl.BlockSpec((128, 2 * B_KWIDTH), lambda i: (0, 0)),
                  pl.BlockSpec((1, 2 * B_KWIDTH), lambda i: (0, 0))],
        out_specs=[outs(ident), outs(rev)],
        out_shape=[out_sds, out_sds],
        scratch_shapes=[pltpu.VMEM((2, B_VWIDTH, B_KWIDTH), F32), pltpu.VMEM((2, BLK, B_KWIDTH), F32)],
        compiler_params=_cparams(("arbitrary",)),
        name="gla_scan",
    )(proj, proj, proj, proj, proj, proj, w2full, b2full)


def _attnprep_kernel(cq_ref, ckv_ref, qg_ref, kg_ref, cos_ref, sin_ref, q_ref, k_ref, v_ref):
    cos = cos_ref[...]
    sin = sin_ref[...]
    lane = lax.broadcasted_iota(I32, cos.shape, 1)
    first = (lane % 32) < 16

    def norm_rope(x, g):
        y = x * lax.rsqrt(jnp.sum(x * x, axis=-1, keepdims=True) * (1.0 / HD_C) + EPS) * g
        partner = jnp.where(first, pltpu.roll(y, HD_CP - 16, 1), pltpu.roll(y, 16, 1))
        return y * cos + partner * sin

    for h in range(H_C):
        seg = slice(h * HD_CP, (h + 1) * HD_CP)
        q_ref[:, seg] = (norm_rope(cq_ref[:, seg], qg_ref[...]) * (HD_C ** -0.5)).astype(BF16)
    for g in range(KV_C):
        seg = slice(g * HD_CP, (g + 1) * HD_CP)
        k_ref[:, seg] = norm_rope(ckv_ref[:, seg], kg_ref[...]).astype(BF16)
    v_ref[...] = ckv_ref[:, KV_C * HD_CP:2 * KV_C * HD_CP].astype(BF16)


def _attnprep(proj, qg, kg, cos_t, sin_t):
    t = proj.shape[0]
    tm = BLK
    qw, kw = H_C * HD_CP, KV_C * HD_CP
    return pl.pallas_call(
        _attnprep_kernel,
        grid=(t // tm,),
        in_specs=[pl.BlockSpec((tm, qw), lambda i: (i, COL_CQ // qw)),
                  pl.BlockSpec((tm, 2 * kw), lambda i: (i, COL_CKV // (2 * kw))),
                  pl.BlockSpec((1, HD_CP), lambda i: (0, 0)),
                  pl.BlockSpec((1, HD_CP), lambda i: (0, 0)),
                  pl.BlockSpec((tm, HD_CP), lambda i: (i, 0)),
                  pl.BlockSpec((tm, HD_CP), lambda i: (i, 0))],
        out_specs=[pl.BlockSpec((tm, qw), lambda i: (i, 0)),
                   pl.BlockSpec((tm, kw), lambda i: (i, 0)),
                   pl.BlockSpec((tm, kw), lambda i: (i, 0))],
        out_shape=[jax.ShapeDtypeStruct((t, qw), BF16),
                   jax.ShapeDtypeStruct((t, kw), BF16),
                   jax.ShapeDtypeStruct((t, kw), BF16)],
        compiler_params=_cparams(("arbitrary",)),
        name="attn_prep",
    )(proj, proj, qg, kg, cos_t, sin_t)


def _attn_kernel(sink_ref, q_ref, kp_ref, kc_ref, kn_ref, kx_ref, vp_ref, vc_ref, vn_ref, vx_ref,
                 o_ref, *, nq):
    qi = pl.program_id(0)
    nctx = CTX // ATT_BLOCK
    latent = qi >= nctx
    ql = lax.broadcasted_iota(I32, (ATT_BLOCK, ATT_BLOCK), 0)
    kl = lax.broadcasted_iota(I32, (ATT_BLOCK, ATT_BLOCK), 1)
    ok_prev = jnp.logical_and(qi - 1 >= nctx, kl >= ql)
    ok_next = jnp.logical_and(jnp.logical_and(latent, qi + 1 <= nq - 1), kl <= ql)
    ok_cur = jnp.logical_and(latent, kl >= 0)
    bias = jnp.concatenate([jnp.where(ok_prev, 0.0, NEG), jnp.where(ok_cur, 0.0, NEG),
                            jnp.where(ok_next, 0.0, NEG),
                            jnp.zeros((ATT_BLOCK, CTX), F32)], axis=1)
    for g in range(KV_C):
        seg = slice(g * HD_CP, (g + 1) * HD_CP)
        kcat = jnp.concatenate([kp_ref[:, seg], kc_ref[:, seg], kn_ref[:, seg], kx_ref[:, seg]], axis=0)
        vcat = jnp.concatenate([vp_ref[:, seg], vc_ref[:, seg], vn_ref[:, seg], vx_ref[:, seg]], axis=0)
        for r in range(REP_C):
            h = g * REP_C + r
            hs = slice(h * HD_CP, (h + 1) * HD_CP)
            s = lax.dot_general(q_ref[:, hs], kcat, (((1,), (1,)), ((), ())),
                                preferred_element_type=F32) + bias
            sink = sink_ref[h]
            m = jnp.maximum(jnp.max(s, axis=-1, keepdims=True), sink)
            p = jnp.exp(s - m)
            den = jnp.sum(p, axis=-1, keepdims=True) + jnp.exp(sink - m)
            o = jnp.dot(p.astype(BF16), vcat, preferred_element_type=F32)
            o_ref[:, hs] = (o / den).astype(BF16)


def _attn(qr, kr, vr, sink):
    t = qr.shape[0]
    nq = t // ATT_BLOCK
    nctx = CTX // ATT_BLOCK
    qw, kw = H_C * HD_CP, KV_C * HD_CP
    prev = lambda i: (jnp.maximum(i - 1, nctx), 0)
    cur = lambda i: (i, 0)
    nxt = lambda i: (jnp.minimum(jnp.maximum(i + 1, nctx), nq - 1), 0)
    kv = lambda f: pl.BlockSpec((ATT_BLOCK, kw), f)
    ctxs = pl.BlockSpec((CTX, kw), lambda i: (0, 0))
    return pl.pallas_call(
        functools.partial(_attn_kernel, nq=nq),
        grid=(nq,),
        in_specs=[pl.BlockSpec(memory_space=pltpu.SMEM),
                  pl.BlockSpec((ATT_BLOCK, qw), cur),
                  kv(prev), kv(cur), kv(nxt), ctxs, kv(prev), kv(cur), kv(nxt), ctxs],
        out_specs=pl.BlockSpec((ATT_BLOCK, qw), cur),
        out_shape=jax.ShapeDtypeStruct((t, qw), BF16),
        compiler_params=_cparams(("arbitrary",)),
        name="attn",
    )(sink, qr, kr, kr, kr, kr, vr, vr, vr, vr)


def _head_rms_gate(o, z, g):
    outs = []
    for h in range(o.shape[1] // 128):
        seg = o[:, h * 128:(h + 1) * 128]
        nrm = seg * lax.rsqrt(jnp.mean(seg * seg, axis=-1, keepdims=True) + EPS) * g
        outs.append(nrm * _silu(z[:, h * 128:(h + 1) * 128]))
    return jnp.concatenate(outs, axis=1)


def _merge_kernel(oaf, oab, az, obf, obb, bz, yc, gl, xs, dng, glag, wpa, wpb, wpc, wo, mod_ref,
                  o_ref, *, tm):
    i = pl.program_id(0)
    ya = _head_rms_gate(oaf[...] + oab[...], az[...], dng[...])
    yb = _head_rms_gate(obf[...] + obb[...], bz[...], glag[...])
    pa = _bdot(ya, wpa[...])
    pb = _bdot(yb, wpb[...])
    pc = jnp.dot(yc[...], wpc[...], preferred_element_type=F32)
    merged = (_sigmoid(gl[:, 0:D_MODEL]) * pa + _sigmoid(gl[:, D_MODEL:2 * D_MODEL]) * pb
              + _sigmoid(gl[:, 2 * D_MODEL:3 * D_MODEL]) * pc)
    y = _bdot(merged, wo[...])
    o_ref[...] = xs[...] + y * _res_gate(mod_ref, 0, tm, i * tm)


def _merge(oaf, oab, obf, obb, yc, proj, xs, dng, glag, wpa, wpb, wpc, wo, mod):
    t = xs.shape[0]
    tm = BLK
    row = lambda w, c=0: pl.BlockSpec((tm, w), lambda i: (i, c))
    full = lambda a: pl.BlockSpec(a.shape, lambda i: (0,) * a.ndim)
    return pl.pallas_call(
        functools.partial(_merge_kernel, tm=tm),
        grid=(t // tm,),
        in_specs=[row(A_WIDTH), row(A_WIDTH), row(A_WIDTH, COL_AZ // A_WIDTH),
                  row(B_VWIDTH), row(B_VWIDTH), row(B_VWIDTH, COL_BZ // B_VWIDTH),
                  row(H_C * HD_CP), row(3 * D_MODEL, COL_GL // (3 * D_MODEL)), row(D_MODEL),
                  full(dng), full(glag), full(wpa), full(wpb), full(wpc), full(wo), full(mod)],
        out_specs=row(D_MODEL),
        out_shape=jax.ShapeDtypeStruct((t, D_MODEL), F32),
        compiler_params=_cparams(("arbitrary",)),
        name="merge",
    )(oaf, oab, proj, obf, obb, proj, yc, proj, xs, dng, glag, wpa, wpb, wpc, wo, mod)


def _router_kernel(x_ref, mod_ref, g_ref, rwt_ref, rb_ref, h_ref, e_ref, rank_ref, gate_ref, cnt_ref,
                   carry_ref, *, tm):
    i = pl.program_id(0)

    @pl.when(i == 0)
    def _():
        carry_ref[...] = jnp.zeros_like(carry_ref)

    h = _norm_mod(x_ref[...], g_ref[...], mod_ref, 3 * D_MODEL, i * tm)
    h_ref[...] = h
    logit = lax.dot_general(rwt_ref[...], h, (((1,), (1,)), ((), ())),
                            precision=lax.Precision.HIGHEST, preferred_element_type=F32) + rb_ref[...]
    erow = lax.broadcasted_iota(I32, (N_EXPERTS, tm), 0)
    vals, idxs, sels = [], [], []
    cur = logit
    for _ in range(TOP_K):
        m = jnp.max(cur, axis=0, keepdims=True)
        idx = jnp.min(jnp.where(cur == m, erow, N_EXPERTS), axis=0, keepdims=True)
        sel = erow == idx
        vals.append(m)
        idxs.append(idx)
        sels.append(sel)
        cur = jnp.where(sel, -jnp.inf, cur)
    ex = [jnp.exp(v - vals[0]) for v in vals]
    den = ex[0] + ex[1] + ex[2] + ex[3]
    onehot = jnp.where(sels[0] | sels[1] | sels[2] | sels[3], 1.0, 0.0).astype(F32)
    ss = lax.broadcasted_iota(I32, (tm, tm), 0)
    tt = lax.broadcasted_iota(I32, (tm, tm), 1)
    before = jnp.where(ss < tt, 1.0, 0.0).astype(BF16)
    cnt = jnp.dot(onehot.astype(BF16), before, preferred_element_type=F32) + carry_ref[:, 0:1]
    ranks = [jnp.sum(jnp.where(s, cnt, 0.0), axis=0, keepdims=True) for s in sels]
    pad_i = jnp.zeros((8 - TOP_K, tm), I32)
    e_ref[...] = jnp.concatenate(idxs + [pad_i], axis=0)
    rank_ref[...] = jnp.concatenate([r.astype(I32) for r in ranks] + [pad_i], axis=0)
    gates = jnp.concatenate([e / den for e in ex] + [jnp.zeros((128 - TOP_K, tm), F32)], axis=0)
    gate_ref[...] = gates.T
    carry_ref[...] = carry_ref[...] + jnp.sum(onehot, axis=1, keepdims=True)
    cnt_ref[...] = carry_ref[...]


def _router(xs, mod, g, rwt, rb):
    t = xs.shape[0]
    tm = BLK
    return pl.pallas_call(
        functools.partial(_router_kernel, tm=tm),
        grid=(t // tm,),
        in_specs=[pl.BlockSpec((tm, D_MODEL), lambda i: (i, 0)),
                  pl.BlockSpec((8, 6 * D_MODEL), lambda i: (0, 0)),
                  pl.BlockSpec((1, D_MODEL), lambda i: (0, 0)),
                  pl.BlockSpec((N_EXPERTS, D_MODEL), lambda i: (0, 0)),
                  pl.BlockSpec((N_EXPERTS, 1), lambda i: (0, 0))],
        out_specs=[pl.BlockSpec((tm, D_MODEL), lambda i: (i, 0)),
                   pl.BlockSpec((8, tm), lambda i: (0, i)),
                   pl.BlockSpec((8, tm), lambda i: (0, i)),
                   pl.BlockSpec((tm, 128), lambda i: (i, 0)),
                   pl.BlockSpec((N_EXPERTS, 128), lambda i: (0, 0))],
        out_shape=[jax.ShapeDtypeStruct((t, D_MODEL), F32),
                   jax.ShapeDtypeStruct((8, t), I32),
                   jax.ShapeDtypeStruct((8, t), I32),
                   jax.ShapeDtypeStruct((t, 128), F32),
                   jax.ShapeDtypeStruct((N_EXPERTS, 128), F32)],
        scratch_shapes=[pltpu.VMEM((N_EXPERTS, 128), F32)],
        compiler_params=_cparams(("arbitrary",)),
        name="router",
    )(xs, mod, g, rwt, rb)


def _dest_kernel(pstart_ref, e_ref, rank_ref, d_ref):
    e = e_ref[...]
    acc = rank_ref[...]
    for ee in range(N_EXPERTS):
        acc = acc + jnp.where(e == ee, pstart_ref[ee], 0)
    d_ref[...] = acc


def _dest(pstart, e, rank):
    t = e.shape[1]
    tl = _pick(t, (3328, 1280, 768, 256, 128))
    blk = pl.BlockSpec((8, tl), lambda i: (0, i))
    return pl.pallas_call(
        _dest_kernel,
        grid=(t // tl,),
        in_specs=[pl.BlockSpec(memory_space=pltpu.SMEM), blk, blk],
        out_specs=blk,
        out_shape=jax.ShapeDtypeStruct((8, t), I32),
        compiler_params=_cparams(("arbitrary",)),
        name="moe_dest",
    )(pstart, e, rank)


def _dispatch_kernel(zs_ref, dest_ref, h_ref, xs_ref, zbuf, sem, zsem, *, tm):
    @pl.when(pl.program_id(0) == 0)
    def _():
        zbuf[...] = jnp.zeros_like(zbuf)
        for e in range(N_ZERO_FILLS):
            @pl.when(zs_ref[e] >= 0)
            def _():
                pltpu.make_async_copy(zbuf, xs_ref.at[pl.ds(pl.multiple_of(zs_ref[e], MOE_BM), MOE_BM), :],
                                      zsem).start()
        for e in range(N_ZERO_FILLS):
            @pl.when(zs_ref[e] >= 0)
            def _():
                pltpu.make_async_copy(zbuf, xs_ref.at[pl.ds(0, MOE_BM), :], zsem).wait()

    def body(tok, carry):
        for kk in range(TOP_K):
            pltpu.make_async_copy(h_ref.at[pl.ds(tok, 1), :],
                                  xs_ref.at[pl.ds(dest_ref[kk, tok], 1), :], sem).start()
        return carry

    lax.fori_loop(0, tm, body, 0, unroll=4)
    for _ in range(TOP_K):
        pltpu.make_async_copy(h_ref, xs_ref.at[pl.ds(0, tm), :], sem).wait()


def _dispatch(zstart, dest, h, p_rows):
    t = h.shape[0]
    tm = BLK
    grid_spec = pltpu.PrefetchScalarGridSpec(
        num_scalar_prefetch=1,
        grid=(t // tm,),
        in_specs=[pl.BlockSpec((8, tm), lambda i, zs: (0, i), memory_space=pltpu.SMEM),
                  pl.BlockSpec((tm, D_MODEL), lambda i, zs: (i, 0))],
        out_specs=pl.BlockSpec(memory_space=pl.ANY),
        scratch_shapes=[pltpu.VMEM((MOE_BM, D_MODEL), F32), pltpu.SemaphoreType.DMA(()),
                        pltpu.SemaphoreType.DMA(())],
    )
    return pl.pallas_call(
        functools.partial(_dispatch_kernel, tm=tm),
        grid_spec=grid_spec,
        out_shape=jax.ShapeDtypeStruct((p_rows, D_MODEL), F32),
        compiler_params=_cparams(("arbitrary",)),
        name="moe_dispatch",
    )(zstart, dest, h)


def _expert_kernel(be_ref, nused_ref, first_ref, slot_ref, nxt_ref, x_ref, wgu_hbm, bgu_ref, wdn_hbm, bdn_ref,
                   y_ref, gu_stage, dn_stage, wgu_bf, wdn_bf, sem, *, layer):
    b = pl.program_id(0)

    def fetch(e, s):
        return (pltpu.make_async_copy(wgu_hbm.at[layer, e], gu_stage.at[s], sem.at[0, s]),
                pltpu.make_async_copy(wdn_hbm.at[layer, e], dn_stage.at[s], sem.at[1, s]))

    @pl.when(b == 0)
    def _():
        for cp in fetch(be_ref[0], 0):
            cp.start()

    @pl.when(first_ref[b] == 1)
    def _():
        s = slot_ref[b]
        for cp in fetch(be_ref[b], s):
            cp.wait()
        wgu_bf[...] = gu_stage[s].astype(BF16)
        wdn_bf[...] = dn_stage[s].astype(BF16)

        @pl.when(nxt_ref[b] >= 0)
        def _():
            for cp in fetch(nxt_ref[b], 1 - s):
                cp.start()

    @pl.when(b < nused_ref[0])
    def _():
        gu = jnp.dot(x_ref[...].astype(BF16), wgu_bf[...], preferred_element_type=F32) + bgu_ref[...]
        g_ = jnp.minimum(gu[:, :D_FF], SWIGLU_LIMIT)
        u_ = jnp.clip(gu[:, D_FF:], -SWIGLU_LIMIT, SWIGLU_LIMIT)
        act = (u_ + 1.0) * (g_ * _sigmoid(g_ * SWIGLU_ALPHA))
        y_ref[...] = jnp.dot(act.astype(BF16), wdn_bf[...], preferred_element_type=F32) + bdn_ref[...]

    @pl.when(b >= nused_ref[0])
    def _():
        y_ref[...] = jnp.zeros_like(y_ref)


def _experts(plan, xsorted, w_gu, b_gu, w_dn, b_dn, layer):
    blk_e, nused, first, slot, nxt = plan
    p_rows = xsorted.shape[0]
    nblk = p_rows // MOE_BM
    depth = w_gu.shape[0]
    bsel = lambda b, be, nu, fi, sl, nx: (layer, be[b], 0, 0)
    grid_spec = pltpu.PrefetchScalarGridSpec(
        num_scalar_prefetch=5,
        grid=(nblk,),
        in_specs=[pl.BlockSpec((MOE_BM, D_MODEL), lambda b, be, nu, fi, sl, nx: (jnp.minimum(b, nu[0] - 1), 0)),
                  pl.BlockSpec(memory_space=pl.ANY),
                  pl.BlockSpec((None, None, 1, 2 * D_FF), bsel),
                  pl.BlockSpec(memory_space=pl.ANY),
                  pl.BlockSpec((None, None, 1, D_MODEL), bsel)],
        out_specs=pl.BlockSpec((MOE_BM, D_MODEL), lambda b, be, nu, fi, sl, nx: (b, 0)),
        scratch_shapes=[pltpu.VMEM((2, D_MODEL, 2 * D_FF), F32), pltpu.VMEM((2, D_FF, D_MODEL), F32),
                        pltpu.VMEM((D_MODEL, 2 * D_FF), BF16), pltpu.VMEM((D_FF, D_MODEL), BF16),
                        pltpu.SemaphoreType.DMA((2, 2))],
    )
    return pl.pallas_call(
        functools.partial(_expert_kernel, layer=layer),
        grid_spec=grid_spec,
        out_shape=jax.ShapeDtypeStruct((p_rows, D_MODEL), F32),
        compiler_params=_cparams(("arbitrary",)),
        name="moe_experts",
    )(blk_e, nused, first, slot, nxt, xsorted, w_gu, b_gu.reshape(depth, N_EXPERTS, 1, 2 * D_FF), w_dn,
      b_dn.reshape(depth, N_EXPERTS, 1, D_MODEL))


def _combine_kernel(dest_ref, destn_ref, y_ref, gate_ref, xs_ref, mod_ref, o_ref, buf, sem, *, tm, nsteps):
    i = pl.program_id(0)
    slot = i % 2

    def issue(d_ref, s):
        def body(tok, carry):
            for kk in range(TOP_K):
                pltpu.make_async_copy(y_ref.at[pl.ds(d_ref[kk, tok], 1), :],
                                      buf.at[s, kk, pl.ds(tok, 1), :], sem.at[s]).start()
            return carry

        lax.fori_loop(0, tm, body, 0, unroll=4)

    @pl.when(i == 0)
    def _():
        issue(dest_ref, 0)

    @pl.when(i + 1 < nsteps)
    def _():
        issue(destn_ref, 1 - slot)

    for kk in range(TOP_K):
        pltpu.make_async_copy(y_ref.at[pl.ds(0, tm), :], buf.at[slot, kk], sem.at[slot]).wait()
    gate = gate_ref[...]
    acc = gate[:, 0:1] * buf[slot, 0]
    for kk in range(1, TOP_K):
        acc = acc + gate[:, kk:kk + 1] * buf[slot, kk]
    o_ref[...] = xs_ref[...] + acc * _res_gate(mod_ref, 3 * D_MODEL, tm, i * tm)


def _combine(dest, y, gate_col, xs, mod):
    t = xs.shape[0]
    tm = 128
    nsteps = t // tm
    return pl.pallas_call(
        functools.partial(_combine_kernel, tm=tm, nsteps=nsteps),
        grid=(nsteps,),
        in_specs=[pl.BlockSpec((8, tm), lambda i: (0, i), memory_space=pltpu.SMEM),
                  pl.BlockSpec((8, tm), lambda i: (0, jnp.minimum(i + 1, nsteps - 1)), memory_space=pltpu.SMEM),
                  pl.BlockSpec(memory_space=pl.ANY),
                  pl.BlockSpec((tm, 128), lambda i: (i, 0)),
                  pl.BlockSpec((tm, D_MODEL), lambda i: (i, 0)),
                  pl.BlockSpec((8, 6 * D_MODEL), lambda i: (0, 0))],
        out_specs=pl.BlockSpec((tm, D_MODEL), lambda i: (i, 0)),
        out_shape=jax.ShapeDtypeStruct((t, D_MODEL), F32),
        scratch_shapes=[pltpu.VMEM((2, TOP_K, tm, D_MODEL), F32), pltpu.SemaphoreType.DMA((2,))],
        compiler_params=_cparams(("arbitrary",)),
        name="moe_combine",
    )(dest, dest, y, gate_col, xs, mod)


def _pad_heads(w, nh, hd, hdp):
    d = w.shape[0]
    return jnp.pad(w.reshape(d, nh, hd), ((0, 0), (0, 0), (0, hdp - hd))).reshape(d, nh * hdp)


def _layout_w_in(w):
    return _layout_w_in_f32(w).astype(BF16)


def _layout_w_in_f32(w):
    pts = np.cumsum([A_WIDTH] * 4 + [2 * H_A, 2 * H_A, B_KWIDTH, B_KWIDTH, B_VWIDTH, B_VWIDTH,
                                     2 * GLA_RANK, C_WIDTH, KV_C * HD_C, KV_C * HD_C])
    (aq, ak, av, az, aa, ab, bq, bk, bv, bz, bg, cq, ck, cv, gl) = jnp.split(w, pts.tolist(), axis=1)
    small = jnp.concatenate([aa, ab, bg], axis=1)
    small = jnp.pad(small, ((0, 0), (0, 256 - small.shape[1])))
    cols = [gl, aq, ak, av, az, _pad_heads(cq, H_C, HD_C, HD_CP), _pad_heads(ck, KV_C, HD_C, HD_CP),
            _pad_heads(cv, KV_C, HD_C, HD_CP), bq, bk, bv, bz, small]
    out = jnp.concatenate(cols, axis=1)
    assert out.shape[1] == IN_COLS_P
    return out


def _rope_tables(t):
    s_len = t - CTX
    half = HD_C // 2
    inv_freq = ROPE_THETA ** (-jnp.arange(0, half, 2, dtype=F32) / half)
    pos = jnp.arange(s_len)
    rows = (pos // GRID_W).astype(F32)[:, None] * inv_freq[None, :]
    cols = (pos % GRID_W).astype(F32)[:, None] * inv_freq[None, :]
    cr, sr, cc, sc = jnp.cos(rows), jnp.sin(rows), jnp.cos(cols), jnp.sin(cols)
    zpad = jnp.zeros((s_len, HD_CP - HD_C), F32)
    cos_l = jnp.concatenate([cr, cr, cc, cc, zpad], axis=1)
    sin_l = jnp.concatenate([-sr, sr, -sc, sc, zpad], axis=1)
    cos_t = jnp.concatenate([jnp.ones((CTX, HD_CP), F32), cos_l], axis=0)
    sin_t = jnp.concatenate([jnp.zeros((CTX, HD_CP), F32), sin_l], axis=0)
    return cos_t, sin_t


def _lane_vec(v, width=128):
    v = v.reshape(1, -1).astype(F32)
    return jnp.pad(v, ((0, 0), (0, width - v.shape[1])))


def _moe_plan(counts, tk):
    padded = (counts + MOE_BM - 1) // MOE_BM * MOE_BM
    pend = jnp.cumsum(padded)
    pstart = pend - padded
    nblk = (tk + N_EXPERTS * (MOE_BM - 1) + MOE_BM - 1) // MOE_BM
    p_rows = nblk * MOE_BM
    blk = jnp.arange(nblk, dtype=I32)
    blk_e = jnp.minimum(jnp.sum((pend[None, :] <= (blk * MOE_BM)[:, None]).astype(I32), axis=1), N_EXPERTS - 1)
    nused = (pend[-1] // MOE_BM).astype(I32)
    prev_e = jnp.concatenate([jnp.full((1,), -1, I32), blk_e[:-1]])
    first = ((blk_e != prev_e) & (blk < nused)).astype(I32)
    slot = (jnp.cumsum(first) - 1) % 2
    pos = jnp.where(first == 1, blk, nblk)
    nxt_pos = jnp.concatenate([lax.cummin(pos, axis=0, reverse=True)[1:], jnp.full((1,), nblk, I32)])
    nxt = jnp.where(nxt_pos < nblk, blk_e[jnp.minimum(nxt_pos, nblk - 1)], -1)
    last_blk = jnp.where(counts > 0, pend - MOE_BM, -1)
    tail = pend[-1] + jnp.arange(N_ZERO_FILLS - N_EXPERTS, dtype=I32) * MOE_BM
    zstart = jnp.concatenate([last_blk, jnp.where(tail < p_rows, tail, -1)]).astype(I32)
    plan = (blk_e, nused.reshape(1), first, slot.astype(I32), nxt.astype(I32))
    return pstart.astype(I32), zstart, plan, p_rows


def kernel(x, c, ctx, c_ctx, ada_w, ada_b, norm_mix_g, norm_ffn_g, w_in, dn_conv_w, dn_a_log, dn_dt_bias,
           dn_norm_g, gla_w2, gla_b2, gla_norm_g, attn_q_norm_g, attn_k_norm_g, attn_sink, w_branch_a,
           w_branch_b, w_branch_c, w_out, router_w, router_b, w_gate_up, b_gate_up, w_down, b_down):
    assert x.shape[0] == 1 and c.shape[0] == 1 and ctx.shape[1] == CTX
    depth = ada_w.shape[0]
    xs = jnp.concatenate([ctx[0], x[0]], axis=0)
    t = xs.shape[0]
    assert t % BLK == 0 and (t - CTX) % GRID_W == 0
    cc = jnp.concatenate([c, c_ctx[None, :], jnp.zeros((6, D_MODEL), F32)], axis=0)
    mods = _ada_mod(cc, ada_w, ada_b)
    cos_t, sin_t = _rope_tables(t)
    for l in range(depth):
        mod = mods[l]
        proj = _inproj(xs, mod, norm_mix_g[l][None, :], _layout_w_in(w_in[l]))
        qa, ka, va, gcol = _dnprep(proj, dn_conv_w[l], _lane_vec(dn_a_log[l]), _lane_vec(dn_dt_bias[l]))
        oaf, oab = _dnscan(qa, ka, va, gcol)
        w2 = gla_w2[l].astype(F32)
        w2full = jnp.zeros((128, 2 * B_KWIDTH), F32)
        for d in range(2):
            r0 = 4 * H_A + d * GLA_RANK
            w2full = w2full.at[r0:r0 + GLA_RANK, d * B_KWIDTH:(d + 1) * B_KWIDTH].set(w2[d])
        obf, obb = _glascan(proj, w2full, gla_b2[l].reshape(1, 2 * B_KWIDTH).astype(F32))
        qg = _lane_vec(attn_q_norm_g[l])
        kg = _lane_vec(attn_k_norm_g[l])
        qr, kr, vr = _attnprep(proj, qg, kg, cos_t, sin_t)
        yc = _attn(qr, kr, vr, attn_sink[l].astype(F32))
        wpc = jnp.pad(w_branch_c[l].reshape(H_C, HD_C, D_MODEL),
                      ((0, 0), (0, HD_CP - HD_C), (0, 0))).reshape(H_C * HD_CP, D_MODEL)
        xs = _merge(oaf, oab, obf, obb, yc, proj, xs,
                    dn_norm_g[l][None, :], gla_norm_g[l][None, :],
                    w_branch_a[l].astype(BF16), w_branch_b[l].astype(BF16), wpc.astype(BF16),
                    w_out[l].astype(BF16), mod)
        h2, top_e, rank, gate_col, cnt = _router(xs, mod, norm_ffn_g[l][None, :], router_w[l].T,
                                                 router_b[l][:, None])
        pstart, zstart, plan, p_rows = _moe_plan(cnt[:, 0].astype(I32), t * TOP_K)
        dest = _dest(pstart, top_e, rank)
        xsorted = _dispatch(zstart, dest, h2, p_rows)
        y = _experts(plan, xsorted, w_gate_up, b_gate_up, w_down, b_down, l)
        xs = _combine(dest, y, gate_col, xs, mod)
    return xs[CTX:][None]
```

```python
import functools
import math

import jax
import jax.numpy as jnp
import numpy as np
from jax import lax
from jax.experimental import pallas as pl
from jax.experimental.pallas import tpu as pltpu

F32 = jnp.float32
BF16 = jnp.bfloat16
I32 = jnp.int32

D_MODEL = 1024
DEPTH = 4
GRID_W = 64
CTX = 256
H_A = 4
HD_A = 128
A_WIDTH = H_A * HD_A
CONV_W = 5
H_B = 4
DK_B = 64
DV_B = 128
B_KWIDTH = H_B * DK_B
B_VWIDTH = H_B * DV_B
GLA_RANK = 16
GLA_NORMALIZER = 16.0
H_C = 8
KV_C = 2
REP_C = H_C // KV_C
HD_C = 64
HD_CP = 128
C_WIDTH = H_C * HD_C
ATT_BLOCK = 128
ROPE_THETA = 10000.0
CHUNK = 64
SUB = 16
N_EXPERTS = 32
TOP_K = 4
D_FF = 1024
SWIGLU_LIMIT = 7.0
SWIGLU_ALPHA = 1.702
EPS = 1e-6
NEG = -1e30

BLK = 256
MOE_BM = 256
N_ZERO_FILLS = 2 * N_EXPERTS + 1
VMEM_LIMIT = 56 * 1024 * 1024

COL_GL = 0
COL_QKV = 3072
COL_AZ = 4608
COL_CQ = 5120
COL_CKV = 6144
COL_BQK = 6656
COL_BV = 7168
COL_BZ = 7680
COL_SMALL = 8192
IN_COLS_P = 8448
IN_TN = 768


def _pick(n, cands):
    for c in cands:
        if n % c == 0:
            return c
    raise ValueError(f"no tile for {n}")


def _cparams(sem):
    return pltpu.CompilerParams(dimension_semantics=sem, vmem_limit_bytes=VMEM_LIMIT)


def _bdot(a, b):
    return jnp.dot(a.astype(BF16), b.astype(BF16), preferred_element_type=F32)


def _bdot_nt(a, b):
    return lax.dot_general(a.astype(BF16), b.astype(BF16), (((1,), (1,)), ((), ())),
                           preferred_element_type=F32)


def _bdot_tn(a, b):
    return lax.dot_general(a.astype(BF16), b.astype(BF16), (((0,), (0,)), ((), ())),
                           preferred_element_type=F32)


def _fdot(a, b):
    return jnp.dot(a, b, precision=lax.Precision.HIGHEST, preferred_element_type=F32)


def _split2(a):
    hi = a.astype(BF16)
    lo = (a - hi.astype(F32)).astype(BF16)
    return hi, lo


def _dot3(a, b):
    ah, al = _split2(a)
    bh, bl = _split2(b)
    d = functools.partial(jnp.dot, preferred_element_type=F32)
    return d(ah, bh) + (d(ah, bl) + d(al, bh))


def _cumsum_dot(tri, x):
    t = tri.astype(BF16)
    hi = x.astype(BF16)
    r1 = x - hi.astype(F32)
    mid = r1.astype(BF16)
    lo = (r1 - mid.astype(F32)).astype(BF16)
    d = functools.partial(jnp.dot, preferred_element_type=F32)
    return d(t, hi) + (d(t, mid) + d(t, lo))


def _sigmoid(x):
    return 1.0 / (1.0 + jnp.exp(-x))


def _silu(x):
    return x * _sigmoid(x)


def _softplus(x):
    return jnp.maximum(x, 0.0) + jnp.log(1.0 + jnp.exp(-jnp.abs(x)))


def _ada_kernel(cc_ref, w_ref, b_ref, o_ref):
    o_ref[...] = _fdot(_silu(cc_ref[...]), w_ref[...]) + b_ref[...]


def _ada_mod(cc, ada_w, ada_b):
    depth = ada_w.shape[0]
    tn = 1536
    return pl.pallas_call(
        _ada_kernel,
        grid=(depth, 6 * D_MODEL // tn),
        in_specs=[pl.BlockSpec((8, D_MODEL), lambda l, j: (0, 0)),
                  pl.BlockSpec((None, D_MODEL, tn), lambda l, j: (l, 0, j)),
                  pl.BlockSpec((None, 1, tn), lambda l, j: (l, 0, j))],
        out_specs=pl.BlockSpec((None, 8, tn), lambda l, j: (l, 0, j)),
        out_shape=jax.ShapeDtypeStruct((depth, 8, 6 * D_MODEL), F32),
        compiler_params=_cparams(("arbitrary", "arbitrary")),
        name="ada_mod",
    )(cc, ada_w, ada_b.reshape(depth, 1, 6 * D_MODEL))


def _norm_mod(x, g, mod_ref, moff, row0):
    tm = x.shape[0]
    y = x * lax.rsqrt(jnp.mean(x * x, axis=-1, keepdims=True) + EPS) * g
    isc = (row0 + lax.broadcasted_iota(I32, (tm, 1), 0)) < CTX
    shift = jnp.where(isc, mod_ref[1:2, moff:moff + D_MODEL], mod_ref[0:1, moff:moff + D_MODEL])
    scale = jnp.where(isc, mod_ref[1:2, moff + D_MODEL:moff + 2 * D_MODEL],
                      mod_ref[0:1, moff + D_MODEL:moff + 2 * D_MODEL])
    return y * (1.0 + scale) + shift


def _res_gate(mod_ref, moff, tm, row0):
    isc = (row0 + lax.broadcasted_iota(I32, (tm, 1), 0)) < CTX
    return jnp.where(isc, mod_ref[1:2, moff + 2 * D_MODEL:moff + 3 * D_MODEL],
                     mod_ref[0:1, moff + 2 * D_MODEL:moff + 3 * D_MODEL])


def _inproj_kernel(x_ref, mod_ref, g_ref, w_ref, o_ref, h_ref, *, tm):
    i = pl.program_id(0)

    @pl.when(pl.program_id(1) == 0)
    def _():
        h_ref[...] = _norm_mod(x_ref[...], g_ref[...], mod_ref, 0, i * tm).astype(BF16)

    o_ref[...] = jnp.dot(h_ref[...], w_ref[...], preferred_element_type=F32)


def _inproj(xs, mod, g, w):
    t = xs.shape[0]
    tm = _pick(t, (1280, 640, 256))
    return pl.pallas_call(
        functools.partial(_inproj_kernel, tm=tm),
        grid=(t // tm, IN_COLS_P // IN_TN),
        in_specs=[pl.BlockSpec((tm, D_MODEL), lambda i, j: (i, 0)),
                  pl.BlockSpec((8, 6 * D_MODEL), lambda i, j: (0, 0)),
                  pl.BlockSpec((1, D_MODEL), lambda i, j: (0, 0)),
                  pl.BlockSpec((D_MODEL, IN_TN), lambda i, j: (0, j))],
        out_specs=pl.BlockSpec((tm, IN_TN), lambda i, j: (i, j)),
        out_shape=jax.ShapeDtypeStruct((t, IN_COLS_P), F32),
        scratch_shapes=[pltpu.VMEM((tm, D_MODEL), BF16)],
        compiler_params=_cparams(("arbitrary", "arbitrary")),
        name="inproj",
    )(xs, mod, g, w)


def _tri_blockdiag(n, lower):
    ii = lax.broadcasted_iota(I32, (n, n), 0)
    jj = lax.broadcasted_iota(I32, (n, n), 1)
    same = (ii >> 6) == (jj >> 6)
    tri = (ii >= jj) if lower else (ii <= jj)
    return jnp.where(same, jnp.where(tri, 1.0, 0.0), 0.0).astype(F32)


def _dnprep_kernel(main_ref, prev_ref, next_ref, small_ref, cw_ref, alog_ref, dtb_ref,
                   q_ref, k_ref, v_ref, gcol_ref, ext_ref, *, nb):
    i = pl.program_id(0)
    use_prev = i >= 2
    use_next = jnp.logical_and(i >= 1, i <= nb - 2)
    ext_ref[0:8, :] = jnp.where(use_prev, prev_ref[...], 0.0)
    ext_ref[8:8 + BLK, :] = main_ref[...]
    ext_ref[8 + BLK:16 + BLK, :] = jnp.where(use_next, next_ref[...], 0.0)
    acc = ext_ref[6:6 + BLK, :] * cw_ref[0:1, :]
    for d in range(1, CONV_W):
        acc = acc + ext_ref[6 + d:6 + d + BLK, :] * cw_ref[d:d + 1, :]
    s = _silu(acc)
    for h in range(H_A):
        for part, ref, mul in ((0, q_ref, HD_A ** -0.5), (1, k_ref, 1.0)):
            seg = s[:, part * A_WIDTH + h * HD_A: part * A_WIDTH + (h + 1) * HD_A]
            nrm = seg * lax.rsqrt(jnp.sum(seg * seg, axis=-1, keepdims=True) + EPS)
            ref[:, h * HD_A:(h + 1) * HD_A] = nrm * mul
    v_ref[...] = s[:, 2 * A_WIDTH:3 * A_WIDTH]
    sm = small_ref[...]
    lane = lax.broadcasted_iota(I32, sm.shape, 1)
    g = -jnp.exp(alog_ref[...]) * _softplus(sm + dtb_ref[...])
    gb = jnp.where(lane < 2 * H_A, g, jnp.where(lane < 4 * H_A, _sigmoid(sm), 0.0))
    cf = _fdot(_tri_blockdiag(BLK, True), gb)
    cr = _fdot(_tri_blockdiag(BLK, False), gb)
    gc = jnp.where(lane < H_A, cf, jnp.where(lane < 2 * H_A, cr, gb))
    gcol_ref[...] = gc


def _dnprep(proj, conv_w, alog_vec, dtb_vec):
    t = proj.shape[0]
    nb = t // BLK
    qkv_blk = COL_QKV // (3 * A_WIDTH)
    last8 = t // 8 - 1
    out_sds = jax.ShapeDtypeStruct((t, A_WIDTH), F32)
    return pl.pallas_call(
        functools.partial(_dnprep_kernel, nb=nb),
        grid=(nb,),
        in_specs=[pl.BlockSpec((BLK, 3 * A_WIDTH), lambda i: (i, qkv_blk)),
                  pl.BlockSpec((8, 3 * A_WIDTH), lambda i: (jnp.maximum(i * (BLK // 8) - 1, 0), qkv_blk)),
                  pl.BlockSpec((8, 3 * A_WIDTH), lambda i: (jnp.minimum((i + 1) * (BLK // 8), last8), qkv_blk)),
                  pl.BlockSpec((BLK, 128), lambda i: (i, COL_SMALL // 128)),
                  pl.BlockSpec((CONV_W, 3 * A_WIDTH), lambda i: (0, 0)),
                  pl.BlockSpec((1, 128), lambda i: (0, 0)),
                  pl.BlockSpec((1, 128), lambda i: (0, 0))],
        out_specs=[pl.BlockSpec((BLK, A_WIDTH), lambda i: (i, 0)),
                   pl.BlockSpec((BLK, A_WIDTH), lambda i: (i, 0)),
                   pl.BlockSpec((BLK, A_WIDTH), lambda i: (i, 0)),
                   pl.BlockSpec((BLK, 128), lambda i: (i, 0))],
        out_shape=[out_sds, out_sds, out_sds,
                   jax.ShapeDtypeStruct((t, 128), F32)],
        scratch_shapes=[pltpu.VMEM((BLK + 16, 3 * A_WIDTH), F32)],
        compiler_params=_cparams(("arbitrary",)),
        name="dn_prep",
    )(proj, proj, proj, proj, conv_w, alog_vec, dtb_vec)


def _dot3_all(a_list, b_list):
    d = functools.partial(jnp.dot, preferred_element_type=F32)
    sa = [_split2(a) for a in a_list]
    sb = [_split2(b) for b in b_list]
    hh = [d(a[0], b[0]) for a, b in zip(sa, sb)]
    hl = [d(a[0], b[1]) for a, b in zip(sa, sb)]
    lh = [d(a[1], b[0]) for a, b in zip(sa, sb)]
    return [x + (y + z) for x, y, z in zip(hh, hl, lh)]


def _unit_tri_inverse_all(l_mats, masks):
    eye, m_diag, m_l1, m_l2 = masks
    ld = [l * m_diag for l in l_mats]
    x = [eye - a for a in ld]
    p = _dot3_all(ld, ld)
    for it in range(3):
        xp = _dot3_all(x, p)
        if it < 2:
            p = _dot3_all(p, p)
        x = [a + b for a, b in zip(x, xp)]
    for m in (m_l1, m_l2):
        cx = _dot3_all([l * m for l in l_mats], x)
        xcx = _dot3_all(x, cx)
        x = [a - b for a, b in zip(x, xcx)]
    return x


def _dn_masks():
    ii = lax.broadcasted_iota(I32, (CHUNK, CHUNK), 0)
    jj = lax.broadcasted_iota(I32, (CHUNK, CHUNK), 1)
    one = lambda c: jnp.where(c, 1.0, 0.0).astype(F32)
    eye = one(ii == jj)
    m_diag = one((ii >> 4) == (jj >> 4))
    m_l2 = one((ii >> 5) != (jj >> 5))
    m_l1 = 1.0 - m_diag - m_l2
    return ii, jj, (eye, m_diag, m_l1, m_l2)


def _dn_local(items, ii, jj, masks):
    n = len(items)
    dec, lmat, qk, kb, eg = [], [], [], [], []
    for q, k, v, gcol, grow, bcol, fwd in items:
        incl = (ii >= jj) if fwd else (ii <= jj)
        dec.append(jnp.exp(jnp.where(incl, gcol - grow, NEG)))
        kb.append(k * bcol)
        eg.append(jnp.exp(gcol))
    kh = [it[1].astype(BF16) for it in items]
    kk = [_bdot_nt(kb[i], kh[i]) for i in range(n)]
    qkr = [_bdot_nt(items[i][0], kh[i]) for i in range(n)]
    for i in range(n):
        fwd = items[i][6]
        strict = (ii > jj) if fwd else (ii < jj)
        lmat.append(kk[i] * jnp.where(strict, dec[i], 0.0))
        qk.append((qkr[i] * dec[i]).astype(BF16))
    rhs = [jnp.concatenate([items[i][2] * items[i][5], kb[i] * eg[i]], axis=1) for i in range(n)]
    sol = _dot3_all(_unit_tri_inverse_all(lmat, masks), rhs)
    out = []
    for i in range(n):
        q, k, _, gcol, _, _, fwd = items[i]
        glast = gcol[CHUNK - 1:CHUNK, :] if fwd else gcol[0:1, :]
        out.append((sol[i][:, :HD_A], sol[i][:, HD_A:].astype(BF16), qk[i], (q * eg[i]).astype(BF16),
                    (k * jnp.exp(glast - gcol)).astype(BF16), jnp.exp(glast)))
    return out


def _dn_step(local, states):
    n = len(local)
    sb = [s.astype(BF16) for s in states]
    d = functools.partial(jnp.dot, preferred_element_type=F32)
    ws = [d(local[i][1], sb[i]) for i in range(n)]
    qs = [d(local[i][3], sb[i]) for i in range(n)]
    v_new = [(local[i][0] - ws[i]).astype(BF16) for i in range(n)]
    o2 = [d(local[i][2], v_new[i]) for i in range(n)]
    kv = [lax.dot_general(local[i][4], v_new[i], (((0,), (0,)), ((), ())), preferred_element_type=F32)
          for i in range(n)]
    return [qs[i] + o2[i] for i in range(n)], [states[i] * local[i][5] + kv[i] for i in range(n)]


def _dnscan_kernel(qf, kf, vf, gcf, qb, kb, vb, gcb, of_ref, ob_ref, s_ref):
    @pl.when(pl.program_id(0) == 0)
    def _():
        s_ref[...] = jnp.zeros_like(s_ref)

    ii, jj, masks = _dn_masks()
    nch = BLK // CHUNK
    pick = jnp.where(lax.broadcasted_iota(I32, (16, 128), 0) == lax.broadcasted_iota(I32, (16, 128), 1),
                     1.0, 0.0).astype(F32)

    dirs = ((True, (qf, kf, vf, gcf, of_ref)), (False, (qb, kb, vb, gcb, ob_ref)))
    items, sinks = [], []
    for step in range(nch):
        for fwd, (q_r, k_r, v_r, gc_r, o_r) in dirs:
            c = step if fwd else nch - 1 - step
            rows = slice(c * CHUNK, (c + 1) * CHUNK)
            d = 0 if fwd else 1
            gct = gc_r[rows, :]
            grows = lax.dot_general(pick, gct, (((1,), (1,)), ((), ())),
                                    precision=lax.Precision.HIGHEST, preferred_element_type=F32)
            for h in range(H_A):
                lanes = slice(h * HD_A, (h + 1) * HD_A)
                gi = d * H_A + h
                items.append((q_r[rows, lanes], k_r[rows, lanes], v_r[rows, lanes],
                              gct[:, gi:gi + 1], grows[gi:gi + 1, :],
                              gct[:, 2 * H_A + gi:2 * H_A + gi + 1], fwd))
                sinks.append((o_r, rows, lanes))
    local = _dn_local(items, ii, jj, masks)
    nchain = 2 * H_A
    states = [s_ref[gi] for gi in range(nchain)]
    for step in range(nch):
        outs, states = _dn_step(local[step * nchain:(step + 1) * nchain], states)
        for (o_r, rows, lanes), o in zip(sinks[step * nchain:(step + 1) * nchain], outs):
            o_r[rows, lanes] = o
    for gi in range(nchain):
        s_ref[gi] = states[gi]


def _rev_block(nb):
    return lambda i: jnp.where(i == 0, 0, nb - i)


def _dnscan(q, k, v, gcol):
    t = q.shape[0]
    nb = t // BLK
    rev = _rev_block(nb)
    wide = lambda f: pl.BlockSpec((BLK, A_WIDTH), lambda i: (f(i), 0))
    col = lambda f: pl.BlockSpec((BLK, 128), lambda i: (f(i), 0))
    ident = lambda i: i
    out_sds = jax.ShapeDtypeStruct((t, A_WIDTH), F32)
    return pl.pallas_call(
        _dnscan_kernel,
        grid=(nb,),
        in_specs=[wide(ident), wide(ident), wide(ident), col(ident),
                  wide(rev), wide(rev), wide(rev), col(rev)],
        out_specs=[wide(ident), wide(rev)],
        out_shape=[out_sds, out_sds],
        scratch_shapes=[pltpu.VMEM((2 * H_A, HD_A, HD_A), F32)],
        compiler_params=_cparams(("arbitrary",)),
        name="dn_scan",
    )(q, k, v, gcol, q, k, v, gcol)


def _gla_gates(small_ref, w2_ref, b2_ref, b_ref, d, fwd):
    cols = slice(d * B_KWIDTH, (d + 1) * B_KWIDTH)
    pre = _dot3(small_ref[...], w2_ref[:, cols]) + b2_ref[:, cols]
    gk = -_softplus(-pre) * (1.0 / GLA_NORMALIZER)
    b_ref[d] = _cumsum_dot(_tri_blockdiag(BLK, fwd), gk)


def _gla_chunk(qk_ref, v_ref, b_ref, o_ref, st_ref, d, fwd, consts, step):
    sel, headmask_k, st_mask = consts
    nch = BLK // CHUNK
    nsub = CHUNK // SUB
    sub_i = lax.broadcasted_iota(I32, (SUB, 1), 0)
    row_c = lax.broadcasted_iota(I32, (CHUNK, 1), 0)
    if True:
        c = step if fwd else nch - 1 - step
        rows = pl.ds(pl.multiple_of(c * CHUNK, CHUNK), CHUNK)
        q = qk_ref[rows, 0:B_KWIDTH] * (DK_B ** -0.5)
        k = qk_ref[rows, B_KWIDTH:2 * B_KWIDTH]
        v = v_ref[rows, :]
        b = b_ref[d, rows, :]
        vh = v.astype(BF16)
        st = st_ref[d]
        o = _bdot_nt(q * jnp.exp(b), st)
        refs = []
        for sb in range(nsub):
            if fwd:
                r = b[sb * SUB - 1:sb * SUB, :] if sb > 0 else jnp.zeros((1, B_KWIDTH), F32)
            else:
                r = b[(sb + 1) * SUB:(sb + 1) * SUB + 1, :] if sb < nsub - 1 else jnp.zeros((1, B_KWIDTH), F32)
            refs.append(r)
        rfull = jnp.concatenate([jnp.broadcast_to(r, (SUB, B_KWIDTH)) for r in refs], axis=0)
        qs = q * jnp.exp(b - rfull)
        a_off = [None] * H_B
        for sb in (range(1, nsub) if fwd else range(0, nsub - 1)):
            jmask = (row_c < sb * SUB) if fwd else (row_c >= (sb + 1) * SUB)
            ks = (k * jnp.exp(jnp.where(jmask, refs[sb] - b, NEG))).astype(BF16)
            rowmask = jnp.where((row_c >> 4) == sb, 1.0, 0.0)
            for h in range(H_B):
                a = _bdot_nt(qs * headmask_k[h], ks) * rowmask
                a_off[h] = a if a_off[h] is None else a_off[h] + a
        o = o + jnp.concatenate(
            [_bdot(a_off[h], vh[:, h * DV_B:(h + 1) * DV_B]) for h in range(H_B)], axis=1)
        diag = []
        for sb in range(nsub):
            s0 = sb * SUB
            bs, qsb, ksb = b[s0:s0 + SUB, :], q[s0:s0 + SUB, :], k[s0:s0 + SUB, :]
            tiles = []
            for jl in range(SUB):
                causal = (sub_i >= jl) if fwd else (sub_i <= jl)
                e = jnp.exp(jnp.where(causal, bs - bs[jl:jl + 1, :], NEG))
                tiles.append((qsb * ksb[jl:jl + 1, :] * e).astype(BF16))
            red = jnp.dot(jnp.concatenate(tiles, axis=0), sel, preferred_element_type=F32)
            acc = red[0:SUB, :] * v[s0:s0 + 1, :]
            for jl in range(1, SUB):
                acc = acc + red[jl * SUB:(jl + 1) * SUB, :] * v[s0 + jl:s0 + jl + 1, :]
            diag.append(acc)
        o_ref[rows, :] = o + jnp.concatenate(diag, axis=0)
        blast = b[CHUNK - 1:CHUNK, :] if fwd else b[0:1, :]
        kd = k * jnp.exp(blast - b)
        st_ref[d] = st * jnp.exp(blast) + _bdot_tn(v, kd) * st_mask


def _glascan_kernel(qkf, vf, smf, qkb, vb, smb, w2_ref, b2_ref, of_ref, ob_ref, st_ref, b_ref):
    @pl.when(pl.program_id(0) == 0)
    def _():
        st_ref[...] = jnp.zeros_like(st_ref)

    kk = lax.broadcasted_iota(I32, (B_KWIDTH, B_VWIDTH), 0)
    cc = lax.broadcasted_iota(I32, (B_KWIDTH, B_VWIDTH), 1)
    sel = jnp.where((kk >> 6) == (cc >> 7), 1.0, 0.0).astype(BF16)
    lane = lax.broadcasted_iota(I32, (1, B_KWIDTH), 1)
    headmask_k = [jnp.where((lane >> 6) == h, 1.0, 0.0).astype(F32) for h in range(H_B)]
    rr = lax.broadcasted_iota(I32, (B_VWIDTH, B_KWIDTH), 0)
    kc = lax.broadcasted_iota(I32, (B_VWIDTH, B_KWIDTH), 1)
    st_mask = jnp.where((rr >> 7) == (kc >> 6), 1.0, 0.0).astype(F32)
    consts = (sel, headmask_k, st_mask)
    _gla_gates(smf, w2_ref, b2_ref, b_ref, 0, True)
    _gla_gates(smb, w2_ref, b2_ref, b_ref, 1, False)

    def body(step, carry):
        _gla_chunk(qkf, vf, b_ref, of_ref, st_ref, 0, True, consts, step)
        _gla_chunk(qkb, vb, b_ref, ob_ref, st_ref, 1, False, consts, step)
        return carry

    lax.fori_loop(0, BLK // CHUNK, body, 0)


def _glascan(proj, w2full, b2full):
    t = proj.shape[0]
    nb = t // BLK
    rev = _rev_block(nb)
    ident = lambda i: i
    qk = lambda f: pl.BlockSpec((BLK, 2 * B_KWIDTH), lambda i: (f(i), COL_BQK // (2 * B_KWIDTH)))
    vv = lambda f: pl.BlockSpec((BLK, B_VWIDTH), lambda i: (f(i), COL_BV // B_VWIDTH))
    sm = lambda f: pl.BlockSpec((BLK, 128), lambda i: (f(i), COL_SMALL // 128))
    outs = lambda f: pl.BlockSpec((BLK, B_VWIDTH), lambda i: (f(i), 0))
    out_sds = jax.ShapeDtypeStruct((t, B_VWIDTH), F32)
    return pl.pallas_call(
        _glascan_kernel,
        grid=(nb,),
        in_specs=[qk(ident), vv(ident), sm(ident), qk(rev), vv(rev), sm(rev),
                  pl.BlockSpec((128, 2 * B_KWIDTH), lambda i: (0, 0)),
                  pl.BlockSpec((1, 2 * B_KWIDTH), lambda i: (0, 0))],
        out_specs=[outs(ident), outs(rev)],
        out_shape=[out_sds, out_sds],
        scratch_shapes=[pltpu.VMEM((2, B_VWIDTH, B_KWIDTH), F32), pltpu.VMEM((2, BLK, B_KWIDTH), F32)],
        compiler_params=_cparams(("arbitrary",)),
        name="gla_scan",
    )(proj, proj, proj, proj, proj, proj, w2full, b2full)


def _attnprep_kernel(cq_ref, ckv_ref, qg_ref, kg_ref, cos_ref, sin_ref, q_ref, k_ref, v_ref):
    cos = cos_ref[...]
    sin = sin_ref[...]
    lane = lax.broadcasted_iota(I32, cos.shape, 1)
    first = (lane % 32) < 16

    def norm_rope(x, g):
        y = x * lax.rsqrt(jnp.sum(x * x, axis=-1, keepdims=True) * (1.0 / HD_C) + EPS) * g
        partner = jnp.where(first, pltpu.roll(y, HD_CP - 16, 1), pltpu.roll(y, 16, 1))
        return y * cos + partner * sin

    for h in range(H_C):
        seg = slice(h * HD_CP, (h + 1) * HD_CP)
        q_ref[:, seg] = (norm_rope(cq_ref[:, seg], qg_ref[...]) * (HD_C ** -0.5)).astype(BF16)
    for g in range(KV_C):
        seg = slice(g * HD_CP, (g + 1) * HD_CP)
        k_ref[:, seg] = norm_rope(ckv_ref[:, seg], kg_ref[...]).astype(BF16)
    v_ref[...] = ckv_ref[:, KV_C * HD_CP:2 * KV_C * HD_CP].astype(BF16)


def _attnprep(proj, qg, kg, cos_t, sin_t):
    t = proj.shape[0]
    tm = BLK
    qw, kw = H_C * HD_CP, KV_C * HD_CP
    return pl.pallas_call(
        _attnprep_kernel,
        grid=(t // tm,),
        in_specs=[pl.BlockSpec((tm, qw), lambda i: (i, COL_CQ // qw)),
                  pl.BlockSpec((tm, 2 * kw), lambda i: (i, COL_CKV // (2 * kw))),
                  pl.BlockSpec((1, HD_CP), lambda i: (0, 0)),
                  pl.BlockSpec((1, HD_CP), lambda i: (0, 0)),
                  pl.BlockSpec((tm, HD_CP), lambda i: (i, 0)),
                  pl.BlockSpec((tm, HD_CP), lambda i: (i, 0))],
        out_specs=[pl.BlockSpec((tm, qw), lambda i: (i, 0)),
                   pl.BlockSpec((tm, kw), lambda i: (i, 0)),
                   pl.BlockSpec((tm, kw), lambda i: (i, 0))],
        out_shape=[jax.ShapeDtypeStruct((t, qw), BF16),
                   jax.ShapeDtypeStruct((t, kw), BF16),
                   jax.ShapeDtypeStruct((t, kw), BF16)],
        compiler_params=_cparams(("arbitrary",)),
        name="attn_prep",
    )(proj, proj, qg, kg, cos_t, sin_t)


def _attn_kernel(sink_ref, q_ref, kp_ref, kc_ref, kn_ref, kx_ref, vp_ref, vc_ref, vn_ref, vx_ref,
                 o_ref, *, nq):
    qi = pl.program_id(0)
    nctx = CTX // ATT_BLOCK
    latent = qi >= nctx
    ql = lax.broadcasted_iota(I32, (ATT_BLOCK, ATT_BLOCK), 0)
    kl = lax.broadcasted_iota(I32, (ATT_BLOCK, ATT_BLOCK), 1)
    ok_prev = jnp.logical_and(qi - 1 >= nctx, kl >= ql)
    ok_next = jnp.logical_and(jnp.logical_and(latent, qi + 1 <= nq - 1), kl <= ql)
    ok_cur = jnp.logical_and(latent, kl >= 0)
    bias = jnp.concatenate([jnp.where(ok_prev, 0.0, NEG), jnp.where(ok_cur, 0.0, NEG),
                            jnp.where(ok_next, 0.0, NEG),
                            jnp.zeros((ATT_BLOCK, CTX), F32)], axis=1)
    for g in range(KV_C):
        seg = slice(g * HD_CP, (g + 1) * HD_CP)
        kcat = jnp.concatenate([kp_ref[:, seg], kc_ref[:, seg], kn_ref[:, seg], kx_ref[:, seg]], axis=0)
        vcat = jnp.concatenate([vp_ref[:, seg], vc_ref[:, seg], vn_ref[:, seg], vx_ref[:, seg]], axis=0)
        heads = [g * REP_C + r for r in range(REP_C)]
        cols = [slice(h * HD_CP, (h + 1) * HD_CP) for h in heads]
        s = [lax.dot_general(q_ref[:, c], kcat, (((1,), (1,)), ((), ())), preferred_element_type=F32) + bias
             for c in cols]
        m = [jnp.maximum(jnp.max(s[i], axis=-1, keepdims=True), sink_ref[heads[i]]) for i in range(REP_C)]
        p = [jnp.exp(s[i] - m[i]) for i in range(REP_C)]
        den = [jnp.sum(p[i], axis=-1, keepdims=True) + jnp.exp(sink_ref[heads[i]] - m[i]) for i in range(REP_C)]
        o = [jnp.dot(p[i].astype(BF16), vcat, preferred_element_type=F32) for i in range(REP_C)]
        for i in range(REP_C):
            o_ref[:, cols[i]] = (o[i] / den[i]).astype(BF16)


def _attn(qr, kr, vr, sink):
    t = qr.shape[0]
    nq = t // ATT_BLOCK
    nctx = CTX // ATT_BLOCK
    qw, kw = H_C * HD_CP, KV_C * HD_CP
    prev = lambda i: (jnp.maximum(i - 1, nctx), 0)
    cur = lambda i: (i, 0)
    nxt = lambda i: (jnp.minimum(jnp.maximum(i + 1, nctx), nq - 1), 0)
    kv = lambda f: pl.BlockSpec((ATT_BLOCK, kw), f)
    ctxs = pl.BlockSpec((CTX, kw), lambda i: (0, 0))
    return pl.pallas_call(
        functools.partial(_attn_kernel, nq=nq),
        grid=(nq,),
        in_specs=[pl.BlockSpec(memory_space=pltpu.SMEM),
                  pl.BlockSpec((ATT_BLOCK, qw), cur),
                  kv(prev), kv(cur), kv(nxt), ctxs, kv(prev), kv(cur), kv(nxt), ctxs],
        out_specs=pl.BlockSpec((ATT_BLOCK, qw), cur),
        out_shape=jax.ShapeDtypeStruct((t, qw), BF16),
        compiler_params=_cparams(("arbitrary",)),
        name="attn",
    )(sink, qr, kr, kr, kr, kr, vr, vr, vr, vr)


def _head_rms_gate(o, z, g):
    outs = []
    for h in range(o.shape[1] // 128):
        seg = o[:, h * 128:(h + 1) * 128]
        nrm = seg * lax.rsqrt(jnp.mean(seg * seg, axis=-1, keepdims=True) + EPS) * g
        outs.append(nrm * _silu(z[:, h * 128:(h + 1) * 128]))
    return jnp.concatenate(outs, axis=1)


def _merge_kernel(oaf, oab, az, obf, obb, bz, yc, gl, xs, dng, glag, wpa, wpb, wpc, wo, mod_ref,
                  o_ref, *, tm):
    i = pl.program_id(0)
    ya = _head_rms_gate(oaf[...] + oab[...], az[...], dng[...])
    yb = _head_rms_gate(obf[...] + obb[...], bz[...], glag[...])
    pa = _bdot(ya, wpa[...])
    pb = _bdot(yb, wpb[...])
    pc = jnp.dot(yc[...], wpc[...], preferred_element_type=F32)
    merged = (_sigmoid(gl[:, 0:D_MODEL]) * pa + _sigmoid(gl[:, D_MODEL:2 * D_MODEL]) * pb
              + _sigmoid(gl[:, 2 * D_MODEL:3 * D_MODEL]) * pc)
    y = _bdot(merged, wo[...])
    o_ref[...] = xs[...] + y * _res_gate(mod_ref, 0, tm, i * tm)


def _merge(oaf, oab, obf, obb, yc, proj, xs, dng, glag, wpa, wpb, wpc, wo, mod):
    t = xs.shape[0]
    tm = BLK
    row = lambda w, c=0: pl.BlockSpec((tm, w), lambda i: (i, c))
    full = lambda a: pl.BlockSpec(a.shape, lambda i: (0,) * a.ndim)
    return pl.pallas_call(
        functools.partial(_merge_kernel, tm=tm),
        grid=(t // tm,),
        in_specs=[row(A_WIDTH), row(A_WIDTH), row(A_WIDTH, COL_AZ // A_WIDTH),
                  row(B_VWIDTH), row(B_VWIDTH), row(B_VWIDTH, COL_BZ // B_VWIDTH),
                  row(H_C * HD_CP), row(3 * D_MODEL, COL_GL // (3 * D_MODEL)), row(D_MODEL),
                  full(dng), full(glag), full(wpa), full(wpb), full(wpc), full(wo), full(mod)],
        out_specs=row(D_MODEL),
        out_shape=jax.ShapeDtypeStruct((t, D_MODEL), F32),
        compiler_params=_cparams(("arbitrary",)),
        name="merge",
    )(oaf, oab, proj, obf, obb, proj, yc, proj, xs, dng, glag, wpa, wpb, wpc, wo, mod)


def _router_kernel(x_ref, mod_ref, g_ref, rwt_ref, rb_ref, h_ref, e_ref, rank_ref, gate_ref, cnt_ref,
                   carry_ref, *, tm):
    i = pl.program_id(0)

    @pl.when(i == 0)
    def _():
        carry_ref[...] = jnp.zeros_like(carry_ref)

    h = _norm_mod(x_ref[...], g_ref[...], mod_ref, 3 * D_MODEL, i * tm)
    h_ref[...] = h
    logit = lax.dot_general(rwt_ref[...], h, (((1,), (1,)), ((), ())),
                            precision=lax.Precision.HIGHEST, preferred_element_type=F32) + rb_ref[...]
    erow = lax.broadcasted_iota(I32, (N_EXPERTS, tm), 0)
    vals, idxs, sels = [], [], []
    cur = logit
    for _ in range(TOP_K):
        m = jnp.max(cur, axis=0, keepdims=True)
        idx = jnp.min(jnp.where(cur == m, erow, N_EXPERTS), axis=0, keepdims=True)
        sel = erow == idx
        vals.append(m)
        idxs.append(idx)
        sels.append(sel)
        cur = jnp.where(sel, -jnp.inf, cur)
    ex = [jnp.exp(v - vals[0]) for v in vals]
    den = ex[0] + ex[1] + ex[2] + ex[3]
    onehot = jnp.where(sels[0] | sels[1] | sels[2] | sels[3], 1.0, 0.0).astype(F32)
    ss = lax.broadcasted_iota(I32, (tm, tm), 0)
    tt = lax.broadcasted_iota(I32, (tm, tm), 1)
    before = jnp.where(ss < tt, 1.0, 0.0).astype(BF16)
    cnt = jnp.dot(onehot.astype(BF16), before, preferred_element_type=F32) + carry_ref[:, 0:1]
    ranks = [jnp.sum(jnp.where(s, cnt, 0.0), axis=0, keepdims=True) for s in sels]
    pad_i = jnp.zeros((8 - TOP_K, tm), I32)
    e_ref[...] = jnp.concatenate(idxs + [pad_i], axis=0)
    rank_ref[...] = jnp.concatenate([r.astype(I32) for r in ranks] + [pad_i], axis=0)
    gates = jnp.concatenate([e / den for e in ex] + [jnp.zeros((128 - TOP_K, tm), F32)], axis=0)
    gate_ref[...] = gates.T
    carry_ref[...] = carry_ref[...] + jnp.sum(onehot, axis=1, keepdims=True)
    cnt_ref[...] = carry_ref[...]


def _router(xs, mod, g, rwt, rb):
    t = xs.shape[0]
    tm = BLK
    return pl.pallas_call(
        functools.partial(_router_kernel, tm=tm),
        grid=(t // tm,),
        in_specs=[pl.BlockSpec((tm, D_MODEL), lambda i: (i, 0)),
                  pl.BlockSpec((8, 6 * D_MODEL), lambda i: (0, 0)),
                  pl.BlockSpec((1, D_MODEL), lambda i: (0, 0)),
                  pl.BlockSpec((N_EXPERTS, D_MODEL), lambda i: (0, 0)),
                  pl.BlockSpec((N_EXPERTS, 1), lambda i: (0, 0))],
        out_specs=[pl.BlockSpec((tm, D_MODEL), lambda i: (i, 0)),
                   pl.BlockSpec((8, tm), lambda i: (0, i)),
                   pl.BlockSpec((8, tm), lambda i: (0, i)),
                   pl.BlockSpec((tm, 128), lambda i: (i, 0)),
                   pl.BlockSpec((N_EXPERTS, 128), lambda i: (0, 0))],
        out_shape=[jax.ShapeDtypeStruct((t, D_MODEL), F32),
                   jax.ShapeDtypeStruct((8, t), I32),
                   jax.ShapeDtypeStruct((8, t), I32),
                   jax.ShapeDtypeStruct((t, 128), F32),
                   jax.ShapeDtypeStruct((N_EXPERTS, 128), F32)],
        scratch_shapes=[pltpu.VMEM((N_EXPERTS, 128), F32)],
        compiler_params=_cparams(("arbitrary",)),
        name="router",
    )(xs, mod, g, rwt, rb)


def _dest_kernel(pstart_ref, e_ref, rank_ref, d_ref):
    e = e_ref[...]
    acc = rank_ref[...]
    for ee in range(N_EXPERTS):
        acc = acc + jnp.where(e == ee, pstart_ref[ee], 0)
    d_ref[...] = acc


def _dest(pstart, e, rank):
    t = e.shape[1]
    tl = _pick(t, (3328, 1280, 768, 256, 128))
    blk = pl.BlockSpec((8, tl), lambda i: (0, i))
    return pl.pallas_call(
        _dest_kernel,
        grid=(t // tl,),
        in_specs=[pl.BlockSpec(memory_space=pltpu.SMEM), blk, blk],
        out_specs=blk,
        out_shape=jax.ShapeDtypeStruct((8, t), I32),
        compiler_params=_cparams(("arbitrary",)),
        name="moe_dest",
    )(pstart, e, rank)


def _dispatch_kernel(zs_ref, dest_ref, h_ref, xs_ref, zbuf, sem, zsem, *, tm):
    @pl.when(pl.program_id(0) == 0)
    def _():
        zbuf[...] = jnp.zeros_like(zbuf)
        for e in range(N_ZERO_FILLS):
            @pl.when(zs_ref[e] >= 0)
            def _():
                pltpu.make_async_copy(zbuf, xs_ref.at[pl.ds(pl.multiple_of(zs_ref[e], MOE_BM), MOE_BM), :],
                                      zsem).start()
        for e in range(N_ZERO_FILLS):
            @pl.when(zs_ref[e] >= 0)
            def _():
                pltpu.make_async_copy(zbuf, xs_ref.at[pl.ds(0, MOE_BM), :], zsem).wait()

    def body(tok, carry):
        for kk in range(TOP_K):
            pltpu.make_async_copy(h_ref.at[pl.ds(tok, 1), :],
                                  xs_ref.at[pl.ds(dest_ref[kk, tok], 1), :], sem).start()
        return carry

    lax.fori_loop(0, tm, body, 0, unroll=4)
    for _ in range(TOP_K):
        pltpu.make_async_copy(h_ref, xs_ref.at[pl.ds(0, tm), :], sem).wait()


def _dispatch(zstart, dest, h, p_rows):
    t = h.shape[0]
    tm = BLK
    grid_spec = pltpu.PrefetchScalarGridSpec(
        num_scalar_prefetch=1,
        grid=(t // tm,),
        in_specs=[pl.BlockSpec((8, tm), lambda i, zs: (0, i), memory_space=pltpu.SMEM),
                  pl.BlockSpec((tm, D_MODEL), lambda i, zs: (i, 0))],
        out_specs=pl.BlockSpec(memory_space=pl.ANY),
        scratch_shapes=[pltpu.VMEM((MOE_BM, D_MODEL), F32), pltpu.SemaphoreType.DMA(()),
                        pltpu.SemaphoreType.DMA(())],
    )
    return pl.pallas_call(
        functools.partial(_dispatch_kernel, tm=tm),
        grid_spec=grid_spec,
        out_shape=jax.ShapeDtypeStruct((p_rows, D_MODEL), F32),
        compiler_params=_cparams(("arbitrary",)),
        name="moe_dispatch",
    )(zstart, dest, h)


def _expert_kernel(be_ref, nused_ref, first_ref, slot_ref, nxt_ref, x_ref, wgu_hbm, bgu_ref, wdn_hbm, bdn_ref,
                   y_ref, gu_stage, dn_stage, wgu_bf, wdn_bf, sem, *, layer):
    b = pl.program_id(0)

    def fetch(e, s):
        return (pltpu.make_async_copy(wgu_hbm.at[layer, e], gu_stage.at[s], sem.at[0, s]),
                pltpu.make_async_copy(wdn_hbm.at[layer, e], dn_stage.at[s], sem.at[1, s]))

    @pl.when(b == 0)
    def _():
        for cp in fetch(be_ref[0], 0):
            cp.start()

    @pl.when(first_ref[b] == 1)
    def _():
        s = slot_ref[b]
        for cp in fetch(be_ref[b], s):
            cp.wait()
        wgu_bf[...] = gu_stage[s].astype(BF16)
        wdn_bf[...] = dn_stage[s].astype(BF16)

        @pl.when(nxt_ref[b] >= 0)
        def _():
            for cp in fetch(nxt_ref[b], 1 - s):
                cp.start()

    @pl.when(b < nused_ref[0])
    def _():
        gu = jnp.dot(x_ref[...].astype(BF16), wgu_bf[...], preferred_element_type=F32) + bgu_ref[...]
        g_ = jnp.minimum(gu[:, :D_FF], SWIGLU_LIMIT)
        u_ = jnp.clip(gu[:, D_FF:], -SWIGLU_LIMIT, SWIGLU_LIMIT)
        act = (u_ + 1.0) * (g_ * _sigmoid(g_ * SWIGLU_ALPHA))
        y_ref[...] = jnp.dot(act.astype(BF16), wdn_bf[...], preferred_element_type=F32) + bdn_ref[...]

    @pl.when(b >= nused_ref[0])
    def _():
        y_ref[...] = jnp.zeros_like(y_ref)


def _experts(plan, xsorted, w_gu, b_gu, w_dn, b_dn, layer):
    blk_e, nused, first, slot, nxt = plan
    p_rows = xsorted.shape[0]
    nblk = p_rows // MOE_BM
    depth = w_gu.shape[0]
    bsel = lambda b, be, nu, fi, sl, nx: (layer, be[b], 0, 0)
    grid_spec = pltpu.PrefetchScalarGridSpec(
        num_scalar_prefetch=5,
        grid=(nblk,),
        in_specs=[pl.BlockSpec((MOE_BM, D_MODEL), lambda b, be, nu, fi, sl, nx: (jnp.minimum(b, nu[0] - 1), 0)),
                  pl.BlockSpec(memory_space=pl.ANY),
                  pl.BlockSpec((None, None, 1, 2 * D_FF), bsel),
                  pl.BlockSpec(memory_space=pl.ANY),
                  pl.BlockSpec((None, None, 1, D_MODEL), bsel)],
        out_specs=pl.BlockSpec((MOE_BM, D_MODEL), lambda b, be, nu, fi, sl, nx: (b, 0)),
        scratch_shapes=[pltpu.VMEM((2, D_MODEL, 2 * D_FF), F32), pltpu.VMEM((2, D_FF, D_MODEL), F32),
                        pltpu.VMEM((D_MODEL, 2 * D_FF), BF16), pltpu.VMEM((D_FF, D_MODEL), BF16),
                        pltpu.SemaphoreType.DMA((2, 2))],
    )
    return pl.pallas_call(
        functools.partial(_expert_kernel, layer=layer),
        grid_spec=grid_spec,
        out_shape=jax.ShapeDtypeStruct((p_rows, D_MODEL), F32),
        compiler_params=_cparams(("arbitrary",)),
        name="moe_experts",
    )(blk_e, nused, first, slot, nxt, xsorted, w_gu, b_gu.reshape(depth, N_EXPERTS, 1, 2 * D_FF), w_dn,
      b_dn.reshape(depth, N_EXPERTS, 1, D_MODEL))


def _combine_kernel(dest_ref, destn_ref, y_ref, gate_ref, xs_ref, mod_ref, o_ref, buf, sem, *, tm, nsteps):
    i = pl.program_id(0)
    slot = i % 2

    def issue(d_ref, s):
        def body(tok, carry):
            for kk in range(TOP_K):
                pltpu.make_async_copy(y_ref.at[pl.ds(d_ref[kk, tok], 1), :],
                                      buf.at[s, kk, pl.ds(tok, 1), :], sem.at[s]).start()
            return carry

        lax.fori_loop(0, tm, body, 0, unroll=4)

    @pl.when(i == 0)
    def _():
        issue(dest_ref, 0)

    @pl.when(i + 1 < nsteps)
    def _():
        issue(destn_ref, 1 - slot)

    for kk in range(TOP_K):
        pltpu.make_async_copy(y_ref.at[pl.ds(0, tm), :], buf.at[slot, kk], sem.at[slot]).wait()
    gate = gate_ref[...]
    acc = gate[:, 0:1] * buf[slot, 0]
    for kk in range(1, TOP_K):
        acc = acc + gate[:, kk:kk + 1] * buf[slot, kk]
    o_ref[...] = xs_ref[...] + acc * _res_gate(mod_ref, 3 * D_MODEL, tm, i * tm)


def _combine(dest, y, gate_col, xs, mod):
    t = xs.shape[0]
    tm = 128
    nsteps = t // tm
    return pl.pallas_call(
        functools.partial(_combine_kernel, tm=tm, nsteps=nsteps),
        grid=(nsteps,),
        in_specs=[pl.BlockSpec((8, tm), lambda i: (0, i), memory_space=pltpu.SMEM),
                  pl.BlockSpec((8, tm), lambda i: (0, jnp.minimum(i + 1, nsteps - 1)), memory_space=pltpu.SMEM),
                  pl.BlockSpec(memory_space=pl.ANY),
                  pl.BlockSpec((tm, 128), lambda i: (i, 0)),
                  pl.BlockSpec((tm, D_MODEL), lambda i: (i, 0)),
                  pl.BlockSpec((8, 6 * D_MODEL), lambda i: (0, 0))],
        out_specs=pl.BlockSpec((tm, D_MODEL), lambda i: (i, 0)),
        out_shape=jax.ShapeDtypeStruct((t, D_MODEL), F32),
        scratch_shapes=[pltpu.VMEM((2, TOP_K, tm, D_MODEL), F32), pltpu.SemaphoreType.DMA((2,))],
        compiler_params=_cparams(("arbitrary",)),
        name="moe_combine",
    )(dest, dest, y, gate_col, xs, mod)


def _pad_heads(w, nh, hd, hdp):
    d = w.shape[0]
    return jnp.pad(w.reshape(d, nh, hd), ((0, 0), (0, 0), (0, hdp - hd))).reshape(d, nh * hdp)


def _layout_w_in(w):
    return _layout_w_in_f32(w).astype(BF16)


def _layout_w_in_f32(w):
    pts = np.cumsum([A_WIDTH] * 4 + [2 * H_A, 2 * H_A, B_KWIDTH, B_KWIDTH, B_VWIDTH, B_VWIDTH,
                                     2 * GLA_RANK, C_WIDTH, KV_C * HD_C, KV_C * HD_C])
    (aq, ak, av, az, aa, ab, bq, bk, bv, bz, bg, cq, ck, cv, gl) = jnp.split(w, pts.tolist(), axis=1)
    small = jnp.concatenate([aa, ab, bg], axis=1)
    small = jnp.pad(small, ((0, 0), (0, 256 - small.shape[1])))
    cols = [gl, aq, ak, av, az, _pad_heads(cq, H_C, HD_C, HD_CP), _pad_heads(ck, KV_C, HD_C, HD_CP),
            _pad_heads(cv, KV_C, HD_C, HD_CP), bq, bk, bv, bz, small]
    out = jnp.concatenate(cols, axis=1)
    assert out.shape[1] == IN_COLS_P
    return out


def _rope_tables(t):
    s_len = t - CTX
    half = HD_C // 2
    inv_freq = ROPE_THETA ** (-jnp.arange(0, half, 2, dtype=F32) / half)
    pos = jnp.arange(s_len)
    rows = (pos // GRID_W).astype(F32)[:, None] * inv_freq[None, :]
    cols = (pos % GRID_W).astype(F32)[:, None] * inv_freq[None, :]
    cr, sr, cc, sc = jnp.cos(rows), jnp.sin(rows), jnp.cos(cols), jnp.sin(cols)
    zpad = jnp.zeros((s_len, HD_CP - HD_C), F32)
    cos_l = jnp.concatenate([cr, cr, cc, cc, zpad], axis=1)
    sin_l = jnp.concatenate([-sr, sr, -sc, sc, zpad], axis=1)
    cos_t = jnp.concatenate([jnp.ones((CTX, HD_CP), F32), cos_l], axis=0)
    sin_t = jnp.concatenate([jnp.zeros((CTX, HD_CP), F32), sin_l], axis=0)
    return cos_t, sin_t


def _lane_vec(v, width=128):
    v = v.reshape(1, -1).astype(F32)
    return jnp.pad(v, ((0, 0), (0, width - v.shape[1])))


def _moe_plan(counts, tk):
    padded = (counts + MOE_BM - 1) // MOE_BM * MOE_BM
    pend = jnp.cumsum(padded)
    pstart = pend - padded
    nblk = (tk + N_EXPERTS * (MOE_BM - 1) + MOE_BM - 1) // MOE_BM
    p_rows = nblk * MOE_BM
    blk = jnp.arange(nblk, dtype=I32)
    blk_e = jnp.minimum(jnp.sum((pend[None, :] <= (blk * MOE_BM)[:, None]).astype(I32), axis=1), N_EXPERTS - 1)
    nused = (pend[-1] // MOE_BM).astype(I32)
    prev_e = jnp.concatenate([jnp.full((1,), -1, I32), blk_e[:-1]])
    first = ((blk_e != prev_e) & (blk < nused)).astype(I32)
    slot = (jnp.cumsum(first) - 1) % 2
    pos = jnp.where(first == 1, blk, nblk)
    nxt_pos = jnp.concatenate([lax.cummin(pos, axis=0, reverse=True)[1:], jnp.full((1,), nblk, I32)])
    nxt = jnp.where(nxt_pos < nblk, blk_e[jnp.minimum(nxt_pos, nblk - 1)], -1)
    last_blk = jnp.where(counts > 0, pend - MOE_BM, -1)
    tail = pend[-1] + jnp.arange(N_ZERO_FILLS - N_EXPERTS, dtype=I32) * MOE_BM
    zstart = jnp.concatenate([last_blk, jnp.where(tail < p_rows, tail, -1)]).astype(I32)
    plan = (blk_e, nused.reshape(1), first, slot.astype(I32), nxt.astype(I32))
    return pstart.astype(I32), zstart, plan, p_rows


def kernel(x, c, ctx, c_ctx, ada_w, ada_b, norm_mix_g, norm_ffn_g, w_in, dn_conv_w, dn_a_log, dn_dt_bias,
           dn_norm_g, gla_w2, gla_b2, gla_norm_g, attn_q_norm_g, attn_k_norm_g, attn_sink, w_branch_a,
           w_branch_b, w_branch_c, w_out, router_w, router_b, w_gate_up, b_gate_up, w_down, b_down):
    assert x.shape[0] == 1 and c.shape[0] == 1 and ctx.shape[1] == CTX
    depth = ada_w.shape[0]
    xs = jnp.concatenate([ctx[0], x[0]], axis=0)
    t = xs.shape[0]
    assert t % BLK == 0 and (t - CTX) % GRID_W == 0
    cc = jnp.concatenate([c, c_ctx[None, :], jnp.zeros((6, D_MODEL), F32)], axis=0)
    mods = _ada_mod(cc, ada_w, ada_b)
    cos_t, sin_t = _rope_tables(t)
    for l in range(depth):
        mod = mods[l]
        proj = _inproj(xs, mod, norm_mix_g[l][None, :], _layout_w_in(w_in[l]))
        qa, ka, va, gcol = _dnprep(proj, dn_conv_w[l], _lane_vec(dn_a_log[l]), _lane_vec(dn_dt_bias[l]))
        oaf, oab = _dnscan(qa, ka, va, gcol)
        w2 = gla_w2[l].astype(F32)
        w2full = jnp.zeros((128, 2 * B_KWIDTH), F32)
        for d in range(2):
            r0 = 4 * H_A + d * GLA_RANK
            w2full = w2full.at[r0:r0 + GLA_RANK, d * B_KWIDTH:(d + 1) * B_KWIDTH].set(w2[d])
        obf, obb = _glascan(proj, w2full, gla_b2[l].reshape(1, 2 * B_KWIDTH).astype(F32))
        qg = _lane_vec(attn_q_norm_g[l])
        kg = _lane_vec(attn_k_norm_g[l])
        qr, kr, vr = _attnprep(proj, qg, kg, cos_t, sin_t)
        yc = _attn(qr, kr, vr, attn_sink[l].astype(F32))
        wpc = jnp.pad(w_branch_c[l].reshape(H_C, HD_C, D_MODEL),
                      ((0, 0), (0, HD_CP - HD_C), (0, 0))).reshape(H_C * HD_CP, D_MODEL)
        xs = _merge(oaf, oab, obf, obb, yc, proj, xs,
                    dn_norm_g[l][None, :], gla_norm_g[l][None, :],
                    w_branch_a[l].astype(BF16), w_branch_b[l].astype(BF16), wpc.astype(BF16),
                    w_out[l].astype(BF16), mod)
        h2, top_e, rank, gate_col, cnt = _router(xs, mod, norm_ffn_g[l][None, :], router_w[l].T,
                                                 router_b[l][:, None])
        pstart, zstart, plan, p_rows = _moe_plan(cnt[:, 0].astype(I32), t * TOP_K)
        dest = _dest(pstart, top_e, rank)
        xsorted = _dispatch(zstart, dest, h2, p_rows)
        y = _experts(plan, xsorted, w_gate_up, b_gate_up, w_down, b_down, l)
        xs = _combine(dest, y, gate_col, xs, mod)
    return xs[CTX:][None]
```

```python
import functools
import math

import jax
import jax.numpy as jnp
import numpy as np
from jax import lax
from jax.experimental import pallas as pl
from jax.experimental.pallas import tpu as pltpu

F32 = jnp.float32
BF16 = jnp.bfloat16
I32 = jnp.int32

D_MODEL = 1024
DEPTH = 4
GRID_W = 64
CTX = 256
H_A = 4
HD_A = 128
A_WIDTH = H_A * HD_A
CONV_W = 5
H_B = 4
DK_B = 64
DV_B = 128
B_KWIDTH = H_B * DK_B
B_VWIDTH = H_B * DV_B
GLA_RANK = 16
GLA_NORMALIZER = 16.0
H_C = 8
KV_C = 2
REP_C = H_C // KV_C
HD_C = 64
HD_CP = 128
C_WIDTH = H_C * HD_C
ATT_BLOCK = 128
ROPE_THETA = 10000.0
CHUNK = 64
SUB = 16
N_EXPERTS = 32
TOP_K = 4
D_FF = 1024
SWIGLU_LIMIT = 7.0
SWIGLU_ALPHA = 1.702
EPS = 1e-6
NEG = -1e30

BLK = 256
MOE_BM = 256
VMEM_LIMIT = 56 * 1024 * 1024

COL_GL = 0
COL_QKV = 3072
COL_AZ = 4608
COL_CQ = 5120
COL_CKV = 6144
COL_BQK = 6656
COL_BV = 7168
COL_BZ = 7680
COL_SMALL = 8192
IN_COLS_P = 8448
IN_TN = 768


def _pick(n, cands):
    for c in cands:
        if n % c == 0:
            return c
    raise ValueError(f"no tile for {n}")


def _cparams(sem):
    return pltpu.CompilerParams(dimension_semantics=sem, vmem_limit_bytes=VMEM_LIMIT)


def _bdot(a, b):
    return jnp.dot(a.astype(BF16), b.astype(BF16), preferred_element_type=F32)


def _bdot_nt(a, b):
    return lax.dot_general(a.astype(BF16), b.astype(BF16), (((1,), (1,)), ((), ())),
                           preferred_element_type=F32)


def _bdot_tn(a, b):
    return lax.dot_general(a.astype(BF16), b.astype(BF16), (((0,), (0,)), ((), ())),
                           preferred_element_type=F32)


def _fdot(a, b):
    return jnp.dot(a, b, precision=lax.Precision.HIGHEST, preferred_element_type=F32)


def _split2(a):
    hi = a.astype(BF16)
    lo = (a - hi.astype(F32)).astype(BF16)
    return hi, lo


def _dot3(a, b):
    ah, al = _split2(a)
    bh, bl = _split2(b)
    d = functools.partial(jnp.dot, preferred_element_type=F32)
    return d(ah, bh) + (d(ah, bl) + d(al, bh))


def _cumsum_dot(tri, x):
    t = tri.astype(BF16)
    hi = x.astype(BF16)
    r1 = x - hi.astype(F32)
    mid = r1.astype(BF16)
    lo = (r1 - mid.astype(F32)).astype(BF16)
    d = functools.partial(jnp.dot, preferred_element_type=F32)
    return d(t, hi) + (d(t, mid) + d(t, lo))


def _sigmoid(x):
    return 1.0 / (1.0 + jnp.exp(-x))


def _silu(x):
    return x * _sigmoid(x)


def _softplus(x):
    return jnp.maximum(x, 0.0) + jnp.log(1.0 + jnp.exp(-jnp.abs(x)))


def _ada_kernel(cc_ref, w_ref, b_ref, o_ref):
    o_ref[...] = _fdot(_silu(cc_ref[...]), w_ref[...]) + b_ref[...]


def _ada_mod(cc, ada_w, ada_b):
    depth = ada_w.shape[0]
    tn = 1536
    return pl.pallas_call(
        _ada_kernel,
        grid=(depth, 6 * D_MODEL // tn),
        in_specs=[pl.BlockSpec((8, D_MODEL), lambda l, j: (0, 0)),
                  pl.BlockSpec((None, D_MODEL, tn), lambda l, j: (l, 0, j)),
                  pl.BlockSpec((None, 1, tn), lambda l, j: (l, 0, j))],
        out_specs=pl.BlockSpec((None, 8, tn), lambda l, j: (l, 0, j)),
        out_shape=jax.ShapeDtypeStruct((depth, 8, 6 * D_MODEL), F32),
        compiler_params=_cparams(("arbitrary", "arbitrary")),
        name="ada_mod",
    )(cc, ada_w, ada_b.reshape(depth, 1, 6 * D_MODEL))


def _norm_mod(x, g, mod_ref, moff, row0):
    tm = x.shape[0]
    y = x * lax.rsqrt(jnp.mean(x * x, axis=-1, keepdims=True) + EPS) * g
    isc = (row0 + lax.broadcasted_iota(I32, (tm, 1), 0)) < CTX
    shift = jnp.where(isc, mod_ref[1:2, moff:moff + D_MODEL], mod_ref[0:1, moff:moff + D_MODEL])
    scale = jnp.where(isc, mod_ref[1:2, moff + D_MODEL:moff + 2 * D_MODEL],
                      mod_ref[0:1, moff + D_MODEL:moff + 2 * D_MODEL])
    return y * (1.0 + scale) + shift


def _res_gate(mod_ref, moff, tm, row0):
    isc = (row0 + lax.broadcasted_iota(I32, (tm, 1), 0)) < CTX
    return jnp.where(isc, mod_ref[1:2, moff + 2 * D_MODEL:moff + 3 * D_MODEL],
                     mod_ref[0:1, moff + 2 * D_MODEL:moff + 3 * D_MODEL])


def _inproj_kernel(x_ref, mod_ref, g_ref, w_ref, o_ref, h_ref, *, tm):
    i = pl.program_id(0)

    @pl.when(pl.program_id(1) == 0)
    def _():
        h_ref[...] = _norm_mod(x_ref[...], g_ref[...], mod_ref, 0, i * tm).astype(BF16)

    o_ref[...] = jnp.dot(h_ref[...], w_ref[...], preferred_element_type=F32)


def _inproj(xs, mod, g, w):
    t = xs.shape[0]
    tm = _pick(t, (1280, 640, 256))
    return pl.pallas_call(
        functools.partial(_inproj_kernel, tm=tm),
        grid=(t // tm, IN_COLS_P // IN_TN),
        in_specs=[pl.BlockSpec((tm, D_MODEL), lambda i, j: (i, 0)),
                  pl.BlockSpec((8, 6 * D_MODEL), lambda i, j: (0, 0)),
                  pl.BlockSpec((1, D_MODEL), lambda i, j: (0, 0)),
                  pl.BlockSpec((D_MODEL, IN_TN), lambda i, j: (0, j))],
        out_specs=pl.BlockSpec((tm, IN_TN), lambda i, j: (i, j)),
        out_shape=jax.ShapeDtypeStruct((t, IN_COLS_P), F32),
        scratch_shapes=[pltpu.VMEM((tm, D_MODEL), BF16)],
        compiler_params=_cparams(("arbitrary", "arbitrary")),
        name="inproj",
    )(xs, mod, g, w)


def _tri_blockdiag(n, lower):
    ii = lax.broadcasted_iota(I32, (n, n), 0)
    jj = lax.broadcasted_iota(I32, (n, n), 1)
    same = (ii >> 6) == (jj >> 6)
    tri = (ii >= jj) if lower else (ii <= jj)
    return jnp.where(same, jnp.where(tri, 1.0, 0.0), 0.0).astype(F32)


def _dnprep_kernel(main_ref, prev_ref, next_ref, small_ref, cw_ref, alog_ref, dtb_ref,
                   q_ref, k_ref, v_ref, gcol_ref, ext_ref, *, nb):
    i = pl.program_id(0)
    use_prev = i >= 2
    use_next = jnp.logical_and(i >= 1, i <= nb - 2)
    ext_ref[0:8, :] = jnp.where(use_prev, prev_ref[...], 0.0)
    ext_ref[8:8 + BLK, :] = main_ref[...]
    ext_ref[8 + BLK:16 + BLK, :] = jnp.where(use_next, next_ref[...], 0.0)
    acc = ext_ref[6:6 + BLK, :] * cw_ref[0:1, :]
    for d in range(1, CONV_W):
        acc = acc + ext_ref[6 + d:6 + d + BLK, :] * cw_ref[d:d + 1, :]
    s = _silu(acc)
    for h in range(H_A):
        for part, ref, mul in ((0, q_ref, HD_A ** -0.5), (1, k_ref, 1.0)):
            seg = s[:, part * A_WIDTH + h * HD_A: part * A_WIDTH + (h + 1) * HD_A]
            nrm = seg * lax.rsqrt(jnp.sum(seg * seg, axis=-1, keepdims=True) + EPS)
            ref[:, h * HD_A:(h + 1) * HD_A] = nrm * mul
    v_ref[...] = s[:, 2 * A_WIDTH:3 * A_WIDTH]
    sm = small_ref[...]
    lane = lax.broadcasted_iota(I32, sm.shape, 1)
    g = -jnp.exp(alog_ref[...]) * _softplus(sm + dtb_ref[...])
    gb = jnp.where(lane < 2 * H_A, g, jnp.where(lane < 4 * H_A, _sigmoid(sm), 0.0))
    cf = _fdot(_tri_blockdiag(BLK, True), gb)
    cr = _fdot(_tri_blockdiag(BLK, False), gb)
    gc = jnp.where(lane < H_A, cf, jnp.where(lane < 2 * H_A, cr, gb))
    gcol_ref[...] = gc


def _dnprep(proj, conv_w, alog_vec, dtb_vec):
    t = proj.shape[0]
    nb = t // BLK
    qkv_blk = COL_QKV // (3 * A_WIDTH)
    last8 = t // 8 - 1
    out_sds = jax.ShapeDtypeStruct((t, A_WIDTH), F32)
    return pl.pallas_call(
        functools.partial(_dnprep_kernel, nb=nb),
        grid=(nb,),
        in_specs=[pl.BlockSpec((BLK, 3 * A_WIDTH), lambda i: (i, qkv_blk)),
                  pl.BlockSpec((8, 3 * A_WIDTH), lambda i: (jnp.maximum(i * (BLK // 8) - 1, 0), qkv_blk)),
                  pl.BlockSpec((8, 3 * A_WIDTH), lambda i: (jnp.minimum((i + 1) * (BLK // 8), last8), qkv_blk)),
                  pl.BlockSpec((BLK, 128), lambda i: (i, COL_SMALL // 128)),
                  pl.BlockSpec((CONV_W, 3 * A_WIDTH), lambda i: (0, 0)),
                  pl.BlockSpec((1, 128), lambda i: (0, 0)),
                  pl.BlockSpec((1, 128), lambda i: (0, 0))],
        out_specs=[pl.BlockSpec((BLK, A_WIDTH), lambda i: (i, 0)),
                   pl.BlockSpec((BLK, A_WIDTH), lambda i: (i, 0)),
                   pl.BlockSpec((BLK, A_WIDTH), lambda i: (i, 0)),
                   pl.BlockSpec((BLK, 128), lambda i: (i, 0))],
        out_shape=[out_sds, out_sds, out_sds,
                   jax.ShapeDtypeStruct((t, 128), F32)],
        scratch_shapes=[pltpu.VMEM((BLK + 16, 3 * A_WIDTH), F32)],
        compiler_params=_cparams(("arbitrary",)),
        name="dn_prep",
    )(proj, proj, proj, proj, conv_w, alog_vec, dtb_vec)


def _dot3_all(a_list, b_list):
    d = functools.partial(jnp.dot, preferred_element_type=F32)
    sa = [_split2(a) for a in a_list]
    sb = [_split2(b) for b in b_list]
    hh = [d(a[0], b[0]) for a, b in zip(sa, sb)]
    hl = [d(a[0], b[1]) for a, b in zip(sa, sb)]
    lh = [d(a[1], b[0]) for a, b in zip(sa, sb)]
    return [x + (y + z) for x, y, z in zip(hh, hl, lh)]


def _unit_tri_inverse_all(l_mats, masks):
    eye, m_diag, m_l1, m_l2 = masks
    ld = [l * m_diag for l in l_mats]
    x = [eye - a for a in ld]
    p = _dot3_all(ld, ld)
    for it in range(3):
        xp = _dot3_all(x, p)
        if it < 2:
            p = _dot3_all(p, p)
        x = [a + b for a, b in zip(x, xp)]
    for m in (m_l1, m_l2):
        cx = _dot3_all([l * m for l in l_mats], x)
        xcx = _dot3_all(x, cx)
        x = [a - b for a, b in zip(x, xcx)]
    return x


def _dn_masks():
    ii = lax.broadcasted_iota(I32, (CHUNK, CHUNK), 0)
    jj = lax.broadcasted_iota(I32, (CHUNK, CHUNK), 1)
    one = lambda c: jnp.where(c, 1.0, 0.0).astype(F32)
    eye = one(ii == jj)
    m_diag = one((ii >> 4) == (jj >> 4))
    m_l2 = one((ii >> 5) != (jj >> 5))
    m_l1 = 1.0 - m_diag - m_l2
    return ii, jj, (eye, m_diag, m_l1, m_l2)


def _dn_local(items, ii, jj, masks):
    n = len(items)
    dec, lmat, qk, kb, eg = [], [], [], [], []
    for q, k, v, gcol, grow, bcol, fwd in items:
        incl = (ii >= jj) if fwd else (ii <= jj)
        dec.append(jnp.exp(jnp.where(incl, gcol - grow, NEG)))
        kb.append(k * bcol)
        eg.append(jnp.exp(gcol))
    kh = [it[1].astype(BF16) for it in items]
    kk = [_bdot_nt(kb[i], kh[i]) for i in range(n)]
    qkr = [_bdot_nt(items[i][0], kh[i]) for i in range(n)]
    for i in range(n):
        fwd = items[i][6]
        strict = (ii > jj) if fwd else (ii < jj)
        lmat.append(kk[i] * jnp.where(strict, dec[i], 0.0))
        qk.append((qkr[i] * dec[i]).astype(BF16))
    rhs = [jnp.concatenate([items[i][2] * items[i][5], kb[i] * eg[i]], axis=1) for i in range(n)]
    sol = _dot3_all(_unit_tri_inverse_all(lmat, masks), rhs)
    out = []
    for i in range(n):
        q, k, _, gcol, _, _, fwd = items[i]
        glast = gcol[CHUNK - 1:CHUNK, :] if fwd else gcol[0:1, :]
        out.append((sol[i][:, :HD_A], sol[i][:, HD_A:].astype(BF16), qk[i], (q * eg[i]).astype(BF16),
                    (k * jnp.exp(glast - gcol)).astype(BF16), jnp.exp(glast)))
    return out


def _dn_step(local, states):
    n = len(local)
    sb = [s.astype(BF16) for s in states]
    d = functools.partial(jnp.dot, preferred_element_type=F32)
    ws = [d(local[i][1], sb[i]) for i in range(n)]
    qs = [d(local[i][3], sb[i]) for i in range(n)]
    v_new = [(local[i][0] - ws[i]).astype(BF16) for i in range(n)]
    o2 = [d(local[i][2], v_new[i]) for i in range(n)]
    kv = [lax.dot_general(local[i][4], v_new[i], (((0,), (0,)), ((), ())), preferred_element_type=F32)
          for i in range(n)]
    return [qs[i] + o2[i] for i in range(n)], [states[i] * local[i][5] + kv[i] for i in range(n)]


def _dnscan_kernel(qf, kf, vf, gcf, qb, kb, vb, gcb, of_ref, ob_ref, s_ref):
    @pl.when(pl.program_id(0) == 0)
    def _():
        s_ref[...] = jnp.zeros_like(s_ref)

    ii, jj, masks = _dn_masks()
    nch = BLK // CHUNK
    pick = jnp.where(lax.broadcasted_iota(I32, (16, 128), 0) == lax.broadcasted_iota(I32, (16, 128), 1),
                     1.0, 0.0).astype(F32)

    dirs = ((True, (qf, kf, vf, gcf, of_ref)), (False, (qb, kb, vb, gcb, ob_ref)))
    items, sinks = [], []
    for step in range(nch):
        for fwd, (q_r, k_r, v_r, gc_r, o_r) in dirs:
            c = step if fwd else nch - 1 - step
            rows = slice(c * CHUNK, (c + 1) * CHUNK)
            d = 0 if fwd else 1
            gct = gc_r[rows, :]
            grows = lax.dot_general(pick, gct, (((1,), (1,)), ((), ())),
                                    precision=lax.Precision.HIGHEST, preferred_element_type=F32)
            for h in range(H_A):
                lanes = slice(h * HD_A, (h + 1) * HD_A)
                gi = d * H_A + h
                items.append((q_r[rows, lanes], k_r[rows, lanes], v_r[rows, lanes],
                              gct[:, gi:gi + 1], grows[gi:gi + 1, :],
                              gct[:, 2 * H_A + gi:2 * H_A + gi + 1], fwd))
                sinks.append((o_r, rows, lanes))
    local = _dn_local(items, ii, jj, masks)
    nchain = 2 * H_A
    states = [s_ref[gi] for gi in range(nchain)]
    for step in range(nch):
        outs, states = _dn_step(local[step * nchain:(step + 1) * nchain], states)
        for (o_r, rows, lanes), o in zip(sinks[step * nchain:(step + 1) * nchain], outs):
            o_r[rows, lanes] = o
    for gi in range(nchain):
        s_ref[gi] = states[gi]


def _rev_block(nb):
    return lambda i: jnp.where(i == 0, 0, nb - i)


def _dnscan(q, k, v, gcol):
    t = q.shape[0]
    nb = t // BLK
    rev = _rev_block(nb)
    wide = lambda f: pl.BlockSpec((BLK, A_WIDTH), lambda i: (f(i), 0))
    col = lambda f: pl.BlockSpec((BLK, 128), lambda i: (f(i), 0))
    ident = lambda i: i
    out_sds = jax.ShapeDtypeStruct((t, A_WIDTH), F32)
    return pl.pallas_call(
        _dnscan_kernel,
        grid=(nb,),
        in_specs=[wide(ident), wide(ident), wide(ident), col(ident),
                  wide(rev), wide(rev), wide(rev), col(rev)],
        out_specs=[wide(ident), wide(rev)],
        out_shape=[out_sds, out_sds],
        scratch_shapes=[pltpu.VMEM((2 * H_A, HD_A, HD_A), F32)],
        compiler_params=_cparams(("arbitrary",)),
        name="dn_scan",
    )(q, k, v, gcol, q, k, v, gcol)


def _gla_gates(small_ref, w2_ref, b2_ref, b_ref, d, fwd):
    cols = slice(d * B_KWIDTH, (d + 1) * B_KWIDTH)
    pre = _dot3(small_ref[...], w2_ref[:, cols]) + b2_ref[:, cols]
    gk = -_softplus(-pre) * (1.0 / GLA_NORMALIZER)
    b_ref[d] = _cumsum_dot(_tri_blockdiag(BLK, fwd), gk)


def _gla_chunk(qk_ref, v_ref, b_ref, o_ref, st_ref, d, fwd, consts, step):
    sel, headmask_k, st_mask = consts
    nch = BLK // CHUNK
    nsub = CHUNK // SUB
    sub_i = lax.broadcasted_iota(I32, (SUB, 1), 0)
    row_c = lax.broadcasted_iota(I32, (CHUNK, 1), 0)
    if True:
        c = step if fwd else nch - 1 - step
        rows = pl.ds(pl.multiple_of(c * CHUNK, CHUNK), CHUNK)
        q = qk_ref[rows, 0:B_KWIDTH] * (DK_B ** -0.5)
        k = qk_ref[rows, B_KWIDTH:2 * B_KWIDTH]
        v = v_ref[rows, :]
        b = b_ref[d, rows, :]
        vh = v.astype(BF16)
        st = st_ref[d]
        o = _bdot_nt(q * jnp.exp(b), st)
        refs = []
        for sb in range(nsub):
            if fwd:
                r = b[sb * SUB - 1:sb * SUB, :] if sb > 0 else jnp.zeros((1, B_KWIDTH), F32)
            else:
                r = b[(sb + 1) * SUB:(sb + 1) * SUB + 1, :] if sb < nsub - 1 else jnp.zeros((1, B_KWIDTH), F32)
            refs.append(r)
        rfull = jnp.concatenate([jnp.broadcast_to(r, (SUB, B_KWIDTH)) for r in refs], axis=0)
        qs = q * jnp.exp(b - rfull)
        a_off = [None] * H_B
        for sb in (range(1, nsub) if fwd else range(0, nsub - 1)):
            jmask = (row_c < sb * SUB) if fwd else (row_c >= (sb + 1) * SUB)
            ks = (k * jnp.exp(jnp.where(jmask, refs[sb] - b, NEG))).astype(BF16)
            rowmask = jnp.where((row_c >> 4) == sb, 1.0, 0.0)
            for h in range(H_B):
                a = _bdot_nt(qs * headmask_k[h], ks) * rowmask
                a_off[h] = a if a_off[h] is None else a_off[h] + a
        o = o + jnp.concatenate(
            [_bdot(a_off[h], vh[:, h * DV_B:(h + 1) * DV_B]) for h in range(H_B)], axis=1)
        diag = []
        for sb in range(nsub):
            s0 = sb * SUB
            bs, qsb, ksb = b[s0:s0 + SUB, :], q[s0:s0 + SUB, :], k[s0:s0 + SUB, :]
            tiles = []
            for jl in range(SUB):
                causal = (sub_i >= jl) if fwd else (sub_i <= jl)
                e = jnp.exp(jnp.where(causal, bs - bs[jl:jl + 1, :], NEG))
                tiles.append((qsb * ksb[jl:jl + 1, :] * e).astype(BF16))
            red = jnp.dot(jnp.concatenate(tiles, axis=0), sel, preferred_element_type=F32)
            acc = red[0:SUB, :] * v[s0:s0 + 1, :]
            for jl in range(1, SUB):
                acc = acc + red[jl * SUB:(jl + 1) * SUB, :] * v[s0 + jl:s0 + jl + 1, :]
            diag.append(acc)
        o_ref[rows, :] = o + jnp.concatenate(diag, axis=0)
        blast = b[CHUNK - 1:CHUNK, :] if fwd else b[0:1, :]
        kd = k * jnp.exp(blast - b)
        st_ref[d] = st * jnp.exp(blast) + _bdot_tn(v, kd) * st_mask


def _glascan_kernel(qkf, vf, smf, qkb, vb, smb, w2_ref, b2_ref, of_ref, ob_ref, st_ref, b_ref):
    @pl.when(pl.program_id(0) == 0)
    def _():
        st_ref[...] = jnp.zeros_like(st_ref)

    kk = lax.broadcasted_iota(I32, (B_KWIDTH, B_VWIDTH), 0)
    cc = lax.broadcasted_iota(I32, (B_KWIDTH, B_VWIDTH), 1)
    sel = jnp.where((kk >> 6) == (cc >> 7), 1.0, 0.0).astype(BF16)
    lane = lax.broadcasted_iota(I32, (1, B_KWIDTH), 1)
    headmask_k = [jnp.where((lane >> 6) == h, 1.0, 0.0).astype(F32) for h in range(H_B)]
    rr = lax.broadcasted_iota(I32, (B_VWIDTH, B_KWIDTH), 0)
    kc = lax.broadcasted_iota(I32, (B_VWIDTH, B_KWIDTH), 1)
    st_mask = jnp.where((rr >> 7) == (kc >> 6), 1.0, 0.0).astype(F32)
    consts = (sel, headmask_k, st_mask)
    _gla_gates(smf, w2_ref, b2_ref, b_ref, 0, True)
    _gla_gates(smb, w2_ref, b2_ref, b_ref, 1, False)

    def body(step, carry):
        _gla_chunk(qkf, vf, b_ref, of_ref, st_ref, 0, True, consts, step)
        _gla_chunk(qkb, vb, b_ref, ob_ref, st_ref, 1, False, consts, step)
        return carry

    lax.fori_loop(0, BLK // CHUNK, body, 0)


def _glascan(proj, w2full, b2full):
    t = proj.shape[0]
    nb = t // BLK
    rev = _rev_block(nb)
    ident = lambda i: i
    qk = lambda f: pl.BlockSpec((BLK, 2 * B_KWIDTH), lambda i: (f(i), COL_BQK // (2 * B_KWIDTH)))
    vv = lambda f: pl.BlockSpec((BLK, B_VWIDTH), lambda i: (f(i), COL_BV // B_VWIDTH))
    sm = lambda f: pl.BlockSpec((BLK, 128), lambda i: (f(i), COL_SMALL // 128))
    outs = lambda f: pl.BlockSpec((BLK, B_VWIDTH), lambda i: (f(i), 0))
    out_sds = jax.ShapeDtypeStruct((t, B_VWIDTH), F32)
    return pl.pallas_call(
        _glascan_kernel,
        grid=(nb,),
        in_specs=[qk(ident), vv(ident), sm(ident), qk(rev), vv(rev), sm(rev),
                  pl.BlockSpec((128, 2 * B_KWIDTH), lambda i: (0, 0)),
                  pl.BlockSpec((1, 2 * B_KWIDTH), lambda i: (0, 0))],
        out_specs=[outs(ident), outs(rev)],
        out_shape=[out_sds, out_sds],
        scratch_shapes=[pltpu.VMEM((2, B_VWIDTH, B_KWIDTH), F32), pltpu.VMEM((2, BLK, B_KWIDTH), F32)],
        compiler_params=_cparams(("arbitrary",)),
        name="gla_scan",
    )(proj, proj, proj, proj, proj, proj, w2full, b2full)


def _attnprep_kernel(cq_ref, ckv_ref, qg_ref, kg_ref, cos_ref, sin_ref, q_ref, k_ref, v_ref):
    cos = cos_ref[...]
    sin = sin_ref[...]
    lane = lax.broadcasted_iota(I32, cos.shape, 1)
    first = (lane % 32) < 16

    def norm_rope(x, g):
        y = x * lax.rsqrt(jnp.sum(x * x, axis=-1, keepdims=True) * (1.0 / HD_C) + EPS) * g
        partner = jnp.where(first, pltpu.roll(y, HD_CP - 16, 1), pltpu.roll(y, 16, 1))
        return y * cos + partner * sin

    for h in range(H_C):
        seg = slice(h * HD_CP, (h + 1) * HD_CP)
        q_ref[:, seg] = (norm_rope(cq_ref[:, seg], qg_ref[...]) * (HD_C ** -0.5)).astype(BF16)
    for g in range(KV_C):
        seg = slice(g * HD_CP, (g + 1) * HD_CP)
        k_ref[:, seg] = norm_rope(ckv_ref[:, seg], kg_ref[...]).astype(BF16)
    v_ref[...] = ckv_ref[:, KV_C * HD_CP:2 * KV_C * HD_CP].astype(BF16)


def _attnprep(proj, qg, kg, cos_t, sin_t):
    t = proj.shape[0]
    tm = BLK
    qw, kw = H_C * HD_CP, KV_C * HD_CP
    return pl.pallas_call(
        _attnprep_kernel,
        grid=(t // tm,),
        in_specs=[pl.BlockSpec((tm, qw), lambda i: (i, COL_CQ // qw)),
                  pl.BlockSpec((tm, 2 * kw), lambda i: (i, COL_CKV // (2 * kw))),
                  pl.BlockSpec((1, HD_CP), lambda i: (0, 0)),
                  pl.BlockSpec((1, HD_CP), lambda i: (0, 0)),
                  pl.BlockSpec((tm, HD_CP), lambda i: (i, 0)),
                  pl.BlockSpec((tm, HD_CP), lambda i: (i, 0))],
        out_specs=[pl.BlockSpec((tm, qw), lambda i: (i, 0)),
                   pl.BlockSpec((tm, kw), lambda i: (i, 0)),
                   pl.BlockSpec((tm, kw), lambda i: (i, 0))],
        out_shape=[jax.ShapeDtypeStruct((t, qw), BF16),
                   jax.ShapeDtypeStruct((t, kw), BF16),
                   jax.ShapeDtypeStruct((t, kw), BF16)],
        compiler_params=_cparams(("arbitrary",)),
        name="attn_prep",
    )(proj, proj, qg, kg, cos_t, sin_t)


def _attn_kernel(sink_ref, q_ref, kp_ref, kc_ref, kn_ref, kx_ref, vp_ref, vc_ref, vn_ref, vx_ref,
                 o_ref, *, nq):
    qi = pl.program_id(0)
    nctx = CTX // ATT_BLOCK
    latent = qi >= nctx
    ql = lax.broadcasted_iota(I32, (ATT_BLOCK, ATT_BLOCK), 0)
    kl = lax.broadcasted_iota(I32, (ATT_BLOCK, ATT_BLOCK), 1)
    ok_prev = jnp.logical_and(qi - 1 >= nctx, kl >= ql)
    ok_next = jnp.logical_and(jnp.logical_and(latent, qi + 1 <= nq - 1), kl <= ql)
    ok_cur = jnp.logical_and(latent, kl >= 0)
    bias = jnp.concatenate([jnp.where(ok_prev, 0.0, NEG), jnp.where(ok_cur, 0.0, NEG),
                            jnp.where(ok_next, 0.0, NEG),
                            jnp.zeros((ATT_BLOCK, CTX), F32)], axis=1)
    for g in range(KV_C):
        seg = slice(g * HD_CP, (g + 1) * HD_CP)
        kcat = jnp.concatenate([kp_ref[:, seg], kc_ref[:, seg], kn_ref[:, seg], kx_ref[:, seg]], axis=0)
        vcat = jnp.concatenate([vp_ref[:, seg], vc_ref[:, seg], vn_ref[:, seg], vx_ref[:, seg]], axis=0)
        heads = [g * REP_C + r for r in range(REP_C)]
        cols = [slice(h * HD_CP, (h + 1) * HD_CP) for h in heads]
        s = [lax.dot_general(q_ref[:, c], kcat, (((1,), (1,)), ((), ())), preferred_element_type=F32) + bias
             for c in cols]
        m = [jnp.maximum(jnp.max(s[i], axis=-1, keepdims=True), sink_ref[heads[i]]) for i in range(REP_C)]
        p = [jnp.exp(s[i] - m[i]) for i in range(REP_C)]
        den = [jnp.sum(p[i], axis=-1, keepdims=True) + jnp.exp(sink_ref[heads[i]] - m[i]) for i in range(REP_C)]
        o = [jnp.dot(p[i].astype(BF16), vcat, preferred_element_type=F32) for i in range(REP_C)]
        for i in range(REP_C):
            o_ref[:, cols[i]] = (o[i] / den[i]).astype(BF16)


def _attn(qr, kr, vr, sink):
    t = qr.shape[0]
    nq = t // ATT_BLOCK
    nctx = CTX // ATT_BLOCK
    qw, kw = H_C * HD_CP, KV_C * HD_CP
    prev = lambda i: (jnp.maximum(i - 1, nctx), 0)
    cur = lambda i: (i, 0)
    nxt = lambda i: (jnp.minimum(jnp.maximum(i + 1, nctx), nq - 1), 0)
    kv = lambda f: pl.BlockSpec((ATT_BLOCK, kw), f)
    ctxs = pl.BlockSpec((CTX, kw), lambda i: (0, 0))
    return pl.pallas_call(
        functools.partial(_attn_kernel, nq=nq),
        grid=(nq,),
        in_specs=[pl.BlockSpec(memory_space=pltpu.SMEM),
                  pl.BlockSpec((ATT_BLOCK, qw), cur),
                  kv(prev), kv(cur), kv(nxt), ctxs, kv(prev), kv(cur), kv(nxt), ctxs],
        out_specs=pl.BlockSpec((ATT_BLOCK, qw), cur),
        out_shape=jax.ShapeDtypeStruct((t, qw), BF16),
        compiler_params=_cparams(("arbitrary",)),
        name="attn",
    )(sink, qr, kr, kr, kr, kr, vr, vr, vr, vr)


def _head_rms_gate(o, z, g):
    outs = []
    for h in range(o.shape[1] // 128):
        seg = o[:, h * 128:(h + 1) * 128]
        nrm = seg * lax.rsqrt(jnp.mean(seg * seg, axis=-1, keepdims=True) + EPS) * g
        outs.append(nrm * _silu(z[:, h * 128:(h + 1) * 128]))
    return jnp.concatenate(outs, axis=1)


def _merge_kernel(oaf, oab, az, obf, obb, bz, yc, gl, xs, dng, glag, wpa, wpb, wpc, wo, mod_ref,
                  o_ref, *, tm):
    i = pl.program_id(0)
    ya = _head_rms_gate(oaf[...] + oab[...], az[...], dng[...])
    yb = _head_rms_gate(obf[...] + obb[...], bz[...], glag[...])
    pa = _bdot(ya, wpa[...])
    pb = _bdot(yb, wpb[...])
    pc = jnp.dot(yc[...], wpc[...], preferred_element_type=F32)
    merged = (_sigmoid(gl[:, 0:D_MODEL]) * pa + _sigmoid(gl[:, D_MODEL:2 * D_MODEL]) * pb
              + _sigmoid(gl[:, 2 * D_MODEL:3 * D_MODEL]) * pc)
    y = _bdot(merged, wo[...])
    o_ref[...] = xs[...] + y * _res_gate(mod_ref, 0, tm, i * tm)


def _merge(oaf, oab, obf, obb, yc, proj, xs, dng, glag, wpa, wpb, wpc, wo, mod):
    t = xs.shape[0]
    tm = BLK
    row = lambda w, c=0: pl.BlockSpec((tm, w), lambda i: (i, c))
    full = lambda a: pl.BlockSpec(a.shape, lambda i: (0,) * a.ndim)
    return pl.pallas_call(
        functools.partial(_merge_kernel, tm=tm),
        grid=(t // tm,),
        in_specs=[row(A_WIDTH), row(A_WIDTH), row(A_WIDTH, COL_AZ // A_WIDTH),
                  row(B_VWIDTH), row(B_VWIDTH), row(B_VWIDTH, COL_BZ // B_VWIDTH),
                  row(H_C * HD_CP), row(3 * D_MODEL, COL_GL // (3 * D_MODEL)), row(D_MODEL),
                  full(dng), full(glag), full(wpa), full(wpb), full(wpc), full(wo), full(mod)],
        out_specs=row(D_MODEL),
        out_shape=jax.ShapeDtypeStruct((t, D_MODEL), F32),
        compiler_params=_cparams(("arbitrary",)),
        name="merge",
    )(oaf, oab, proj, obf, obb, proj, yc, proj, xs, dng, glag, wpa, wpb, wpc, wo, mod)


def _router_kernel(x_ref, mod_ref, g_ref, rwt_ref, rb_ref, h_ref, key_ref, gk_ref, cnt_ref, *, tm):
    i = pl.program_id(0)
    h = _norm_mod(x_ref[...], g_ref[...], mod_ref, 3 * D_MODEL, i * tm)
    h_ref[...] = h
    logit = lax.dot_general(rwt_ref[...], h, (((1,), (1,)), ((), ())),
                            precision=lax.Precision.HIGHEST, preferred_element_type=F32) + rb_ref[...]
    erow = lax.broadcasted_iota(I32, (N_EXPERTS, tm), 0)
    vals, idxs, sels = [], [], []
    cur = logit
    for _ in range(TOP_K):
        m = jnp.max(cur, axis=0, keepdims=True)
        idx = jnp.min(jnp.where(cur == m, erow, N_EXPERTS), axis=0, keepdims=True)
        sel = erow == idx
        vals.append(m)
        idxs.append(idx)
        sels.append(sel)
        cur = jnp.where(sel, -jnp.inf, cur)
    ex = [jnp.exp(v - vals[0]) for v in vals]
    den = ex[0] + ex[1] + ex[2] + ex[3]
    onehot = jnp.where(sels[0] | sels[1] | sels[2] | sels[3], 1.0, 0.0).astype(F32)
    ss = lax.broadcasted_iota(I32, (tm, tm), 0)
    tt = lax.broadcasted_iota(I32, (tm, tm), 1)
    before = jnp.where(ss < tt, 1.0, 0.0).astype(BF16)
    cnt = jnp.dot(onehot.astype(BF16), before, preferred_element_type=F32)
    ranks = [jnp.sum(jnp.where(s, cnt, 0.0), axis=0, keepdims=True) for s in sels]
    keys = [idxs[k] * tm + ranks[k].astype(I32) for k in range(TOP_K)]
    key_ref[...] = jnp.concatenate(keys + [jnp.full((8 - TOP_K, tm), -1, I32)], axis=0)
    cols = jnp.concatenate([e / den for e in ex] + [k.astype(F32) for k in keys]
                           + [jnp.zeros((128 - 2 * TOP_K, tm), F32)], axis=0)
    gk_ref[...] = cols.T
    cnt_ref[...] = jnp.broadcast_to(jnp.sum(onehot, axis=1, keepdims=True), (N_EXPERTS, 128))


def _router(xs, mod, g, rwt, rb):
    t = xs.shape[0]
    tm = BLK
    nt = t // tm
    return pl.pallas_call(
        functools.partial(_router_kernel, tm=tm),
        grid=(nt,),
        in_specs=[pl.BlockSpec((tm, D_MODEL), lambda i: (i, 0)),
                  pl.BlockSpec((8, 6 * D_MODEL), lambda i: (0, 0)),
                  pl.BlockSpec((1, D_MODEL), lambda i: (0, 0)),
                  pl.BlockSpec((N_EXPERTS, D_MODEL), lambda i: (0, 0)),
                  pl.BlockSpec((N_EXPERTS, 1), lambda i: (0, 0))],
        out_specs=[pl.BlockSpec((tm, D_MODEL), lambda i: (i, 0)),
                   pl.BlockSpec((8, tm), lambda i: (0, i)),
                   pl.BlockSpec((tm, 128), lambda i: (i, 0)),
                   pl.BlockSpec((None, N_EXPERTS, 128), lambda i: (i, 0, 0))],
        out_shape=[jax.ShapeDtypeStruct((t, D_MODEL), F32),
                   jax.ShapeDtypeStruct((8, t), I32),
                   jax.ShapeDtypeStruct((t, 128), F32),
                   jax.ShapeDtypeStruct((nt, N_EXPERTS, 128), F32)],
        compiler_params=_cparams(("arbitrary",)),
        name="router",
    )(xs, mod, g, rwt, rb)


RUN_CAP = 64
RUN_SHIFT = 6
RUN_BITS = (8, 4, 2, 1)


def _run_pieces(n8, nmax):
    n = jnp.minimum(n8, nmax)
    return [((n & bit) != 0, 8 * bit, pl.multiple_of(8 * (n & ~(2 * bit - 1)), 8)) for bit in RUN_BITS]


def _start_runs(i, base_ref, n8_ref, src_of, dst_of, sem):
    cap8 = RUN_CAP // 8
    total8 = jnp.int32(0)
    for e in range(N_EXPERTS):
        b = pl.multiple_of(base_ref[i * N_EXPERTS + e], 8)
        n = jnp.minimum(n8_ref[i * N_EXPERTS + e], cap8)

        def make_case(j, e=e, b=b):
            def case():
                off = 0
                for bit in RUN_BITS:
                    if j & bit:
                        pltpu.make_async_copy(src_of(e, b, off, 8 * bit), dst_of(e, b, off, 8 * bit), sem).start()
                        off += 8 * bit
                return jnp.int32(0)
            return case

        lax.switch(n, [make_case(j) for j in range(cap8 + 1)])
        total8 = total8 + n
    return total8


def _wait_rows(total8, desc_of, nbits):
    for j in range(nbits):
        @pl.when(((total8 >> j) & 1) == 1)
        def _():
            desc_of(8 << j).wait()


def _dispatch_kernel(base_ref, n8_ref, zs_ref, key_ref, h_ref, xs_ref, runs, ovf, zbuf, sem, zsem, *, tm, nzero):
    i = pl.program_id(0)

    @pl.when(i == 0)
    def _():
        zbuf[...] = jnp.zeros_like(zbuf)
        for z in range(nzero):
            @pl.when(zs_ref[z] >= 0)
            def _():
                pltpu.make_async_copy(zbuf, xs_ref.at[pl.ds(pl.multiple_of(zs_ref[z], MOE_BM), MOE_BM), :],
                                      zsem).start()
        for z in range(nzero):
            @pl.when(zs_ref[z] >= 0)
            def _():
                pltpu.make_async_copy(zbuf, xs_ref.at[pl.ds(0, MOE_BM), :], zsem).wait()

    hb = h_ref[...].astype(BF16)
    keys = [key_ref[k:k + 1, :] for k in range(TOP_K)]
    nrow = N_EXPERTS * RUN_CAP
    rr = lax.broadcasted_iota(I32, (nrow, tm), 0)
    rowkey = (rr >> RUN_SHIFT) * tm + (rr & (RUN_CAP - 1))
    hit = (rowkey == keys[0]) | (rowkey == keys[1]) | (rowkey == keys[2]) | (rowkey == keys[3])
    runs[...] = jnp.dot(jnp.where(hit, 1.0, 0.0).astype(BF16), hb, preferred_element_type=F32)
    total8 = _start_runs(i, base_ref, n8_ref,
                         lambda e, b, off, size: runs.at[pl.ds(e * RUN_CAP + off, size), :],
                         lambda e, b, off, size: xs_ref.at[pl.ds(b + off, size), :], sem)
    _wait_rows(total8, lambda size: pltpu.make_async_copy(runs.at[pl.ds(0, size), :],
                                                          xs_ref.at[pl.ds(0, size), :], sem), 9)

    def ovf_body(e, carry):
        n8 = n8_ref[i * N_EXPERTS + e]
        b = pl.multiple_of(base_ref[i * N_EXPERTS + e], 8)
        for c in range(1, tm // RUN_CAP):
            @pl.when(n8 > c * (RUN_CAP // 8))
            def _():
                ck = lax.broadcasted_iota(I32, (RUN_CAP, tm), 0) + (e * tm + c * RUN_CAP)
                hit_c = (ck == keys[0]) | (ck == keys[1]) | (ck == keys[2]) | (ck == keys[3])
                ovf[...] = jnp.dot(jnp.where(hit_c, 1.0, 0.0).astype(BF16), hb, preferred_element_type=F32)
                for phase in (0, 1):
                    for live, size, off in _run_pieces(n8 - c * (RUN_CAP // 8), RUN_CAP // 8):
                        @pl.when(live)
                        def _():
                            cp = pltpu.make_async_copy(
                                ovf.at[pl.ds(off, size), :],
                                xs_ref.at[pl.ds(b + c * RUN_CAP + off, size), :], sem)
                            if phase == 0:
                                cp.start()
                            else:
                                cp.wait()
        return carry

    lax.fori_loop(0, N_EXPERTS, ovf_body, 0)


def _dispatch(plan, keys, h):
    base, n8, zstart, p_rows = plan
    t = h.shape[0]
    tm = BLK
    grid_spec = pltpu.PrefetchScalarGridSpec(
        num_scalar_prefetch=3,
        grid=(t // tm,),
        in_specs=[pl.BlockSpec((8, tm), lambda i, b, n, z: (0, i)),
                  pl.BlockSpec((tm, D_MODEL), lambda i, b, n, z: (i, 0))],
        out_specs=pl.BlockSpec(memory_space=pl.ANY),
        scratch_shapes=[pltpu.VMEM((N_EXPERTS * RUN_CAP, D_MODEL), F32), pltpu.VMEM((RUN_CAP, D_MODEL), F32),
                        pltpu.VMEM((MOE_BM, D_MODEL), F32), pltpu.SemaphoreType.DMA(()),
                        pltpu.SemaphoreType.DMA(())],
    )
    return pl.pallas_call(
        functools.partial(_dispatch_kernel, tm=tm, nzero=zstart.shape[0]),
        grid_spec=grid_spec,
        out_shape=jax.ShapeDtypeStruct((p_rows, D_MODEL), F32),
        compiler_params=_cparams(("arbitrary",)),
        name="moe_dispatch",
    )(base, n8, zstart, keys, h)


def _expert_kernel(be_ref, nused_ref, first_ref, slot_ref, nxt_ref, x_ref, wgu_hbm, bgu_ref, wdn_hbm, bdn_ref,
                   y_ref, gu_stage, dn_stage, wgu_bf, wdn_bf, sem, *, layer):
    b = pl.program_id(0)

    def fetch(e, s):
        return (pltpu.make_async_copy(wgu_hbm.at[layer, e], gu_stage.at[s], sem.at[0, s]),
                pltpu.make_async_copy(wdn_hbm.at[layer, e], dn_stage.at[s], sem.at[1, s]))

    @pl.when(b == 0)
    def _():
        for cp in fetch(be_ref[0], 0):
            cp.start()

    @pl.when(first_ref[b] == 1)
    def _():
        s = slot_ref[b]
        for cp in fetch(be_ref[b], s):
            cp.wait()
        wgu_bf[...] = gu_stage[s].astype(BF16)
        wdn_bf[...] = dn_stage[s].astype(BF16)

        @pl.when(nxt_ref[b] >= 0)
        def _():
            for cp in fetch(nxt_ref[b], 1 - s):
                cp.start()

    @pl.when(b < nused_ref[0])
    def _():
        gu = jnp.dot(x_ref[...].astype(BF16), wgu_bf[...], preferred_element_type=F32) + bgu_ref[...]
        g_ = jnp.minimum(gu[:, :D_FF], SWIGLU_LIMIT)
        u_ = jnp.clip(gu[:, D_FF:], -SWIGLU_LIMIT, SWIGLU_LIMIT)
        act = (u_ + 1.0) * (g_ * _sigmoid(g_ * SWIGLU_ALPHA))
        y_ref[...] = jnp.dot(act.astype(BF16), wdn_bf[...], preferred_element_type=F32) + bdn_ref[...]

    @pl.when(b >= nused_ref[0])
    def _():
        y_ref[...] = jnp.zeros_like(y_ref)


def _experts(plan, xsorted, w_gu, b_gu, w_dn, b_dn, layer):
    blk_e, nused, first, slot, nxt = plan
    p_rows = xsorted.shape[0]
    nblk = p_rows // MOE_BM
    depth = w_gu.shape[0]
    bsel = lambda b, be, nu, fi, sl, nx: (layer, be[b], 0, 0)
    grid_spec = pltpu.PrefetchScalarGridSpec(
        num_scalar_prefetch=5,
        grid=(nblk,),
        in_specs=[pl.BlockSpec((MOE_BM, D_MODEL), lambda b, be, nu, fi, sl, nx: (jnp.minimum(b, nu[0] - 1), 0)),
                  pl.BlockSpec(memory_space=pl.ANY),
                  pl.BlockSpec((None, None, 1, 2 * D_FF), bsel),
                  pl.BlockSpec(memory_space=pl.ANY),
                  pl.BlockSpec((None, None, 1, D_MODEL), bsel)],
        out_specs=pl.BlockSpec((MOE_BM, D_MODEL), lambda b, be, nu, fi, sl, nx: (b, 0)),
        scratch_shapes=[pltpu.VMEM((2, D_MODEL, 2 * D_FF), F32), pltpu.VMEM((2, D_FF, D_MODEL), F32),
                        pltpu.VMEM((D_MODEL, 2 * D_FF), BF16), pltpu.VMEM((D_FF, D_MODEL), BF16),
                        pltpu.SemaphoreType.DMA((2, 2))],
    )
    return pl.pallas_call(
        functools.partial(_expert_kernel, layer=layer),
        grid_spec=grid_spec,
        out_shape=jax.ShapeDtypeStruct((p_rows, D_MODEL), F32),
        compiler_params=_cparams(("arbitrary",)),
        name="moe_experts",
    )(blk_e, nused, first, slot, nxt, xsorted, w_gu, b_gu.reshape(depth, N_EXPERTS, 1, 2 * D_FF), w_dn,
      b_dn.reshape(depth, N_EXPERTS, 1, D_MODEL))


def _combine_kernel(base_ref, n8_ref, y_ref, gk_ref, xs_ref, mod_ref, o_ref, runs, ovf, sem, *, tm):
    i = pl.program_id(0)

    @pl.when(i == 0)
    def _():
        runs[...] = jnp.zeros_like(runs)
        ovf[...] = jnp.zeros_like(ovf)

    total8 = _start_runs(i, base_ref, n8_ref,
                         lambda e, b, off, size: y_ref.at[pl.ds(b + off, size), :],
                         lambda e, b, off, size: runs.at[pl.ds(e * RUN_CAP + off, size), :], sem)
    _wait_rows(total8, lambda size: pltpu.make_async_copy(y_ref.at[pl.ds(0, size), :],
                                                          runs.at[pl.ds(0, size), :], sem), 9)
    gk = gk_ref[...]
    gates = [gk[:, k:k + 1] for k in range(TOP_K)]
    keys = [gk[:, TOP_K + k:TOP_K + k + 1].astype(I32) for k in range(TOP_K)]
    nrow = N_EXPERTS * RUN_CAP
    cc = lax.broadcasted_iota(I32, (1, nrow), 1)
    colkey = (cc >> RUN_SHIFT) * tm + (cc & (RUN_CAP - 1))
    w = jnp.where(colkey == keys[0], gates[0], 0.0)
    for k in range(1, TOP_K):
        w = w + jnp.where(colkey == keys[k], gates[k], 0.0)
    acc = jnp.dot(w.astype(BF16), runs[...].astype(BF16), preferred_element_type=F32)
    o_ref[...] = xs_ref[...] + acc * _res_gate(mod_ref, 3 * D_MODEL, tm, i * tm)

    def ovf_body(e, carry):
        n8 = n8_ref[i * N_EXPERTS + e]
        b = pl.multiple_of(base_ref[i * N_EXPERTS + e], 8)
        for c in range(1, tm // RUN_CAP):
            @pl.when(n8 > c * (RUN_CAP // 8))
            def _():
                for phase in (0, 1):
                    for live, size, off in _run_pieces(n8 - c * (RUN_CAP // 8), RUN_CAP // 8):
                        @pl.when(live)
                        def _():
                            cp = pltpu.make_async_copy(
                                y_ref.at[pl.ds(b + c * RUN_CAP + off, size), :],
                                ovf.at[pl.ds(off, size), :], sem)
                            if phase == 0:
                                cp.start()
                            else:
                                cp.wait()
                ck = lax.broadcasted_iota(I32, (1, RUN_CAP), 1) + (e * tm + c * RUN_CAP)
                wc = jnp.where(ck == keys[0], gates[0], 0.0)
                for k in range(1, TOP_K):
                    wc = wc + jnp.where(ck == keys[k], gates[k], 0.0)
                part = jnp.dot(wc.astype(BF16), ovf[...].astype(BF16), preferred_element_type=F32)
                o_ref[...] = o_ref[...] + part * _res_gate(mod_ref, 3 * D_MODEL, tm, i * tm)
        return carry

    lax.fori_loop(0, N_EXPERTS, ovf_body, 0)


def _combine(plan, y, gk_col, xs, mod):
    base, n8, _, _ = plan
    t = xs.shape[0]
    tm = BLK
    grid_spec = pltpu.PrefetchScalarGridSpec(
        num_scalar_prefetch=2,
        grid=(t // tm,),
        in_specs=[pl.BlockSpec(memory_space=pl.ANY),
                  pl.BlockSpec((tm, 128), lambda i, b, n: (i, 0)),
                  pl.BlockSpec((tm, D_MODEL), lambda i, b, n: (i, 0)),
                  pl.BlockSpec((8, 6 * D_MODEL), lambda i, b, n: (0, 0))],
        out_specs=pl.BlockSpec((tm, D_MODEL), lambda i, b, n: (i, 0)),
        scratch_shapes=[pltpu.VMEM((N_EXPERTS * RUN_CAP, D_MODEL), F32), pltpu.VMEM((RUN_CAP, D_MODEL), F32),
                        pltpu.SemaphoreType.DMA(())],
    )
    return pl.pallas_call(
        functools.partial(_combine_kernel, tm=tm),
        grid_spec=grid_spec,
        out_shape=jax.ShapeDtypeStruct((t, D_MODEL), F32),
        compiler_params=_cparams(("arbitrary",)),
        name="moe_combine",
    )(base, n8, y, gk_col, xs, mod)


def _pad_heads(w, nh, hd, hdp):
    d = w.shape[0]
    return jnp.pad(w.reshape(d, nh, hd), ((0, 0), (0, 0), (0, hdp - hd))).reshape(d, nh * hdp)


def _layout_w_in(w):
    return _layout_w_in_f32(w).astype(BF16)


def _layout_w_in_f32(w):
    pts = np.cumsum([A_WIDTH] * 4 + [2 * H_A, 2 * H_A, B_KWIDTH, B_KWIDTH, B_VWIDTH, B_VWIDTH,
                                     2 * GLA_RANK, C_WIDTH, KV_C * HD_C, KV_C * HD_C])
    (aq, ak, av, az, aa, ab, bq, bk, bv, bz, bg, cq, ck, cv, gl) = jnp.split(w, pts.tolist(), axis=1)
    small = jnp.concatenate([aa, ab, bg], axis=1)
    small = jnp.pad(small, ((0, 0), (0, 256 - small.shape[1])))
    cols = [gl, aq, ak, av, az, _pad_heads(cq, H_C, HD_C, HD_CP), _pad_heads(ck, KV_C, HD_C, HD_CP),
            _pad_heads(cv, KV_C, HD_C, HD_CP), bq, bk, bv, bz, small]
    out = jnp.concatenate(cols, axis=1)
    assert out.shape[1] == IN_COLS_P
    return out


def _rope_tables(t):
    s_len = t - CTX
    half = HD_C // 2
    inv_freq = ROPE_THETA ** (-jnp.arange(0, half, 2, dtype=F32) / half)
    pos = jnp.arange(s_len)
    rows = (pos // GRID_W).astype(F32)[:, None] * inv_freq[None, :]
    cols = (pos % GRID_W).astype(F32)[:, None] * inv_freq[None, :]
    cr, sr, cc, sc = jnp.cos(rows), jnp.sin(rows), jnp.cos(cols), jnp.sin(cols)
    zpad = jnp.zeros((s_len, HD_CP - HD_C), F32)
    cos_l = jnp.concatenate([cr, cr, cc, cc, zpad], axis=1)
    sin_l = jnp.concatenate([-sr, sr, -sc, sc, zpad], axis=1)
    cos_t = jnp.concatenate([jnp.ones((CTX, HD_CP), F32), cos_l], axis=0)
    sin_t = jnp.concatenate([jnp.zeros((CTX, HD_CP), F32), sin_l], axis=0)
    return cos_t, sin_t


def _lane_vec(v, width=128):
    v = v.reshape(1, -1).astype(F32)
    return jnp.pad(v, ((0, 0), (0, width - v.shape[1])))


def _moe_plan(cnt_tile, tk):
    nt = cnt_tile.shape[0]
    pad8 = (cnt_tile + 7) // 8 * 8
    total = jnp.sum(pad8, axis=0)
    padded = (total + MOE_BM - 1) // MOE_BM * MOE_BM
    pend = jnp.cumsum(padded)
    estart = pend - padded
    base = estart[None, :] + jnp.cumsum(pad8, axis=0) - pad8
    nblk = (tk + nt * N_EXPERTS * 7 + N_EXPERTS * (MOE_BM - 1) + MOE_BM - 1) // MOE_BM
    p_rows = nblk * MOE_BM
    blk = jnp.arange(nblk, dtype=I32)
    blk_e = jnp.minimum(jnp.sum((pend[None, :] <= (blk * MOE_BM)[:, None]).astype(I32), axis=1), N_EXPERTS - 1)
    nused = (pend[-1] // MOE_BM).astype(I32)
    prev_e = jnp.concatenate([jnp.full((1,), -1, I32), blk_e[:-1]])
    first = ((blk_e != prev_e) & (blk < nused)).astype(I32)
    slot = (jnp.cumsum(first) - 1) % 2
    pos = jnp.where(first == 1, blk, nblk)
    nxt_pos = jnp.concatenate([lax.cummin(pos, axis=0, reverse=True)[1:], jnp.full((1,), nblk, I32)])
    nxt = jnp.where(nxt_pos < nblk, blk_e[jnp.minimum(nxt_pos, nblk - 1)], -1)
    last_blk = jnp.where(total > 0, pend - MOE_BM, -1)
    ntail = (nt * N_EXPERTS * 7 + N_EXPERTS * (MOE_BM - 1)) // MOE_BM + 1
    tail = pend[-1] + jnp.arange(ntail, dtype=I32) * MOE_BM
    zstart = jnp.concatenate([last_blk, jnp.where(tail < p_rows, tail, -1)]).astype(I32)
    eplan = (blk_e, nused.reshape(1), first, slot.astype(I32), nxt.astype(I32))
    mplan = (base.reshape(-1).astype(I32), (pad8 // 8).reshape(-1).astype(I32), zstart, p_rows)
    return mplan, eplan


def kernel(x, c, ctx, c_ctx, ada_w, ada_b, norm_mix_g, norm_ffn_g, w_in, dn_conv_w, dn_a_log, dn_dt_bias,
           dn_norm_g, gla_w2, gla_b2, gla_norm_g, attn_q_norm_g, attn_k_norm_g, attn_sink, w_branch_a,
           w_branch_b, w_branch_c, w_out, router_w, router_b, w_gate_up, b_gate_up, w_down, b_down):
    assert x.shape[0] == 1 and c.shape[0] == 1 and ctx.shape[1] == CTX
    depth = ada_w.shape[0]
    xs = jnp.concatenate([ctx[0], x[0]], axis=0)
    t = xs.shape[0]
    assert t % BLK == 0 and (t - CTX) % GRID_W == 0
    cc = jnp.concatenate([c, c_ctx[None, :], jnp.zeros((6, D_MODEL), F32)], axis=0)
    mods = _ada_mod(cc, ada_w, ada_b)
    cos_t, sin_t = _rope_tables(t)
    for l in range(depth):
        mod = mods[l]
        proj = _inproj(xs, mod, norm_mix_g[l][None, :], _layout_w_in(w_in[l]))
        qa, ka, va, gcol = _dnprep(proj, dn_conv_w[l], _lane_vec(dn_a_log[l]), _lane_vec(dn_dt_bias[l]))
        oaf, oab = _dnscan(qa, ka, va, gcol)
        w2 = gla_w2[l].astype(F32)
        w2full = jnp.zeros((128, 2 * B_KWIDTH), F32)
        for d in range(2):
            r0 = 4 * H_A + d * GLA_RANK
            w2full = w2full.at[r0:r0 + GLA_RANK, d * B_KWIDTH:(d + 1) * B_KWIDTH].set(w2[d])
        obf, obb = _glascan(proj, w2full, gla_b2[l].reshape(1, 2 * B_KWIDTH).astype(F32))
        qg = _lane_vec(attn_q_norm_g[l])
        kg = _lane_vec(attn_k_norm_g[l])
        qr, kr, vr = _attnprep(proj, qg, kg, cos_t, sin_t)
        yc = _attn(qr, kr, vr, attn_sink[l].astype(F32))
        wpc = jnp.pad(w_branch_c[l].reshape(H_C, HD_C, D_MODEL),
                      ((0, 0), (0, HD_CP - HD_C), (0, 0))).reshape(H_C * HD_CP, D_MODEL)
        xs = _merge(oaf, oab, obf, obb, yc, proj, xs,
                    dn_norm_g[l][None, :], gla_norm_g[l][None, :],
                    w_branch_a[l].astype(BF16), w_branch_b[l].astype(BF16), wpc.astype(BF16),
                    w_out[l].astype(BF16), mod)
        h2, keys, gk_col, cnt = _router(xs, mod, norm_ffn_g[l][None, :], router_w[l].T, router_b[l][:, None])
        mplan, eplan = _moe_plan(cnt[:, :, 0].astype(I32), t * TOP_K)
        xsorted = _dispatch(mplan, keys, h2)
        y = _experts(eplan, xsorted, w_gate_up, b_gate_up, w_down, b_down, l)
        xs = _combine(mplan, y, gk_col, xs, mod)
    return xs[CTX:][None]
```

```python
import functools
import math

import jax
import jax.numpy as jnp
import numpy as np
from jax import lax
from jax.experimental import pallas as pl
from jax.experimental.pallas import tpu as pltpu

F32 = jnp.float32
BF16 = jnp.bfloat16
I32 = jnp.int32

D_MODEL = 1024
DEPTH = 4
GRID_W = 64
CTX = 256
H_A = 4
HD_A = 128
A_WIDTH = H_A * HD_A
CONV_W = 5
H_B = 4
DK_B = 64
DV_B = 128
B_KWIDTH = H_B * DK_B
B_VWIDTH = H_B * DV_B
GLA_RANK = 16
GLA_NORMALIZER = 16.0
H_C = 8
KV_C = 2
REP_C = H_C // KV_C
HD_C = 64
HD_CP = 128
C_WIDTH = H_C * HD_C
ATT_BLOCK = 128
ROPE_THETA = 10000.0
CHUNK = 64
SUB = 16
N_EXPERTS = 32
TOP_K = 4
D_FF = 1024
SWIGLU_LIMIT = 7.0
SWIGLU_ALPHA = 1.702
EPS = 1e-6
NEG = -1e30

BLK = 256
MOE_BM = 512
VMEM_LIMIT = 56 * 1024 * 1024

COL_GL = 0
COL_QKV = 3072
COL_AZ = 4608
COL_CQ = 5120
COL_CKV = 6144
COL_BQK = 6656
COL_BV = 7168
COL_BZ = 7680
COL_SMALL = 8192
IN_COLS_P = 8448
IN_TN = 768


def _pick(n, cands):
    for c in cands:
        if n % c == 0:
            return c
    raise ValueError(f"no tile for {n}")


def _cparams(sem):
    return pltpu.CompilerParams(dimension_semantics=sem, vmem_limit_bytes=VMEM_LIMIT)


def _bdot(a, b):
    return jnp.dot(a.astype(BF16), b.astype(BF16), preferred_element_type=F32)


def _bdot_nt(a, b):
    return lax.dot_general(a.astype(BF16), b.astype(BF16), (((1,), (1,)), ((), ())),
                           preferred_element_type=F32)


def _bdot_tn(a, b):
    return lax.dot_general(a.astype(BF16), b.astype(BF16), (((0,), (0,)), ((), ())),
                           preferred_element_type=F32)


def _fdot(a, b):
    return jnp.dot(a, b, precision=lax.Precision.HIGHEST, preferred_element_type=F32)


def _split2(a):
    hi = a.astype(BF16)
    lo = (a - hi.astype(F32)).astype(BF16)
    return hi, lo


def _dot3(a, b):
    ah, al = _split2(a)
    bh, bl = _split2(b)
    d = functools.partial(jnp.dot, preferred_element_type=F32)
    return d(ah, bh) + (d(ah, bl) + d(al, bh))


def _cumsum_dot(tri, x):
    t = tri.astype(BF16)
    hi = x.astype(BF16)
    r1 = x - hi.astype(F32)
    mid = r1.astype(BF16)
    lo = (r1 - mid.astype(F32)).astype(BF16)
    d = functools.partial(jnp.dot, preferred_element_type=F32)
    return d(t, hi) + (d(t, mid) + d(t, lo))


def _sigmoid(x):
    return 1.0 / (1.0 + jnp.exp(-x))


def _silu(x):
    return x * _sigmoid(x)


def _softplus(x):
    return jnp.maximum(x, 0.0) + jnp.log(1.0 + jnp.exp(-jnp.abs(x)))


def _ada_kernel(cc_ref, w_ref, b_ref, o_ref):
    o_ref[...] = _fdot(_silu(cc_ref[...]), w_ref[...]) + b_ref[...]


def _ada_mod(cc, ada_w, ada_b):
    depth = ada_w.shape[0]
    tn = 1536
    return pl.pallas_call(
        _ada_kernel,
        grid=(depth, 6 * D_MODEL // tn),
        in_specs=[pl.BlockSpec((8, D_MODEL), lambda l, j: (0, 0)),
                  pl.BlockSpec((None, D_MODEL, tn), lambda l, j: (l, 0, j)),
                  pl.BlockSpec((None, 1, tn), lambda l, j: (l, 0, j))],
        out_specs=pl.BlockSpec((None, 8, tn), lambda l, j: (l, 0, j)),
        out_shape=jax.ShapeDtypeStruct((depth, 8, 6 * D_MODEL), F32),
        compiler_params=_cparams(("arbitrary", "arbitrary")),
        name="ada_mod",
    )(cc, ada_w, ada_b.reshape(depth, 1, 6 * D_MODEL))


def _norm_mod(x, g, mod_ref, moff, row0):
    tm = x.shape[0]
    y = x * lax.rsqrt(jnp.mean(x * x, axis=-1, keepdims=True) + EPS) * g
    isc = (row0 + lax.broadcasted_iota(I32, (tm, 1), 0)) < CTX
    shift = jnp.where(isc, mod_ref[1:2, moff:moff + D_MODEL], mod_ref[0:1, moff:moff + D_MODEL])
    scale = jnp.where(isc, mod_ref[1:2, moff + D_MODEL:moff + 2 * D_MODEL],
                      mod_ref[0:1, moff + D_MODEL:moff + 2 * D_MODEL])
    return y * (1.0 + scale) + shift


def _res_gate(mod_ref, moff, tm, row0):
    isc = (row0 + lax.broadcasted_iota(I32, (tm, 1), 0)) < CTX
    return jnp.where(isc, mod_ref[1:2, moff + 2 * D_MODEL:moff + 3 * D_MODEL],
                     mod_ref[0:1, moff + 2 * D_MODEL:moff + 3 * D_MODEL])


def _inproj_kernel(x_ref, mod_ref, g_ref, w_ref, o_ref, h_ref, *, tm):
    i = pl.program_id(0)

    @pl.when(pl.program_id(1) == 0)
    def _():
        h_ref[...] = _norm_mod(x_ref[...], g_ref[...], mod_ref, 0, i * tm).astype(BF16)

    o_ref[...] = jnp.dot(h_ref[...], w_ref[...], preferred_element_type=F32)


def _inproj(xs, mod, g, w):
    t = xs.shape[0]
    tm = _pick(t, (1280, 640, 256))
    return pl.pallas_call(
        functools.partial(_inproj_kernel, tm=tm),
        grid=(t // tm, IN_COLS_P // IN_TN),
        in_specs=[pl.BlockSpec((tm, D_MODEL), lambda i, j: (i, 0)),
                  pl.BlockSpec((8, 6 * D_MODEL), lambda i, j: (0, 0)),
                  pl.BlockSpec((1, D_MODEL), lambda i, j: (0, 0)),
                  pl.BlockSpec((D_MODEL, IN_TN), lambda i, j: (0, j))],
        out_specs=pl.BlockSpec((tm, IN_TN), lambda i, j: (i, j)),
        out_shape=jax.ShapeDtypeStruct((t, IN_COLS_P), F32),
        scratch_shapes=[pltpu.VMEM((tm, D_MODEL), BF16)],
        compiler_params=_cparams(("arbitrary", "arbitrary")),
        name="inproj",
    )(xs, mod, g, w)


def _tri_blockdiag(n, lower):
    ii = lax.broadcasted_iota(I32, (n, n), 0)
    jj = lax.broadcasted_iota(I32, (n, n), 1)
    same = (ii >> 6) == (jj >> 6)
    tri = (ii >= jj) if lower else (ii <= jj)
    return jnp.where(same, jnp.where(tri, 1.0, 0.0), 0.0).astype(F32)


def _dnprep_kernel(main_ref, prev_ref, next_ref, small_ref, cw_ref, alog_ref, dtb_ref,
                   q_ref, k_ref, v_ref, gcol_ref, ext_ref, *, nb):
    i = pl.program_id(0)
    use_prev = i >= 2
    use_next = jnp.logical_and(i >= 1, i <= nb - 2)
    ext_ref[0:8, :] = jnp.where(use_prev, prev_ref[...], 0.0)
    ext_ref[8:8 + BLK, :] = main_ref[...]
    ext_ref[8 + BLK:16 + BLK, :] = jnp.where(use_next, next_ref[...], 0.0)
    acc = ext_ref[6:6 + BLK, :] * cw_ref[0:1, :]
    for d in range(1, CONV_W):
        acc = acc + ext_ref[6 + d:6 + d + BLK, :] * cw_ref[d:d + 1, :]
    s = _silu(acc)
    for h in range(H_A):
        for part, ref, mul in ((0, q_ref, HD_A ** -0.5), (1, k_ref, 1.0)):
            seg = s[:, part * A_WIDTH + h * HD_A: part * A_WIDTH + (h + 1) * HD_A]
            nrm = seg * lax.rsqrt(jnp.sum(seg * seg, axis=-1, keepdims=True) + EPS)
            ref[:, h * HD_A:(h + 1) * HD_A] = nrm * mul
    v_ref[...] = s[:, 2 * A_WIDTH:3 * A_WIDTH]
    sm = small_ref[...]
    lane = lax.broadcasted_iota(I32, sm.shape, 1)
    g = -jnp.exp(alog_ref[...]) * _softplus(sm + dtb_ref[...])
    gb = jnp.where(lane < 2 * H_A, g, jnp.where(lane < 4 * H_A, _sigmoid(sm), 0.0))
    cf = _fdot(_tri_blockdiag(BLK, True), gb)
    cr = _fdot(_tri_blockdiag(BLK, False), gb)
    gc = jnp.where(lane < H_A, cf, jnp.where(lane < 2 * H_A, cr, gb))
    gcol_ref[...] = gc


def _dnprep(proj, conv_w, alog_vec, dtb_vec):
    t = proj.shape[0]
    nb = t // BLK
    qkv_blk = COL_QKV // (3 * A_WIDTH)
    last8 = t // 8 - 1
    out_sds = jax.ShapeDtypeStruct((t, A_WIDTH), F32)
    return pl.pallas_call(
        functools.partial(_dnprep_kernel, nb=nb),
        grid=(nb,),
        in_specs=[pl.BlockSpec((BLK, 3 * A_WIDTH), lambda i: (i, qkv_blk)),
                  pl.BlockSpec((8, 3 * A_WIDTH), lambda i: (jnp.maximum(i * (BLK // 8) - 1, 0), qkv_blk)),
                  pl.BlockSpec((8, 3 * A_WIDTH), lambda i: (jnp.minimum((i + 1) * (BLK // 8), last8), qkv_blk)),
                  pl.BlockSpec((BLK, 128), lambda i: (i, COL_SMALL // 128)),
                  pl.BlockSpec((CONV_W, 3 * A_WIDTH), lambda i: (0, 0)),
                  pl.BlockSpec((1, 128), lambda i: (0, 0)),
                  pl.BlockSpec((1, 128), lambda i: (0, 0))],
        out_specs=[pl.BlockSpec((BLK, A_WIDTH), lambda i: (i, 0)),
                   pl.BlockSpec((BLK, A_WIDTH), lambda i: (i, 0)),
                   pl.BlockSpec((BLK, A_WIDTH), lambda i: (i, 0)),
                   pl.BlockSpec((BLK, 128), lambda i: (i, 0))],
        out_shape=[out_sds, out_sds, out_sds,
                   jax.ShapeDtypeStruct((t, 128), F32)],
        scratch_shapes=[pltpu.VMEM((BLK + 16, 3 * A_WIDTH), F32)],
        compiler_params=_cparams(("arbitrary",)),
        name="dn_prep",
    )(proj, proj, proj, proj, conv_w, alog_vec, dtb_vec)


def _dot3_all(a_list, b_list):
    d = functools.partial(jnp.dot, preferred_element_type=F32)
    sa = [_split2(a) for a in a_list]
    sb = [_split2(b) for b in b_list]
    hh = [d(a[0], b[0]) for a, b in zip(sa, sb)]
    hl = [d(a[0], b[1]) for a, b in zip(sa, sb)]
    lh = [d(a[1], b[0]) for a, b in zip(sa, sb)]
    return [x + (y + z) for x, y, z in zip(hh, hl, lh)]


def _unit_tri_inverse_all(l_mats, masks):
    eye, m_diag, m_l1, m_l2 = masks
    ld = [l * m_diag for l in l_mats]
    x = [eye - a for a in ld]
    p = _dot3_all(ld, ld)
    for it in range(3):
        xp = _dot3_all(x, p)
        if it < 2:
            p = _dot3_all(p, p)
        x = [a + b for a, b in zip(x, xp)]
    for m in (m_l1, m_l2):
        cx = _dot3_all([l * m for l in l_mats], x)
        xcx = _dot3_all(x, cx)
        x = [a - b for a, b in zip(x, xcx)]
    return x


def _dn_masks():
    ii = lax.broadcasted_iota(I32, (CHUNK, CHUNK), 0)
    jj = lax.broadcasted_iota(I32, (CHUNK, CHUNK), 1)
    one = lambda c: jnp.where(c, 1.0, 0.0).astype(F32)
    eye = one(ii == jj)
    m_diag = one((ii >> 4) == (jj >> 4))
    m_l2 = one((ii >> 5) != (jj >> 5))
    m_l1 = 1.0 - m_diag - m_l2
    return ii, jj, (eye, m_diag, m_l1, m_l2)


def _dn_local(items, ii, jj, masks):
    n = len(items)
    dec, lmat, qk, kb, eg = [], [], [], [], []
    for q, k, v, gcol, grow, bcol, fwd in items:
        incl = (ii >= jj) if fwd else (ii <= jj)
        dec.append(jnp.exp(jnp.where(incl, gcol - grow, NEG)))
        kb.append(k * bcol)
        eg.append(jnp.exp(gcol))
    kh = [it[1].astype(BF16) for it in items]
    kk = [_bdot_nt(kb[i], kh[i]) for i in range(n)]
    qkr = [_bdot_nt(items[i][0], kh[i]) for i in range(n)]
    for i in range(n):
        fwd = items[i][6]
        strict = (ii > jj) if fwd else (ii < jj)
        lmat.append(kk[i] * jnp.where(strict, dec[i], 0.0))
        qk.append((qkr[i] * dec[i]).astype(BF16))
    rhs = [jnp.concatenate([items[i][2] * items[i][5], kb[i] * eg[i]], axis=1) for i in range(n)]
    sol = _dot3_all(_unit_tri_inverse_all(lmat, masks), rhs)
    out = []
    for i in range(n):
        q, k, _, gcol, _, _, fwd = items[i]
        glast = gcol[CHUNK - 1:CHUNK, :] if fwd else gcol[0:1, :]
        out.append((sol[i][:, :HD_A], sol[i][:, HD_A:].astype(BF16), qk[i], (q * eg[i]).astype(BF16),
                    (k * jnp.exp(glast - gcol)).astype(BF16), jnp.exp(glast)))
    return out


def _dn_step(local, states):
    n = len(local)
    sb = [s.astype(BF16) for s in states]
    d = functools.partial(jnp.dot, preferred_element_type=F32)
    ws = [d(local[i][1], sb[i]) for i in range(n)]
    qs = [d(local[i][3], sb[i]) for i in range(n)]
    v_new = [(local[i][0] - ws[i]).astype(BF16) for i in range(n)]
    o2 = [d(local[i][2], v_new[i]) for i in range(n)]
    kv = [lax.dot_general(local[i][4], v_new[i], (((0,), (0,)), ((), ())), preferred_element_type=F32)
          for i in range(n)]
    return [qs[i] + o2[i] for i in range(n)], [states[i] * local[i][5] + kv[i] for i in range(n)]


def _dnscan_kernel(qf, kf, vf, gcf, qb, kb, vb, gcb, of_ref, ob_ref, s_ref):
    @pl.when(pl.program_id(0) == 0)
    def _():
        s_ref[...] = jnp.zeros_like(s_ref)

    ii, jj, masks = _dn_masks()
    nch = BLK // CHUNK
    pick = jnp.where(lax.broadcasted_iota(I32, (16, 128), 0) == lax.broadcasted_iota(I32, (16, 128), 1),
                     1.0, 0.0).astype(F32)

    dirs = ((True, (qf, kf, vf, gcf, of_ref)), (False, (qb, kb, vb, gcb, ob_ref)))
    items, sinks = [], []
    for step in range(nch):
        for fwd, (q_r, k_r, v_r, gc_r, o_r) in dirs:
            c = step if fwd else nch - 1 - step
            rows = slice(c * CHUNK, (c + 1) * CHUNK)
            d = 0 if fwd else 1
            gct = gc_r[rows, :]
            grows = lax.dot_general(pick, gct, (((1,), (1,)), ((), ())),
                                    precision=lax.Precision.HIGHEST, preferred_element_type=F32)
            for h in range(H_A):
                lanes = slice(h * HD_A, (h + 1) * HD_A)
                gi = d * H_A + h
                items.append((q_r[rows, lanes], k_r[rows, lanes], v_r[rows, lanes],
                              gct[:, gi:gi + 1], grows[gi:gi + 1, :],
                              gct[:, 2 * H_A + gi:2 * H_A + gi + 1], fwd))
                sinks.append((o_r, rows, lanes))
    local = _dn_local(items, ii, jj, masks)
    nchain = 2 * H_A
    states = [s_ref[gi] for gi in range(nchain)]
    for step in range(nch):
        outs, states = _dn_step(local[step * nchain:(step + 1) * nchain], states)
        for (o_r, rows, lanes), o in zip(sinks[step * nchain:(step + 1) * nchain], outs):
            o_r[rows, lanes] = o
    for gi in range(nchain):
        s_ref[gi] = states[gi]


def _rev_block(nb):
    return lambda i: jnp.where(i == 0, 0, nb - i)


def _dnscan(q, k, v, gcol):
    t = q.shape[0]
    nb = t // BLK
    rev = _rev_block(nb)
    wide = lambda f: pl.BlockSpec((BLK, A_WIDTH), lambda i: (f(i), 0))
    col = lambda f: pl.BlockSpec((BLK, 128), lambda i: (f(i), 0))
    ident = lambda i: i
    out_sds = jax.ShapeDtypeStruct((t, A_WIDTH), F32)
    return pl.pallas_call(
        _dnscan_kernel,
        grid=(nb,),
        in_specs=[wide(ident), wide(ident), wide(ident), col(ident),
                  wide(rev), wide(rev), wide(rev), col(rev)],
        out_specs=[wide(ident), wide(rev)],
        out_shape=[out_sds, out_sds],
        scratch_shapes=[pltpu.VMEM((2 * H_A, HD_A, HD_A), F32)],
        compiler_params=_cparams(("arbitrary",)),
        name="dn_scan",
    )(q, k, v, gcol, q, k, v, gcol)


def _gla_gates(small_ref, w2_ref, b2_ref, b_ref, d, fwd):
    cols = slice(d * B_KWIDTH, (d + 1) * B_KWIDTH)
    pre = _dot3(small_ref[...], w2_ref[:, cols]) + b2_ref[:, cols]
    gk = -_softplus(-pre) * (1.0 / GLA_NORMALIZER)
    b_ref[d] = _cumsum_dot(_tri_blockdiag(BLK, fwd), gk)


def _gla_chunk(qk_ref, v_ref, b_ref, o_ref, st_ref, d, fwd, consts, step):
    sel, headmask_k, st_mask = consts
    nch = BLK // CHUNK
    nsub = CHUNK // SUB
    sub_i = lax.broadcasted_iota(I32, (SUB, 1), 0)
    row_c = lax.broadcasted_iota(I32, (CHUNK, 1), 0)
    if True:
        c = step if fwd else nch - 1 - step
        rows = pl.ds(pl.multiple_of(c * CHUNK, CHUNK), CHUNK)
        q = qk_ref[rows, 0:B_KWIDTH] * (DK_B ** -0.5)
        k = qk_ref[rows, B_KWIDTH:2 * B_KWIDTH]
        v = v_ref[rows, :]
        b = b_ref[d, rows, :]
        vh = v.astype(BF16)
        st = st_ref[d]
        o = _bdot_nt(q * jnp.exp(b), st)
        refs = []
        for sb in range(nsub):
            if fwd:
                r = b[sb * SUB - 1:sb * SUB, :] if sb > 0 else jnp.zeros((1, B_KWIDTH), F32)
            else:
                r = b[(sb + 1) * SUB:(sb + 1) * SUB + 1, :] if sb < nsub - 1 else jnp.zeros((1, B_KWIDTH), F32)
            refs.append(r)
        rfull = jnp.concatenate([jnp.broadcast_to(r, (SUB, B_KWIDTH)) for r in refs], axis=0)
        qs = q * jnp.exp(b - rfull)
        a_off = [None] * H_B
        for sb in (range(1, nsub) if fwd else range(0, nsub - 1)):
            jmask = (row_c < sb * SUB) if fwd else (row_c >= (sb + 1) * SUB)
            ks = (k * jnp.exp(jnp.where(jmask, refs[sb] - b, NEG))).astype(BF16)
            rowmask = jnp.where((row_c >> 4) == sb, 1.0, 0.0)
            for h in range(H_B):
                a = _bdot_nt(qs * headmask_k[h], ks) * rowmask
                a_off[h] = a if a_off[h] is None else a_off[h] + a
        o = o + jnp.concatenate(
            [_bdot(a_off[h], vh[:, h * DV_B:(h + 1) * DV_B]) for h in range(H_B)], axis=1)
        diag = []
        for sb in range(nsub):
            s0 = sb * SUB
            bs, qsb, ksb = b[s0:s0 + SUB, :], q[s0:s0 + SUB, :], k[s0:s0 + SUB, :]
            tiles = []
            for jl in range(SUB):
                causal = (sub_i >= jl) if fwd else (sub_i <= jl)
                e = jnp.exp(jnp.where(causal, bs - bs[jl:jl + 1, :], NEG))
                tiles.append((qsb * ksb[jl:jl + 1, :] * e).astype(BF16))
            red = jnp.dot(jnp.concatenate(tiles, axis=0), sel, preferred_element_type=F32)
            acc = red[0:SUB, :] * v[s0:s0 + 1, :]
            for jl in range(1, SUB):
                acc = acc + red[jl * SUB:(jl + 1) * SUB, :] * v[s0 + jl:s0 + jl + 1, :]
            diag.append(acc)
        o_ref[rows, :] = o + jnp.concatenate(diag, axis=0)
        blast = b[CHUNK - 1:CHUNK, :] if fwd else b[0:1, :]
        kd = k * jnp.exp(blast - b)
        st_ref[d] = st * jnp.exp(blast) + _bdot_tn(v, kd) * st_mask


def _glascan_kernel(qkf, vf, smf, qkb, vb, smb, w2_ref, b2_ref, of_ref, ob_ref, st_ref, b_ref):
    @pl.when(pl.program_id(0) == 0)
    def _():
        st_ref[...] = jnp.zeros_like(st_ref)

    kk = lax.broadcasted_iota(I32, (B_KWIDTH, B_VWIDTH), 0)
    cc = lax.broadcasted_iota(I32, (B_KWIDTH, B_VWIDTH), 1)
    sel = jnp.where((kk >> 6) == (cc >> 7), 1.0, 0.0).astype(BF16)
    lane = lax.broadcasted_iota(I32, (1, B_KWIDTH), 1)
    headmask_k = [jnp.where((lane >> 6) == h, 1.0, 0.0).astype(F32) for h in range(H_B)]
    rr = lax.broadcasted_iota(I32, (B_VWIDTH, B_KWIDTH), 0)
    kc = lax.broadcasted_iota(I32, (B_VWIDTH, B_KWIDTH), 1)
    st_mask = jnp.where((rr >> 7) == (kc >> 6), 1.0, 0.0).astype(F32)
    consts = (sel, headmask_k, st_mask)
    _gla_gates(smf, w2_ref, b2_ref, b_ref, 0, True)
    _gla_gates(smb, w2_ref, b2_ref, b_ref, 1, False)

    def body(step, carry):
        _gla_chunk(qkf, vf, b_ref, of_ref, st_ref, 0, True, consts, step)
        _gla_chunk(qkb, vb, b_ref, ob_ref, st_ref, 1, False, consts, step)
        return carry

    lax.fori_loop(0, BLK // CHUNK, body, 0)


def _glascan(proj, w2full, b2full):
    t = proj.shape[0]
    nb = t // BLK
    rev = _rev_block(nb)
    ident = lambda i: i
    qk = lambda f: pl.BlockSpec((BLK, 2 * B_KWIDTH), lambda i: (f(i), COL_BQK // (2 * B_KWIDTH)))
    vv = lambda f: pl.BlockSpec((BLK, B_VWIDTH), lambda i: (f(i), COL_BV // B_VWIDTH))
    sm = lambda f: pl.BlockSpec((BLK, 128), lambda i: (f(i), COL_SMALL // 128))
    outs = lambda f: pl.BlockSpec((BLK, B_VWIDTH), lambda i: (f(i), 0))
    out_sds = jax.ShapeDtypeStruct((t, B_VWIDTH), F32)
    return pl.pallas_call(
        _glascan_kernel,
        grid=(nb,),
        in_specs=[qk(ident), vv(ident), sm(ident), qk(rev), vv(rev), sm(rev),
                  pl.BlockSpec((128, 2 * B_KWIDTH), lambda i: (0, 0)),
                  pl.BlockSpec((1, 2 * B_KWIDTH), lambda i: (0, 0))],
        out_specs=[outs(ident), outs(rev)],
        out_shape=[out_sds, out_sds],
        scratch_shapes=[pltpu.VMEM((2, B_VWIDTH, B_KWIDTH), F32), pltpu.VMEM((2, BLK, B_KWIDTH), F32)],
        compiler_params=_cparams(("arbitrary",)),
        name="gla_scan",
    )(proj, proj, proj, proj, proj, proj, w2full, b2full)


def _attnprep_kernel(cq_ref, ckv_ref, qg_ref, kg_ref, cos_ref, sin_ref, q_ref, k_ref, v_ref):
    cos = cos_ref[...]
    sin = sin_ref[...]
    lane = lax.broadcasted_iota(I32, cos.shape, 1)
    first = (lane % 32) < 16

    def norm_rope(x, g):
        y = x * lax.rsqrt(jnp.sum(x * x, axis=-1, keepdims=True) * (1.0 / HD_C) + EPS) * g
        partner = jnp.where(first, pltpu.roll(y, HD_CP - 16, 1), pltpu.roll(y, 16, 1))
        return y * cos + partner * sin

    for h in range(H_C):
        seg = slice(h * HD_CP, (h + 1) * HD_CP)
        q_ref[:, seg] = (norm_rope(cq_ref[:, seg], qg_ref[...]) * (HD_C ** -0.5)).astype(BF16)
    for g in range(KV_C):
        seg = slice(g * HD_CP, (g + 1) * HD_CP)
        k_ref[:, seg] = norm_rope(ckv_ref[:, seg], kg_ref[...]).astype(BF16)
    v_ref[...] = ckv_ref[:, KV_C * HD_CP:2 * KV_C * HD_CP].astype(BF16)


def _attnprep(proj, qg, kg, cos_t, sin_t):
    t = proj.shape[0]
    tm = BLK
    qw, kw = H_C * HD_CP, KV_C * HD_CP
    return pl.pallas_call(
        _attnprep_kernel,
        grid=(t // tm,),
        in_specs=[pl.BlockSpec((tm, qw), lambda i: (i, COL_CQ // qw)),
                  pl.BlockSpec((tm, 2 * kw), lambda i: (i, COL_CKV // (2 * kw))),
                  pl.BlockSpec((1, HD_CP), lambda i: (0, 0)),
                  pl.BlockSpec((1, HD_CP), lambda i: (0, 0)),
                  pl.BlockSpec((tm, HD_CP), lambda i: (i, 0)),
                  pl.BlockSpec((tm, HD_CP), lambda i: (i, 0))],
        out_specs=[pl.BlockSpec((tm, qw), lambda i: (i, 0)),
                   pl.BlockSpec((tm, kw), lambda i: (i, 0)),
                   pl.BlockSpec((tm, kw), lambda i: (i, 0))],
        out_shape=[jax.ShapeDtypeStruct((t, qw), BF16),
                   jax.ShapeDtypeStruct((t, kw), BF16),
                   jax.ShapeDtypeStruct((t, kw), BF16)],
        compiler_params=_cparams(("arbitrary",)),
        name="attn_prep",
    )(proj, proj, qg, kg, cos_t, sin_t)


def _attn_kernel(sink_ref, q_ref, kp_ref, kc_ref, kn_ref, kx_ref, vp_ref, vc_ref, vn_ref, vx_ref,
                 o_ref, *, nq):
    qi = pl.program_id(0)
    nctx = CTX // ATT_BLOCK
    latent = qi >= nctx
    ql = lax.broadcasted_iota(I32, (ATT_BLOCK, ATT_BLOCK), 0)
    kl = lax.broadcasted_iota(I32, (ATT_BLOCK, ATT_BLOCK), 1)
    ok_prev = jnp.logical_and(qi - 1 >= nctx, kl >= ql)
    ok_next = jnp.logical_and(jnp.logical_and(latent, qi + 1 <= nq - 1), kl <= ql)
    ok_cur = jnp.logical_and(latent, kl >= 0)
    bias = jnp.concatenate([jnp.where(ok_prev, 0.0, NEG), jnp.where(ok_cur, 0.0, NEG),
                            jnp.where(ok_next, 0.0, NEG),
                            jnp.zeros((ATT_BLOCK, CTX), F32)], axis=1)
    for g in range(KV_C):
        seg = slice(g * HD_CP, (g + 1) * HD_CP)
        kcat = jnp.concatenate([kp_ref[:, seg], kc_ref[:, seg], kn_ref[:, seg], kx_ref[:, seg]], axis=0)
        vcat = jnp.concatenate([vp_ref[:, seg], vc_ref[:, seg], vn_ref[:, seg], vx_ref[:, seg]], axis=0)
        heads = [g * REP_C + r for r in range(REP_C)]
        cols = [slice(h * HD_CP, (h + 1) * HD_CP) for h in heads]
        s = [lax.dot_general(q_ref[:, c], kcat, (((1,), (1,)), ((), ())), preferred_element_type=F32) + bias
             for c in cols]
        m = [jnp.maximum(jnp.max(s[i], axis=-1, keepdims=True), sink_ref[heads[i]]) for i in range(REP_C)]
        p = [jnp.exp(s[i] - m[i]) for i in range(REP_C)]
        den = [jnp.sum(p[i], axis=-1, keepdims=True) + jnp.exp(sink_ref[heads[i]] - m[i]) for i in range(REP_C)]
        o = [jnp.dot(p[i].astype(BF16), vcat, preferred_element_type=F32) for i in range(REP_C)]
        for i in range(REP_C):
            o_ref[:, cols[i]] = (o[i] / den[i]).astype(BF16)


def _attn(qr, kr, vr, sink):
    t = qr.shape[0]
    nq = t // ATT_BLOCK
    nctx = CTX // ATT_BLOCK
    qw, kw = H_C * HD_CP, KV_C * HD_CP
    prev = lambda i: (jnp.maximum(i - 1, nctx), 0)
    cur = lambda i: (i, 0)
    nxt = lambda i: (jnp.minimum(jnp.maximum(i + 1, nctx), nq - 1), 0)
    kv = lambda f: pl.BlockSpec((ATT_BLOCK, kw), f)
    ctxs = pl.BlockSpec((CTX, kw), lambda i: (0, 0))
    return pl.pallas_call(
        functools.partial(_attn_kernel, nq=nq),
        grid=(nq,),
        in_specs=[pl.BlockSpec(memory_space=pltpu.SMEM),
                  pl.BlockSpec((ATT_BLOCK, qw), cur),
                  kv(prev), kv(cur), kv(nxt), ctxs, kv(prev), kv(cur), kv(nxt), ctxs],
        out_specs=pl.BlockSpec((ATT_BLOCK, qw), cur),
        out_shape=jax.ShapeDtypeStruct((t, qw), BF16),
        compiler_params=_cparams(("arbitrary",)),
        name="attn",
    )(sink, qr, kr, kr, kr, kr, vr, vr, vr, vr)


def _head_rms_gate(o, z, g):
    outs = []
    for h in range(o.shape[1] // 128):
        seg = o[:, h * 128:(h + 1) * 128]
        nrm = seg * lax.rsqrt(jnp.mean(seg * seg, axis=-1, keepdims=True) + EPS) * g
        outs.append(nrm * _silu(z[:, h * 128:(h + 1) * 128]))
    return jnp.concatenate(outs, axis=1)


def _merge_kernel(oaf, oab, az, obf, obb, bz, yc, gl, xs, dng, glag, wpa, wpb, wpc, wo, mod_ref,
                  o_ref, *, tm):
    i = pl.program_id(0)
    ya = _head_rms_gate(oaf[...] + oab[...], az[...], dng[...])
    yb = _head_rms_gate(obf[...] + obb[...], bz[...], glag[...])
    pa = _bdot(ya, wpa[...])
    pb = _bdot(yb, wpb[...])
    pc = jnp.dot(yc[...], wpc[...], preferred_element_type=F32)
    merged = (_sigmoid(gl[:, 0:D_MODEL]) * pa + _sigmoid(gl[:, D_MODEL:2 * D_MODEL]) * pb
              + _sigmoid(gl[:, 2 * D_MODEL:3 * D_MODEL]) * pc)
    y = _bdot(merged, wo[...])
    o_ref[...] = xs[...] + y * _res_gate(mod_ref, 0, tm, i * tm)


def _merge(oaf, oab, obf, obb, yc, proj, xs, dng, glag, wpa, wpb, wpc, wo, mod):
    t = xs.shape[0]
    tm = BLK
    row = lambda w, c=0: pl.BlockSpec((tm, w), lambda i: (i, c))
    full = lambda a: pl.BlockSpec(a.shape, lambda i: (0,) * a.ndim)
    return pl.pallas_call(
        functools.partial(_merge_kernel, tm=tm),
        grid=(t // tm,),
        in_specs=[row(A_WIDTH), row(A_WIDTH), row(A_WIDTH, COL_AZ // A_WIDTH),
                  row(B_VWIDTH), row(B_VWIDTH), row(B_VWIDTH, COL_BZ // B_VWIDTH),
                  row(H_C * HD_CP), row(3 * D_MODEL, COL_GL // (3 * D_MODEL)), row(D_MODEL),
                  full(dng), full(glag), full(wpa), full(wpb), full(wpc), full(wo), full(mod)],
        out_specs=row(D_MODEL),
        out_shape=jax.ShapeDtypeStruct((t, D_MODEL), F32),
        compiler_params=_cparams(("arbitrary",)),
        name="merge",
    )(oaf, oab, proj, obf, obb, proj, yc, proj, xs, dng, glag, wpa, wpb, wpc, wo, mod)


def _router_kernel(x_ref, mod_ref, g_ref, rwt_ref, rb_ref, h_ref, key_ref, gk_ref, cnt_ref, *, tm):
    i = pl.program_id(0)
    h = _norm_mod(x_ref[...], g_ref[...], mod_ref, 3 * D_MODEL, i * tm)
    h_ref[...] = h
    logit = lax.dot_general(rwt_ref[...], h, (((1,), (1,)), ((), ())),
                            precision=lax.Precision.HIGHEST, preferred_element_type=F32) + rb_ref[...]
    erow = lax.broadcasted_iota(I32, (N_EXPERTS, tm), 0)
    vals, idxs, sels = [], [], []
    cur = logit
    for _ in range(TOP_K):
        m = jnp.max(cur, axis=0, keepdims=True)
        idx = jnp.min(jnp.where(cur == m, erow, N_EXPERTS), axis=0, keepdims=True)
        sel = erow == idx
        vals.append(m)
        idxs.append(idx)
        sels.append(sel)
        cur = jnp.where(sel, -jnp.inf, cur)
    ex = [jnp.exp(v - vals[0]) for v in vals]
    den = ex[0] + ex[1] + ex[2] + ex[3]
    onehot = jnp.where(sels[0] | sels[1] | sels[2] | sels[3], 1.0, 0.0).astype(F32)
    ss = lax.broadcasted_iota(I32, (tm, tm), 0)
    tt = lax.broadcasted_iota(I32, (tm, tm), 1)
    before = jnp.where(ss < tt, 1.0, 0.0).astype(BF16)
    cnt = jnp.dot(onehot.astype(BF16), before, preferred_element_type=F32)
    ranks = [jnp.sum(jnp.where(s, cnt, 0.0), axis=0, keepdims=True) for s in sels]
    keys = [idxs[k] * tm + ranks[k].astype(I32) for k in range(TOP_K)]
    key_ref[...] = jnp.concatenate(keys + [jnp.full((8 - TOP_K, tm), -1, I32)], axis=0)
    cols = jnp.concatenate([e / den for e in ex] + [k.astype(F32) for k in keys]
                           + [jnp.zeros((128 - 2 * TOP_K, tm), F32)], axis=0)
    gk_ref[...] = cols.T
    cnt_ref[...] = jnp.broadcast_to(jnp.sum(onehot, axis=1, keepdims=True), (N_EXPERTS, 128))


def _router(xs, mod, g, rwt, rb):
    t = xs.shape[0]
    tm = BLK
    nt = t // tm
    return pl.pallas_call(
        functools.partial(_router_kernel, tm=tm),
        grid=(nt,),
        in_specs=[pl.BlockSpec((tm, D_MODEL), lambda i: (i, 0)),
                  pl.BlockSpec((8, 6 * D_MODEL), lambda i: (0, 0)),
                  pl.BlockSpec((1, D_MODEL), lambda i: (0, 0)),
                  pl.BlockSpec((N_EXPERTS, D_MODEL), lambda i: (0, 0)),
                  pl.BlockSpec((N_EXPERTS, 1), lambda i: (0, 0))],
        out_specs=[pl.BlockSpec((tm, D_MODEL), lambda i: (i, 0)),
                   pl.BlockSpec((8, tm), lambda i: (0, i)),
                   pl.BlockSpec((tm, 128), lambda i: (i, 0)),
                   pl.BlockSpec((None, N_EXPERTS, 128), lambda i: (i, 0, 0))],
        out_shape=[jax.ShapeDtypeStruct((t, D_MODEL), F32),
                   jax.ShapeDtypeStruct((8, t), I32),
                   jax.ShapeDtypeStruct((t, 128), F32),
                   jax.ShapeDtypeStruct((nt, N_EXPERTS, 128), F32)],
        compiler_params=_cparams(("arbitrary",)),
        name="router",
    )(xs, mod, g, rwt, rb)


RUN_CAP = 64
RUN_SHIFT = 6
RUN_BITS = (8, 4, 2, 1)


def _run_pieces(n8, nmax):
    n = jnp.minimum(n8, nmax)
    return [((n & bit) != 0, 8 * bit, pl.multiple_of(8 * (n & ~(2 * bit - 1)), 8)) for bit in RUN_BITS]


def _start_runs(i, base_ref, n8_ref, src_of, dst_of, sem):
    cap8 = RUN_CAP // 8
    total8 = jnp.int32(0)
    for e in range(N_EXPERTS):
        b = pl.multiple_of(base_ref[i * N_EXPERTS + e], 8)
        n = jnp.minimum(n8_ref[i * N_EXPERTS + e], cap8)

        def make_case(j, e=e, b=b):
            def case():
                off = 0
                for bit in RUN_BITS:
                    if j & bit:
                        pltpu.make_async_copy(src_of(e, b, off, 8 * bit), dst_of(e, b, off, 8 * bit), sem).start()
                        off += 8 * bit
                return jnp.int32(0)
            return case

        lax.switch(n, [make_case(j) for j in range(cap8 + 1)])
        total8 = total8 + n
    return total8


def _wait_rows(total8, desc_of, nbits):
    for j in range(nbits):
        @pl.when(((total8 >> j) & 1) == 1)
        def _():
            desc_of(8 << j).wait()


def _dispatch_kernel(base_ref, n8_ref, zs_ref, key_ref, h_ref, xs_ref, runs, ovf, zbuf, pend_ref, sem, osem, zsem,
                     *, tm, nzero, nsteps):
    i = pl.program_id(0)

    @pl.when(i == 0)
    def _():
        zbuf[...] = jnp.zeros_like(zbuf)
        for z in range(nzero):
            @pl.when(zs_ref[z] >= 0)
            def _():
                pltpu.make_async_copy(zbuf, xs_ref.at[pl.ds(pl.multiple_of(zs_ref[z], MOE_BM), MOE_BM), :],
                                      zsem).start()
        for z in range(nzero):
            @pl.when(zs_ref[z] >= 0)
            def _():
                pltpu.make_async_copy(zbuf, xs_ref.at[pl.ds(0, MOE_BM), :], zsem).wait()

    hb = h_ref[...].astype(BF16)
    keys = [key_ref[k:k + 1, :] for k in range(TOP_K)]
    nrow = N_EXPERTS * RUN_CAP
    rr = lax.broadcasted_iota(I32, (nrow, tm), 0)
    rowkey = (rr >> RUN_SHIFT) * tm + (rr & (RUN_CAP - 1))
    hit = (rowkey == keys[0]) | (rowkey == keys[1]) | (rowkey == keys[2]) | (rowkey == keys[3])
    slot = i % 2
    runs[slot] = jnp.dot(jnp.where(hit, 1.0, 0.0).astype(BF16), hb, preferred_element_type=F32)
    total8 = _start_runs(i, base_ref, n8_ref,
                         lambda e, b, off, size: runs.at[slot, pl.ds(e * RUN_CAP + off, size), :],
                         lambda e, b, off, size: xs_ref.at[pl.ds(b + off, size), :], sem.at[slot])
    pend_ref[slot] = total8

    def wait_slot(s):
        _wait_rows(pend_ref[s], lambda size: pltpu.make_async_copy(
            runs.at[s, pl.ds(0, size), :], xs_ref.at[pl.ds(0, size), :], sem.at[s]), 9)

    @pl.when(i > 0)
    def _():
        wait_slot(1 - slot)

    @pl.when(i == nsteps - 1)
    def _():
        wait_slot(slot)

    def ovf_body(e, carry):
        n8 = n8_ref[i * N_EXPERTS + e]
        b = pl.multiple_of(base_ref[i * N_EXPERTS + e], 8)
        for c in range(1, tm // RUN_CAP):
            @pl.when(n8 > c * (RUN_CAP // 8))
            def _():
                ck = lax.broadcasted_iota(I32, (RUN_CAP, tm), 0) + (e * tm + c * RUN_CAP)
                hit_c = (ck == keys[0]) | (ck == keys[1]) | (ck == keys[2]) | (ck == keys[3])
                ovf[...] = jnp.dot(jnp.where(hit_c, 1.0, 0.0).astype(BF16), hb, preferred_element_type=F32)
                for phase in (0, 1):
                    for live, size, off in _run_pieces(n8 - c * (RUN_CAP // 8), RUN_CAP // 8):
                        @pl.when(live)
                        def _():
                            cp = pltpu.make_async_copy(
                                ovf.at[pl.ds(off, size), :],
                                xs_ref.at[pl.ds(b + c * RUN_CAP + off, size), :], osem)
                            if phase == 0:
                                cp.start()
                            else:
                                cp.wait()
        return carry

    lax.fori_loop(0, N_EXPERTS, ovf_body, 0)


def _dispatch(plan, keys, h):
    base, n8, zstart, p_rows = plan
    t = h.shape[0]
    tm = BLK
    grid_spec = pltpu.PrefetchScalarGridSpec(
        num_scalar_prefetch=3,
        grid=(t // tm,),
        in_specs=[pl.BlockSpec((8, tm), lambda i, b, n, z: (0, i)),
                  pl.BlockSpec((tm, D_MODEL), lambda i, b, n, z: (i, 0))],
        out_specs=pl.BlockSpec(memory_space=pl.ANY),
        scratch_shapes=[pltpu.VMEM((2, N_EXPERTS * RUN_CAP, D_MODEL), F32), pltpu.VMEM((RUN_CAP, D_MODEL), F32),
                        pltpu.VMEM((MOE_BM, D_MODEL), F32), pltpu.SMEM((2,), I32),
                        pltpu.SemaphoreType.DMA((2,)), pltpu.SemaphoreType.DMA(()), pltpu.SemaphoreType.DMA(())],
    )
    return pl.pallas_call(
        functools.partial(_dispatch_kernel, tm=tm, nzero=zstart.shape[0], nsteps=t // tm),
        grid_spec=grid_spec,
        out_shape=jax.ShapeDtypeStruct((p_rows, D_MODEL), F32),
        compiler_params=_cparams(("arbitrary",)),
        name="moe_dispatch",
    )(base, n8, zstart, keys, h)


def _expert_kernel(be_ref, nused_ref, first_ref, slot_ref, nxt_ref, x_ref, wgu_hbm, bgu_ref, wdn_hbm, bdn_ref,
                   y_ref, gu_stage, dn_stage, wgu_bf, wdn_bf, sem, *, layer):
    b = pl.program_id(0)

    def fetch(e, s):
        return (pltpu.make_async_copy(wgu_hbm.at[layer, e], gu_stage.at[s], sem.at[0, s]),
                pltpu.make_async_copy(wdn_hbm.at[layer, e], dn_stage.at[s], sem.at[1, s]))

    @pl.when(b == 0)
    def _():
        for cp in fetch(be_ref[0], 0):
            cp.start()

    @pl.when(first_ref[b] == 1)
    def _():
        s = slot_ref[b]
        for cp in fetch(be_ref[b], s):
            cp.wait()
        wgu_bf[...] = gu_stage[s].astype(BF16)
        wdn_bf[...] = dn_stage[s].astype(BF16)

        @pl.when(nxt_ref[b] >= 0)
        def _():
            for cp in fetch(nxt_ref[b], 1 - s):
                cp.start()

    @pl.when(b < nused_ref[0])
    def _():
        gu = jnp.dot(x_ref[...].astype(BF16), wgu_bf[...], preferred_element_type=F32) + bgu_ref[...]
        g_ = jnp.minimum(gu[:, :D_FF], SWIGLU_LIMIT)
        u_ = jnp.clip(gu[:, D_FF:], -SWIGLU_LIMIT, SWIGLU_LIMIT)
        act = (u_ + 1.0) * (g_ * _sigmoid(g_ * SWIGLU_ALPHA))
        y_ref[...] = jnp.dot(act.astype(BF16), wdn_bf[...], preferred_element_type=F32) + bdn_ref[...]

    @pl.when(b >= nused_ref[0])
    def _():
        y_ref[...] = jnp.zeros_like(y_ref)


def _experts(plan, xsorted, w_gu, b_gu, w_dn, b_dn, layer):
    blk_e, nused, first, slot, nxt = plan
    p_rows = xsorted.shape[0]
    nblk = p_rows // MOE_BM
    depth = w_gu.shape[0]
    bsel = lambda b, be, nu, fi, sl, nx: (layer, be[b], 0, 0)
    grid_spec = pltpu.PrefetchScalarGridSpec(
        num_scalar_prefetch=5,
        grid=(nblk,),
        in_specs=[pl.BlockSpec((MOE_BM, D_MODEL), lambda b, be, nu, fi, sl, nx: (jnp.minimum(b, nu[0] - 1), 0)),
                  pl.BlockSpec(memory_space=pl.ANY),
                  pl.BlockSpec((None, None, 1, 2 * D_FF), bsel),
                  pl.BlockSpec(memory_space=pl.ANY),
                  pl.BlockSpec((None, None, 1, D_MODEL), bsel)],
        out_specs=pl.BlockSpec((MOE_BM, D_MODEL), lambda b, be, nu, fi, sl, nx: (b, 0)),
        scratch_shapes=[pltpu.VMEM((2, D_MODEL, 2 * D_FF), F32), pltpu.VMEM((2, D_FF, D_MODEL), F32),
                        pltpu.VMEM((D_MODEL, 2 * D_FF), BF16), pltpu.VMEM((D_FF, D_MODEL), BF16),
                        pltpu.SemaphoreType.DMA((2, 2))],
    )
    return pl.pallas_call(
        functools.partial(_expert_kernel, layer=layer),
        grid_spec=grid_spec,
        out_shape=jax.ShapeDtypeStruct((p_rows, D_MODEL), F32),
        compiler_params=_cparams(("arbitrary",)),
        name="moe_experts",
    )(blk_e, nused, first, slot, nxt, xsorted, w_gu, b_gu.reshape(depth, N_EXPERTS, 1, 2 * D_FF), w_dn,
      b_dn.reshape(depth, N_EXPERTS, 1, D_MODEL))


def _combine_kernel(base_ref, n8_ref, y_ref, gk_ref, xs_ref, mod_ref, o_ref, runs, ovf, pend_ref, sem, osem,
                    *, tm, nsteps):
    i = pl.program_id(0)

    slot = i % 2

    def gather(tile, s):
        pend_ref[s] = _start_runs(tile, base_ref, n8_ref,
                                  lambda e, b, off, size: y_ref.at[pl.ds(b + off, size), :],
                                  lambda e, b, off, size: runs.at[s, pl.ds(e * RUN_CAP + off, size), :],
                                  sem.at[s])

    @pl.when(i == 0)
    def _():
        runs[...] = jnp.zeros_like(runs)
        ovf[...] = jnp.zeros_like(ovf)
        gather(0, 0)

    @pl.when(i + 1 < nsteps)
    def _():
        gather(i + 1, 1 - slot)

    _wait_rows(pend_ref[slot], lambda size: pltpu.make_async_copy(
        y_ref.at[pl.ds(0, size), :], runs.at[slot, pl.ds(0, size), :], sem.at[slot]), 9)
    gk = gk_ref[...]
    gates = [gk[:, k:k + 1] for k in range(TOP_K)]
    keys = [gk[:, TOP_K + k:TOP_K + k + 1].astype(I32) for k in range(TOP_K)]
    nrow = N_EXPERTS * RUN_CAP
    cc = lax.broadcasted_iota(I32, (1, nrow), 1)
    colkey = (cc >> RUN_SHIFT) * tm + (cc & (RUN_CAP - 1))
    w = jnp.where(colkey == keys[0], gates[0], 0.0)
    for k in range(1, TOP_K):
        w = w + jnp.where(colkey == keys[k], gates[k], 0.0)
    acc = jnp.dot(w.astype(BF16), runs[slot].astype(BF16), preferred_element_type=F32)
    o_ref[...] = xs_ref[...] + acc * _res_gate(mod_ref, 3 * D_MODEL, tm, i * tm)

    def ovf_body(e, carry):
        n8 = n8_ref[i * N_EXPERTS + e]
        b = pl.multiple_of(base_ref[i * N_EXPERTS + e], 8)
        for c in range(1, tm // RUN_CAP):
            @pl.when(n8 > c * (RUN_CAP // 8))
            def _():
                for phase in (0, 1):
                    for live, size, off in _run_pieces(n8 - c * (RUN_CAP // 8), RUN_CAP // 8):
                        @pl.when(live)
                        def _():
                            cp = pltpu.make_async_copy(
                                y_ref.at[pl.ds(b + c * RUN_CAP + off, size), :],
                                ovf.at[pl.ds(off, size), :], osem)
                            if phase == 0:
                                cp.start()
                            else:
                                cp.wait()
                ck = lax.broadcasted_iota(I32, (1, RUN_CAP), 1) + (e * tm + c * RUN_CAP)
                wc = jnp.where(ck == keys[0], gates[0], 0.0)
                for k in range(1, TOP_K):
                    wc = wc + jnp.where(ck == keys[k], gates[k], 0.0)
                part = jnp.dot(wc.astype(BF16), ovf[...].astype(BF16), preferred_element_type=F32)
                o_ref[...] = o_ref[...] + part * _res_gate(mod_ref, 3 * D_MODEL, tm, i * tm)
        return carry

    lax.fori_loop(0, N_EXPERTS, ovf_body, 0)


def _combine(plan, y, gk_col, xs, mod):
    base, n8, _, _ = plan
    t = xs.shape[0]
    tm = BLK
    grid_spec = pltpu.PrefetchScalarGridSpec(
        num_scalar_prefetch=2,
        grid=(t // tm,),
        in_specs=[pl.BlockSpec(memory_space=pl.ANY),
                  pl.BlockSpec((tm, 128), lambda i, b, n: (i, 0)),
                  pl.BlockSpec((tm, D_MODEL), lambda i, b, n: (i, 0)),
                  pl.BlockSpec((8, 6 * D_MODEL), lambda i, b, n: (0, 0))],
        out_specs=pl.BlockSpec((tm, D_MODEL), lambda i, b, n: (i, 0)),
        scratch_shapes=[pltpu.VMEM((2, N_EXPERTS * RUN_CAP, D_MODEL), F32), pltpu.VMEM((RUN_CAP, D_MODEL), F32),
                        pltpu.SMEM((2,), I32), pltpu.SemaphoreType.DMA((2,)), pltpu.SemaphoreType.DMA(())],
    )
    return pl.pallas_call(
        functools.partial(_combine_kernel, tm=tm, nsteps=t // tm),
        grid_spec=grid_spec,
        out_shape=jax.ShapeDtypeStruct((t, D_MODEL), F32),
        compiler_params=_cparams(("arbitrary",)),
        name="moe_combine",
    )(base, n8, y, gk_col, xs, mod)


def _pad_heads(w, nh, hd, hdp):
    d = w.shape[0]
    return jnp.pad(w.reshape(d, nh, hd), ((0, 0), (0, 0), (0, hdp - hd))).reshape(d, nh * hdp)


def _layout_w_in(w):
    return _layout_w_in_f32(w).astype(BF16)


def _layout_w_in_f32(w):
    pts = np.cumsum([A_WIDTH] * 4 + [2 * H_A, 2 * H_A, B_KWIDTH, B_KWIDTH, B_VWIDTH, B_VWIDTH,
                                     2 * GLA_RANK, C_WIDTH, KV_C * HD_C, KV_C * HD_C])
    (aq, ak, av, az, aa, ab, bq, bk, bv, bz, bg, cq, ck, cv, gl) = jnp.split(w, pts.tolist(), axis=1)
    small = jnp.concatenate([aa, ab, bg], axis=1)
    small = jnp.pad(small, ((0, 0), (0, 256 - small.shape[1])))
    cols = [gl, aq, ak, av, az, _pad_heads(cq, H_C, HD_C, HD_CP), _pad_heads(ck, KV_C, HD_C, HD_CP),
            _pad_heads(cv, KV_C, HD_C, HD_CP), bq, bk, bv, bz, small]
    out = jnp.concatenate(cols, axis=1)
    assert out.shape[1] == IN_COLS_P
    return out


def _rope_tables(t):
    s_len = t - CTX
    half = HD_C // 2
    inv_freq = ROPE_THETA ** (-jnp.arange(0, half, 2, dtype=F32) / half)
    pos = jnp.arange(s_len)
    rows = (pos // GRID_W).astype(F32)[:, None] * inv_freq[None, :]
    cols = (pos % GRID_W).astype(F32)[:, None] * inv_freq[None, :]
    cr, sr, cc, sc = jnp.cos(rows), jnp.sin(rows), jnp.cos(cols), jnp.sin(cols)
    zpad = jnp.zeros((s_len, HD_CP - HD_C), F32)
    cos_l = jnp.concatenate([cr, cr, cc, cc, zpad], axis=1)
    sin_l = jnp.concatenate([-sr, sr, -sc, sc, zpad], axis=1)
    cos_t = jnp.concatenate([jnp.ones((CTX, HD_CP), F32), cos_l], axis=0)
    sin_t = jnp.concatenate([jnp.zeros((CTX, HD_CP), F32), sin_l], axis=0)
    return cos_t, sin_t


def _lane_vec(v, width=128):
    v = v.reshape(1, -1).astype(F32)
    return jnp.pad(v, ((0, 0), (0, width - v.shape[1])))


def _moe_plan(cnt_tile, tk):
    nt = cnt_tile.shape[0]
    pad8 = (cnt_tile + 7) // 8 * 8
    total = jnp.sum(pad8, axis=0)
    padded = (total + MOE_BM - 1) // MOE_BM * MOE_BM
    pend = jnp.cumsum(padded)
    estart = pend - padded
    base = estart[None, :] + jnp.cumsum(pad8, axis=0) - pad8
    nblk = (tk + nt * N_EXPERTS * 7 + N_EXPERTS * (MOE_BM - 1) + MOE_BM - 1) // MOE_BM
    p_rows = nblk * MOE_BM
    blk = jnp.arange(nblk, dtype=I32)
    blk_e = jnp.minimum(jnp.sum((pend[None, :] <= (blk * MOE_BM)[:, None]).astype(I32), axis=1), N_EXPERTS - 1)
    nused = (pend[-1] // MOE_BM).astype(I32)
    prev_e = jnp.concatenate([jnp.full((1,), -1, I32), blk_e[:-1]])
    first = ((blk_e != prev_e) & (blk < nused)).astype(I32)
    slot = (jnp.cumsum(first) - 1) % 2
    pos = jnp.where(first == 1, blk, nblk)
    nxt_pos = jnp.concatenate([lax.cummin(pos, axis=0, reverse=True)[1:], jnp.full((1,), nblk, I32)])
    nxt = jnp.where(nxt_pos < nblk, blk_e[jnp.minimum(nxt_pos, nblk - 1)], -1)
    last_blk = jnp.where(total > 0, pend - MOE_BM, -1)
    ntail = (nt * N_EXPERTS * 7 + N_EXPERTS * (MOE_BM - 1)) // MOE_BM + 1
    tail = pend[-1] + jnp.arange(ntail, dtype=I32) * MOE_BM
    zstart = jnp.concatenate([last_blk, jnp.where(tail < p_rows, tail, -1)]).astype(I32)
    eplan = (blk_e, nused.reshape(1), first, slot.astype(I32), nxt.astype(I32))
    mplan = (base.reshape(-1).astype(I32), (pad8 // 8).reshape(-1).astype(I32), zstart, p_rows)
    return mplan, eplan


def kernel(x, c, ctx, c_ctx, ada_w, ada_b, norm_mix_g, norm_ffn_g, w_in, dn_conv_w, dn_a_log, dn_dt_bias,
           dn_norm_g, gla_w2, gla_b2, gla_norm_g, attn_q_norm_g, attn_k_norm_g, attn_sink, w_branch_a,
           w_branch_b, w_branch_c, w_out, router_w, router_b, w_gate_up, b_gate_up, w_down, b_down):
    assert x.shape[0] == 1 and c.shape[0] == 1 and ctx.shape[1] == CTX
    depth = ada_w.shape[0]
    xs = jnp.concatenate([ctx[0], x[0]], axis=0)
    t = xs.shape[0]
    assert t % BLK == 0 and (t - CTX) % GRID_W == 0
    cc = jnp.concatenate([c, c_ctx[None, :], jnp.zeros((6, D_MODEL), F32)], axis=0)
    mods = _ada_mod(cc, ada_w, ada_b)
    cos_t, sin_t = _rope_tables(t)
    for l in range(depth):
        mod = mods[l]
        proj = _inproj(xs, mod, norm_mix_g[l][None, :], _layout_w_in(w_in[l]))
        qa, ka, va, gcol = _dnprep(proj, dn_conv_w[l], _lane_vec(dn_a_log[l]), _lane_vec(dn_dt_bias[l]))
        oaf, oab = _dnscan(qa, ka, va, gcol)
        w2 = gla_w2[l].astype(F32)
        w2full = jnp.zeros((128, 2 * B_KWIDTH), F32)
        for d in range(2):
            r0 = 4 * H_A + d * GLA_RANK
            w2full = w2full.at[r0:r0 + GLA_RANK, d * B_KWIDTH:(d + 1) * B_KWIDTH].set(w2[d])
        obf, obb = _glascan(proj, w2full, gla_b2[l].reshape(1, 2 * B_KWIDTH).astype(F32))
        qg = _lane_vec(attn_q_norm_g[l])
        kg = _lane_vec(attn_k_norm_g[l])
        qr, kr, vr = _attnprep(proj, qg, kg, cos_t, sin_t)
        yc = _attn(qr, kr, vr, attn_sink[l].astype(F32))
        wpc = jnp.pad(w_branch_c[l].reshape(H_C, HD_C, D_MODEL),
                      ((0, 0), (0, HD_CP - HD_C), (0, 0))).reshape(H_C * HD_CP, D_MODEL)
        xs = _merge(oaf, oab, obf, obb, yc, proj, xs,
                    dn_norm_g[l][None, :], gla_norm_g[l][None, :],
                    w_branch_a[l].astype(BF16), w_branch_b[l].astype(BF16), wpc.astype(BF16),
                    w_out[l].astype(BF16), mod)
        h2, keys, gk_col, cnt = _router(xs, mod, norm_ffn_g[l][None, :], router_w[l].T, router_b[l][:, None])
        mplan, eplan = _moe_plan(cnt[:, :, 0].astype(I32), t * TOP_K)
        xsorted = _dispatch(mplan, keys, h2)
        y = _experts(eplan, xsorted, w_gate_up, b_gate_up, w_down, b_down, l)
        xs = _combine(mplan, y, gk_col, xs, mod)
    return xs[CTX:][None]
```

```python
import functools
import math

import jax
import jax.numpy as jnp
import numpy as np
from jax import lax
from jax.experimental import pallas as pl
from jax.experimental.pallas import tpu as pltpu

F32 = jnp.float32
BF16 = jnp.bfloat16
I32 = jnp.int32

D_MODEL = 1024
DEPTH = 4
GRID_W = 64
CTX = 256
H_A = 4
HD_A = 128
A_WIDTH = H_A * HD_A
CONV_W = 5
H_B = 4
DK_B = 64
DV_B = 128
B_KWIDTH = H_B * DK_B
B_VWIDTH = H_B * DV_B
GLA_RANK = 16
GLA_NORMALIZER = 16.0
H_C = 8
KV_C = 2
REP_C = H_C // KV_C
HD_C = 64
HD_CP = 128
C_WIDTH = H_C * HD_C
ATT_BLOCK = 128
ROPE_THETA = 10000.0
CHUNK = 64
SUB = 16
N_EXPERTS = 32
TOP_K = 4
D_FF = 1024
SWIGLU_LIMIT = 7.0
SWIGLU_ALPHA = 1.702
EPS = 1e-6
NEG = -1e30

BLK = 256
MOE_BM = 512
VMEM_LIMIT = 56 * 1024 * 1024

COL_GL = 0
COL_QKV = 3072
COL_AZ = 4608
COL_CQ = 5120
COL_CKV = 6144
COL_BQK = 6656
COL_BV = 7168
COL_BZ = 7680
COL_SMALL = 8192
IN_COLS_P = 8448
IN_TN = 768


def _pick(n, cands):
    for c in cands:
        if n % c == 0:
            return c
    raise ValueError(f"no tile for {n}")


def _cparams(sem):
    return pltpu.CompilerParams(dimension_semantics=sem, vmem_limit_bytes=VMEM_LIMIT)


def _bdot(a, b):
    return jnp.dot(a.astype(BF16), b.astype(BF16), preferred_element_type=F32)


def _bdot_nt(a, b):
    return lax.dot_general(a.astype(BF16), b.astype(BF16), (((1,), (1,)), ((), ())),
                           preferred_element_type=F32)


def _bdot_tn(a, b):
    return lax.dot_general(a.astype(BF16), b.astype(BF16), (((0,), (0,)), ((), ())),
                           preferred_element_type=F32)


def _fdot(a, b):
    return jnp.dot(a, b, precision=lax.Precision.HIGHEST, preferred_element_type=F32)


def _split2(a):
    hi = a.astype(BF16)
    lo = (a - hi.astype(F32)).astype(BF16)
    return hi, lo


def _dot3(a, b):
    ah, al = _split2(a)
    bh, bl = _split2(b)
    d = functools.partial(jnp.dot, preferred_element_type=F32)
    return d(ah, bh) + (d(ah, bl) + d(al, bh))


def _cumsum_dot(tri, x):
    t = tri.astype(BF16)
    hi = x.astype(BF16)
    r1 = x - hi.astype(F32)
    mid = r1.astype(BF16)
    lo = (r1 - mid.astype(F32)).astype(BF16)
    d = functools.partial(jnp.dot, preferred_element_type=F32)
    return d(t, hi) + (d(t, mid) + d(t, lo))


def _sigmoid(x):
    return 1.0 / (1.0 + jnp.exp(-x))


def _silu(x):
    return x * _sigmoid(x)


def _softplus(x):
    return jnp.maximum(x, 0.0) + jnp.log(1.0 + jnp.exp(-jnp.abs(x)))


def _ada_kernel(cc_ref, w_ref, b_ref, o_ref):
    o_ref[...] = _fdot(_silu(cc_ref[...]), w_ref[...]) + b_ref[...]


def _ada_mod(cc, ada_w, ada_b):
    depth = ada_w.shape[0]
    tn = 1536
    return pl.pallas_call(
        _ada_kernel,
        grid=(depth, 6 * D_MODEL // tn),
        in_specs=[pl.BlockSpec((8, D_MODEL), lambda l, j: (0, 0)),
                  pl.BlockSpec((None, D_MODEL, tn), lambda l, j: (l, 0, j)),
                  pl.BlockSpec((None, 1, tn), lambda l, j: (l, 0, j))],
        out_specs=pl.BlockSpec((None, 8, tn), lambda l, j: (l, 0, j)),
        out_shape=jax.ShapeDtypeStruct((depth, 8, 6 * D_MODEL), F32),
        compiler_params=_cparams(("arbitrary", "arbitrary")),
        name="ada_mod",
    )(cc, ada_w, ada_b.reshape(depth, 1, 6 * D_MODEL))


def _norm_mod(x, g, mod_ref, moff, row0):
    tm = x.shape[0]
    y = x * lax.rsqrt(jnp.mean(x * x, axis=-1, keepdims=True) + EPS) * g
    isc = (row0 + lax.broadcasted_iota(I32, (tm, 1), 0)) < CTX
    shift = jnp.where(isc, mod_ref[1:2, moff:moff + D_MODEL], mod_ref[0:1, moff:moff + D_MODEL])
    scale = jnp.where(isc, mod_ref[1:2, moff + D_MODEL:moff + 2 * D_MODEL],
                      mod_ref[0:1, moff + D_MODEL:moff + 2 * D_MODEL])
    return y * (1.0 + scale) + shift


def _res_gate(mod_ref, moff, tm, row0):
    isc = (row0 + lax.broadcasted_iota(I32, (tm, 1), 0)) < CTX
    return jnp.where(isc, mod_ref[1:2, moff + 2 * D_MODEL:moff + 3 * D_MODEL],
                     mod_ref[0:1, moff + 2 * D_MODEL:moff + 3 * D_MODEL])


def _inproj_kernel(x_ref, mod_ref, g_ref, w_ref, o_ref, h_ref, *, tm):
    i = pl.program_id(0)

    @pl.when(pl.program_id(1) == 0)
    def _():
        h_ref[...] = _norm_mod(x_ref[...], g_ref[...], mod_ref, 0, i * tm).astype(BF16)

    o_ref[...] = jnp.dot(h_ref[...], w_ref[...], preferred_element_type=F32)


def _inproj(xs, mod, g, w):
    t = xs.shape[0]
    tm = _pick(t, (1664, 1280, 640, 256))
    return pl.pallas_call(
        functools.partial(_inproj_kernel, tm=tm),
        grid=(t // tm, IN_COLS_P // IN_TN),
        in_specs=[pl.BlockSpec((tm, D_MODEL), lambda i, j: (i, 0)),
                  pl.BlockSpec((8, 6 * D_MODEL), lambda i, j: (0, 0)),
                  pl.BlockSpec((1, D_MODEL), lambda i, j: (0, 0)),
                  pl.BlockSpec((D_MODEL, IN_TN), lambda i, j: (0, j))],
        out_specs=pl.BlockSpec((tm, IN_TN), lambda i, j: (i, j)),
        out_shape=jax.ShapeDtypeStruct((t, IN_COLS_P), F32),
        scratch_shapes=[pltpu.VMEM((tm, D_MODEL), BF16)],
        compiler_params=_cparams(("arbitrary", "arbitrary")),
        name="inproj",
    )(xs, mod, g, w)


def _tri_blockdiag(n, lower):
    ii = lax.broadcasted_iota(I32, (n, n), 0)
    jj = lax.broadcasted_iota(I32, (n, n), 1)
    same = (ii >> 6) == (jj >> 6)
    tri = (ii >= jj) if lower else (ii <= jj)
    return jnp.where(same, jnp.where(tri, 1.0, 0.0), 0.0).astype(F32)


def _dnprep_kernel(main_ref, prev_ref, next_ref, small_ref, cw_ref, alog_ref, dtb_ref,
                   q_ref, k_ref, v_ref, gcol_ref, ext_ref, *, nb):
    i = pl.program_id(0)
    use_prev = i >= 2
    use_next = jnp.logical_and(i >= 1, i <= nb - 2)
    ext_ref[0:8, :] = jnp.where(use_prev, prev_ref[...], 0.0)
    ext_ref[8:8 + BLK, :] = main_ref[...]
    ext_ref[8 + BLK:16 + BLK, :] = jnp.where(use_next, next_ref[...], 0.0)
    acc = ext_ref[6:6 + BLK, :] * cw_ref[0:1, :]
    for d in range(1, CONV_W):
        acc = acc + ext_ref[6 + d:6 + d + BLK, :] * cw_ref[d:d + 1, :]
    s = _silu(acc)
    for h in range(H_A):
        for part, ref, mul in ((0, q_ref, HD_A ** -0.5), (1, k_ref, 1.0)):
            seg = s[:, part * A_WIDTH + h * HD_A: part * A_WIDTH + (h + 1) * HD_A]
            nrm = seg * lax.rsqrt(jnp.sum(seg * seg, axis=-1, keepdims=True) + EPS)
            ref[:, h * HD_A:(h + 1) * HD_A] = nrm * mul
    v_ref[...] = s[:, 2 * A_WIDTH:3 * A_WIDTH]
    sm = small_ref[...]
    lane = lax.broadcasted_iota(I32, sm.shape, 1)
    g = -jnp.exp(alog_ref[...]) * _softplus(sm + dtb_ref[...])
    gb = jnp.where(lane < 2 * H_A, g, jnp.where(lane < 4 * H_A, _sigmoid(sm), 0.0))
    cf = _fdot(_tri_blockdiag(BLK, True), gb)
    cr = _fdot(_tri_blockdiag(BLK, False), gb)
    gc = jnp.where(lane < H_A, cf, jnp.where(lane < 2 * H_A, cr, gb))
    gcol_ref[...] = gc


def _dnprep(proj, conv_w, alog_vec, dtb_vec):
    t = proj.shape[0]
    nb = t // BLK
    qkv_blk = COL_QKV // (3 * A_WIDTH)
    last8 = t // 8 - 1
    out_sds = jax.ShapeDtypeStruct((t, A_WIDTH), F32)
    return pl.pallas_call(
        functools.partial(_dnprep_kernel, nb=nb),
        grid=(nb,),
        in_specs=[pl.BlockSpec((BLK, 3 * A_WIDTH), lambda i: (i, qkv_blk)),
                  pl.BlockSpec((8, 3 * A_WIDTH), lambda i: (jnp.maximum(i * (BLK // 8) - 1, 0), qkv_blk)),
                  pl.BlockSpec((8, 3 * A_WIDTH), lambda i: (jnp.minimum((i + 1) * (BLK // 8), last8), qkv_blk)),
                  pl.BlockSpec((BLK, 128), lambda i: (i, COL_SMALL // 128)),
                  pl.BlockSpec((CONV_W, 3 * A_WIDTH), lambda i: (0, 0)),
                  pl.BlockSpec((1, 128), lambda i: (0, 0)),
                  pl.BlockSpec((1, 128), lambda i: (0, 0))],
        out_specs=[pl.BlockSpec((BLK, A_WIDTH), lambda i: (i, 0)),
                   pl.BlockSpec((BLK, A_WIDTH), lambda i: (i, 0)),
                   pl.BlockSpec((BLK, A_WIDTH), lambda i: (i, 0)),
                   pl.BlockSpec((BLK, 128), lambda i: (i, 0))],
        out_shape=[out_sds, out_sds, out_sds,
                   jax.ShapeDtypeStruct((t, 128), F32)],
        scratch_shapes=[pltpu.VMEM((BLK + 16, 3 * A_WIDTH), F32)],
        compiler_params=_cparams(("arbitrary",)),
        name="dn_prep",
    )(proj, proj, proj, proj, conv_w, alog_vec, dtb_vec)


def _dot3_all(a_list, b_list):
    d = functools.partial(jnp.dot, preferred_element_type=F32)
    sa = [_split2(a) for a in a_list]
    sb = [_split2(b) for b in b_list]
    hh = [d(a[0], b[0]) for a, b in zip(sa, sb)]
    hl = [d(a[0], b[1]) for a, b in zip(sa, sb)]
    lh = [d(a[1], b[0]) for a, b in zip(sa, sb)]
    return [x + (y + z) for x, y, z in zip(hh, hl, lh)]


def _unit_tri_inverse_all(l_mats, masks):
    eye, m_diag, m_l1, m_l2 = masks
    ld = [l * m_diag for l in l_mats]
    x = [eye - a for a in ld]
    p = _dot3_all(ld, ld)
    for it in range(3):
        xp = _dot3_all(x, p)
        if it < 2:
            p = _dot3_all(p, p)
        x = [a + b for a, b in zip(x, xp)]
    for m in (m_l1, m_l2):
        cx = _dot3_all([l * m for l in l_mats], x)
        xcx = _dot3_all(x, cx)
        x = [a - b for a, b in zip(x, xcx)]
    return x


def _dn_masks():
    ii = lax.broadcasted_iota(I32, (CHUNK, CHUNK), 0)
    jj = lax.broadcasted_iota(I32, (CHUNK, CHUNK), 1)
    one = lambda c: jnp.where(c, 1.0, 0.0).astype(F32)
    eye = one(ii == jj)
    m_diag = one((ii >> 4) == (jj >> 4))
    m_l2 = one((ii >> 5) != (jj >> 5))
    m_l1 = 1.0 - m_diag - m_l2
    return ii, jj, (eye, m_diag, m_l1, m_l2)


def _dn_local(items, ii, jj, masks):
    n = len(items)
    dec, lmat, qk, kb, eg = [], [], [], [], []
    for q, k, v, gcol, grow, bcol, fwd in items:
        incl = (ii >= jj) if fwd else (ii <= jj)
        dec.append(jnp.exp(jnp.where(incl, gcol - grow, NEG)))
        kb.append(k * bcol)
        eg.append(jnp.exp(gcol))
    kh = [it[1].astype(BF16) for it in items]
    kk = [_bdot_nt(kb[i], kh[i]) for i in range(n)]
    qkr = [_bdot_nt(items[i][0], kh[i]) for i in range(n)]
    for i in range(n):
        fwd = items[i][6]
        strict = (ii > jj) if fwd else (ii < jj)
        lmat.append(kk[i] * jnp.where(strict, dec[i], 0.0))
        qk.append((qkr[i] * dec[i]).astype(BF16))
    rhs = [jnp.concatenate([items[i][2] * items[i][5], kb[i] * eg[i]], axis=1) for i in range(n)]
    sol = _dot3_all(_unit_tri_inverse_all(lmat, masks), rhs)
    out = []
    for i in range(n):
        q, k, _, gcol, _, _, fwd = items[i]
        glast = gcol[CHUNK - 1:CHUNK, :] if fwd else gcol[0:1, :]
        out.append((sol[i][:, :HD_A], sol[i][:, HD_A:].astype(BF16), qk[i], (q * eg[i]).astype(BF16),
                    (k * jnp.exp(glast - gcol)).astype(BF16), jnp.exp(glast)))
    return out


def _dn_step(local, states):
    n = len(local)
    sb = [s.astype(BF16) for s in states]
    d = functools.partial(jnp.dot, preferred_element_type=F32)
    ws = [d(local[i][1], sb[i]) for i in range(n)]
    qs = [d(local[i][3], sb[i]) for i in range(n)]
    v_new = [(local[i][0] - ws[i]).astype(BF16) for i in range(n)]
    o2 = [d(local[i][2], v_new[i]) for i in range(n)]
    kv = [lax.dot_general(local[i][4], v_new[i], (((0,), (0,)), ((), ())), preferred_element_type=F32)
          for i in range(n)]
    return [qs[i] + o2[i] for i in range(n)], [states[i] * local[i][5] + kv[i] for i in range(n)]


def _dnscan_kernel(qf, kf, vf, gcf, qb, kb, vb, gcb, of_ref, ob_ref, s_ref):
    @pl.when(pl.program_id(0) == 0)
    def _():
        s_ref[...] = jnp.zeros_like(s_ref)

    ii, jj, masks = _dn_masks()
    nch = BLK // CHUNK
    pick = jnp.where(lax.broadcasted_iota(I32, (16, 128), 0) == lax.broadcasted_iota(I32, (16, 128), 1),
                     1.0, 0.0).astype(F32)

    dirs = ((True, (qf, kf, vf, gcf, of_ref)), (False, (qb, kb, vb, gcb, ob_ref)))
    items, sinks = [], []
    for step in range(nch):
        for fwd, (q_r, k_r, v_r, gc_r, o_r) in dirs:
            c = step if fwd else nch - 1 - step
            rows = slice(c * CHUNK, (c + 1) * CHUNK)
            d = 0 if fwd else 1
            gct = gc_r[rows, :]
            grows = lax.dot_general(pick, gct, (((1,), (1,)), ((), ())),
                                    precision=lax.Precision.HIGHEST, preferred_element_type=F32)
            for h in range(H_A):
                lanes = slice(h * HD_A, (h + 1) * HD_A)
                gi = d * H_A + h
                items.append((q_r[rows, lanes], k_r[rows, lanes], v_r[rows, lanes],
                              gct[:, gi:gi + 1], grows[gi:gi + 1, :],
                              gct[:, 2 * H_A + gi:2 * H_A + gi + 1], fwd))
                sinks.append((o_r, rows, lanes))
    local = _dn_local(items, ii, jj, masks)
    nchain = 2 * H_A
    states = [s_ref[gi] for gi in range(nchain)]
    for step in range(nch):
        outs, states = _dn_step(local[step * nchain:(step + 1) * nchain], states)
        for (o_r, rows, lanes), o in zip(sinks[step * nchain:(step + 1) * nchain], outs):
            o_r[rows, lanes] = o
    for gi in range(nchain):
        s_ref[gi] = states[gi]


def _rev_block(nb):
    return lambda i: jnp.where(i == 0, 0, nb - i)


def _dnscan(q, k, v, gcol):
    t = q.shape[0]
    nb = t // BLK
    rev = _rev_block(nb)
    wide = lambda f: pl.BlockSpec((BLK, A_WIDTH), lambda i: (f(i), 0))
    col = lambda f: pl.BlockSpec((BLK, 128), lambda i: (f(i), 0))
    ident = lambda i: i
    out_sds = jax.ShapeDtypeStruct((t, A_WIDTH), F32)
    return pl.pallas_call(
        _dnscan_kernel,
        grid=(nb,),
        in_specs=[wide(ident), wide(ident), wide(ident), col(ident),
                  wide(rev), wide(rev), wide(rev), col(rev)],
        out_specs=[wide(ident), wide(rev)],
        out_shape=[out_sds, out_sds],
        scratch_shapes=[pltpu.VMEM((2 * H_A, HD_A, HD_A), F32)],
        compiler_params=_cparams(("arbitrary",)),
        name="dn_scan",
    )(q, k, v, gcol, q, k, v, gcol)


def _gla_gates(small_ref, w2_ref, b2_ref, b_ref, d, fwd):
    cols = slice(d * B_KWIDTH, (d + 1) * B_KWIDTH)
    pre = _dot3(small_ref[...], w2_ref[:, cols]) + b2_ref[:, cols]
    gk = -_softplus(-pre) * (1.0 / GLA_NORMALIZER)
    b_ref[d] = _cumsum_dot(_tri_blockdiag(BLK, fwd), gk)


def _gla_chunk(qk_ref, v_ref, b_ref, o_ref, st_ref, d, fwd, consts, step):
    sel, headmask_k, st_mask = consts
    nch = BLK // CHUNK
    nsub = CHUNK // SUB
    sub_i = lax.broadcasted_iota(I32, (SUB, 1), 0)
    row_c = lax.broadcasted_iota(I32, (CHUNK, 1), 0)
    if True:
        c = step if fwd else nch - 1 - step
        rows = pl.ds(pl.multiple_of(c * CHUNK, CHUNK), CHUNK)
        q = qk_ref[rows, 0:B_KWIDTH] * (DK_B ** -0.5)
        k = qk_ref[rows, B_KWIDTH:2 * B_KWIDTH]
        v = v_ref[rows, :]
        b = b_ref[d, rows, :]
        vh = v.astype(BF16)
        st = st_ref[d]
        o = _bdot_nt(q * jnp.exp(b), st)
        refs = []
        for sb in range(nsub):
            if fwd:
                r = b[sb * SUB - 1:sb * SUB, :] if sb > 0 else jnp.zeros((1, B_KWIDTH), F32)
            else:
                r = b[(sb + 1) * SUB:(sb + 1) * SUB + 1, :] if sb < nsub - 1 else jnp.zeros((1, B_KWIDTH), F32)
            refs.append(r)
        rfull = jnp.concatenate([jnp.broadcast_to(r, (SUB, B_KWIDTH)) for r in refs], axis=0)
        qs = q * jnp.exp(b - rfull)
        a_off = [None] * H_B
        for sb in (range(1, nsub) if fwd else range(0, nsub - 1)):
            jmask = (row_c < sb * SUB) if fwd else (row_c >= (sb + 1) * SUB)
            ks = (k * jnp.exp(jnp.where(jmask, refs[sb] - b, NEG))).astype(BF16)
            rowmask = jnp.where((row_c >> 4) == sb, 1.0, 0.0)
            for h in range(H_B):
                a = _bdot_nt(qs * headmask_k[h], ks) * rowmask
                a_off[h] = a if a_off[h] is None else a_off[h] + a
        o = o + jnp.concatenate(
            [_bdot(a_off[h], vh[:, h * DV_B:(h + 1) * DV_B]) for h in range(H_B)], axis=1)
        diag = []
        for sb in range(nsub):
            s0 = sb * SUB
            bs, qsb, ksb = b[s0:s0 + SUB, :], q[s0:s0 + SUB, :], k[s0:s0 + SUB, :]
            tiles = []
            for jl in range(SUB):
                causal = (sub_i >= jl) if fwd else (sub_i <= jl)
                e = jnp.exp(jnp.where(causal, bs - bs[jl:jl + 1, :], NEG))
                tiles.append((qsb * ksb[jl:jl + 1, :] * e).astype(BF16))
            red = jnp.dot(jnp.concatenate(tiles, axis=0), sel, preferred_element_type=F32)
            acc = red[0:SUB, :] * v[s0:s0 + 1, :]
            for jl in range(1, SUB):
                acc = acc + red[jl * SUB:(jl + 1) * SUB, :] * v[s0 + jl:s0 + jl + 1, :]
            diag.append(acc)
        o_ref[rows, :] = o + jnp.concatenate(diag, axis=0)
        blast = b[CHUNK - 1:CHUNK, :] if fwd else b[0:1, :]
        kd = k * jnp.exp(blast - b)
        st_ref[d] = st * jnp.exp(blast) + _bdot_tn(v, kd) * st_mask


def _glascan_kernel(qkf, vf, smf, qkb, vb, smb, w2_ref, b2_ref, of_ref, ob_ref, st_ref, b_ref):
    @pl.when(pl.program_id(0) == 0)
    def _():
        st_ref[...] = jnp.zeros_like(st_ref)

    kk = lax.broadcasted_iota(I32, (B_KWIDTH, B_VWIDTH), 0)
    cc = lax.broadcasted_iota(I32, (B_KWIDTH, B_VWIDTH), 1)
    sel = jnp.where((kk >> 6) == (cc >> 7), 1.0, 0.0).astype(BF16)
    lane = lax.broadcasted_iota(I32, (1, B_KWIDTH), 1)
    headmask_k = [jnp.where((lane >> 6) == h, 1.0, 0.0).astype(F32) for h in range(H_B)]
    rr = lax.broadcasted_iota(I32, (B_VWIDTH, B_KWIDTH), 0)
    kc = lax.broadcasted_iota(I32, (B_VWIDTH, B_KWIDTH), 1)
    st_mask = jnp.where((rr >> 7) == (kc >> 6), 1.0, 0.0).astype(F32)
    consts = (sel, headmask_k, st_mask)
    _gla_gates(smf, w2_ref, b2_ref, b_ref, 0, True)
    _gla_gates(smb, w2_ref, b2_ref, b_ref, 1, False)

    def body(step, carry):
        _gla_chunk(qkf, vf, b_ref, of_ref, st_ref, 0, True, consts, step)
        _gla_chunk(qkb, vb, b_ref, ob_ref, st_ref, 1, False, consts, step)
        return carry

    lax.fori_loop(0, BLK // CHUNK, body, 0)


def _glascan(proj, w2full, b2full):
    t = proj.shape[0]
    nb = t // BLK
    rev = _rev_block(nb)
    ident = lambda i: i
    qk = lambda f: pl.BlockSpec((BLK, 2 * B_KWIDTH), lambda i: (f(i), COL_BQK // (2 * B_KWIDTH)))
    vv = lambda f: pl.BlockSpec((BLK, B_VWIDTH), lambda i: (f(i), COL_BV // B_VWIDTH))
    sm = lambda f: pl.BlockSpec((BLK, 128), lambda i: (f(i), COL_SMALL // 128))
    outs = lambda f: pl.BlockSpec((BLK, B_VWIDTH), lambda i: (f(i), 0))
    out_sds = jax.ShapeDtypeStruct((t, B_VWIDTH), F32)
    return pl.pallas_call(
        _glascan_kernel,
        grid=(nb,),
        in_specs=[qk(ident), vv(ident), sm(ident), qk(rev), vv(rev), sm(rev),
                  pl.BlockSpec((128, 2 * B_KWIDTH), lambda i: (0, 0)),
                  pl.BlockSpec((1, 2 * B_KWIDTH), lambda i: (0, 0))],
        out_specs=[outs(ident), outs(rev)],
        out_shape=[out_sds, out_sds],
        scratch_shapes=[pltpu.VMEM((2, B_VWIDTH, B_KWIDTH), F32), pltpu.VMEM((2, BLK, B_KWIDTH), F32)],
        compiler_params=_cparams(("arbitrary",)),
        name="gla_scan",
    )(proj, proj, proj, proj, proj, proj, w2full, b2full)


def _attnprep_kernel(cq_ref, ckv_ref, qg_ref, kg_ref, cos_ref, sin_ref, q_ref, k_ref, v_ref):
    cos = cos_ref[...]
    sin = sin_ref[...]
    lane = lax.broadcasted_iota(I32, cos.shape, 1)
    first = (lane % 32) < 16

    def norm_rope(x, g):
        y = x * lax.rsqrt(jnp.sum(x * x, axis=-1, keepdims=True) * (1.0 / HD_C) + EPS) * g
        partner = jnp.where(first, pltpu.roll(y, HD_CP - 16, 1), pltpu.roll(y, 16, 1))
        return y * cos + partner * sin

    for h in range(H_C):
        seg = slice(h * HD_CP, (h + 1) * HD_CP)
        q_ref[:, seg] = (norm_rope(cq_ref[:, seg], qg_ref[...]) * (HD_C ** -0.5)).astype(BF16)
    for g in range(KV_C):
        seg = slice(g * HD_CP, (g + 1) * HD_CP)
        k_ref[:, seg] = norm_rope(ckv_ref[:, seg], kg_ref[...]).astype(BF16)
    v_ref[...] = ckv_ref[:, KV_C * HD_CP:2 * KV_C * HD_CP].astype(BF16)


def _attnprep(proj, qg, kg, cos_t, sin_t):
    t = proj.shape[0]
    tm = BLK
    qw, kw = H_C * HD_CP, KV_C * HD_CP
    return pl.pallas_call(
        _attnprep_kernel,
        grid=(t // tm,),
        in_specs=[pl.BlockSpec((tm, qw), lambda i: (i, COL_CQ // qw)),
                  pl.BlockSpec((tm, 2 * kw), lambda i: (i, COL_CKV // (2 * kw))),
                  pl.BlockSpec((1, HD_CP), lambda i: (0, 0)),
                  pl.BlockSpec((1, HD_CP), lambda i: (0, 0)),
                  pl.BlockSpec((tm, HD_CP), lambda i: (i, 0)),
                  pl.BlockSpec((tm, HD_CP), lambda i: (i, 0))],
        out_specs=[pl.BlockSpec((tm, qw), lambda i: (i, 0)),
                   pl.BlockSpec((tm, kw), lambda i: (i, 0)),
                   pl.BlockSpec((tm, kw), lambda i: (i, 0))],
        out_shape=[jax.ShapeDtypeStruct((t, qw), BF16),
                   jax.ShapeDtypeStruct((t, kw), BF16),
                   jax.ShapeDtypeStruct((t, kw), BF16)],
        compiler_params=_cparams(("arbitrary",)),
        name="attn_prep",
    )(proj, proj, qg, kg, cos_t, sin_t)


def _attn_kernel(sink_ref, q_ref, kp_ref, kc_ref, kn_ref, kx_ref, vp_ref, vc_ref, vn_ref, vx_ref,
                 o_ref, *, nq):
    qi = pl.program_id(0)
    nctx = CTX // ATT_BLOCK
    latent = qi >= nctx
    ql = lax.broadcasted_iota(I32, (ATT_BLOCK, ATT_BLOCK), 0)
    kl = lax.broadcasted_iota(I32, (ATT_BLOCK, ATT_BLOCK), 1)
    ok_prev = jnp.logical_and(qi - 1 >= nctx, kl >= ql)
    ok_next = jnp.logical_and(jnp.logical_and(latent, qi + 1 <= nq - 1), kl <= ql)
    ok_cur = jnp.logical_and(latent, kl >= 0)
    bias = jnp.concatenate([jnp.where(ok_prev, 0.0, NEG), jnp.where(ok_cur, 0.0, NEG),
                            jnp.where(ok_next, 0.0, NEG),
                            jnp.zeros((ATT_BLOCK, CTX), F32)], axis=1)
    for g in range(KV_C):
        seg = slice(g * HD_CP, (g + 1) * HD_CP)
        kcat = jnp.concatenate([kp_ref[:, seg], kc_ref[:, seg], kn_ref[:, seg], kx_ref[:, seg]], axis=0)
        vcat = jnp.concatenate([vp_ref[:, seg], vc_ref[:, seg], vn_ref[:, seg], vx_ref[:, seg]], axis=0)
        heads = [g * REP_C + r for r in range(REP_C)]
        cols = [slice(h * HD_CP, (h + 1) * HD_CP) for h in heads]
        s = [lax.dot_general(q_ref[:, c], kcat, (((1,), (1,)), ((), ())), preferred_element_type=F32) + bias
             for c in cols]
        m = [jnp.maximum(jnp.max(s[i], axis=-1, keepdims=True), sink_ref[heads[i]]) for i in range(REP_C)]
        p = [jnp.exp(s[i] - m[i]) for i in range(REP_C)]
        den = [jnp.sum(p[i], axis=-1, keepdims=True) + jnp.exp(sink_ref[heads[i]] - m[i]) for i in range(REP_C)]
        o = [jnp.dot(p[i].astype(BF16), vcat, preferred_element_type=F32) for i in range(REP_C)]
        for i in range(REP_C):
            o_ref[:, cols[i]] = (o[i] / den[i]).astype(BF16)


def _attn(qr, kr, vr, sink):
    t = qr.shape[0]
    nq = t // ATT_BLOCK
    nctx = CTX // ATT_BLOCK
    qw, kw = H_C * HD_CP, KV_C * HD_CP
    prev = lambda i: (jnp.maximum(i - 1, nctx), 0)
    cur = lambda i: (i, 0)
    nxt = lambda i: (jnp.minimum(jnp.maximum(i + 1, nctx), nq - 1), 0)
    kv = lambda f: pl.BlockSpec((ATT_BLOCK, kw), f)
    ctxs = pl.BlockSpec((CTX, kw), lambda i: (0, 0))
    return pl.pallas_call(
        functools.partial(_attn_kernel, nq=nq),
        grid=(nq,),
        in_specs=[pl.BlockSpec(memory_space=pltpu.SMEM),
                  pl.BlockSpec((ATT_BLOCK, qw), cur),
                  kv(prev), kv(cur), kv(nxt), ctxs, kv(prev), kv(cur), kv(nxt), ctxs],
        out_specs=pl.BlockSpec((ATT_BLOCK, qw), cur),
        out_shape=jax.ShapeDtypeStruct((t, qw), BF16),
        compiler_params=_cparams(("arbitrary",)),
        name="attn",
    )(sink, qr, kr, kr, kr, kr, vr, vr, vr, vr)


def _head_rms_gate(o, z, g):
    outs = []
    for h in range(o.shape[1] // 128):
        seg = o[:, h * 128:(h + 1) * 128]
        nrm = seg * lax.rsqrt(jnp.mean(seg * seg, axis=-1, keepdims=True) + EPS) * g
        outs.append(nrm * _silu(z[:, h * 128:(h + 1) * 128]))
    return jnp.concatenate(outs, axis=1)


def _merge_kernel(oaf, oab, az, obf, obb, bz, yc, gl, xs, dng, glag, wpa, wpb, wpc, wo, mod_ref,
                  o_ref, *, tm):
    i = pl.program_id(0)
    ya = _head_rms_gate(oaf[...] + oab[...], az[...], dng[...])
    yb = _head_rms_gate(obf[...] + obb[...], bz[...], glag[...])
    pa = _bdot(ya, wpa[...])
    pb = _bdot(yb, wpb[...])
    pc = jnp.dot(yc[...], wpc[...], preferred_element_type=F32)
    merged = (_sigmoid(gl[:, 0:D_MODEL]) * pa + _sigmoid(gl[:, D_MODEL:2 * D_MODEL]) * pb
              + _sigmoid(gl[:, 2 * D_MODEL:3 * D_MODEL]) * pc)
    y = _bdot(merged, wo[...])
    o_ref[...] = xs[...] + y * _res_gate(mod_ref, 0, tm, i * tm)


def _merge(oaf, oab, obf, obb, yc, proj, xs, dng, glag, wpa, wpb, wpc, wo, mod):
    t = xs.shape[0]
    tm = BLK
    row = lambda w, c=0: pl.BlockSpec((tm, w), lambda i: (i, c))
    full = lambda a: pl.BlockSpec(a.shape, lambda i: (0,) * a.ndim)
    return pl.pallas_call(
        functools.partial(_merge_kernel, tm=tm),
        grid=(t // tm,),
        in_specs=[row(A_WIDTH), row(A_WIDTH), row(A_WIDTH, COL_AZ // A_WIDTH),
                  row(B_VWIDTH), row(B_VWIDTH), row(B_VWIDTH, COL_BZ // B_VWIDTH),
                  row(H_C * HD_CP), row(3 * D_MODEL, COL_GL // (3 * D_MODEL)), row(D_MODEL),
                  full(dng), full(glag), full(wpa), full(wpb), full(wpc), full(wo), full(mod)],
        out_specs=row(D_MODEL),
        out_shape=jax.ShapeDtypeStruct((t, D_MODEL), F32),
        compiler_params=_cparams(("arbitrary",)),
        name="merge",
    )(oaf, oab, proj, obf, obb, proj, yc, proj, xs, dng, glag, wpa, wpb, wpc, wo, mod)


def _router_kernel(x_ref, mod_ref, g_ref, rwt_ref, rb_ref, h_ref, key_ref, gk_ref, cnt_ref, *, tm):
    i = pl.program_id(0)
    h = _norm_mod(x_ref[...], g_ref[...], mod_ref, 3 * D_MODEL, i * tm)
    h_ref[...] = h
    logit = lax.dot_general(rwt_ref[...], h, (((1,), (1,)), ((), ())),
                            precision=lax.Precision.HIGHEST, preferred_element_type=F32) + rb_ref[...]
    erow = lax.broadcasted_iota(I32, (N_EXPERTS, tm), 0)
    vals, idxs, sels = [], [], []
    cur = logit
    for _ in range(TOP_K):
        m = jnp.max(cur, axis=0, keepdims=True)
        idx = jnp.min(jnp.where(cur == m, erow, N_EXPERTS), axis=0, keepdims=True)
        sel = erow == idx
        vals.append(m)
        idxs.append(idx)
        sels.append(sel)
        cur = jnp.where(sel, -jnp.inf, cur)
    ex = [jnp.exp(v - vals[0]) for v in vals]
    den = ex[0] + ex[1] + ex[2] + ex[3]
    onehot = jnp.where(sels[0] | sels[1] | sels[2] | sels[3], 1.0, 0.0).astype(F32)
    ss = lax.broadcasted_iota(I32, (tm, tm), 0)
    tt = lax.broadcasted_iota(I32, (tm, tm), 1)
    before = jnp.where(ss < tt, 1.0, 0.0).astype(BF16)
    cnt = jnp.dot(onehot.astype(BF16), before, preferred_element_type=F32)
    ranks = [jnp.sum(jnp.where(s, cnt, 0.0), axis=0, keepdims=True) for s in sels]
    keys = [idxs[k] * tm + ranks[k].astype(I32) for k in range(TOP_K)]
    key_ref[...] = jnp.concatenate(keys + [jnp.full((8 - TOP_K, tm), -1, I32)], axis=0)
    cols = jnp.concatenate([e / den for e in ex] + [k.astype(F32) for k in keys]
                           + [jnp.zeros((128 - 2 * TOP_K, tm), F32)], axis=0)
    gk_ref[...] = cols.T
    cnt_ref[...] = jnp.broadcast_to(jnp.sum(onehot, axis=1, keepdims=True), (N_EXPERTS, 128))


def _router(xs, mod, g, rwt, rb):
    t = xs.shape[0]
    tm = BLK
    nt = t // tm
    return pl.pallas_call(
        functools.partial(_router_kernel, tm=tm),
        grid=(nt,),
        in_specs=[pl.BlockSpec((tm, D_MODEL), lambda i: (i, 0)),
                  pl.BlockSpec((8, 6 * D_MODEL), lambda i: (0, 0)),
                  pl.BlockSpec((1, D_MODEL), lambda i: (0, 0)),
                  pl.BlockSpec((N_EXPERTS, D_MODEL), lambda i: (0, 0)),
                  pl.BlockSpec((N_EXPERTS, 1), lambda i: (0, 0))],
        out_specs=[pl.BlockSpec((tm, D_MODEL), lambda i: (i, 0)),
                   pl.BlockSpec((8, tm), lambda i: (0, i)),
                   pl.BlockSpec((tm, 128), lambda i: (i, 0)),
                   pl.BlockSpec((None, N_EXPERTS, 128), lambda i: (i, 0, 0))],
        out_shape=[jax.ShapeDtypeStruct((t, D_MODEL), F32),
                   jax.ShapeDtypeStruct((8, t), I32),
                   jax.ShapeDtypeStruct((t, 128), F32),
                   jax.ShapeDtypeStruct((nt, N_EXPERTS, 128), F32)],
        compiler_params=_cparams(("arbitrary",)),
        name="router",
    )(xs, mod, g, rwt, rb)


RUN_CAP = 64
RUN_SHIFT = 6
RUN_BITS = (8, 4, 2, 1)


def _run_pieces(n8, nmax):
    n = jnp.minimum(n8, nmax)
    return [((n & bit) != 0, 8 * bit, pl.multiple_of(8 * (n & ~(2 * bit - 1)), 8)) for bit in RUN_BITS]


def _start_runs(i, base_ref, n8_ref, src_of, dst_of, sem):
    cap8 = RUN_CAP // 8
    total8 = jnp.int32(0)
    for e in range(N_EXPERTS):
        b = pl.multiple_of(base_ref[i * N_EXPERTS + e], 8)
        n = jnp.minimum(n8_ref[i * N_EXPERTS + e], cap8)

        def make_case(j, e=e, b=b):
            def case():
                off = 0
                for bit in RUN_BITS:
                    if j & bit:
                        pltpu.make_async_copy(src_of(e, b, off, 8 * bit), dst_of(e, b, off, 8 * bit), sem).start()
                        off += 8 * bit
                return jnp.int32(0)
            return case

        lax.switch(n, [make_case(j) for j in range(cap8 + 1)])
        total8 = total8 + n
    return total8


def _wait_rows(total8, desc_of, nbits):
    for j in range(nbits):
        @pl.when(((total8 >> j) & 1) == 1)
        def _():
            desc_of(8 << j).wait()


def _dispatch_kernel(base_ref, n8_ref, long_ref, zs_ref, key_ref, h_ref, xs_ref, runs, ovf, zbuf, pend_ref, sem, osem, zsem,
                     *, tm, nzero, nsteps):
    i = pl.program_id(0)

    @pl.when(i == 0)
    def _():
        zbuf[...] = jnp.zeros_like(zbuf)
        for z in range(nzero):
            @pl.when(zs_ref[z] >= 0)
            def _():
                pltpu.make_async_copy(zbuf, xs_ref.at[pl.ds(pl.multiple_of(zs_ref[z], MOE_BM), MOE_BM), :],
                                      zsem).start()
        for z in range(nzero):
            @pl.when(zs_ref[z] >= 0)
            def _():
                pltpu.make_async_copy(zbuf, xs_ref.at[pl.ds(0, MOE_BM), :], zsem).wait()

    hb = h_ref[...].astype(BF16)
    keys = [key_ref[k:k + 1, :] for k in range(TOP_K)]
    nrow = N_EXPERTS * RUN_CAP
    rr = lax.broadcasted_iota(I32, (nrow, tm), 0)
    rowkey = (rr >> RUN_SHIFT) * tm + (rr & (RUN_CAP - 1))
    hit = (rowkey == keys[0]) | (rowkey == keys[1]) | (rowkey == keys[2]) | (rowkey == keys[3])
    slot = i % 2
    runs[slot] = jnp.dot(jnp.where(hit, 1.0, 0.0).astype(BF16), hb, preferred_element_type=F32)
    total8 = _start_runs(i, base_ref, n8_ref,
                         lambda e, b, off, size: runs.at[slot, pl.ds(e * RUN_CAP + off, size), :],
                         lambda e, b, off, size: xs_ref.at[pl.ds(b + off, size), :], sem.at[slot])
    pend_ref[slot] = total8

    def wait_slot(s):
        _wait_rows(pend_ref[s], lambda size: pltpu.make_async_copy(
            runs.at[s, pl.ds(0, size), :], xs_ref.at[pl.ds(0, size), :], sem.at[s]), 9)

    @pl.when(i > 0)
    def _():
        wait_slot(1 - slot)

    @pl.when(i == nsteps - 1)
    def _():
        wait_slot(slot)

    def ovf_body(e, carry):
        n8 = n8_ref[i * N_EXPERTS + e]
        b = pl.multiple_of(base_ref[i * N_EXPERTS + e], 8)
        for c in range(1, tm // RUN_CAP):
            @pl.when(n8 > c * (RUN_CAP // 8))
            def _():
                ck = lax.broadcasted_iota(I32, (RUN_CAP, tm), 0) + (e * tm + c * RUN_CAP)
                hit_c = (ck == keys[0]) | (ck == keys[1]) | (ck == keys[2]) | (ck == keys[3])
                ovf[...] = jnp.dot(jnp.where(hit_c, 1.0, 0.0).astype(BF16), hb, preferred_element_type=F32)
                for phase in (0, 1):
                    for live, size, off in _run_pieces(n8 - c * (RUN_CAP // 8), RUN_CAP // 8):
                        @pl.when(live)
                        def _():
                            cp = pltpu.make_async_copy(
                                ovf.at[pl.ds(off, size), :],
                                xs_ref.at[pl.ds(b + c * RUN_CAP + off, size), :], osem)
                            if phase == 0:
                                cp.start()
                            else:
                                cp.wait()
        return carry

    @pl.when(long_ref[i] > 0)
    def _():
        lax.fori_loop(0, N_EXPERTS, ovf_body, 0)


def _dispatch(plan, keys, h):
    base, n8, long_run, zstart, p_rows = plan
    t = h.shape[0]
    tm = BLK
    grid_spec = pltpu.PrefetchScalarGridSpec(
        num_scalar_prefetch=4,
        grid=(t // tm,),
        in_specs=[pl.BlockSpec((8, tm), lambda i, b, n, lg, z: (0, i)),
                  pl.BlockSpec((tm, D_MODEL), lambda i, b, n, lg, z: (i, 0))],
        out_specs=pl.BlockSpec(memory_space=pl.ANY),
        scratch_shapes=[pltpu.VMEM((2, N_EXPERTS * RUN_CAP, D_MODEL), F32), pltpu.VMEM((RUN_CAP, D_MODEL), F32),
                        pltpu.VMEM((MOE_BM, D_MODEL), F32), pltpu.SMEM((2,), I32),
                        pltpu.SemaphoreType.DMA((2,)), pltpu.SemaphoreType.DMA(()), pltpu.SemaphoreType.DMA(())],
    )
    return pl.pallas_call(
        functools.partial(_dispatch_kernel, tm=tm, nzero=zstart.shape[0], nsteps=t // tm),
        grid_spec=grid_spec,
        out_shape=jax.ShapeDtypeStruct((p_rows, D_MODEL), F32),
        compiler_params=_cparams(("arbitrary",)),
        name="moe_dispatch",
    )(base, n8, long_run, zstart, keys, h)


def _expert_kernel(be_ref, nused_ref, first_ref, slot_ref, nxt_ref, x_ref, wgu_hbm, bgu_ref, wdn_hbm, bdn_ref,
                   y_ref, gu_stage, dn_stage, wgu_bf, wdn_bf, sem, *, layer):
    b = pl.program_id(0)

    def fetch(e, s):
        return (pltpu.make_async_copy(wgu_hbm.at[layer, e], gu_stage.at[s], sem.at[0, s]),
                pltpu.make_async_copy(wdn_hbm.at[layer, e], dn_stage.at[s], sem.at[1, s]))

    @pl.when(b == 0)
    def _():
        for cp in fetch(be_ref[0], 0):
            cp.start()

    @pl.when(first_ref[b] == 1)
    def _():
        s = slot_ref[b]
        for cp in fetch(be_ref[b], s):
            cp.wait()
        wgu_bf[...] = gu_stage[s].astype(BF16)
        wdn_bf[...] = dn_stage[s].astype(BF16)

        @pl.when(nxt_ref[b] >= 0)
        def _():
            for cp in fetch(nxt_ref[b], 1 - s):
                cp.start()

    @pl.when(b < nused_ref[0])
    def _():
        gu = jnp.dot(x_ref[...].astype(BF16), wgu_bf[...], preferred_element_type=F32) + bgu_ref[...]
        g_ = jnp.minimum(gu[:, :D_FF], SWIGLU_LIMIT)
        u_ = jnp.clip(gu[:, D_FF:], -SWIGLU_LIMIT, SWIGLU_LIMIT)
        act = (u_ + 1.0) * (g_ * _sigmoid(g_ * SWIGLU_ALPHA))
        y_ref[...] = jnp.dot(act.astype(BF16), wdn_bf[...], preferred_element_type=F32) + bdn_ref[...]

    @pl.when(b >= nused_ref[0])
    def _():
        y_ref[...] = jnp.zeros_like(y_ref)


def _experts(plan, xsorted, w_gu, b_gu, w_dn, b_dn, layer):
    blk_e, nused, first, slot, nxt = plan
    p_rows = xsorted.shape[0]
    nblk = p_rows // MOE_BM
    depth = w_gu.shape[0]
    bsel = lambda b, be, nu, fi, sl, nx: (layer, be[b], 0, 0)
    grid_spec = pltpu.PrefetchScalarGridSpec(
        num_scalar_prefetch=5,
        grid=(nblk,),
        in_specs=[pl.BlockSpec((MOE_BM, D_MODEL), lambda b, be, nu, fi, sl, nx: (jnp.minimum(b, nu[0] - 1), 0)),
                  pl.BlockSpec(memory_space=pl.ANY),
                  pl.BlockSpec((None, None, 1, 2 * D_FF), bsel),
                  pl.BlockSpec(memory_space=pl.ANY),
                  pl.BlockSpec((None, None, 1, D_MODEL), bsel)],
        out_specs=pl.BlockSpec((MOE_BM, D_MODEL), lambda b, be, nu, fi, sl, nx: (b, 0)),
        scratch_shapes=[pltpu.VMEM((2, D_MODEL, 2 * D_FF), F32), pltpu.VMEM((2, D_FF, D_MODEL), F32),
                        pltpu.VMEM((D_MODEL, 2 * D_FF), BF16), pltpu.VMEM((D_FF, D_MODEL), BF16),
                        pltpu.SemaphoreType.DMA((2, 2))],
    )
    return pl.pallas_call(
        functools.partial(_expert_kernel, layer=layer),
        grid_spec=grid_spec,
        out_shape=jax.ShapeDtypeStruct((p_rows, D_MODEL), F32),
        compiler_params=_cparams(("arbitrary",)),
        name="moe_experts",
    )(blk_e, nused, first, slot, nxt, xsorted, w_gu, b_gu.reshape(depth, N_EXPERTS, 1, 2 * D_FF), w_dn,
      b_dn.reshape(depth, N_EXPERTS, 1, D_MODEL))


def _combine_kernel(base_ref, n8_ref, long_ref, y_ref, gk_ref, xs_ref, mod_ref, o_ref, runs, ovf, pend_ref, sem, osem,
                    *, tm, nsteps):
    i = pl.program_id(0)

    slot = i % 2

    def gather(tile, s):
        pend_ref[s] = _start_runs(tile, base_ref, n8_ref,
                                  lambda e, b, off, size: y_ref.at[pl.ds(b + off, size), :],
                                  lambda e, b, off, size: runs.at[s, pl.ds(e * RUN_CAP + off, size), :],
                                  sem.at[s])

    @pl.when(i == 0)
    def _():
        runs[...] = jnp.zeros_like(runs)
        ovf[...] = jnp.zeros_like(ovf)
        gather(0, 0)

    @pl.when(i + 1 < nsteps)
    def _():
        gather(i + 1, 1 - slot)

    _wait_rows(pend_ref[slot], lambda size: pltpu.make_async_copy(
        y_ref.at[pl.ds(0, size), :], runs.at[slot, pl.ds(0, size), :], sem.at[slot]), 9)
    gk = gk_ref[...]
    gates = [gk[:, k:k + 1] for k in range(TOP_K)]
    keys = [gk[:, TOP_K + k:TOP_K + k + 1].astype(I32) for k in range(TOP_K)]
    nrow = N_EXPERTS * RUN_CAP
    cc = lax.broadcasted_iota(I32, (1, nrow), 1)
    colkey = (cc >> RUN_SHIFT) * tm + (cc & (RUN_CAP - 1))
    w = jnp.where(colkey == keys[0], gates[0], 0.0)
    for k in range(1, TOP_K):
        w = w + jnp.where(colkey == keys[k], gates[k], 0.0)
    acc = jnp.dot(w.astype(BF16), runs[slot].astype(BF16), preferred_element_type=F32)
    o_ref[...] = xs_ref[...] + acc * _res_gate(mod_ref, 3 * D_MODEL, tm, i * tm)

    def ovf_body(e, carry):
        n8 = n8_ref[i * N_EXPERTS + e]
        b = pl.multiple_of(base_ref[i * N_EXPERTS + e], 8)
        for c in range(1, tm // RUN_CAP):
            @pl.when(n8 > c * (RUN_CAP // 8))
            def _():
                for phase in (0, 1):
                    for live, size, off in _run_pieces(n8 - c * (RUN_CAP // 8), RUN_CAP // 8):
                        @pl.when(live)
                        def _():
                            cp = pltpu.make_async_copy(
                                y_ref.at[pl.ds(b + c * RUN_CAP + off, size), :],
                                ovf.at[pl.ds(off, size), :], osem)
                            if phase == 0:
                                cp.start()
                            else:
                                cp.wait()
                ck = lax.broadcasted_iota(I32, (1, RUN_CAP), 1) + (e * tm + c * RUN_CAP)
                wc = jnp.where(ck == keys[0], gates[0], 0.0)
                for k in range(1, TOP_K):
                    wc = wc + jnp.where(ck == keys[k], gates[k], 0.0)
                part = jnp.dot(wc.astype(BF16), ovf[...].astype(BF16), preferred_element_type=F32)
                o_ref[...] = o_ref[...] + part * _res_gate(mod_ref, 3 * D_MODEL, tm, i * tm)
        return carry

    @pl.when(long_ref[i] > 0)
    def _():
        lax.fori_loop(0, N_EXPERTS, ovf_body, 0)


def _combine(plan, y, gk_col, xs, mod):
    base, n8, long_run, _, _ = plan
    t = xs.shape[0]
    tm = BLK
    grid_spec = pltpu.PrefetchScalarGridSpec(
        num_scalar_prefetch=3,
        grid=(t // tm,),
        in_specs=[pl.BlockSpec(memory_space=pl.ANY),
                  pl.BlockSpec((tm, 128), lambda i, b, n, lg: (i, 0)),
                  pl.BlockSpec((tm, D_MODEL), lambda i, b, n, lg: (i, 0)),
                  pl.BlockSpec((8, 6 * D_MODEL), lambda i, b, n, lg: (0, 0))],
        out_specs=pl.BlockSpec((tm, D_MODEL), lambda i, b, n, lg: (i, 0)),
        scratch_shapes=[pltpu.VMEM((2, N_EXPERTS * RUN_CAP, D_MODEL), F32), pltpu.VMEM((RUN_CAP, D_MODEL), F32),
                        pltpu.SMEM((2,), I32), pltpu.SemaphoreType.DMA((2,)), pltpu.SemaphoreType.DMA(())],
    )
    return pl.pallas_call(
        functools.partial(_combine_kernel, tm=tm, nsteps=t // tm),
        grid_spec=grid_spec,
        out_shape=jax.ShapeDtypeStruct((t, D_MODEL), F32),
        compiler_params=_cparams(("arbitrary",)),
        name="moe_combine",
    )(base, n8, long_run, y, gk_col, xs, mod)


def _pad_heads(w, nh, hd, hdp):
    d = w.shape[0]
    return jnp.pad(w.reshape(d, nh, hd), ((0, 0), (0, 0), (0, hdp - hd))).reshape(d, nh * hdp)


def _layout_w_in(w):
    return _layout_w_in_f32(w).astype(BF16)


def _layout_w_in_f32(w):
    pts = np.cumsum([A_WIDTH] * 4 + [2 * H_A, 2 * H_A, B_KWIDTH, B_KWIDTH, B_VWIDTH, B_VWIDTH,
                                     2 * GLA_RANK, C_WIDTH, KV_C * HD_C, KV_C * HD_C])
    (aq, ak, av, az, aa, ab, bq, bk, bv, bz, bg, cq, ck, cv, gl) = jnp.split(w, pts.tolist(), axis=1)
    small = jnp.concatenate([aa, ab, bg], axis=1)
    small = jnp.pad(small, ((0, 0), (0, 256 - small.shape[1])))
    cols = [gl, aq, ak, av, az, _pad_heads(cq, H_C, HD_C, HD_CP), _pad_heads(ck, KV_C, HD_C, HD_CP),
            _pad_heads(cv, KV_C, HD_C, HD_CP), bq, bk, bv, bz, small]
    out = jnp.concatenate(cols, axis=1)
    assert out.shape[1] == IN_COLS_P
    return out


def _rope_tables(t):
    s_len = t - CTX
    half = HD_C // 2
    inv_freq = ROPE_THETA ** (-jnp.arange(0, half, 2, dtype=F32) / half)
    pos = jnp.arange(s_len)
    rows = (pos // GRID_W).astype(F32)[:, None] * inv_freq[None, :]
    cols = (pos % GRID_W).astype(F32)[:, None] * inv_freq[None, :]
    cr, sr, cc, sc = jnp.cos(rows), jnp.sin(rows), jnp.cos(cols), jnp.sin(cols)
    zpad = jnp.zeros((s_len, HD_CP - HD_C), F32)
    cos_l = jnp.concatenate([cr, cr, cc, cc, zpad], axis=1)
    sin_l = jnp.concatenate([-sr, sr, -sc, sc, zpad], axis=1)
    cos_t = jnp.concatenate([jnp.ones((CTX, HD_CP), F32), cos_l], axis=0)
    sin_t = jnp.concatenate([jnp.zeros((CTX, HD_CP), F32), sin_l], axis=0)
    return cos_t, sin_t


def _lane_vec(v, width=128):
    v = v.reshape(1, -1).astype(F32)
    return jnp.pad(v, ((0, 0), (0, width - v.shape[1])))


def _moe_plan(cnt_tile, tk):
    nt = cnt_tile.shape[0]
    pad8 = (cnt_tile + 7) // 8 * 8
    total = jnp.sum(pad8, axis=0)
    padded = (total + MOE_BM - 1) // MOE_BM * MOE_BM
    pend = jnp.cumsum(padded)
    estart = pend - padded
    base = estart[None, :] + jnp.cumsum(pad8, axis=0) - pad8
    nblk = (tk + nt * N_EXPERTS * 7 + N_EXPERTS * (MOE_BM - 1) + MOE_BM - 1) // MOE_BM
    p_rows = nblk * MOE_BM
    blk = jnp.arange(nblk, dtype=I32)
    blk_e = jnp.minimum(jnp.sum((pend[None, :] <= (blk * MOE_BM)[:, None]).astype(I32), axis=1), N_EXPERTS - 1)
    nused = (pend[-1] // MOE_BM).astype(I32)
    prev_e = jnp.concatenate([jnp.full((1,), -1, I32), blk_e[:-1]])
    first = ((blk_e != prev_e) & (blk < nused)).astype(I32)
    slot = (jnp.cumsum(first) - 1) % 2
    pos = jnp.where(first == 1, blk, nblk)
    nxt_pos = jnp.concatenate([lax.cummin(pos, axis=0, reverse=True)[1:], jnp.full((1,), nblk, I32)])
    nxt = jnp.where(nxt_pos < nblk, blk_e[jnp.minimum(nxt_pos, nblk - 1)], -1)
    last_blk = jnp.where(total > 0, pend - MOE_BM, -1)
    ntail = (nt * N_EXPERTS * 7 + N_EXPERTS * (MOE_BM - 1)) // MOE_BM + 1
    tail = pend[-1] + jnp.arange(ntail, dtype=I32) * MOE_BM
    zstart = jnp.concatenate([last_blk, jnp.where(tail < p_rows, tail, -1)]).astype(I32)
    eplan = (blk_e, nused.reshape(1), first, slot.astype(I32), nxt.astype(I32))
    long_run = (jnp.max(pad8, axis=1) > RUN_CAP).astype(I32)
    mplan = (base.reshape(-1).astype(I32), (pad8 // 8).reshape(-1).astype(I32), long_run, zstart, p_rows)
    return mplan, eplan


def kernel(x, c, ctx, c_ctx, ada_w, ada_b, norm_mix_g, norm_ffn_g, w_in, dn_conv_w, dn_a_log, dn_dt_bias,
           dn_norm_g, gla_w2, gla_b2, gla_norm_g, attn_q_norm_g, attn_k_norm_g, attn_sink, w_branch_a,
           w_branch_b, w_branch_c, w_out, router_w, router_b, w_gate_up, b_gate_up, w_down, b_down):
    assert x.shape[0] == 1 and c.shape[0] == 1 and ctx.shape[1] == CTX
    depth = ada_w.shape[0]
    xs = jnp.concatenate([ctx[0], x[0]], axis=0)
    t = xs.shape[0]
    assert t % BLK == 0 and (t - CTX) % GRID_W == 0
    cc = jnp.concatenate([c, c_ctx[None, :], jnp.zeros((6, D_MODEL), F32)], axis=0)
    mods = _ada_mod(cc, ada_w, ada_b)
    cos_t, sin_t = _rope_tables(t)
    for l in range(depth):
        mod = mods[l]
        proj = _inproj(xs, mod, norm_mix_g[l][None, :], _layout_w_in(w_in[l]))
        qa, ka, va, gcol = _dnprep(proj, dn_conv_w[l], _lane_vec(dn_a_log[l]), _lane_vec(dn_dt_bias[l]))
        oaf, oab = _dnscan(qa, ka, va, gcol)
        w2 = gla_w2[l].astype(F32)
        w2full = jnp.zeros((128, 2 * B_KWIDTH), F32)
        for d in range(2):
            r0 = 4 * H_A + d * GLA_RANK
            w2full = w2full.at[r0:r0 + GLA_RANK, d * B_KWIDTH:(d + 1) * B_KWIDTH].set(w2[d])
        obf, obb = _glascan(proj, w2full, gla_b2[l].reshape(1, 2 * B_KWIDTH).astype(F32))
        qg = _lane_vec(attn_q_norm_g[l])
        kg = _lane_vec(attn_k_norm_g[l])
        qr, kr, vr = _attnprep(proj, qg, kg, cos_t, sin_t)
        yc = _attn(qr, kr, vr, attn_sink[l].astype(F32))
        wpc = jnp.pad(w_branch_c[l].reshape(H_C, HD_C, D_MODEL),
                      ((0, 0), (0, HD_CP - HD_C), (0, 0))).reshape(H_C * HD_CP, D_MODEL)
        xs = _merge(oaf, oab, obf, obb, yc, proj, xs,
                    dn_norm_g[l][None, :], gla_norm_g[l][None, :],
                    w_branch_a[l].astype(BF16), w_branch_b[l].astype(BF16), wpc.astype(BF16),
                    w_out[l].astype(BF16), mod)
        h2, keys, gk_col, cnt = _router(xs, mod, norm_ffn_g[l][None, :], router_w[l].T, router_b[l][:, None])
        mplan, eplan = _moe_plan(cnt[:, :, 0].astype(I32), t * TOP_K)
        xsorted = _dispatch(mplan, keys, h2)
        y = _experts(eplan, xsorted, w_gate_up, b_gate_up, w_down, b_down, l)
        xs = _combine(mplan, y, gk_col, xs, mod)
    return xs[CTX:][None]
```

```python
import functools

import jax
import jax.numpy as jnp
import numpy as np
from jax import lax
from jax.experimental import pallas as pl
from jax.experimental.pallas import tpu as pltpu

F32 = jnp.float32
BF16 = jnp.bfloat16
I32 = jnp.int32

D_MODEL = 1024
DEPTH = 4
GRID_W = 64
CTX = 256
H_A = 4
HD_A = 128
A_WIDTH = H_A * HD_A
CONV_W = 5
H_B = 4
DK_B = 64
DV_B = 128
B_KWIDTH = H_B * DK_B
B_VWIDTH = H_B * DV_B
GLA_RANK = 16
GLA_NORMALIZER = 16.0
H_C = 8
KV_C = 2
REP_C = H_C // KV_C
HD_C = 64
HD_CP = 128
C_WIDTH = H_C * HD_C
ATT_BLOCK = 128
ROPE_THETA = 10000.0
CHUNK = 64
SUB = 16
N_EXPERTS = 32
TOP_K = 4
D_FF = 1024
SWIGLU_LIMIT = 7.0
SWIGLU_ALPHA = 1.702
EPS = 1e-6
NEG = -1e30
CHUNK_SHIFT = CHUNK.bit_length() - 1
SUB_SHIFT = SUB.bit_length() - 1
DK_SHIFT = DK_B.bit_length() - 1
DV_SHIFT = DV_B.bit_length() - 1

BLK = 256
MOE_BM = 512
VMEM_LIMIT = 56 * 1024 * 1024

COL_GL = 0
COL_QKV = 3072
COL_AZ = 4608
COL_CQ = 5120
COL_CKV = 6144
COL_BQK = 6656
COL_BV = 7168
COL_BZ = 7680
COL_SMALL = 8192
IN_COLS_P = 8448
IN_TN = 768


def _pick(n, cands):
    for c in cands:
        if n % c == 0:
            return c
    raise ValueError(f"no tile for {n}")


def _cparams(sem):
    return pltpu.CompilerParams(dimension_semantics=sem, vmem_limit_bytes=VMEM_LIMIT)


def _bdot(a, b):
    return jnp.dot(a.astype(BF16), b.astype(BF16), preferred_element_type=F32)


def _bdot_nt(a, b):
    return lax.dot_general(a.astype(BF16), b.astype(BF16), (((1,), (1,)), ((), ())),
                           preferred_element_type=F32)


def _bdot_tn(a, b):
    return lax.dot_general(a.astype(BF16), b.astype(BF16), (((0,), (0,)), ((), ())),
                           preferred_element_type=F32)


def _fdot(a, b):
    return jnp.dot(a, b, precision=lax.Precision.HIGHEST, preferred_element_type=F32)


def _split2(a):
    hi = a.astype(BF16)
    lo = (a - hi.astype(F32)).astype(BF16)
    return hi, lo


def _dot3(a, b):
    ah, al = _split2(a)
    bh, bl = _split2(b)
    d = functools.partial(jnp.dot, preferred_element_type=F32)
    return d(ah, bh) + (d(ah, bl) + d(al, bh))


def _cumsum_dot(tri, x):
    t = tri.astype(BF16)
    hi = x.astype(BF16)
    r1 = x - hi.astype(F32)
    mid = r1.astype(BF16)
    lo = (r1 - mid.astype(F32)).astype(BF16)
    d = functools.partial(jnp.dot, preferred_element_type=F32)
    return d(t, hi) + (d(t, mid) + d(t, lo))


def _sigmoid(x):
    return 1.0 / (1.0 + jnp.exp(-x))


def _silu(x):
    return x * _sigmoid(x)


def _softplus(x):
    return jnp.maximum(x, 0.0) + jnp.log(1.0 + jnp.exp(-jnp.abs(x)))


def _ada_kernel(cc_ref, w_ref, b_ref, o_ref):
    o_ref[...] = _fdot(_silu(cc_ref[...]), w_ref[...]) + b_ref[...]


def _ada_mod(cc, ada_w, ada_b):
    depth = ada_w.shape[0]
    tn = 1536
    return pl.pallas_call(
        _ada_kernel,
        grid=(depth, 6 * D_MODEL // tn),
        in_specs=[pl.BlockSpec((8, D_MODEL), lambda l, j: (0, 0)),
                  pl.BlockSpec((None, D_MODEL, tn), lambda l, j: (l, 0, j)),
                  pl.BlockSpec((None, 1, tn), lambda l, j: (l, 0, j))],
        out_specs=pl.BlockSpec((None, 8, tn), lambda l, j: (l, 0, j)),
        out_shape=jax.ShapeDtypeStruct((depth, 8, 6 * D_MODEL), F32),
        compiler_params=_cparams(("arbitrary", "arbitrary")),
        name="ada_mod",
    )(cc, ada_w, ada_b.reshape(depth, 1, 6 * D_MODEL))


def _norm_mod(x, g, mod_ref, moff, row0):
    tm = x.shape[0]
    y = x * lax.rsqrt(jnp.mean(x * x, axis=-1, keepdims=True) + EPS) * g
    isc = (row0 + lax.broadcasted_iota(I32, (tm, 1), 0)) < CTX
    shift = jnp.where(isc, mod_ref[1:2, moff:moff + D_MODEL], mod_ref[0:1, moff:moff + D_MODEL])
    scale = jnp.where(isc, mod_ref[1:2, moff + D_MODEL:moff + 2 * D_MODEL],
                      mod_ref[0:1, moff + D_MODEL:moff + 2 * D_MODEL])
    return y * (1.0 + scale) + shift


def _res_gate(mod_ref, moff, tm, row0):
    isc = (row0 + lax.broadcasted_iota(I32, (tm, 1), 0)) < CTX
    return jnp.where(isc, mod_ref[1:2, moff + 2 * D_MODEL:moff + 3 * D_MODEL],
                     mod_ref[0:1, moff + 2 * D_MODEL:moff + 3 * D_MODEL])


def _inproj_kernel(x_ref, mod_ref, g_ref, w_ref, o_ref, h_ref, *, tm):
    i = pl.program_id(0)

    @pl.when(pl.program_id(1) == 0)
    def _():
        h_ref[...] = _norm_mod(x_ref[...], g_ref[...], mod_ref, 0, i * tm).astype(BF16)

    o_ref[...] = jnp.dot(h_ref[...], w_ref[...], preferred_element_type=F32)


def _inproj(xs, mod, g, w):
    t = xs.shape[0]
    tm = _pick(t, (1664, 1280, 640, 256))
    return pl.pallas_call(
        functools.partial(_inproj_kernel, tm=tm),
        grid=(t // tm, IN_COLS_P // IN_TN),
        in_specs=[pl.BlockSpec((tm, D_MODEL), lambda i, j: (i, 0)),
                  pl.BlockSpec((8, 6 * D_MODEL), lambda i, j: (0, 0)),
                  pl.BlockSpec((1, D_MODEL), lambda i, j: (0, 0)),
                  pl.BlockSpec((D_MODEL, IN_TN), lambda i, j: (0, j))],
        out_specs=pl.BlockSpec((tm, IN_TN), lambda i, j: (i, j)),
        out_shape=jax.ShapeDtypeStruct((t, IN_COLS_P), F32),
        scratch_shapes=[pltpu.VMEM((tm, D_MODEL), BF16)],
        compiler_params=_cparams(("arbitrary", "arbitrary")),
        name="inproj",
    )(xs, mod, g, w)


def _tri_blockdiag(n, lower):
    ii = lax.broadcasted_iota(I32, (n, n), 0)
    jj = lax.broadcasted_iota(I32, (n, n), 1)
    same = (ii >> CHUNK_SHIFT) == (jj >> CHUNK_SHIFT)
    tri = (ii >= jj) if lower else (ii <= jj)
    return jnp.where(same, jnp.where(tri, 1.0, 0.0), 0.0).astype(F32)


def _dnprep_kernel(main_ref, prev_ref, next_ref, small_ref, cw_ref, alog_ref, dtb_ref,
                   q_ref, k_ref, v_ref, gcol_ref, ext_ref, *, nb):
    i = pl.program_id(0)
    use_prev = i >= 2
    use_next = jnp.logical_and(i >= 1, i <= nb - 2)
    ext_ref[0:8, :] = jnp.where(use_prev, prev_ref[...], 0.0)
    ext_ref[8:8 + BLK, :] = main_ref[...]
    ext_ref[8 + BLK:16 + BLK, :] = jnp.where(use_next, next_ref[...], 0.0)
    acc = ext_ref[6:6 + BLK, :] * cw_ref[0:1, :]
    for d in range(1, CONV_W):
        acc = acc + ext_ref[6 + d:6 + d + BLK, :] * cw_ref[d:d + 1, :]
    s = _silu(acc)
    for h in range(H_A):
        for part, ref, mul in ((0, q_ref, HD_A ** -0.5), (1, k_ref, 1.0)):
            seg = s[:, part * A_WIDTH + h * HD_A: part * A_WIDTH + (h + 1) * HD_A]
            nrm = seg * lax.rsqrt(jnp.sum(seg * seg, axis=-1, keepdims=True) + EPS)
            ref[:, h * HD_A:(h + 1) * HD_A] = nrm * mul
    v_ref[...] = s[:, 2 * A_WIDTH:3 * A_WIDTH]
    sm = small_ref[...]
    lane = lax.broadcasted_iota(I32, sm.shape, 1)
    g = -jnp.exp(alog_ref[...]) * _softplus(sm + dtb_ref[...])
    gb = jnp.where(lane < 2 * H_A, g, jnp.where(lane < 4 * H_A, _sigmoid(sm), 0.0))
    cf = _fdot(_tri_blockdiag(BLK, True), gb)
    cr = _fdot(_tri_blockdiag(BLK, False), gb)
    gc = jnp.where(lane < H_A, cf, jnp.where(lane < 2 * H_A, cr, gb))
    gcol_ref[...] = gc


def _dnprep(proj, conv_w, alog_vec, dtb_vec):
    t = proj.shape[0]
    nb = t // BLK
    qkv_blk = COL_QKV // (3 * A_WIDTH)
    last8 = t // 8 - 1
    out_sds = jax.ShapeDtypeStruct((t, A_WIDTH), F32)
    return pl.pallas_call(
        functools.partial(_dnprep_kernel, nb=nb),
        grid=(nb,),
        in_specs=[pl.BlockSpec((BLK, 3 * A_WIDTH), lambda i: (i, qkv_blk)),
                  pl.BlockSpec((8, 3 * A_WIDTH), lambda i: (jnp.maximum(i * (BLK // 8) - 1, 0), qkv_blk)),
                  pl.BlockSpec((8, 3 * A_WIDTH), lambda i: (jnp.minimum((i + 1) * (BLK // 8), last8), qkv_blk)),
                  pl.BlockSpec((BLK, 128), lambda i: (i, COL_SMALL // 128)),
                  pl.BlockSpec((CONV_W, 3 * A_WIDTH), lambda i: (0, 0)),
                  pl.BlockSpec((1, 128), lambda i: (0, 0)),
                  pl.BlockSpec((1, 128), lambda i: (0, 0))],
        out_specs=[pl.BlockSpec((BLK, A_WIDTH), lambda i: (i, 0)),
                   pl.BlockSpec((BLK, A_WIDTH), lambda i: (i, 0)),
                   pl.BlockSpec((BLK, A_WIDTH), lambda i: (i, 0)),
                   pl.BlockSpec((BLK, 128), lambda i: (i, 0))],
        out_shape=[out_sds, out_sds, out_sds,
                   jax.ShapeDtypeStruct((t, 128), F32)],
        scratch_shapes=[pltpu.VMEM((BLK + 16, 3 * A_WIDTH), F32)],
        compiler_params=_cparams(("arbitrary",)),
        name="dn_prep",
    )(proj, proj, proj, proj, conv_w, alog_vec, dtb_vec)


def _dot3_all(a_list, b_list):
    d = functools.partial(jnp.dot, preferred_element_type=F32)
    sa = [_split2(a) for a in a_list]
    sb = [_split2(b) for b in b_list]
    hh = [d(a[0], b[0]) for a, b in zip(sa, sb)]
    hl = [d(a[0], b[1]) for a, b in zip(sa, sb)]
    lh = [d(a[1], b[0]) for a, b in zip(sa, sb)]
    return [x + (y + z) for x, y, z in zip(hh, hl, lh)]


def _unit_tri_inverse_all(l_mats, masks):
    eye, m_diag, m_l1, m_l2 = masks
    ld = [l * m_diag for l in l_mats]
    x = [eye - a for a in ld]
    p = _dot3_all(ld, ld)
    for it in range(3):
        xp = _dot3_all(x, p)
        if it < 2:
            p = _dot3_all(p, p)
        x = [a + b for a, b in zip(x, xp)]
    for m in (m_l1, m_l2):
        cx = _dot3_all([l * m for l in l_mats], x)
        xcx = _dot3_all(x, cx)
        x = [a - b for a, b in zip(x, xcx)]
    return x


def _dn_masks():
    ii = lax.broadcasted_iota(I32, (CHUNK, CHUNK), 0)
    jj = lax.broadcasted_iota(I32, (CHUNK, CHUNK), 1)
    one = lambda c: jnp.where(c, 1.0, 0.0).astype(F32)
    eye = one(ii == jj)
    m_diag = one((ii >> SUB_SHIFT) == (jj >> SUB_SHIFT))
    m_l2 = one((ii >> (SUB_SHIFT + 1)) != (jj >> (SUB_SHIFT + 1)))
    m_l1 = 1.0 - m_diag - m_l2
    return ii, jj, (eye, m_diag, m_l1, m_l2)


def _dn_local(items, ii, jj, masks):
    n = len(items)
    dec, lmat, qk, kb, eg = [], [], [], [], []
    for q, k, v, gcol, grow, bcol, fwd in items:
        incl = (ii >= jj) if fwd else (ii <= jj)
        dec.append(jnp.exp(jnp.where(incl, gcol - grow, NEG)))
        kb.append(k * bcol)
        eg.append(jnp.exp(gcol))
    kh = [it[1].astype(BF16) for it in items]
    kk = [_bdot_nt(kb[i], kh[i]) for i in range(n)]
    qkr = [_bdot_nt(items[i][0], kh[i]) for i in range(n)]
    for i in range(n):
        fwd = items[i][6]
        strict = (ii > jj) if fwd else (ii < jj)
        lmat.append(kk[i] * jnp.where(strict, dec[i], 0.0))
        qk.append((qkr[i] * dec[i]).astype(BF16))
    rhs = [jnp.concatenate([items[i][2] * items[i][5], kb[i] * eg[i]], axis=1) for i in range(n)]
    sol = _dot3_all(_unit_tri_inverse_all(lmat, masks), rhs)
    out = []
    for i in range(n):
        q, k, _, gcol, _, _, fwd = items[i]
        glast = gcol[CHUNK - 1:CHUNK, :] if fwd else gcol[0:1, :]
        out.append((sol[i][:, :HD_A], sol[i][:, HD_A:].astype(BF16), qk[i], (q * eg[i]).astype(BF16),
                    (k * jnp.exp(glast - gcol)).astype(BF16), jnp.exp(glast)))
    return out


def _dn_step(local, states):
    n = len(local)
    sb = [s.astype(BF16) for s in states]
    d = functools.partial(jnp.dot, preferred_element_type=F32)
    ws = [d(local[i][1], sb[i]) for i in range(n)]
    qs = [d(local[i][3], sb[i]) for i in range(n)]
    v_new = [(local[i][0] - ws[i]).astype(BF16) for i in range(n)]
    o2 = [d(local[i][2], v_new[i]) for i in range(n)]
    kv = [lax.dot_general(local[i][4], v_new[i], (((0,), (0,)), ((), ())), preferred_element_type=F32)
          for i in range(n)]
    return [qs[i] + o2[i] for i in range(n)], [states[i] * local[i][5] + kv[i] for i in range(n)]


def _dnscan_kernel(qf, kf, vf, gcf, qb, kb, vb, gcb, of_ref, ob_ref, s_ref):
    @pl.when(pl.program_id(0) == 0)
    def _():
        s_ref[...] = jnp.zeros_like(s_ref)

    ii, jj, masks = _dn_masks()
    nch = BLK // CHUNK
    pick = jnp.where(lax.broadcasted_iota(I32, (16, 128), 0) == lax.broadcasted_iota(I32, (16, 128), 1),
                     1.0, 0.0).astype(F32)

    dirs = ((True, (qf, kf, vf, gcf, of_ref)), (False, (qb, kb, vb, gcb, ob_ref)))
    items, sinks = [], []
    for step in range(nch):
        for fwd, (q_r, k_r, v_r, gc_r, o_r) in dirs:
            c = step if fwd else nch - 1 - step
            rows = slice(c * CHUNK, (c + 1) * CHUNK)
            d = 0 if fwd else 1
            gct = gc_r[rows, :]
            grows = lax.dot_general(pick, gct, (((1,), (1,)), ((), ())),
                                    precision=lax.Precision.HIGHEST, preferred_element_type=F32)
            for h in range(H_A):
                lanes = slice(h * HD_A, (h + 1) * HD_A)
                gi = d * H_A + h
                items.append((q_r[rows, lanes], k_r[rows, lanes], v_r[rows, lanes],
                              gct[:, gi:gi + 1], grows[gi:gi + 1, :],
                              gct[:, 2 * H_A + gi:2 * H_A + gi + 1], fwd))
                sinks.append((o_r, rows, lanes))
    local = _dn_local(items, ii, jj, masks)
    nchain = 2 * H_A
    states = [s_ref[gi] for gi in range(nchain)]
    for step in range(nch):
        outs, states = _dn_step(local[step * nchain:(step + 1) * nchain], states)
        for (o_r, rows, lanes), o in zip(sinks[step * nchain:(step + 1) * nchain], outs):
            o_r[rows, lanes] = o
    for gi in range(nchain):
        s_ref[gi] = states[gi]


def _rev_block(nb):
    return lambda i: jnp.where(i == 0, 0, nb - i)


def _dnscan(q, k, v, gcol):
    t = q.shape[0]
    nb = t // BLK
    rev = _rev_block(nb)
    wide = lambda f: pl.BlockSpec((BLK, A_WIDTH), lambda i: (f(i), 0))
    col = lambda f: pl.BlockSpec((BLK, 128), lambda i: (f(i), 0))
    ident = lambda i: i
    out_sds = jax.ShapeDtypeStruct((t, A_WIDTH), F32)
    return pl.pallas_call(
        _dnscan_kernel,
        grid=(nb,),
        in_specs=[wide(ident), wide(ident), wide(ident), col(ident),
                  wide(rev), wide(rev), wide(rev), col(rev)],
        out_specs=[wide(ident), wide(rev)],
        out_shape=[out_sds, out_sds],
        scratch_shapes=[pltpu.VMEM((2 * H_A, HD_A, HD_A), F32)],
        compiler_params=_cparams(("arbitrary",)),
        name="dn_scan",
    )(q, k, v, gcol, q, k, v, gcol)


def _gla_gates(small_ref, w2_ref, b2_ref, b_ref, d, fwd):
    cols = slice(d * B_KWIDTH, (d + 1) * B_KWIDTH)
    pre = _dot3(small_ref[...], w2_ref[:, cols]) + b2_ref[:, cols]
    gk = -_softplus(-pre) * (1.0 / GLA_NORMALIZER)
    b_ref[d] = _cumsum_dot(_tri_blockdiag(BLK, fwd), gk)


def _gla_chunk(qk_ref, v_ref, b_ref, o_ref, st_ref, d, fwd, consts, step):
    sel, headmask_k, st_mask = consts
    nch = BLK // CHUNK
    nsub = CHUNK // SUB
    sub_i = lax.broadcasted_iota(I32, (SUB, 1), 0)
    row_c = lax.broadcasted_iota(I32, (CHUNK, 1), 0)
    if True:
        c = step if fwd else nch - 1 - step
        rows = pl.ds(pl.multiple_of(c * CHUNK, CHUNK), CHUNK)
        q = qk_ref[rows, 0:B_KWIDTH] * (DK_B ** -0.5)
        k = qk_ref[rows, B_KWIDTH:2 * B_KWIDTH]
        v = v_ref[rows, :]
        b = b_ref[d, rows, :]
        vh = v.astype(BF16)
        st = st_ref[d]
        o = _bdot_nt(q * jnp.exp(b), st)
        refs = []
        for sb in range(nsub):
            if fwd:
                r = b[sb * SUB - 1:sb * SUB, :] if sb > 0 else jnp.zeros((1, B_KWIDTH), F32)
            else:
                r = b[(sb + 1) * SUB:(sb + 1) * SUB + 1, :] if sb < nsub - 1 else jnp.zeros((1, B_KWIDTH), F32)
            refs.append(r)
        rfull = jnp.concatenate([jnp.broadcast_to(r, (SUB, B_KWIDTH)) for r in refs], axis=0)
        qs = q * jnp.exp(b - rfull)
        a_off = [None] * H_B
        for sb in (range(1, nsub) if fwd else range(0, nsub - 1)):
            jmask = (row_c < sb * SUB) if fwd else (row_c >= (sb + 1) * SUB)
            ks = (k * jnp.exp(jnp.where(jmask, refs[sb] - b, NEG))).astype(BF16)
            rowmask = jnp.where((row_c >> SUB_SHIFT) == sb, 1.0, 0.0)
            for h in range(H_B):
                a = _bdot_nt(qs * headmask_k[h], ks) * rowmask
                a_off[h] = a if a_off[h] is None else a_off[h] + a
        o = o + jnp.concatenate(
            [_bdot(a_off[h], vh[:, h * DV_B:(h + 1) * DV_B]) for h in range(H_B)], axis=1)
        diag = []
        for sb in range(nsub):
            s0 = sb * SUB
            bs, qsb, ksb = b[s0:s0 + SUB, :], q[s0:s0 + SUB, :], k[s0:s0 + SUB, :]
            tiles = []
            for jl in range(SUB):
                causal = (sub_i >= jl) if fwd else (sub_i <= jl)
                e = jnp.exp(jnp.where(causal, bs - bs[jl:jl + 1, :], NEG))
                tiles.append((qsb * ksb[jl:jl + 1, :] * e).astype(BF16))
            red = jnp.dot(jnp.concatenate(tiles, axis=0), sel, preferred_element_type=F32)
            acc = red[0:SUB, :] * v[s0:s0 + 1, :]
            for jl in range(1, SUB):
                acc = acc + red[jl * SUB:(jl + 1) * SUB, :] * v[s0 + jl:s0 + jl + 1, :]
            diag.append(acc)
        o_ref[rows, :] = o + jnp.concatenate(diag, axis=0)
        blast = b[CHUNK - 1:CHUNK, :] if fwd else b[0:1, :]
        kd = k * jnp.exp(blast - b)
        st_ref[d] = st * jnp.exp(blast) + _bdot_tn(v, kd) * st_mask


def _glascan_kernel(qkf, vf, smf, qkb, vb, smb, w2_ref, b2_ref, of_ref, ob_ref, st_ref, b_ref):
    @pl.when(pl.program_id(0) == 0)
    def _():
        st_ref[...] = jnp.zeros_like(st_ref)

    kk = lax.broadcasted_iota(I32, (B_KWIDTH, B_VWIDTH), 0)
    cc = lax.broadcasted_iota(I32, (B_KWIDTH, B_VWIDTH), 1)
    sel = jnp.where((kk >> DK_SHIFT) == (cc >> DV_SHIFT), 1.0, 0.0).astype(BF16)
    lane = lax.broadcasted_iota(I32, (1, B_KWIDTH), 1)
    headmask_k = [jnp.where((lane >> DK_SHIFT) == h, 1.0, 0.0).astype(F32) for h in range(H_B)]
    rr = lax.broadcasted_iota(I32, (B_VWIDTH, B_KWIDTH), 0)
    kc = lax.broadcasted_iota(I32, (B_VWIDTH, B_KWIDTH), 1)
    st_mask = jnp.where((rr >> DV_SHIFT) == (kc >> DK_SHIFT), 1.0, 0.0).astype(F32)
    consts = (sel, headmask_k, st_mask)
    _gla_gates(smf, w2_ref, b2_ref, b_ref, 0, True)
    _gla_gates(smb, w2_ref, b2_ref, b_ref, 1, False)

    def body(step, carry):
        _gla_chunk(qkf, vf, b_ref, of_ref, st_ref, 0, True, consts, step)
        _gla_chunk(qkb, vb, b_ref, ob_ref, st_ref, 1, False, consts, step)
        return carry

    lax.fori_loop(0, BLK // CHUNK, body, 0)


def _glascan(proj, w2full, b2full):
    t = proj.shape[0]
    nb = t // BLK
    rev = _rev_block(nb)
    ident = lambda i: i
    qk = lambda f: pl.BlockSpec((BLK, 2 * B_KWIDTH), lambda i: (f(i), COL_BQK // (2 * B_KWIDTH)))
    vv = lambda f: pl.BlockSpec((BLK, B_VWIDTH), lambda i: (f(i), COL_BV // B_VWIDTH))
    sm = lambda f: pl.BlockSpec((BLK, 128), lambda i: (f(i), COL_SMALL // 128))
    outs = lambda f: pl.BlockSpec((BLK, B_VWIDTH), lambda i: (f(i), 0))
    out_sds = jax.ShapeDtypeStruct((t, B_VWIDTH), F32)
    return pl.pallas_call(
        _glascan_kernel,
        grid=(nb,),
        in_specs=[qk(ident), vv(ident), sm(ident), qk(rev), vv(rev), sm(rev),
                  pl.BlockSpec((128, 2 * B_KWIDTH), lambda i: (0, 0)),
                  pl.BlockSpec((1, 2 * B_KWIDTH), lambda i: (0, 0))],
        out_specs=[outs(ident), outs(rev)],
        out_shape=[out_sds, out_sds],
        scratch_shapes=[pltpu.VMEM((2, B_VWIDTH, B_KWIDTH), F32), pltpu.VMEM((2, BLK, B_KWIDTH), F32)],
        compiler_params=_cparams(("arbitrary",)),
        name="gla_scan",
    )(proj, proj, proj, proj, proj, proj, w2full, b2full)


def _attnprep_kernel(cq_ref, ckv_ref, qg_ref, kg_ref, cos_ref, sin_ref, q_ref, k_ref, v_ref):
    cos = cos_ref[...]
    sin = sin_ref[...]
    lane = lax.broadcasted_iota(I32, cos.shape, 1)
    first = (lane % 32) < 16

    def norm_rope(x, g):
        y = x * lax.rsqrt(jnp.sum(x * x, axis=-1, keepdims=True) * (1.0 / HD_C) + EPS) * g
        partner = jnp.where(first, pltpu.roll(y, HD_CP - 16, 1), pltpu.roll(y, 16, 1))
        return y * cos + partner * sin

    for h in range(H_C):
        seg = slice(h * HD_CP, (h + 1) * HD_CP)
        q_ref[:, seg] = (norm_rope(cq_ref[:, seg], qg_ref[...]) * (HD_C ** -0.5)).astype(BF16)
    for g in range(KV_C):
        seg = slice(g * HD_CP, (g + 1) * HD_CP)
        k_ref[:, seg] = norm_rope(ckv_ref[:, seg], kg_ref[...]).astype(BF16)
    v_ref[...] = ckv_ref[:, KV_C * HD_CP:2 * KV_C * HD_CP].astype(BF16)


def _attnprep(proj, qg, kg, cos_t, sin_t):
    t = proj.shape[0]
    tm = BLK
    qw, kw = H_C * HD_CP, KV_C * HD_CP
    return pl.pallas_call(
        _attnprep_kernel,
        grid=(t // tm,),
        in_specs=[pl.BlockSpec((tm, qw), lambda i: (i, COL_CQ // qw)),
                  pl.BlockSpec((tm, 2 * kw), lambda i: (i, COL_CKV // (2 * kw))),
                  pl.BlockSpec((1, HD_CP), lambda i: (0, 0)),
                  pl.BlockSpec((1, HD_CP), lambda i: (0, 0)),
                  pl.BlockSpec((tm, HD_CP), lambda i: (i, 0)),
                  pl.BlockSpec((tm, HD_CP), lambda i: (i, 0))],
        out_specs=[pl.BlockSpec((tm, qw), lambda i: (i, 0)),
                   pl.BlockSpec((tm, kw), lambda i: (i, 0)),
                   pl.BlockSpec((tm, kw), lambda i: (i, 0))],
        out_shape=[jax.ShapeDtypeStruct((t, qw), BF16),
                   jax.ShapeDtypeStruct((t, kw), BF16),
                   jax.ShapeDtypeStruct((t, kw), BF16)],
        compiler_params=_cparams(("arbitrary",)),
        name="attn_prep",
    )(proj, proj, qg, kg, cos_t, sin_t)


def _attn_kernel(sink_ref, q_ref, kp_ref, kc_ref, kn_ref, kx_ref, vp_ref, vc_ref, vn_ref, vx_ref,
                 o_ref, *, nq):
    qi = pl.program_id(0)
    nctx = CTX // ATT_BLOCK
    latent = qi >= nctx
    ql = lax.broadcasted_iota(I32, (ATT_BLOCK, ATT_BLOCK), 0)
    kl = lax.broadcasted_iota(I32, (ATT_BLOCK, ATT_BLOCK), 1)
    ok_prev = jnp.logical_and(qi - 1 >= nctx, kl >= ql)
    ok_next = jnp.logical_and(jnp.logical_and(latent, qi + 1 <= nq - 1), kl <= ql)
    ok_cur = jnp.logical_and(latent, kl >= 0)
    bias = jnp.concatenate([jnp.where(ok_prev, 0.0, NEG), jnp.where(ok_cur, 0.0, NEG),
                            jnp.where(ok_next, 0.0, NEG),
                            jnp.zeros((ATT_BLOCK, CTX), F32)], axis=1)
    for g in range(KV_C):
        seg = slice(g * HD_CP, (g + 1) * HD_CP)
        kcat = jnp.concatenate([kp_ref[:, seg], kc_ref[:, seg], kn_ref[:, seg], kx_ref[:, seg]], axis=0)
        vcat = jnp.concatenate([vp_ref[:, seg], vc_ref[:, seg], vn_ref[:, seg], vx_ref[:, seg]], axis=0)
        heads = [g * REP_C + r for r in range(REP_C)]
        cols = [slice(h * HD_CP, (h + 1) * HD_CP) for h in heads]
        s = [lax.dot_general(q_ref[:, c], kcat, (((1,), (1,)), ((), ())), preferred_element_type=F32) + bias
             for c in cols]
        m = [jnp.maximum(jnp.max(s[i], axis=-1, keepdims=True), sink_ref[heads[i]]) for i in range(REP_C)]
        p = [jnp.exp(s[i] - m[i]) for i in range(REP_C)]
        den = [jnp.sum(p[i], axis=-1, keepdims=True) + jnp.exp(sink_ref[heads[i]] - m[i]) for i in range(REP_C)]
        o = [jnp.dot(p[i].astype(BF16), vcat, preferred_element_type=F32) for i in range(REP_C)]
        for i in range(REP_C):
            o_ref[:, cols[i]] = (o[i] / den[i]).astype(BF16)


def _attn(qr, kr, vr, sink):
    t = qr.shape[0]
    nq = t // ATT_BLOCK
    nctx = CTX // ATT_BLOCK
    qw, kw = H_C * HD_CP, KV_C * HD_CP
    prev = lambda i: (jnp.maximum(i - 1, nctx), 0)
    cur = lambda i: (i, 0)
    nxt = lambda i: (jnp.minimum(jnp.maximum(i + 1, nctx), nq - 1), 0)
    kv = lambda f: pl.BlockSpec((ATT_BLOCK, kw), f)
    ctxs = pl.BlockSpec((CTX, kw), lambda i: (0, 0))
    return pl.pallas_call(
        functools.partial(_attn_kernel, nq=nq),
        grid=(nq,),
        in_specs=[pl.BlockSpec(memory_space=pltpu.SMEM),
                  pl.BlockSpec((ATT_BLOCK, qw), cur),
                  kv(prev), kv(cur), kv(nxt), ctxs, kv(prev), kv(cur), kv(nxt), ctxs],
        out_specs=pl.BlockSpec((ATT_BLOCK, qw), cur),
        out_shape=jax.ShapeDtypeStruct((t, qw), BF16),
        compiler_params=_cparams(("arbitrary",)),
        name="attn",
    )(sink, qr, kr, kr, kr, kr, vr, vr, vr, vr)


def _head_rms_gate(o, z, g):
    outs = []
    for h in range(o.shape[1] // 128):
        seg = o[:, h * 128:(h + 1) * 128]
        nrm = seg * lax.rsqrt(jnp.mean(seg * seg, axis=-1, keepdims=True) + EPS) * g
        outs.append(nrm * _silu(z[:, h * 128:(h + 1) * 128]))
    return jnp.concatenate(outs, axis=1)


def _merge_kernel(oaf, oab, az, obf, obb, bz, yc, gl, xs, dng, glag, wpa, wpb, wpc, wo, mod_ref,
                  o_ref, *, tm):
    i = pl.program_id(0)
    ya = _head_rms_gate(oaf[...] + oab[...], az[...], dng[...])
    yb = _head_rms_gate(obf[...] + obb[...], bz[...], glag[...])
    pa = _bdot(ya, wpa[...])
    pb = _bdot(yb, wpb[...])
    pc = jnp.dot(yc[...], wpc[...], preferred_element_type=F32)
    merged = (_sigmoid(gl[:, 0:D_MODEL]) * pa + _sigmoid(gl[:, D_MODEL:2 * D_MODEL]) * pb
              + _sigmoid(gl[:, 2 * D_MODEL:3 * D_MODEL]) * pc)
    y = _bdot(merged, wo[...])
    o_ref[...] = xs[...] + y * _res_gate(mod_ref, 0, tm, i * tm)


def _merge(oaf, oab, obf, obb, yc, proj, xs, dng, glag, wpa, wpb, wpc, wo, mod):
    t = xs.shape[0]
    tm = BLK
    row = lambda w, c=0: pl.BlockSpec((tm, w), lambda i: (i, c))
    full = lambda a: pl.BlockSpec(a.shape, lambda i: (0,) * a.ndim)
    return pl.pallas_call(
        functools.partial(_merge_kernel, tm=tm),
        grid=(t // tm,),
        in_specs=[row(A_WIDTH), row(A_WIDTH), row(A_WIDTH, COL_AZ // A_WIDTH),
                  row(B_VWIDTH), row(B_VWIDTH), row(B_VWIDTH, COL_BZ // B_VWIDTH),
                  row(H_C * HD_CP), row(3 * D_MODEL, COL_GL // (3 * D_MODEL)), row(D_MODEL),
                  full(dng), full(glag), full(wpa), full(wpb), full(wpc), full(wo), full(mod)],
        out_specs=row(D_MODEL),
        out_shape=jax.ShapeDtypeStruct((t, D_MODEL), F32),
        compiler_params=_cparams(("arbitrary",)),
        name="merge",
    )(oaf, oab, proj, obf, obb, proj, yc, proj, xs, dng, glag, wpa, wpb, wpc, wo, mod)


def _router_kernel(x_ref, mod_ref, g_ref, rwt_ref, rb_ref, h_ref, key_ref, gk_ref, cnt_ref, *, tm):
    i = pl.program_id(0)
    h = _norm_mod(x_ref[...], g_ref[...], mod_ref, 3 * D_MODEL, i * tm)
    h_ref[...] = h
    logit = lax.dot_general(rwt_ref[...], h, (((1,), (1,)), ((), ())),
                            precision=lax.Precision.HIGHEST, preferred_element_type=F32) + rb_ref[...]
    erow = lax.broadcasted_iota(I32, (N_EXPERTS, tm), 0)
    vals, idxs, sels = [], [], []
    cur = logit
    for _ in range(TOP_K):
        m = jnp.max(cur, axis=0, keepdims=True)
        idx = jnp.min(jnp.where(cur == m, erow, N_EXPERTS), axis=0, keepdims=True)
        sel = erow == idx
        vals.append(m)
        idxs.append(idx)
        sels.append(sel)
        cur = jnp.where(sel, -jnp.inf, cur)
    ex = [jnp.exp(v - vals[0]) for v in vals]
    den = ex[0] + ex[1] + ex[2] + ex[3]
    onehot = jnp.where(sels[0] | sels[1] | sels[2] | sels[3], 1.0, 0.0).astype(F32)
    ss = lax.broadcasted_iota(I32, (tm, tm), 0)
    tt = lax.broadcasted_iota(I32, (tm, tm), 1)
    before = jnp.where(ss < tt, 1.0, 0.0).astype(BF16)
    cnt = jnp.dot(onehot.astype(BF16), before, preferred_element_type=F32)
    ranks = [jnp.sum(jnp.where(s, cnt, 0.0), axis=0, keepdims=True) for s in sels]
    keys = [idxs[k] * tm + ranks[k].astype(I32) for k in range(TOP_K)]
    key_ref[...] = jnp.concatenate(keys + [jnp.full((8 - TOP_K, tm), -1, I32)], axis=0)
    cols = jnp.concatenate([e / den for e in ex] + [k.astype(F32) for k in keys]
                           + [jnp.zeros((128 - 2 * TOP_K, tm), F32)], axis=0)
    gk_ref[...] = cols.T
    cnt_ref[...] = jnp.broadcast_to(jnp.sum(onehot, axis=1, keepdims=True), (N_EXPERTS, 128))


def _router(xs, mod, g, rwt, rb):
    t = xs.shape[0]
    tm = BLK
    nt = t // tm
    return pl.pallas_call(
        functools.partial(_router_kernel, tm=tm),
        grid=(nt,),
        in_specs=[pl.BlockSpec((tm, D_MODEL), lambda i: (i, 0)),
                  pl.BlockSpec((8, 6 * D_MODEL), lambda i: (0, 0)),
                  pl.BlockSpec((1, D_MODEL), lambda i: (0, 0)),
                  pl.BlockSpec((N_EXPERTS, D_MODEL), lambda i: (0, 0)),
                  pl.BlockSpec((N_EXPERTS, 1), lambda i: (0, 0))],
        out_specs=[pl.BlockSpec((tm, D_MODEL), lambda i: (i, 0)),
                   pl.BlockSpec((8, tm), lambda i: (0, i)),
                   pl.BlockSpec((tm, 128), lambda i: (i, 0)),
                   pl.BlockSpec((None, N_EXPERTS, 128), lambda i: (i, 0, 0))],
        out_shape=[jax.ShapeDtypeStruct((t, D_MODEL), F32),
                   jax.ShapeDtypeStruct((8, t), I32),
                   jax.ShapeDtypeStruct((t, 128), F32),
                   jax.ShapeDtypeStruct((nt, N_EXPERTS, 128), F32)],
        compiler_params=_cparams(("arbitrary",)),
        name="router",
    )(xs, mod, g, rwt, rb)


RUN_CAP = 64
RUN_SHIFT = RUN_CAP.bit_length() - 1
RUN_BITS = (8, 4, 2, 1)
RUN_WAIT_BITS = (N_EXPERTS * RUN_CAP // 8).bit_length()


def _run_pieces(n8, nmax):
    n = jnp.minimum(n8, nmax)
    return [((n & bit) != 0, 8 * bit, pl.multiple_of(8 * (n & ~(2 * bit - 1)), 8)) for bit in RUN_BITS]


def _start_runs(i, base_ref, n8_ref, src_of, dst_of, sem):
    cap8 = RUN_CAP // 8
    total8 = jnp.int32(0)
    for e in range(N_EXPERTS):
        b = pl.multiple_of(base_ref[i * N_EXPERTS + e], 8)
        n = jnp.minimum(n8_ref[i * N_EXPERTS + e], cap8)

        def make_case(j, e=e, b=b):
            def case():
                off = 0
                for bit in RUN_BITS:
                    if j & bit:
                        pltpu.make_async_copy(src_of(e, b, off, 8 * bit), dst_of(e, b, off, 8 * bit), sem).start()
                        off += 8 * bit
                return jnp.int32(0)
            return case

        lax.switch(n, [make_case(j) for j in range(cap8 + 1)])
        total8 = total8 + n
    return total8


def _wait_rows(total8, desc_of, nbits):
    for j in range(nbits):
        @pl.when(((total8 >> j) & 1) == 1)
        def _():
            desc_of(8 << j).wait()


def _dispatch_kernel(base_ref, n8_ref, long_ref, zs_ref, key_ref, h_ref, xs_ref, runs, ovf, zbuf, pend_ref, sem, osem, zsem,
                     *, tm, nzero, nsteps):
    i = pl.program_id(0)

    @pl.when(i == 0)
    def _():
        zbuf[...] = jnp.zeros_like(zbuf)
        for z in range(nzero):
            @pl.when(zs_ref[z] >= 0)
            def _():
                pltpu.make_async_copy(zbuf, xs_ref.at[pl.ds(pl.multiple_of(zs_ref[z], MOE_BM), MOE_BM), :],
                                      zsem).start()
        for z in range(nzero):
            @pl.when(zs_ref[z] >= 0)
            def _():
                pltpu.make_async_copy(zbuf, xs_ref.at[pl.ds(0, MOE_BM), :], zsem).wait()

    hb = h_ref[...].astype(BF16)
    keys = [key_ref[k:k + 1, :] for k in range(TOP_K)]
    nrow = N_EXPERTS * RUN_CAP
    rr = lax.broadcasted_iota(I32, (nrow, tm), 0)
    rowkey = (rr >> RUN_SHIFT) * tm + (rr & (RUN_CAP - 1))
    hit = (rowkey == keys[0]) | (rowkey == keys[1]) | (rowkey == keys[2]) | (rowkey == keys[3])
    slot = i % 2
    runs[slot] = jnp.dot(jnp.where(hit, 1.0, 0.0).astype(BF16), hb, preferred_element_type=F32)
    total8 = _start_runs(i, base_ref, n8_ref,
                         lambda e, b, off, size: runs.at[slot, pl.ds(e * RUN_CAP + off, size), :],
                         lambda e, b, off, size: xs_ref.at[pl.ds(b + off, size), :], sem.at[slot])
    pend_ref[slot] = total8

    def wait_slot(s):
        _wait_rows(pend_ref[s], lambda size: pltpu.make_async_copy(
            runs.at[s, pl.ds(0, size), :], xs_ref.at[pl.ds(0, size), :], sem.at[s]), RUN_WAIT_BITS)

    @pl.when(i > 0)
    def _():
        wait_slot(1 - slot)

    @pl.when(i == nsteps - 1)
    def _():
        wait_slot(slot)

    def ovf_body(e, carry):
        n8 = n8_ref[i * N_EXPERTS + e]
        b = pl.multiple_of(base_ref[i * N_EXPERTS + e], 8)
        for c in range(1, tm // RUN_CAP):
            @pl.when(n8 > c * (RUN_CAP // 8))
            def _():
                ck = lax.broadcasted_iota(I32, (RUN_CAP, tm), 0) + (e * tm + c * RUN_CAP)
                hit_c = (ck == keys[0]) | (ck == keys[1]) | (ck == keys[2]) | (ck == keys[3])
                ovf[...] = jnp.dot(jnp.where(hit_c, 1.0, 0.0).astype(BF16), hb, preferred_element_type=F32)
                for phase in (0, 1):
                    for live, size, off in _run_pieces(n8 - c * (RUN_CAP // 8), RUN_CAP // 8):
                        @pl.when(live)
                        def _():
                            cp = pltpu.make_async_copy(
                                ovf.at[pl.ds(off, size), :],
                                xs_ref.at[pl.ds(b + c * RUN_CAP + off, size), :], osem)
                            if phase == 0:
                                cp.start()
                            else:
                                cp.wait()
        return carry

    @pl.when(long_ref[i] > 0)
    def _():
        lax.fori_loop(0, N_EXPERTS, ovf_body, 0)


def _dispatch(plan, keys, h):
    base, n8, long_run, zstart, p_rows = plan
    t = h.shape[0]
    tm = BLK
    grid_spec = pltpu.PrefetchScalarGridSpec(
        num_scalar_prefetch=4,
        grid=(t // tm,),
        in_specs=[pl.BlockSpec((8, tm), lambda i, b, n, lg, z: (0, i)),
                  pl.BlockSpec((tm, D_MODEL), lambda i, b, n, lg, z: (i, 0))],
        out_specs=pl.BlockSpec(memory_space=pl.ANY),
        scratch_shapes=[pltpu.VMEM((2, N_EXPERTS * RUN_CAP, D_MODEL), F32), pltpu.VMEM((RUN_CAP, D_MODEL), F32),
                        pltpu.VMEM((MOE_BM, D_MODEL), F32), pltpu.SMEM((2,), I32),
                        pltpu.SemaphoreType.DMA((2,)), pltpu.SemaphoreType.DMA(()), pltpu.SemaphoreType.DMA(())],
    )
    return pl.pallas_call(
        functools.partial(_dispatch_kernel, tm=tm, nzero=zstart.shape[0], nsteps=t // tm),
        grid_spec=grid_spec,
        out_shape=jax.ShapeDtypeStruct((p_rows, D_MODEL), F32),
        compiler_params=_cparams(("arbitrary",)),
        name="moe_dispatch",
    )(base, n8, long_run, zstart, keys, h)


def _expert_kernel(be_ref, nused_ref, first_ref, slot_ref, nxt_ref, x_ref, wgu_hbm, bgu_ref, wdn_hbm, bdn_ref,
                   y_ref, gu_stage, dn_stage, wgu_bf, wdn_bf, sem, *, layer):
    b = pl.program_id(0)

    def fetch(e, s):
        return (pltpu.make_async_copy(wgu_hbm.at[layer, e], gu_stage.at[s], sem.at[0, s]),
                pltpu.make_async_copy(wdn_hbm.at[layer, e], dn_stage.at[s], sem.at[1, s]))

    @pl.when(b == 0)
    def _():
        for cp in fetch(be_ref[0], 0):
            cp.start()

    @pl.when(first_ref[b] == 1)
    def _():
        s = slot_ref[b]
        for cp in fetch(be_ref[b], s):
            cp.wait()
        wgu_bf[...] = gu_stage[s].astype(BF16)
        wdn_bf[...] = dn_stage[s].astype(BF16)

        @pl.when(nxt_ref[b] >= 0)
        def _():
            for cp in fetch(nxt_ref[b], 1 - s):
                cp.start()

    @pl.when(b < nused_ref[0])
    def _():
        gu = jnp.dot(x_ref[...].astype(BF16), wgu_bf[...], preferred_element_type=F32) + bgu_ref[...]
        g_ = jnp.minimum(gu[:, :D_FF], SWIGLU_LIMIT)
        u_ = jnp.clip(gu[:, D_FF:], -SWIGLU_LIMIT, SWIGLU_LIMIT)
        act = (u_ + 1.0) * (g_ * _sigmoid(g_ * SWIGLU_ALPHA))
        y_ref[...] = jnp.dot(act.astype(BF16), wdn_bf[...], preferred_element_type=F32) + bdn_ref[...]

    @pl.when(b >= nused_ref[0])
    def _():
        y_ref[...] = jnp.zeros_like(y_ref)


def _experts(plan, xsorted, w_gu, b_gu, w_dn, b_dn, layer):
    blk_e, nused, first, slot, nxt = plan
    p_rows = xsorted.shape[0]
    nblk = p_rows // MOE_BM
    depth = w_gu.shape[0]
    bsel = lambda b, be, nu, fi, sl, nx: (layer, be[b], 0, 0)
    grid_spec = pltpu.PrefetchScalarGridSpec(
        num_scalar_prefetch=5,
        grid=(nblk,),
        in_specs=[pl.BlockSpec((MOE_BM, D_MODEL), lambda b, be, nu, fi, sl, nx: (jnp.minimum(b, nu[0] - 1), 0)),
                  pl.BlockSpec(memory_space=pl.ANY),
                  pl.BlockSpec((None, None, 1, 2 * D_FF), bsel),
                  pl.BlockSpec(memory_space=pl.ANY),
                  pl.BlockSpec((None, None, 1, D_MODEL), bsel)],
        out_specs=pl.BlockSpec((MOE_BM, D_MODEL), lambda b, be, nu, fi, sl, nx: (b, 0)),
        scratch_shapes=[pltpu.VMEM((2, D_MODEL, 2 * D_FF), F32), pltpu.VMEM((2, D_FF, D_MODEL), F32),
                        pltpu.VMEM((D_MODEL, 2 * D_FF), BF16), pltpu.VMEM((D_FF, D_MODEL), BF16),
                        pltpu.SemaphoreType.DMA((2, 2))],
    )
    return pl.pallas_call(
        functools.partial(_expert_kernel, layer=layer),
        grid_spec=grid_spec,
        out_shape=jax.ShapeDtypeStruct((p_rows, D_MODEL), F32),
        compiler_params=_cparams(("arbitrary",)),
        name="moe_experts",
    )(blk_e, nused, first, slot, nxt, xsorted, w_gu, b_gu.reshape(depth, N_EXPERTS, 1, 2 * D_FF), w_dn,
      b_dn.reshape(depth, N_EXPERTS, 1, D_MODEL))


def _combine_kernel(base_ref, n8_ref, long_ref, y_ref, gk_ref, xs_ref, mod_ref, o_ref, runs, ovf, pend_ref, sem, osem,
                    *, tm, nsteps):
    i = pl.program_id(0)

    slot = i % 2

    def gather(tile, s):
        pend_ref[s] = _start_runs(tile, base_ref, n8_ref,
                                  lambda e, b, off, size: y_ref.at[pl.ds(b + off, size), :],
                                  lambda e, b, off, size: runs.at[s, pl.ds(e * RUN_CAP + off, size), :],
                                  sem.at[s])

    @pl.when(i == 0)
    def _():
        runs[...] = jnp.zeros_like(runs)
        ovf[...] = jnp.zeros_like(ovf)
        gather(0, 0)

    @pl.when(i + 1 < nsteps)
    def _():
        gather(i + 1, 1 - slot)

    _wait_rows(pend_ref[slot], lambda size: pltpu.make_async_copy(
        y_ref.at[pl.ds(0, size), :], runs.at[slot, pl.ds(0, size), :], sem.at[slot]), RUN_WAIT_BITS)
    gk = gk_ref[...]
    gates = [gk[:, k:k + 1] for k in range(TOP_K)]
    keys = [gk[:, TOP_K + k:TOP_K + k + 1].astype(I32) for k in range(TOP_K)]
    nrow = N_EXPERTS * RUN_CAP
    cc = lax.broadcasted_iota(I32, (1, nrow), 1)
    colkey = (cc >> RUN_SHIFT) * tm + (cc & (RUN_CAP - 1))
    w = jnp.zeros((tm, nrow), F32)
    for k in reversed(range(TOP_K)):
        w = jnp.where(colkey == keys[k], gates[k], w)
    acc = jnp.dot(w.astype(BF16), runs[slot].astype(BF16), preferred_element_type=F32)
    o_ref[...] = xs_ref[...] + acc * _res_gate(mod_ref, 3 * D_MODEL, tm, i * tm)

    def ovf_body(e, carry):
        n8 = n8_ref[i * N_EXPERTS + e]
        b = pl.multiple_of(base_ref[i * N_EXPERTS + e], 8)
        for c in range(1, tm // RUN_CAP):
            @pl.when(n8 > c * (RUN_CAP // 8))
            def _():
                for phase in (0, 1):
                    for live, size, off in _run_pieces(n8 - c * (RUN_CAP // 8), RUN_CAP // 8):
                        @pl.when(live)
                        def _():
                            cp = pltpu.make_async_copy(
                                y_ref.at[pl.ds(b + c * RUN_CAP + off, size), :],
                                ovf.at[pl.ds(off, size), :], osem)
                            if phase == 0:
                                cp.start()
                            else:
                                cp.wait()
                ck = lax.broadcasted_iota(I32, (1, RUN_CAP), 1) + (e * tm + c * RUN_CAP)
                wc = jnp.where(ck == keys[0], gates[0], 0.0)
                for k in range(1, TOP_K):
                    wc = wc + jnp.where(ck == keys[k], gates[k], 0.0)
                part = jnp.dot(wc.astype(BF16), ovf[...].astype(BF16), preferred_element_type=F32)
                o_ref[...] = o_ref[...] + part * _res_gate(mod_ref, 3 * D_MODEL, tm, i * tm)
        return carry

    @pl.when(long_ref[i] > 0)
    def _():
        lax.fori_loop(0, N_EXPERTS, ovf_body, 0)


def _combine(plan, y, gk_col, xs, mod):
    base, n8, long_run, _, _ = plan
    t = xs.shape[0]
    tm = BLK
    grid_spec = pltpu.PrefetchScalarGridSpec(
        num_scalar_prefetch=3,
        grid=(t // tm,),
        in_specs=[pl.BlockSpec(memory_space=pl.ANY),
                  pl.BlockSpec((tm, 128), lambda i, b, n, lg: (i, 0)),
                  pl.BlockSpec((tm, D_MODEL), lambda i, b, n, lg: (i, 0)),
                  pl.BlockSpec((8, 6 * D_MODEL), lambda i, b, n, lg: (0, 0))],
        out_specs=pl.BlockSpec((tm, D_MODEL), lambda i, b, n, lg: (i, 0)),
        scratch_shapes=[pltpu.VMEM((2, N_EXPERTS * RUN_CAP, D_MODEL), F32), pltpu.VMEM((RUN_CAP, D_MODEL), F32),
                        pltpu.SMEM((2,), I32), pltpu.SemaphoreType.DMA((2,)), pltpu.SemaphoreType.DMA(())],
    )
    return pl.pallas_call(
        functools.partial(_combine_kernel, tm=tm, nsteps=t // tm),
        grid_spec=grid_spec,
        out_shape=jax.ShapeDtypeStruct((t, D_MODEL), F32),
        compiler_params=_cparams(("arbitrary",)),
        name="moe_combine",
    )(base, n8, long_run, y, gk_col, xs, mod)


def _pad_heads(w, nh, hd, hdp):
    d = w.shape[0]
    return jnp.pad(w.reshape(d, nh, hd), ((0, 0), (0, 0), (0, hdp - hd))).reshape(d, nh * hdp)


def _layout_w_in(w):
    return _layout_w_in_f32(w).astype(BF16)


def _layout_w_in_f32(w):
    pts = np.cumsum([A_WIDTH] * 4 + [2 * H_A, 2 * H_A, B_KWIDTH, B_KWIDTH, B_VWIDTH, B_VWIDTH,
                                     2 * GLA_RANK, C_WIDTH, KV_C * HD_C, KV_C * HD_C])
    (aq, ak, av, az, aa, ab, bq, bk, bv, bz, bg, cq, ck, cv, gl) = jnp.split(w, pts.tolist(), axis=1)
    small = jnp.concatenate([aa, ab, bg], axis=1)
    small = jnp.pad(small, ((0, 0), (0, 256 - small.shape[1])))
    cols = [gl, aq, ak, av, az, _pad_heads(cq, H_C, HD_C, HD_CP), _pad_heads(ck, KV_C, HD_C, HD_CP),
            _pad_heads(cv, KV_C, HD_C, HD_CP), bq, bk, bv, bz, small]
    out = jnp.concatenate(cols, axis=1)
    assert out.shape[1] == IN_COLS_P
    return out


def _rope_tables(t):
    s_len = t - CTX
    half = HD_C // 2
    inv_freq = ROPE_THETA ** (-jnp.arange(0, half, 2, dtype=F32) / half)
    pos = jnp.arange(s_len)
    rows = (pos // GRID_W).astype(F32)[:, None] * inv_freq[None, :]
    cols = (pos % GRID_W).astype(F32)[:, None] * inv_freq[None, :]
    cr, sr, cc, sc = jnp.cos(rows), jnp.sin(rows), jnp.cos(cols), jnp.sin(cols)
    zpad = jnp.zeros((s_len, HD_CP - HD_C), F32)
    cos_l = jnp.concatenate([cr, cr, cc, cc, zpad], axis=1)
    sin_l = jnp.concatenate([-sr, sr, -sc, sc, zpad], axis=1)
    cos_t = jnp.concatenate([jnp.ones((CTX, HD_CP), F32), cos_l], axis=0)
    sin_t = jnp.concatenate([jnp.zeros((CTX, HD_CP), F32), sin_l], axis=0)
    return cos_t, sin_t


def _lane_vec(v, width=128):
    v = v.reshape(1, -1).astype(F32)
    return jnp.pad(v, ((0, 0), (0, width - v.shape[1])))


def _moe_plan(cnt_tile, tk):
    nt = cnt_tile.shape[0]
    pad8 = (cnt_tile + 7) // 8 * 8
    total = jnp.sum(pad8, axis=0)
    padded = (total + MOE_BM - 1) // MOE_BM * MOE_BM
    pend = jnp.cumsum(padded)
    estart = pend - padded
    base = estart[None, :] + jnp.cumsum(pad8, axis=0) - pad8
    nblk = (tk + nt * N_EXPERTS * 7 + N_EXPERTS * (MOE_BM - 1) + MOE_BM - 1) // MOE_BM
    p_rows = nblk * MOE_BM
    blk = jnp.arange(nblk, dtype=I32)
    blk_e = jnp.minimum(jnp.sum((pend[None, :] <= (blk * MOE_BM)[:, None]).astype(I32), axis=1), N_EXPERTS - 1)
    nused = (pend[-1] // MOE_BM).astype(I32)
    prev_e = jnp.concatenate([jnp.full((1,), -1, I32), blk_e[:-1]])
    first = ((blk_e != prev_e) & (blk < nused)).astype(I32)
    slot = (jnp.cumsum(first) - 1) % 2
    pos = jnp.where(first == 1, blk, nblk)
    nxt_pos = jnp.concatenate([lax.cummin(pos, axis=0, reverse=True)[1:], jnp.full((1,), nblk, I32)])
    nxt = jnp.where(nxt_pos < nblk, blk_e[jnp.minimum(nxt_pos, nblk - 1)], -1)
    last_blk = jnp.where(total > 0, pend - MOE_BM, -1)
    ntail = (nt * N_EXPERTS * 7 + N_EXPERTS * (MOE_BM - 1)) // MOE_BM + 1
    tail = pend[-1] + jnp.arange(ntail, dtype=I32) * MOE_BM
    zstart = jnp.concatenate([last_blk, jnp.where(tail < p_rows, tail, -1)]).astype(I32)
    eplan = (blk_e, nused.reshape(1), first, slot.astype(I32), nxt.astype(I32))
    long_run = (jnp.max(pad8, axis=1) > RUN_CAP).astype(I32)
    mplan = (base.reshape(-1).astype(I32), (pad8 // 8).reshape(-1).astype(I32), long_run, zstart, p_rows)
    return mplan, eplan


def kernel(x, c, ctx, c_ctx, ada_w, ada_b, norm_mix_g, norm_ffn_g, w_in, dn_conv_w, dn_a_log, dn_dt_bias,
           dn_norm_g, gla_w2, gla_b2, gla_norm_g, attn_q_norm_g, attn_k_norm_g, attn_sink, w_branch_a,
           w_branch_b, w_branch_c, w_out, router_w, router_b, w_gate_up, b_gate_up, w_down, b_down):
    assert x.shape[0] == 1 and c.shape[0] == 1 and ctx.shape[1] == CTX
    depth = ada_w.shape[0]
    xs = jnp.concatenate([ctx[0], x[0]], axis=0)
    t = xs.shape[0]
    assert t % BLK == 0 and (t - CTX) % GRID_W == 0
    cc = jnp.concatenate([c, c_ctx[None, :], jnp.zeros((6, D_MODEL), F32)], axis=0)
    mods = _ada_mod(cc, ada_w, ada_b)
    cos_t, sin_t = _rope_tables(t)
    for l in range(depth):
        mod = mods[l]
        proj = _inproj(xs, mod, norm_mix_g[l][None, :], _layout_w_in(w_in[l]))
        qa, ka, va, gcol = _dnprep(proj, dn_conv_w[l], _lane_vec(dn_a_log[l]), _lane_vec(dn_dt_bias[l]))
        oaf, oab = _dnscan(qa, ka, va, gcol)
        w2 = gla_w2[l].astype(F32)
        w2full = jnp.zeros((128, 2 * B_KWIDTH), F32)
        for d in range(2):
            r0 = 4 * H_A + d * GLA_RANK
            w2full = w2full.at[r0:r0 + GLA_RANK, d * B_KWIDTH:(d + 1) * B_KWIDTH].set(w2[d])
        obf, obb = _glascan(proj, w2full, gla_b2[l].reshape(1, 2 * B_KWIDTH).astype(F32))
        qg = _lane_vec(attn_q_norm_g[l])
        kg = _lane_vec(attn_k_norm_g[l])
        qr, kr, vr = _attnprep(proj, qg, kg, cos_t, sin_t)
        yc = _attn(qr, kr, vr, attn_sink[l].astype(F32))
        wpc = jnp.pad(w_branch_c[l].reshape(H_C, HD_C, D_MODEL),
                      ((0, 0), (0, HD_CP - HD_C), (0, 0))).reshape(H_C * HD_CP, D_MODEL)
        xs = _merge(oaf, oab, obf, obb, yc, proj, xs,
                    dn_norm_g[l][None, :], gla_norm_g[l][None, :],
                    w_branch_a[l].astype(BF16), w_branch_b[l].astype(BF16), wpc.astype(BF16),
                    w_out[l].astype(BF16), mod)
        h2, keys, gk_col, cnt = _router(xs, mod, norm_ffn_g[l][None, :], router_w[l].T, router_b[l][:, None])
        mplan, eplan = _moe_plan(cnt[:, :, 0].astype(I32), t * TOP_K)
        xsorted = _dispatch(mplan, keys, h2)
        y = _experts(eplan, xsorted, w_gate_up, b_gate_up, w_down, b_down, l)
        xs = _combine(mplan, y, gk_col, xs, mod)
    return xs[CTX:][None]
```

```python
import functools

import jax
import jax.numpy as jnp
import numpy as np
from jax import lax
from jax.experimental import pallas as pl
from jax.experimental.pallas import tpu as pltpu

F32 = jnp.float32
BF16 = jnp.bfloat16
I32 = jnp.int32

D_MODEL = 1024
DEPTH = 4
GRID_W = 64
CTX = 256
H_A = 4
HD_A = 128
A_WIDTH = H_A * HD_A
CONV_W = 5
H_B = 4
DK_B = 64
DV_B = 128
B_KWIDTH = H_B * DK_B
B_VWIDTH = H_B * DV_B
GLA_RANK = 16
GLA_NORMALIZER = 16.0
H_C = 8
KV_C = 2
REP_C = H_C // KV_C
HD_C = 64
HD_CP = 128
C_WIDTH = H_C * HD_C
ATT_BLOCK = 128
ROPE_THETA = 10000.0
CHUNK = 64
SUB = 16
N_EXPERTS = 32
TOP_K = 4
D_FF = 1024
SWIGLU_LIMIT = 7.0
SWIGLU_ALPHA = 1.702
EPS = 1e-6
NEG = -1e30
CHUNK_SHIFT = CHUNK.bit_length() - 1
SUB_SHIFT = SUB.bit_length() - 1
DK_SHIFT = DK_B.bit_length() - 1
DV_SHIFT = DV_B.bit_length() - 1

BLK = 256
MOE_BM = 512
VMEM_LIMIT = 56 * 1024 * 1024

COL_QKV = 0
COL_AZ = 1536
COL_CQ = 2048
COL_CKV = 3072
COL_BQK = 3584
COL_BV = 4096
COL_BZ = 4608
COL_SMALL = 5120
IN_COLS_P = 5376
IN_TN = 768


def _pick(n, cands):
    for c in cands:
        if n % c == 0:
            return c
    raise ValueError(f"no tile for {n}")


def _cparams(sem):
    return pltpu.CompilerParams(dimension_semantics=sem, vmem_limit_bytes=VMEM_LIMIT)


def _bdot(a, b):
    return jnp.dot(a.astype(BF16), b.astype(BF16), preferred_element_type=F32)


def _bdot_nt(a, b):
    return lax.dot_general(a.astype(BF16), b.astype(BF16), (((1,), (1,)), ((), ())),
                           preferred_element_type=F32)


def _bdot_tn(a, b):
    return lax.dot_general(a.astype(BF16), b.astype(BF16), (((0,), (0,)), ((), ())),
                           preferred_element_type=F32)


def _fdot(a, b):
    return jnp.dot(a, b, precision=lax.Precision.HIGHEST, preferred_element_type=F32)


def _split2(a):
    hi = a.astype(BF16)
    lo = (a - hi.astype(F32)).astype(BF16)
    return hi, lo


def _dot3(a, b):
    ah, al = _split2(a)
    bh, bl = _split2(b)
    d = functools.partial(jnp.dot, preferred_element_type=F32)
    return d(ah, bh) + (d(ah, bl) + d(al, bh))


def _cumsum_dot(tri, x):
    t = tri.astype(BF16)
    hi = x.astype(BF16)
    r1 = x - hi.astype(F32)
    mid = r1.astype(BF16)
    lo = (r1 - mid.astype(F32)).astype(BF16)
    d = functools.partial(jnp.dot, preferred_element_type=F32)
    return d(t, hi) + (d(t, mid) + d(t, lo))


def _sigmoid(x):
    return 1.0 / (1.0 + jnp.exp(-x))


def _silu(x):
    return x * _sigmoid(x)


def _softplus(x):
    return jnp.maximum(x, 0.0) + jnp.log(1.0 + jnp.exp(-jnp.abs(x)))


def _ada_kernel(cc_ref, w_ref, b_ref, o_ref):
    o_ref[...] = _fdot(_silu(cc_ref[...]), w_ref[...]) + b_ref[...]


def _ada_mod(cc, ada_w, ada_b):
    depth = ada_w.shape[0]
    tn = 1536
    return pl.pallas_call(
        _ada_kernel,
        grid=(depth, 6 * D_MODEL // tn),
        in_specs=[pl.BlockSpec((8, D_MODEL), lambda l, j: (0, 0)),
                  pl.BlockSpec((None, D_MODEL, tn), lambda l, j: (l, 0, j)),
                  pl.BlockSpec((None, 1, tn), lambda l, j: (l, 0, j))],
        out_specs=pl.BlockSpec((None, 8, tn), lambda l, j: (l, 0, j)),
        out_shape=jax.ShapeDtypeStruct((depth, 8, 6 * D_MODEL), F32),
        compiler_params=_cparams(("arbitrary", "arbitrary")),
        name="ada_mod",
    )(cc, ada_w, ada_b.reshape(depth, 1, 6 * D_MODEL))


def _norm_mod(x, g, mod_ref, moff, row0):
    tm = x.shape[0]
    y = x * lax.rsqrt(jnp.mean(x * x, axis=-1, keepdims=True) + EPS) * g
    isc = (row0 + lax.broadcasted_iota(I32, (tm, 1), 0)) < CTX
    shift = jnp.where(isc, mod_ref[1:2, moff:moff + D_MODEL], mod_ref[0:1, moff:moff + D_MODEL])
    scale = jnp.where(isc, mod_ref[1:2, moff + D_MODEL:moff + 2 * D_MODEL],
                      mod_ref[0:1, moff + D_MODEL:moff + 2 * D_MODEL])
    return y * (1.0 + scale) + shift


def _res_gate(mod_ref, moff, tm, row0):
    isc = (row0 + lax.broadcasted_iota(I32, (tm, 1), 0)) < CTX
    return jnp.where(isc, mod_ref[1:2, moff + 2 * D_MODEL:moff + 3 * D_MODEL],
                     mod_ref[0:1, moff + 2 * D_MODEL:moff + 3 * D_MODEL])


def _inproj_kernel(x_ref, mod_ref, g_ref, w_ref, o_ref, h_ref, *, tm):
    i = pl.program_id(0)

    @pl.when(pl.program_id(1) == 0)
    def _():
        h_ref[...] = _norm_mod(x_ref[...], g_ref[...], mod_ref, 0, i * tm).astype(BF16)

    o_ref[...] = jnp.dot(h_ref[...], w_ref[...], preferred_element_type=F32)


def _inproj(xs, mod, g, w):
    t = xs.shape[0]
    tm = _pick(t, (1664, 1280, 640, 256))
    return pl.pallas_call(
        functools.partial(_inproj_kernel, tm=tm),
        grid=(t // tm, IN_COLS_P // IN_TN),
        in_specs=[pl.BlockSpec((tm, D_MODEL), lambda i, j: (i, 0)),
                  pl.BlockSpec((8, 6 * D_MODEL), lambda i, j: (0, 0)),
                  pl.BlockSpec((1, D_MODEL), lambda i, j: (0, 0)),
                  pl.BlockSpec((D_MODEL, IN_TN), lambda i, j: (0, j))],
        out_specs=pl.BlockSpec((tm, IN_TN), lambda i, j: (i, j)),
        out_shape=jax.ShapeDtypeStruct((t, IN_COLS_P), F32),
        scratch_shapes=[pltpu.VMEM((tm, D_MODEL), BF16)],
        compiler_params=_cparams(("arbitrary", "arbitrary")),
        name="inproj",
    )(xs, mod, g, w)


def _tri_blockdiag(n, lower):
    ii = lax.broadcasted_iota(I32, (n, n), 0)
    jj = lax.broadcasted_iota(I32, (n, n), 1)
    same = (ii >> CHUNK_SHIFT) == (jj >> CHUNK_SHIFT)
    tri = (ii >= jj) if lower else (ii <= jj)
    return jnp.where(same, jnp.where(tri, 1.0, 0.0), 0.0).astype(F32)


def _dnprep_kernel(main_ref, prev_ref, next_ref, small_ref, cw_ref, alog_ref, dtb_ref,
                   q_ref, k_ref, v_ref, gcol_ref, ext_ref, *, nb):
    i = pl.program_id(0)
    use_prev = i >= 2
    use_next = jnp.logical_and(i >= 1, i <= nb - 2)
    ext_ref[0:8, :] = jnp.where(use_prev, prev_ref[...], 0.0)
    ext_ref[8:8 + BLK, :] = main_ref[...]
    ext_ref[8 + BLK:16 + BLK, :] = jnp.where(use_next, next_ref[...], 0.0)
    acc = ext_ref[6:6 + BLK, :] * cw_ref[0:1, :]
    for d in range(1, CONV_W):
        acc = acc + ext_ref[6 + d:6 + d + BLK, :] * cw_ref[d:d + 1, :]
    s = _silu(acc)
    for h in range(H_A):
        for part, ref, mul in ((0, q_ref, HD_A ** -0.5), (1, k_ref, 1.0)):
            seg = s[:, part * A_WIDTH + h * HD_A: part * A_WIDTH + (h + 1) * HD_A]
            nrm = seg * lax.rsqrt(jnp.sum(seg * seg, axis=-1, keepdims=True) + EPS)
            ref[:, h * HD_A:(h + 1) * HD_A] = nrm * mul
    v_ref[...] = s[:, 2 * A_WIDTH:3 * A_WIDTH]
    sm = small_ref[...]
    lane = lax.broadcasted_iota(I32, sm.shape, 1)
    g = -jnp.exp(alog_ref[...]) * _softplus(sm + dtb_ref[...])
    gb = jnp.where(lane < 2 * H_A, g, jnp.where(lane < 4 * H_A, _sigmoid(sm), 0.0))
    cf = _fdot(_tri_blockdiag(BLK, True), gb)
    cr = _fdot(_tri_blockdiag(BLK, False), gb)
    gc = jnp.where(lane < H_A, cf, jnp.where(lane < 2 * H_A, cr, gb))
    gcol_ref[...] = gc


def _dnprep(proj, conv_w, alog_vec, dtb_vec):
    t = proj.shape[0]
    nb = t // BLK
    qkv_blk = COL_QKV // (3 * A_WIDTH)
    last8 = t // 8 - 1
    out_sds = jax.ShapeDtypeStruct((t, A_WIDTH), F32)
    return pl.pallas_call(
        functools.partial(_dnprep_kernel, nb=nb),
        grid=(nb,),
        in_specs=[pl.BlockSpec((BLK, 3 * A_WIDTH), lambda i: (i, qkv_blk)),
                  pl.BlockSpec((8, 3 * A_WIDTH), lambda i: (jnp.maximum(i * (BLK // 8) - 1, 0), qkv_blk)),
                  pl.BlockSpec((8, 3 * A_WIDTH), lambda i: (jnp.minimum((i + 1) * (BLK // 8), last8), qkv_blk)),
                  pl.BlockSpec((BLK, 128), lambda i: (i, COL_SMALL // 128)),
                  pl.BlockSpec((CONV_W, 3 * A_WIDTH), lambda i: (0, 0)),
                  pl.BlockSpec((1, 128), lambda i: (0, 0)),
                  pl.BlockSpec((1, 128), lambda i: (0, 0))],
        out_specs=[pl.BlockSpec((BLK, A_WIDTH), lambda i: (i, 0)),
                   pl.BlockSpec((BLK, A_WIDTH), lambda i: (i, 0)),
                   pl.BlockSpec((BLK, A_WIDTH), lambda i: (i, 0)),
                   pl.BlockSpec((BLK, 128), lambda i: (i, 0))],
        out_shape=[out_sds, out_sds, out_sds,
                   jax.ShapeDtypeStruct((t, 128), F32)],
        scratch_shapes=[pltpu.VMEM((BLK + 16, 3 * A_WIDTH), F32)],
        compiler_params=_cparams(("arbitrary",)),
        name="dn_prep",
    )(proj, proj, proj, proj, conv_w, alog_vec, dtb_vec)


def _dot3_all(a_list, b_list):
    d = functools.partial(jnp.dot, preferred_element_type=F32)
    sa = [_split2(a) for a in a_list]
    sb = [_split2(b) for b in b_list]
    hh = [d(a[0], b[0]) for a, b in zip(sa, sb)]
    hl = [d(a[0], b[1]) for a, b in zip(sa, sb)]
    lh = [d(a[1], b[0]) for a, b in zip(sa, sb)]
    return [x + (y + z) for x, y, z in zip(hh, hl, lh)]


def _unit_tri_inverse_all(l_mats, masks):
    eye, m_diag, m_l1, m_l2 = masks
    ld = [l * m_diag for l in l_mats]
    x = [eye - a for a in ld]
    p = _dot3_all(ld, ld)
    for it in range(3):
        xp = _dot3_all(x, p)
        if it < 2:
            p = _dot3_all(p, p)
        x = [a + b for a, b in zip(x, xp)]
    for m in (m_l1, m_l2):
        cx = _dot3_all([l * m for l in l_mats], x)
        xcx = _dot3_all(x, cx)
        x = [a - b for a, b in zip(x, xcx)]
    return x


def _dn_masks():
    ii = lax.broadcasted_iota(I32, (CHUNK, CHUNK), 0)
    jj = lax.broadcasted_iota(I32, (CHUNK, CHUNK), 1)
    one = lambda c: jnp.where(c, 1.0, 0.0).astype(F32)
    eye = one(ii == jj)
    m_diag = one((ii >> SUB_SHIFT) == (jj >> SUB_SHIFT))
    m_l2 = one((ii >> (SUB_SHIFT + 1)) != (jj >> (SUB_SHIFT + 1)))
    m_l1 = 1.0 - m_diag - m_l2
    return ii, jj, (eye, m_diag, m_l1, m_l2)


def _dn_local(items, ii, jj, masks):
    n = len(items)
    dec, lmat, qk, kb, eg = [], [], [], [], []
    for q, k, v, gcol, grow, bcol, fwd in items:
        incl = (ii >= jj) if fwd else (ii <= jj)
        dec.append(jnp.exp(jnp.where(incl, gcol - grow, NEG)))
        kb.append(k * bcol)
        eg.append(jnp.exp(gcol))
    kh = [it[1].astype(BF16) for it in items]
    kk = [_bdot_nt(kb[i], kh[i]) for i in range(n)]
    qkr = [_bdot_nt(items[i][0], kh[i]) for i in range(n)]
    for i in range(n):
        fwd = items[i][6]
        strict = (ii > jj) if fwd else (ii < jj)
        lmat.append(kk[i] * jnp.where(strict, dec[i], 0.0))
        qk.append((qkr[i] * dec[i]).astype(BF16))
    rhs = [jnp.concatenate([items[i][2] * items[i][5], kb[i] * eg[i]], axis=1) for i in range(n)]
    sol = _dot3_all(_unit_tri_inverse_all(lmat, masks), rhs)
    out = []
    for i in range(n):
        q, k, _, gcol, _, _, fwd = items[i]
        glast = gcol[CHUNK - 1:CHUNK, :] if fwd else gcol[0:1, :]
        out.append((sol[i][:, :HD_A], sol[i][:, HD_A:].astype(BF16), qk[i], (q * eg[i]).astype(BF16),
                    (k * jnp.exp(glast - gcol)).astype(BF16), jnp.exp(glast)))
    return out


def _dn_step(local, states):
    n = len(local)
    sb = [s.astype(BF16) for s in states]
    d = functools.partial(jnp.dot, preferred_element_type=F32)
    ws = [d(local[i][1], sb[i]) for i in range(n)]
    qs = [d(local[i][3], sb[i]) for i in range(n)]
    v_new = [(local[i][0] - ws[i]).astype(BF16) for i in range(n)]
    o2 = [d(local[i][2], v_new[i]) for i in range(n)]
    kv = [lax.dot_general(local[i][4], v_new[i], (((0,), (0,)), ((), ())), preferred_element_type=F32)
          for i in range(n)]
    return [qs[i] + o2[i] for i in range(n)], [states[i] * local[i][5] + kv[i] for i in range(n)]


def _dnscan_kernel(qf, kf, vf, gcf, qb, kb, vb, gcb, of_ref, ob_ref, s_ref):
    @pl.when(pl.program_id(0) == 0)
    def _():
        s_ref[...] = jnp.zeros_like(s_ref)

    ii, jj, masks = _dn_masks()
    nch = BLK // CHUNK
    pick = jnp.where(lax.broadcasted_iota(I32, (16, 128), 0) == lax.broadcasted_iota(I32, (16, 128), 1),
                     1.0, 0.0).astype(F32)

    dirs = ((True, (qf, kf, vf, gcf, of_ref)), (False, (qb, kb, vb, gcb, ob_ref)))
    items, sinks = [], []
    for step in range(nch):
        for fwd, (q_r, k_r, v_r, gc_r, o_r) in dirs:
            c = step if fwd else nch - 1 - step
            rows = slice(c * CHUNK, (c + 1) * CHUNK)
            d = 0 if fwd else 1
            gct = gc_r[rows, :]
            grows = lax.dot_general(pick, gct, (((1,), (1,)), ((), ())),
                                    precision=lax.Precision.HIGHEST, preferred_element_type=F32)
            for h in range(H_A):
                lanes = slice(h * HD_A, (h + 1) * HD_A)
                gi = d * H_A + h
                items.append((q_r[rows, lanes], k_r[rows, lanes], v_r[rows, lanes],
                              gct[:, gi:gi + 1], grows[gi:gi + 1, :],
                              gct[:, 2 * H_A + gi:2 * H_A + gi + 1], fwd))
                sinks.append((o_r, rows, lanes))
    local = _dn_local(items, ii, jj, masks)
    nchain = 2 * H_A
    states = [s_ref[gi] for gi in range(nchain)]
    for step in range(nch):
        outs, states = _dn_step(local[step * nchain:(step + 1) * nchain], states)
        for (o_r, rows, lanes), o in zip(sinks[step * nchain:(step + 1) * nchain], outs):
            o_r[rows, lanes] = o
    for gi in range(nchain):
        s_ref[gi] = states[gi]


def _rev_block(nb):
    return lambda i: jnp.where(i == 0, 0, nb - i)


def _dnscan(q, k, v, gcol):
    t = q.shape[0]
    nb = t // BLK
    rev = _rev_block(nb)
    wide = lambda f: pl.BlockSpec((BLK, A_WIDTH), lambda i: (f(i), 0))
    col = lambda f: pl.BlockSpec((BLK, 128), lambda i: (f(i), 0))
    ident = lambda i: i
    out_sds = jax.ShapeDtypeStruct((t, A_WIDTH), F32)
    return pl.pallas_call(
        _dnscan_kernel,
        grid=(nb,),
        in_specs=[wide(ident), wide(ident), wide(ident), col(ident),
                  wide(rev), wide(rev), wide(rev), col(rev)],
        out_specs=[wide(ident), wide(rev)],
        out_shape=[out_sds, out_sds],
        scratch_shapes=[pltpu.VMEM((2 * H_A, HD_A, HD_A), F32)],
        compiler_params=_cparams(("arbitrary",)),
        name="dn_scan",
    )(q, k, v, gcol, q, k, v, gcol)


def _gla_gates(small_ref, w2_ref, b2_ref, b_ref, d, fwd):
    cols = slice(d * B_KWIDTH, (d + 1) * B_KWIDTH)
    pre = _dot3(small_ref[...], w2_ref[:, cols]) + b2_ref[:, cols]
    gk = -_softplus(-pre) * (1.0 / GLA_NORMALIZER)
    b_ref[d] = _cumsum_dot(_tri_blockdiag(BLK, fwd), gk)


def _gla_chunk(qk_ref, v_ref, b_ref, o_ref, st_ref, d, fwd, consts, step):
    sel, headmask_k, st_mask = consts
    nch = BLK // CHUNK
    nsub = CHUNK // SUB
    sub_i = lax.broadcasted_iota(I32, (SUB, 1), 0)
    row_c = lax.broadcasted_iota(I32, (CHUNK, 1), 0)
    if True:
        c = step if fwd else nch - 1 - step
        rows = pl.ds(pl.multiple_of(c * CHUNK, CHUNK), CHUNK)
        q = qk_ref[rows, 0:B_KWIDTH] * (DK_B ** -0.5)
        k = qk_ref[rows, B_KWIDTH:2 * B_KWIDTH]
        v = v_ref[rows, :]
        b = b_ref[d, rows, :]
        vh = v.astype(BF16)
        st = st_ref[d]
        o = _bdot_nt(q * jnp.exp(b), st)
        refs = []
        for sb in range(nsub):
            if fwd:
                r = b[sb * SUB - 1:sb * SUB, :] if sb > 0 else jnp.zeros((1, B_KWIDTH), F32)
            else:
                r = b[(sb + 1) * SUB:(sb + 1) * SUB + 1, :] if sb < nsub - 1 else jnp.zeros((1, B_KWIDTH), F32)
            refs.append(r)
        rfull = jnp.concatenate([jnp.broadcast_to(r, (SUB, B_KWIDTH)) for r in refs], axis=0)
        qs = q * jnp.exp(b - rfull)
        a_off = [None] * H_B
        for sb in (range(1, nsub) if fwd else range(0, nsub - 1)):
            jmask = (row_c < sb * SUB) if fwd else (row_c >= (sb + 1) * SUB)
            ks = (k * jnp.exp(jnp.where(jmask, refs[sb] - b, NEG))).astype(BF16)
            rowmask = jnp.where((row_c >> SUB_SHIFT) == sb, 1.0, 0.0)
            for h in range(H_B):
                a = _bdot_nt(qs * headmask_k[h], ks) * rowmask
                a_off[h] = a if a_off[h] is None else a_off[h] + a
        o = o + jnp.concatenate(
            [_bdot(a_off[h], vh[:, h * DV_B:(h + 1) * DV_B]) for h in range(H_B)], axis=1)
        diag = []
        for sb in range(nsub):
            s0 = sb * SUB
            bs, qsb, ksb = b[s0:s0 + SUB, :], q[s0:s0 + SUB, :], k[s0:s0 + SUB, :]
            tiles = []
            for jl in range(SUB):
                causal = (sub_i >= jl) if fwd else (sub_i <= jl)
                e = jnp.exp(jnp.where(causal, bs - bs[jl:jl + 1, :], NEG))
                tiles.append((qsb * ksb[jl:jl + 1, :] * e).astype(BF16))
            red = jnp.dot(jnp.concatenate(tiles, axis=0), sel, preferred_element_type=F32)
            acc = red[0:SUB, :] * v[s0:s0 + 1, :]
            for jl in range(1, SUB):
                acc = acc + red[jl * SUB:(jl + 1) * SUB, :] * v[s0 + jl:s0 + jl + 1, :]
            diag.append(acc)
        o_ref[rows, :] = o + jnp.concatenate(diag, axis=0)
        blast = b[CHUNK - 1:CHUNK, :] if fwd else b[0:1, :]
        kd = k * jnp.exp(blast - b)
        st_ref[d] = st * jnp.exp(blast) + _bdot_tn(v, kd) * st_mask


def _glascan_kernel(qkf, vf, smf, qkb, vb, smb, w2_ref, b2_ref, of_ref, ob_ref, st_ref, b_ref):
    @pl.when(pl.program_id(0) == 0)
    def _():
        st_ref[...] = jnp.zeros_like(st_ref)

    kk = lax.broadcasted_iota(I32, (B_KWIDTH, B_VWIDTH), 0)
    cc = lax.broadcasted_iota(I32, (B_KWIDTH, B_VWIDTH), 1)
    sel = jnp.where((kk >> DK_SHIFT) == (cc >> DV_SHIFT), 1.0, 0.0).astype(BF16)
    lane = lax.broadcasted_iota(I32, (1, B_KWIDTH), 1)
    headmask_k = [jnp.where((lane >> DK_SHIFT) == h, 1.0, 0.0).astype(F32) for h in range(H_B)]
    rr = lax.broadcasted_iota(I32, (B_VWIDTH, B_KWIDTH), 0)
    kc = lax.broadcasted_iota(I32, (B_VWIDTH, B_KWIDTH), 1)
    st_mask = jnp.where((rr >> DV_SHIFT) == (kc >> DK_SHIFT), 1.0, 0.0).astype(F32)
    consts = (sel, headmask_k, st_mask)
    _gla_gates(smf, w2_ref, b2_ref, b_ref, 0, True)
    _gla_gates(smb, w2_ref, b2_ref, b_ref, 1, False)

    def body(step, carry):
        _gla_chunk(qkf, vf, b_ref, of_ref, st_ref, 0, True, consts, step)
        _gla_chunk(qkb, vb, b_ref, ob_ref, st_ref, 1, False, consts, step)
        return carry

    lax.fori_loop(0, BLK // CHUNK, body, 0)


def _glascan(proj, w2full, b2full):
    t = proj.shape[0]
    nb = t // BLK
    rev = _rev_block(nb)
    ident = lambda i: i
    qk = lambda f: pl.BlockSpec((BLK, 2 * B_KWIDTH), lambda i: (f(i), COL_BQK // (2 * B_KWIDTH)))
    vv = lambda f: pl.BlockSpec((BLK, B_VWIDTH), lambda i: (f(i), COL_BV // B_VWIDTH))
    sm = lambda f: pl.BlockSpec((BLK, 128), lambda i: (f(i), COL_SMALL // 128))
    outs = lambda f: pl.BlockSpec((BLK, B_VWIDTH), lambda i: (f(i), 0))
    out_sds = jax.ShapeDtypeStruct((t, B_VWIDTH), F32)
    return pl.pallas_call(
        _glascan_kernel,
        grid=(nb,),
        in_specs=[qk(ident), vv(ident), sm(ident), qk(rev), vv(rev), sm(rev),
                  pl.BlockSpec((128, 2 * B_KWIDTH), lambda i: (0, 0)),
                  pl.BlockSpec((1, 2 * B_KWIDTH), lambda i: (0, 0))],
        out_specs=[outs(ident), outs(rev)],
        out_shape=[out_sds, out_sds],
        scratch_shapes=[pltpu.VMEM((2, B_VWIDTH, B_KWIDTH), F32), pltpu.VMEM((2, BLK, B_KWIDTH), F32)],
        compiler_params=_cparams(("arbitrary",)),
        name="gla_scan",
    )(proj, proj, proj, proj, proj, proj, w2full, b2full)


def _attnprep_kernel(cq_ref, ckv_ref, qg_ref, kg_ref, cos_ref, sin_ref, q_ref, k_ref, v_ref):
    cos = cos_ref[...]
    sin = sin_ref[...]
    lane = lax.broadcasted_iota(I32, cos.shape, 1)
    first = (lane % 32) < 16

    def norm_rope(x, g):
        y = x * lax.rsqrt(jnp.sum(x * x, axis=-1, keepdims=True) * (1.0 / HD_C) + EPS) * g
        partner = jnp.where(first, pltpu.roll(y, HD_CP - 16, 1), pltpu.roll(y, 16, 1))
        return y * cos + partner * sin

    for h in range(H_C):
        seg = slice(h * HD_CP, (h + 1) * HD_CP)
        q_ref[:, seg] = (norm_rope(cq_ref[:, seg], qg_ref[...]) * (HD_C ** -0.5)).astype(BF16)
    for g in range(KV_C):
        seg = slice(g * HD_CP, (g + 1) * HD_CP)
        k_ref[:, seg] = norm_rope(ckv_ref[:, seg], kg_ref[...]).astype(BF16)
    v_ref[...] = ckv_ref[:, KV_C * HD_CP:2 * KV_C * HD_CP].astype(BF16)


def _attnprep(proj, qg, kg, cos_t, sin_t):
    t = proj.shape[0]
    tm = BLK
    qw, kw = H_C * HD_CP, KV_C * HD_CP
    return pl.pallas_call(
        _attnprep_kernel,
        grid=(t // tm,),
        in_specs=[pl.BlockSpec((tm, qw), lambda i: (i, COL_CQ // qw)),
                  pl.BlockSpec((tm, 2 * kw), lambda i: (i, COL_CKV // (2 * kw))),
                  pl.BlockSpec((1, HD_CP), lambda i: (0, 0)),
                  pl.BlockSpec((1, HD_CP), lambda i: (0, 0)),
                  pl.BlockSpec((tm, HD_CP), lambda i: (i, 0)),
                  pl.BlockSpec((tm, HD_CP), lambda i: (i, 0))],
        out_specs=[pl.BlockSpec((tm, qw), lambda i: (i, 0)),
                   pl.BlockSpec((tm, kw), lambda i: (i, 0)),
                   pl.BlockSpec((tm, kw), lambda i: (i, 0))],
        out_shape=[jax.ShapeDtypeStruct((t, qw), BF16),
                   jax.ShapeDtypeStruct((t, kw), BF16),
                   jax.ShapeDtypeStruct((t, kw), BF16)],
        compiler_params=_cparams(("arbitrary",)),
        name="attn_prep",
    )(proj, proj, qg, kg, cos_t, sin_t)


def _attn_kernel(sink_ref, q_ref, kp_ref, kc_ref, kn_ref, kx_ref, vp_ref, vc_ref, vn_ref, vx_ref,
                 o_ref, *, nq):
    qi = pl.program_id(0)
    nctx = CTX // ATT_BLOCK
    latent = qi >= nctx
    ql = lax.broadcasted_iota(I32, (ATT_BLOCK, ATT_BLOCK), 0)
    kl = lax.broadcasted_iota(I32, (ATT_BLOCK, ATT_BLOCK), 1)
    ok_prev = jnp.logical_and(qi - 1 >= nctx, kl >= ql)
    ok_next = jnp.logical_and(jnp.logical_and(latent, qi + 1 <= nq - 1), kl <= ql)
    ok_cur = jnp.logical_and(latent, kl >= 0)
    bias = jnp.concatenate([jnp.where(ok_prev, 0.0, NEG), jnp.where(ok_cur, 0.0, NEG),
                            jnp.where(ok_next, 0.0, NEG),
                            jnp.zeros((ATT_BLOCK, CTX), F32)], axis=1)
    for g in range(KV_C):
        seg = slice(g * HD_CP, (g + 1) * HD_CP)
        kcat = jnp.concatenate([kp_ref[:, seg], kc_ref[:, seg], kn_ref[:, seg], kx_ref[:, seg]], axis=0)
        vcat = jnp.concatenate([vp_ref[:, seg], vc_ref[:, seg], vn_ref[:, seg], vx_ref[:, seg]], axis=0)
        heads = [g * REP_C + r for r in range(REP_C)]
        cols = [slice(h * HD_CP, (h + 1) * HD_CP) for h in heads]
        s = [lax.dot_general(q_ref[:, c], kcat, (((1,), (1,)), ((), ())), preferred_element_type=F32) + bias
             for c in cols]
        m = [jnp.maximum(jnp.max(s[i], axis=-1, keepdims=True), sink_ref[heads[i]]) for i in range(REP_C)]
        p = [jnp.exp(s[i] - m[i]) for i in range(REP_C)]
        den = [jnp.sum(p[i], axis=-1, keepdims=True) + jnp.exp(sink_ref[heads[i]] - m[i]) for i in range(REP_C)]
        o = [jnp.dot(p[i].astype(BF16), vcat, preferred_element_type=F32) for i in range(REP_C)]
        for i in range(REP_C):
            o_ref[:, cols[i]] = (o[i] / den[i]).astype(BF16)


def _attn(qr, kr, vr, sink):
    t = qr.shape[0]
    nq = t // ATT_BLOCK
    nctx = CTX // ATT_BLOCK
    qw, kw = H_C * HD_CP, KV_C * HD_CP
    prev = lambda i: (jnp.maximum(i - 1, nctx), 0)
    cur = lambda i: (i, 0)
    nxt = lambda i: (jnp.minimum(jnp.maximum(i + 1, nctx), nq - 1), 0)
    kv = lambda f: pl.BlockSpec((ATT_BLOCK, kw), f)
    ctxs = pl.BlockSpec((CTX, kw), lambda i: (0, 0))
    return pl.pallas_call(
        functools.partial(_attn_kernel, nq=nq),
        grid=(nq,),
        in_specs=[pl.BlockSpec(memory_space=pltpu.SMEM),
                  pl.BlockSpec((ATT_BLOCK, qw), cur),
                  kv(prev), kv(cur), kv(nxt), ctxs, kv(prev), kv(cur), kv(nxt), ctxs],
        out_specs=pl.BlockSpec((ATT_BLOCK, qw), cur),
        out_shape=jax.ShapeDtypeStruct((t, qw), BF16),
        compiler_params=_cparams(("arbitrary",)),
        name="attn",
    )(sink, qr, kr, kr, kr, kr, vr, vr, vr, vr)


def _head_rms_gate(o, z, g):
    outs = []
    for h in range(o.shape[1] // 128):
        seg = o[:, h * 128:(h + 1) * 128]
        nrm = seg * lax.rsqrt(jnp.mean(seg * seg, axis=-1, keepdims=True) + EPS) * g
        outs.append(nrm * _silu(z[:, h * 128:(h + 1) * 128]))
    return jnp.concatenate(outs, axis=1)


def _merge_kernel(oaf, oab, az, obf, obb, bz, yc, xs, g_ref, wgl, dng, glag, wpa, wpb, wpc, wo, mod_ref,
                  o_ref, *, tm):
    i = pl.program_id(0)
    h = _norm_mod(xs[...], g_ref[...], mod_ref, 0, i * tm).astype(BF16)
    gl = jnp.dot(h, wgl[...], preferred_element_type=F32)
    ya = _head_rms_gate(oaf[...] + oab[...], az[...], dng[...])
    yb = _head_rms_gate(obf[...] + obb[...], bz[...], glag[...])
    pa = _bdot(ya, wpa[...])
    pb = _bdot(yb, wpb[...])
    pc = jnp.dot(yc[...], wpc[...], preferred_element_type=F32)
    merged = (_sigmoid(gl[:, 0:D_MODEL]) * pa + _sigmoid(gl[:, D_MODEL:2 * D_MODEL]) * pb
              + _sigmoid(gl[:, 2 * D_MODEL:3 * D_MODEL]) * pc)
    y = _bdot(merged, wo[...])
    o_ref[...] = xs[...] + y * _res_gate(mod_ref, 0, tm, i * tm)


def _merge(oaf, oab, obf, obb, yc, proj, xs, g, wgl, dng, glag, wpa, wpb, wpc, wo, mod):
    t = xs.shape[0]
    tm = BLK
    row = lambda w, c=0: pl.BlockSpec((tm, w), lambda i: (i, c))
    full = lambda a: pl.BlockSpec(a.shape, lambda i: (0,) * a.ndim)
    return pl.pallas_call(
        functools.partial(_merge_kernel, tm=tm),
        grid=(t // tm,),
        in_specs=[row(A_WIDTH), row(A_WIDTH), row(A_WIDTH, COL_AZ // A_WIDTH),
                  row(B_VWIDTH), row(B_VWIDTH), row(B_VWIDTH, COL_BZ // B_VWIDTH),
                  row(H_C * HD_CP), row(D_MODEL), full(g), full(wgl),
                  full(dng), full(glag), full(wpa), full(wpb), full(wpc), full(wo), full(mod)],
        out_specs=row(D_MODEL),
        out_shape=jax.ShapeDtypeStruct((t, D_MODEL), F32),
        compiler_params=_cparams(("arbitrary",)),
        name="merge",
    )(oaf, oab, proj, obf, obb, proj, yc, xs, g, wgl, dng, glag, wpa, wpb, wpc, wo, mod)


def _router_kernel(x_ref, mod_ref, g_ref, rwt_ref, rb_ref, h_ref, key_ref, gk_ref, cnt_ref, *, tm):
    i = pl.program_id(0)
    h = _norm_mod(x_ref[...], g_ref[...], mod_ref, 3 * D_MODEL, i * tm)
    h_ref[...] = h
    logit = lax.dot_general(rwt_ref[...], h, (((1,), (1,)), ((), ())),
                            precision=lax.Precision.HIGHEST, preferred_element_type=F32) + rb_ref[...]
    erow = lax.broadcasted_iota(I32, (N_EXPERTS, tm), 0)
    vals, idxs, sels = [], [], []
    cur = logit
    for _ in range(TOP_K):
        m = jnp.max(cur, axis=0, keepdims=True)
        idx = jnp.min(jnp.where(cur == m, erow, N_EXPERTS), axis=0, keepdims=True)
        sel = erow == idx
        vals.append(m)
        idxs.append(idx)
        sels.append(sel)
        cur = jnp.where(sel, -jnp.inf, cur)
    ex = [jnp.exp(v - vals[0]) for v in vals]
    den = ex[0] + ex[1] + ex[2] + ex[3]
    onehot = jnp.where(sels[0] | sels[1] | sels[2] | sels[3], 1.0, 0.0).astype(F32)
    ss = lax.broadcasted_iota(I32, (tm, tm), 0)
    tt = lax.broadcasted_iota(I32, (tm, tm), 1)
    before = jnp.where(ss < tt, 1.0, 0.0).astype(BF16)
    cnt = jnp.dot(onehot.astype(BF16), before, preferred_element_type=F32)
    ranks = [jnp.sum(jnp.where(s, cnt, 0.0), axis=0, keepdims=True) for s in sels]
    keys = [idxs[k] * tm + ranks[k].astype(I32) for k in range(TOP_K)]
    key_ref[...] = jnp.concatenate(keys + [jnp.full((8 - TOP_K, tm), -1, I32)], axis=0)
    cols = jnp.concatenate([e / den for e in ex] + [k.astype(F32) for k in keys]
                           + [jnp.zeros((128 - 2 * TOP_K, tm), F32)], axis=0)
    gk_ref[...] = cols.T
    cnt_ref[...] = jnp.broadcast_to(jnp.sum(onehot, axis=1, keepdims=True), (N_EXPERTS, 128))


def _router(xs, mod, g, rwt, rb):
    t = xs.shape[0]
    tm = BLK
    nt = t // tm
    return pl.pallas_call(
        functools.partial(_router_kernel, tm=tm),
        grid=(nt,),
        in_specs=[pl.BlockSpec((tm, D_MODEL), lambda i: (i, 0)),
                  pl.BlockSpec((8, 6 * D_MODEL), lambda i: (0, 0)),
                  pl.BlockSpec((1, D_MODEL), lambda i: (0, 0)),
                  pl.BlockSpec((N_EXPERTS, D_MODEL), lambda i: (0, 0)),
                  pl.BlockSpec((N_EXPERTS, 1), lambda i: (0, 0))],
        out_specs=[pl.BlockSpec((tm, D_MODEL), lambda i: (i, 0)),
                   pl.BlockSpec((8, tm), lambda i: (0, i)),
                   pl.BlockSpec((tm, 128), lambda i: (i, 0)),
                   pl.BlockSpec((None, N_EXPERTS, 128), lambda i: (i, 0, 0))],
        out_shape=[jax.ShapeDtypeStruct((t, D_MODEL), F32),
                   jax.ShapeDtypeStruct((8, t), I32),
                   jax.ShapeDtypeStruct((t, 128), F32),
                   jax.ShapeDtypeStruct((nt, N_EXPERTS, 128), F32)],
        compiler_params=_cparams(("arbitrary",)),
        name="router",
    )(xs, mod, g, rwt, rb)


RUN_CAP = 64
RUN_SHIFT = RUN_CAP.bit_length() - 1
RUN_BITS = (8, 4, 2, 1)
RUN_WAIT_BITS = (N_EXPERTS * RUN_CAP // 8).bit_length()


def _run_pieces(n8, nmax):
    n = jnp.minimum(n8, nmax)
    return [((n & bit) != 0, 8 * bit, pl.multiple_of(8 * (n & ~(2 * bit - 1)), 8)) for bit in RUN_BITS]


def _start_runs(i, base_ref, n8_ref, src_of, dst_of, sem):
    cap8 = RUN_CAP // 8
    total8 = jnp.int32(0)
    for e in range(N_EXPERTS):
        b = pl.multiple_of(base_ref[i * N_EXPERTS + e], 8)
        n = jnp.minimum(n8_ref[i * N_EXPERTS + e], cap8)

        def make_case(j, e=e, b=b):
            def case():
                off = 0
                for bit in RUN_BITS:
                    if j & bit:
                        pltpu.make_async_copy(src_of(e, b, off, 8 * bit), dst_of(e, b, off, 8 * bit), sem).start()
                        off += 8 * bit
                return jnp.int32(0)
            return case

        lax.switch(n, [make_case(j) for j in range(cap8 + 1)])
        total8 = total8 + n
    return total8


def _wait_rows(total8, desc_of, nbits):
    for j in range(nbits):
        @pl.when(((total8 >> j) & 1) == 1)
        def _():
            desc_of(8 << j).wait()


def _dispatch_kernel(base_ref, n8_ref, long_ref, zs_ref, key_ref, h_ref, xs_ref, runs, ovf, zbuf, pend_ref, sem, osem, zsem,
                     *, tm, nzero, nsteps):
    i = pl.program_id(0)

    @pl.when(i == 0)
    def _():
        zbuf[...] = jnp.zeros_like(zbuf)
        for z in range(nzero):
            @pl.when(zs_ref[z] >= 0)
            def _():
                pltpu.make_async_copy(zbuf, xs_ref.at[pl.ds(pl.multiple_of(zs_ref[z], MOE_BM), MOE_BM), :],
                                      zsem).start()
        for z in range(nzero):
            @pl.when(zs_ref[z] >= 0)
            def _():
                pltpu.make_async_copy(zbuf, xs_ref.at[pl.ds(0, MOE_BM), :], zsem).wait()

    hb = h_ref[...].astype(BF16)
    keys = [key_ref[k:k + 1, :] for k in range(TOP_K)]
    nrow = N_EXPERTS * RUN_CAP
    rr = lax.broadcasted_iota(I32, (nrow, tm), 0)
    rowkey = (rr >> RUN_SHIFT) * tm + (rr & (RUN_CAP - 1))
    hit = (rowkey == keys[0]) | (rowkey == keys[1]) | (rowkey == keys[2]) | (rowkey == keys[3])
    slot = i % 2
    runs[slot] = jnp.dot(jnp.where(hit, 1.0, 0.0).astype(BF16), hb, preferred_element_type=F32)
    total8 = _start_runs(i, base_ref, n8_ref,
                         lambda e, b, off, size: runs.at[slot, pl.ds(e * RUN_CAP + off, size), :],
                         lambda e, b, off, size: xs_ref.at[pl.ds(b + off, size), :], sem.at[slot])
    pend_ref[slot] = total8

    def wait_slot(s):
        _wait_rows(pend_ref[s], lambda size: pltpu.make_async_copy(
            runs.at[s, pl.ds(0, size), :], xs_ref.at[pl.ds(0, size), :], sem.at[s]), RUN_WAIT_BITS)

    @pl.when(i > 0)
    def _():
        wait_slot(1 - slot)

    @pl.when(i == nsteps - 1)
    def _():
        wait_slot(slot)

    def ovf_body(e, carry):
        n8 = n8_ref[i * N_EXPERTS + e]
        b = pl.multiple_of(base_ref[i * N_EXPERTS + e], 8)
        for c in range(1, tm // RUN_CAP):
            @pl.when(n8 > c * (RUN_CAP // 8))
            def _():
                ck = lax.broadcasted_iota(I32, (RUN_CAP, tm), 0) + (e * tm + c * RUN_CAP)
                hit_c = (ck == keys[0]) | (ck == keys[1]) | (ck == keys[2]) | (ck == keys[3])
                ovf[...] = jnp.dot(jnp.where(hit_c, 1.0, 0.0).astype(BF16), hb, preferred_element_type=F32)
                for phase in (0, 1):
                    for live, size, off in _run_pieces(n8 - c * (RUN_CAP // 8), RUN_CAP // 8):
                        @pl.when(live)
                        def _():
                            cp = pltpu.make_async_copy(
                                ovf.at[pl.ds(off, size), :],
                                xs_ref.at[pl.ds(b + c * RUN_CAP + off, size), :], osem)
                            if phase == 0:
                                cp.start()
                            else:
                                cp.wait()
        return carry

    @pl.when(long_ref[i] > 0)
    def _():
        lax.fori_loop(0, N_EXPERTS, ovf_body, 0)


def _dispatch(plan, keys, h):
    base, n8, long_run, zstart, p_rows = plan
    t = h.shape[0]
    tm = BLK
    grid_spec = pltpu.PrefetchScalarGridSpec(
        num_scalar_prefetch=4,
        grid=(t // tm,),
        in_specs=[pl.BlockSpec((8, tm), lambda i, b, n, lg, z: (0, i)),
                  pl.BlockSpec((tm, D_MODEL), lambda i, b, n, lg, z: (i, 0))],
        out_specs=pl.BlockSpec(memory_space=pl.ANY),
        scratch_shapes=[pltpu.VMEM((2, N_EXPERTS * RUN_CAP, D_MODEL), F32), pltpu.VMEM((RUN_CAP, D_MODEL), F32),
                        pltpu.VMEM((MOE_BM, D_MODEL), F32), pltpu.SMEM((2,), I32),
                        pltpu.SemaphoreType.DMA((2,)), pltpu.SemaphoreType.DMA(()), pltpu.SemaphoreType.DMA(())],
    )
    return pl.pallas_call(
        functools.partial(_dispatch_kernel, tm=tm, nzero=zstart.shape[0], nsteps=t // tm),
        grid_spec=grid_spec,
        out_shape=jax.ShapeDtypeStruct((p_rows, D_MODEL), F32),
        compiler_params=_cparams(("arbitrary",)),
        name="moe_dispatch",
    )(base, n8, long_run, zstart, keys, h)


def _expert_kernel(be_ref, nused_ref, first_ref, slot_ref, nxt_ref, x_ref, wgu_hbm, bgu_ref, wdn_hbm, bdn_ref,
                   y_ref, gu_stage, dn_stage, wgu_bf, wdn_bf, sem, *, layer):
    b = pl.program_id(0)

    def fetch(e, s):
        return (pltpu.make_async_copy(wgu_hbm.at[layer, e], gu_stage.at[s], sem.at[0, s]),
                pltpu.make_async_copy(wdn_hbm.at[layer, e], dn_stage.at[s], sem.at[1, s]))

    @pl.when(b == 0)
    def _():
        for cp in fetch(be_ref[0], 0):
            cp.start()

    @pl.when(first_ref[b] == 1)
    def _():
        s = slot_ref[b]
        for cp in fetch(be_ref[b], s):
            cp.wait()
        wgu_bf[...] = gu_stage[s].astype(BF16)
        wdn_bf[...] = dn_stage[s].astype(BF16)

        @pl.when(nxt_ref[b] >= 0)
        def _():
            for cp in fetch(nxt_ref[b], 1 - s):
                cp.start()

    @pl.when(b < nused_ref[0])
    def _():
        gu = jnp.dot(x_ref[...].astype(BF16), wgu_bf[...], preferred_element_type=F32) + bgu_ref[...]
        g_ = jnp.minimum(gu[:, :D_FF], SWIGLU_LIMIT)
        u_ = jnp.clip(gu[:, D_FF:], -SWIGLU_LIMIT, SWIGLU_LIMIT)
        act = (u_ + 1.0) * (g_ * _sigmoid(g_ * SWIGLU_ALPHA))
        y_ref[...] = jnp.dot(act.astype(BF16), wdn_bf[...], preferred_element_type=F32) + bdn_ref[...]

    @pl.when(b >= nused_ref[0])
    def _():
        y_ref[...] = jnp.zeros_like(y_ref)


def _experts(plan, xsorted, w_gu, b_gu, w_dn, b_dn, layer):
    blk_e, nused, first, slot, nxt = plan
    p_rows = xsorted.shape[0]
    nblk = p_rows // MOE_BM
    depth = w_gu.shape[0]
    bsel = lambda b, be, nu, fi, sl, nx: (layer, be[b], 0, 0)
    grid_spec = pltpu.PrefetchScalarGridSpec(
        num_scalar_prefetch=5,
        grid=(nblk,),
        in_specs=[pl.BlockSpec((MOE_BM, D_MODEL), lambda b, be, nu, fi, sl, nx: (jnp.minimum(b, nu[0] - 1), 0)),
                  pl.BlockSpec(memory_space=pl.ANY),
                  pl.BlockSpec((None, None, 1, 2 * D_FF), bsel),
                  pl.BlockSpec(memory_space=pl.ANY),
                  pl.BlockSpec((None, None, 1, D_MODEL), bsel)],
        out_specs=pl.BlockSpec((MOE_BM, D_MODEL), lambda b, be, nu, fi, sl, nx: (b, 0)),
        scratch_shapes=[pltpu.VMEM((2, D_MODEL, 2 * D_FF), F32), pltpu.VMEM((2, D_FF, D_MODEL), F32),
                        pltpu.VMEM((D_MODEL, 2 * D_FF), BF16), pltpu.VMEM((D_FF, D_MODEL), BF16),
                        pltpu.SemaphoreType.DMA((2, 2))],
    )
    return pl.pallas_call(
        functools.partial(_expert_kernel, layer=layer),
        grid_spec=grid_spec,
        out_shape=jax.ShapeDtypeStruct((p_rows, D_MODEL), F32),
        compiler_params=_cparams(("arbitrary",)),
        name="moe_experts",
    )(blk_e, nused, first, slot, nxt, xsorted, w_gu, b_gu.reshape(depth, N_EXPERTS, 1, 2 * D_FF), w_dn,
      b_dn.reshape(depth, N_EXPERTS, 1, D_MODEL))


def _combine_kernel(base_ref, n8_ref, long_ref, y_ref, gk_ref, xs_ref, mod_ref, o_ref, runs, ovf, pend_ref, sem, osem,
                    *, tm, nsteps):
    i = pl.program_id(0)

    slot = i % 2

    def gather(tile, s):
        pend_ref[s] = _start_runs(tile, base_ref, n8_ref,
                                  lambda e, b, off, size: y_ref.at[pl.ds(b + off, size), :],
                                  lambda e, b, off, size: runs.at[s, pl.ds(e * RUN_CAP + off, size), :],
                                  sem.at[s])

    @pl.when(i == 0)
    def _():
        runs[...] = jnp.zeros_like(runs)
        ovf[...] = jnp.zeros_like(ovf)
        gather(0, 0)

    @pl.when(i + 1 < nsteps)
    def _():
        gather(i + 1, 1 - slot)

    _wait_rows(pend_ref[slot], lambda size: pltpu.make_async_copy(
        y_ref.at[pl.ds(0, size), :], runs.at[slot, pl.ds(0, size), :], sem.at[slot]), RUN_WAIT_BITS)
    gk = gk_ref[...]
    gates = [gk[:, k:k + 1] for k in range(TOP_K)]
    keys = [gk[:, TOP_K + k:TOP_K + k + 1].astype(I32) for k in range(TOP_K)]
    nrow = N_EXPERTS * RUN_CAP
    cc = lax.broadcasted_iota(I32, (1, nrow), 1)
    colkey = (cc >> RUN_SHIFT) * tm + (cc & (RUN_CAP - 1))
    w = jnp.zeros((tm, nrow), F32)
    for k in reversed(range(TOP_K)):
        w = jnp.where(colkey == keys[k], gates[k], w)
    acc = jnp.dot(w.astype(BF16), runs[slot].astype(BF16), preferred_element_type=F32)
    o_ref[...] = xs_ref[...] + acc * _res_gate(mod_ref, 3 * D_MODEL, tm, i * tm)

    def ovf_body(e, carry):
        n8 = n8_ref[i * N_EXPERTS + e]
        b = pl.multiple_of(base_ref[i * N_EXPERTS + e], 8)
        for c in range(1, tm // RUN_CAP):
            @pl.when(n8 > c * (RUN_CAP // 8))
            def _():
                for phase in (0, 1):
                    for live, size, off in _run_pieces(n8 - c * (RUN_CAP // 8), RUN_CAP // 8):
                        @pl.when(live)
                        def _():
                            cp = pltpu.make_async_copy(
                                y_ref.at[pl.ds(b + c * RUN_CAP + off, size), :],
                                ovf.at[pl.ds(off, size), :], osem)
                            if phase == 0:
                                cp.start()
                            else:
                                cp.wait()
                ck = lax.broadcasted_iota(I32, (1, RUN_CAP), 1) + (e * tm + c * RUN_CAP)
                wc = jnp.where(ck == keys[0], gates[0], 0.0)
                for k in range(1, TOP_K):
                    wc = wc + jnp.where(ck == keys[k], gates[k], 0.0)
                part = jnp.dot(wc.astype(BF16), ovf[...].astype(BF16), preferred_element_type=F32)
                o_ref[...] = o_ref[...] + part * _res_gate(mod_ref, 3 * D_MODEL, tm, i * tm)
        return carry

    @pl.when(long_ref[i] > 0)
    def _():
        lax.fori_loop(0, N_EXPERTS, ovf_body, 0)


def _combine(plan, y, gk_col, xs, mod):
    base, n8, long_run, _, _ = plan
    t = xs.shape[0]
    tm = BLK
    grid_spec = pltpu.PrefetchScalarGridSpec(
        num_scalar_prefetch=3,
        grid=(t // tm,),
        in_specs=[pl.BlockSpec(memory_space=pl.ANY),
                  pl.BlockSpec((tm, 128), lambda i, b, n, lg: (i, 0)),
                  pl.BlockSpec((tm, D_MODEL), lambda i, b, n, lg: (i, 0)),
                  pl.BlockSpec((8, 6 * D_MODEL), lambda i, b, n, lg: (0, 0))],
        out_specs=pl.BlockSpec((tm, D_MODEL), lambda i, b, n, lg: (i, 0)),
        scratch_shapes=[pltpu.VMEM((2, N_EXPERTS * RUN_CAP, D_MODEL), F32), pltpu.VMEM((RUN_CAP, D_MODEL), F32),
                        pltpu.SMEM((2,), I32), pltpu.SemaphoreType.DMA((2,)), pltpu.SemaphoreType.DMA(())],
    )
    return pl.pallas_call(
        functools.partial(_combine_kernel, tm=tm, nsteps=t // tm),
        grid_spec=grid_spec,
        out_shape=jax.ShapeDtypeStruct((t, D_MODEL), F32),
        compiler_params=_cparams(("arbitrary",)),
        name="moe_combine",
    )(base, n8, long_run, y, gk_col, xs, mod)


def _pad_heads(w, nh, hd, hdp):
    d = w.shape[0]
    return jnp.pad(w.reshape(d, nh, hd), ((0, 0), (0, 0), (0, hdp - hd))).reshape(d, nh * hdp)


def _layout_w_in(w):
    w_mix, w_gl = _layout_w_in_f32(w)
    return w_mix.astype(BF16), w_gl.astype(BF16)


def _layout_w_in_f32(w):
    pts = np.cumsum([A_WIDTH] * 4 + [2 * H_A, 2 * H_A, B_KWIDTH, B_KWIDTH, B_VWIDTH, B_VWIDTH,
                                     2 * GLA_RANK, C_WIDTH, KV_C * HD_C, KV_C * HD_C])
    (aq, ak, av, az, aa, ab, bq, bk, bv, bz, bg, cq, ck, cv, gl) = jnp.split(w, pts.tolist(), axis=1)
    small = jnp.concatenate([aa, ab, bg], axis=1)
    small = jnp.pad(small, ((0, 0), (0, 256 - small.shape[1])))
    cols = [aq, ak, av, az, _pad_heads(cq, H_C, HD_C, HD_CP), _pad_heads(ck, KV_C, HD_C, HD_CP),
            _pad_heads(cv, KV_C, HD_C, HD_CP), bq, bk, bv, bz, small]
    out = jnp.concatenate(cols, axis=1)
    assert out.shape[1] == IN_COLS_P
    return out, gl


def _rope_tables(t):
    s_len = t - CTX
    half = HD_C // 2
    inv_freq = ROPE_THETA ** (-jnp.arange(0, half, 2, dtype=F32) / half)
    pos = jnp.arange(s_len)
    rows = (pos // GRID_W).astype(F32)[:, None] * inv_freq[None, :]
    cols = (pos % GRID_W).astype(F32)[:, None] * inv_freq[None, :]
    cr, sr, cc, sc = jnp.cos(rows), jnp.sin(rows), jnp.cos(cols), jnp.sin(cols)
    zpad = jnp.zeros((s_len, HD_CP - HD_C), F32)
    cos_l = jnp.concatenate([cr, cr, cc, cc, zpad], axis=1)
    sin_l = jnp.concatenate([-sr, sr, -sc, sc, zpad], axis=1)
    cos_t = jnp.concatenate([jnp.ones((CTX, HD_CP), F32), cos_l], axis=0)
    sin_t = jnp.concatenate([jnp.zeros((CTX, HD_CP), F32), sin_l], axis=0)
    return cos_t, sin_t


def _lane_vec(v, width=128):
    v = v.reshape(1, -1).astype(F32)
    return jnp.pad(v, ((0, 0), (0, width - v.shape[1])))


def _moe_plan(cnt_tile, tk):
    nt = cnt_tile.shape[0]
    pad8 = (cnt_tile + 7) // 8 * 8
    total = jnp.sum(pad8, axis=0)
    padded = (total + MOE_BM - 1) // MOE_BM * MOE_BM
    pend = jnp.cumsum(padded)
    estart = pend - padded
    base = estart[None, :] + jnp.cumsum(pad8, axis=0) - pad8
    nblk = (tk + nt * N_EXPERTS * 7 + N_EXPERTS * (MOE_BM - 1) + MOE_BM - 1) // MOE_BM
    p_rows = nblk * MOE_BM
    blk = jnp.arange(nblk, dtype=I32)
    blk_e = jnp.minimum(jnp.sum((pend[None, :] <= (blk * MOE_BM)[:, None]).astype(I32), axis=1), N_EXPERTS - 1)
    nused = (pend[-1] // MOE_BM).astype(I32)
    prev_e = jnp.concatenate([jnp.full((1,), -1, I32), blk_e[:-1]])
    first = ((blk_e != prev_e) & (blk < nused)).astype(I32)
    slot = (jnp.cumsum(first) - 1) % 2
    pos = jnp.where(first == 1, blk, nblk)
    nxt_pos = jnp.concatenate([lax.cummin(pos, axis=0, reverse=True)[1:], jnp.full((1,), nblk, I32)])
    nxt = jnp.where(nxt_pos < nblk, blk_e[jnp.minimum(nxt_pos, nblk - 1)], -1)
    last_blk = jnp.where(total > 0, pend - MOE_BM, -1)
    ntail = (nt * N_EXPERTS * 7 + N_EXPERTS * (MOE_BM - 1)) // MOE_BM + 1
    tail = pend[-1] + jnp.arange(ntail, dtype=I32) * MOE_BM
    zstart = jnp.concatenate([last_blk, jnp.where(tail < p_rows, tail, -1)]).astype(I32)
    eplan = (blk_e, nused.reshape(1), first, slot.astype(I32), nxt.astype(I32))
    long_run = (jnp.max(pad8, axis=1) > RUN_CAP).astype(I32)
    mplan = (base.reshape(-1).astype(I32), (pad8 // 8).reshape(-1).astype(I32), long_run, zstart, p_rows)
    return mplan, eplan


def kernel(x, c, ctx, c_ctx, ada_w, ada_b, norm_mix_g, norm_ffn_g, w_in, dn_conv_w, dn_a_log, dn_dt_bias,
           dn_norm_g, gla_w2, gla_b2, gla_norm_g, attn_q_norm_g, attn_k_norm_g, attn_sink, w_branch_a,
           w_branch_b, w_branch_c, w_out, router_w, router_b, w_gate_up, b_gate_up, w_down, b_down):
    assert x.shape[0] == 1 and c.shape[0] == 1 and ctx.shape[1] == CTX
    depth = ada_w.shape[0]
    xs = jnp.concatenate([ctx[0], x[0]], axis=0)
    t = xs.shape[0]
    assert t % BLK == 0 and (t - CTX) % GRID_W == 0
    cc = jnp.concatenate([c, c_ctx[None, :], jnp.zeros((6, D_MODEL), F32)], axis=0)
    mods = _ada_mod(cc, ada_w, ada_b)
    cos_t, sin_t = _rope_tables(t)
    for l in range(depth):
        mod = mods[l]
        w_mix, w_gl = _layout_w_in(w_in[l])
        proj = _inproj(xs, mod, norm_mix_g[l][None, :], w_mix)
        qa, ka, va, gcol = _dnprep(proj, dn_conv_w[l], _lane_vec(dn_a_log[l]), _lane_vec(dn_dt_bias[l]))
        oaf, oab = _dnscan(qa, ka, va, gcol)
        w2 = gla_w2[l].astype(F32)
        w2full = jnp.zeros((128, 2 * B_KWIDTH), F32)
        for d in range(2):
            r0 = 4 * H_A + d * GLA_RANK
            w2full = w2full.at[r0:r0 + GLA_RANK, d * B_KWIDTH:(d + 1) * B_KWIDTH].set(w2[d])
        obf, obb = _glascan(proj, w2full, gla_b2[l].reshape(1, 2 * B_KWIDTH).astype(F32))
        qg = _lane_vec(attn_q_norm_g[l])
        kg = _lane_vec(attn_k_norm_g[l])
        qr, kr, vr = _attnprep(proj, qg, kg, cos_t, sin_t)
        yc = _attn(qr, kr, vr, attn_sink[l].astype(F32))
        wpc = jnp.pad(w_branch_c[l].reshape(H_C, HD_C, D_MODEL),
                      ((0, 0), (0, HD_CP - HD_C), (0, 0))).reshape(H_C * HD_CP, D_MODEL)
        xs = _merge(oaf, oab, obf, obb, yc, proj, xs, norm_mix_g[l][None, :], w_gl,
                    dn_norm_g[l][None, :], gla_norm_g[l][None, :],
                    w_branch_a[l].astype(BF16), w_branch_b[l].astype(BF16), wpc.astype(BF16),
                    w_out[l].astype(BF16), mod)
        h2, keys, gk_col, cnt = _router(xs, mod, norm_ffn_g[l][None, :], router_w[l].T, router_b[l][:, None])
        mplan, eplan = _moe_plan(cnt[:, :, 0].astype(I32), t * TOP_K)
        xsorted = _dispatch(mplan, keys, h2)
        y = _experts(eplan, xsorted, w_gate_up, b_gate_up, w_down, b_down, l)
        xs = _combine(mplan, y, gk_col, xs, mod)
    return xs[CTX:][None]
```

```python
import functools

import jax
import jax.numpy as jnp
import numpy as np
from jax import lax
from jax.experimental import pallas as pl
from jax.experimental.pallas import tpu as pltpu

F32 = jnp.float32
BF16 = jnp.bfloat16
I32 = jnp.int32

D_MODEL = 1024
DEPTH = 4
GRID_W = 64
CTX = 256
H_A = 4
HD_A = 128
A_WIDTH = H_A * HD_A
CONV_W = 5
H_B = 4
DK_B = 64
DV_B = 128
B_KWIDTH = H_B * DK_B
B_VWIDTH = H_B * DV_B
GLA_RANK = 16
GLA_NORMALIZER = 16.0
H_C = 8
KV_C = 2
REP_C = H_C // KV_C
HD_C = 64
HD_CP = 128
C_WIDTH = H_C * HD_C
ATT_BLOCK = 128
ROPE_THETA = 10000.0
CHUNK = 64
SUB = 16
N_EXPERTS = 32
TOP_K = 4
D_FF = 1024
SWIGLU_LIMIT = 7.0
SWIGLU_ALPHA = 1.702
EPS = 1e-6
NEG = -1e30
CHUNK_SHIFT = CHUNK.bit_length() - 1
SUB_SHIFT = SUB.bit_length() - 1
DK_SHIFT = DK_B.bit_length() - 1
DV_SHIFT = DV_B.bit_length() - 1

BLK = 256
MOE_BM = 512
VMEM_LIMIT = 56 * 1024 * 1024

COL_QKV = 0
COL_AZ = 1536
COL_CQ = 2048
COL_CKV = 3072
COL_BQK = 3584
COL_BV = 4096
COL_BZ = 4608
COL_SMALL = 5120
IN_COLS_P = 5376
IN_TN = 768


def _pick(n, cands):
    for c in cands:
        if n % c == 0:
            return c
    raise ValueError(f"no tile for {n}")


def _cparams(sem):
    return pltpu.CompilerParams(dimension_semantics=sem, vmem_limit_bytes=VMEM_LIMIT)


def _bdot(a, b):
    return jnp.dot(a.astype(BF16), b.astype(BF16), preferred_element_type=F32)


def _bdot_nt(a, b):
    return lax.dot_general(a.astype(BF16), b.astype(BF16), (((1,), (1,)), ((), ())),
                           preferred_element_type=F32)


def _bdot_tn(a, b):
    return lax.dot_general(a.astype(BF16), b.astype(BF16), (((0,), (0,)), ((), ())),
                           preferred_element_type=F32)


def _fdot(a, b):
    return jnp.dot(a, b, precision=lax.Precision.HIGHEST, preferred_element_type=F32)


def _split2(a):
    hi = a.astype(BF16)
    lo = (a - hi.astype(F32)).astype(BF16)
    return hi, lo


def _dot3(a, b):
    ah, al = _split2(a)
    bh, bl = _split2(b)
    d = functools.partial(jnp.dot, preferred_element_type=F32)
    return d(ah, bh) + (d(ah, bl) + d(al, bh))


def _cumsum_dot(tri, x):
    t = tri.astype(BF16)
    hi = x.astype(BF16)
    r1 = x - hi.astype(F32)
    mid = r1.astype(BF16)
    lo = (r1 - mid.astype(F32)).astype(BF16)
    d = functools.partial(jnp.dot, preferred_element_type=F32)
    return d(t, hi) + (d(t, mid) + d(t, lo))


def _sigmoid(x):
    return 1.0 / (1.0 + jnp.exp(-x))


def _silu(x):
    return x * _sigmoid(x)


def _softplus(x):
    return jnp.maximum(x, 0.0) + jnp.log(1.0 + jnp.exp(-jnp.abs(x)))


def _ada_kernel(cc_ref, w_ref, b_ref, o_ref):
    o_ref[...] = _fdot(_silu(cc_ref[...]), w_ref[...]) + b_ref[...]


def _ada_mod(cc, ada_w, ada_b):
    depth = ada_w.shape[0]
    tn = 1536
    return pl.pallas_call(
        _ada_kernel,
        grid=(depth, 6 * D_MODEL // tn),
        in_specs=[pl.BlockSpec((8, D_MODEL), lambda l, j: (0, 0)),
                  pl.BlockSpec((None, D_MODEL, tn), lambda l, j: (l, 0, j)),
                  pl.BlockSpec((None, 1, tn), lambda l, j: (l, 0, j))],
        out_specs=pl.BlockSpec((None, 8, tn), lambda l, j: (l, 0, j)),
        out_shape=jax.ShapeDtypeStruct((depth, 8, 6 * D_MODEL), F32),
        compiler_params=_cparams(("arbitrary", "arbitrary")),
        name="ada_mod",
    )(cc, ada_w, ada_b.reshape(depth, 1, 6 * D_MODEL))


def _norm_mod(x, g, mod_ref, moff, row0):
    tm = x.shape[0]
    y = x * lax.rsqrt(jnp.mean(x * x, axis=-1, keepdims=True) + EPS) * g
    isc = (row0 + lax.broadcasted_iota(I32, (tm, 1), 0)) < CTX
    shift = jnp.where(isc, mod_ref[1:2, moff:moff + D_MODEL], mod_ref[0:1, moff:moff + D_MODEL])
    scale = jnp.where(isc, mod_ref[1:2, moff + D_MODEL:moff + 2 * D_MODEL],
                      mod_ref[0:1, moff + D_MODEL:moff + 2 * D_MODEL])
    return y * (1.0 + scale) + shift


def _res_gate(mod_ref, moff, tm, row0):
    isc = (row0 + lax.broadcasted_iota(I32, (tm, 1), 0)) < CTX
    return jnp.where(isc, mod_ref[1:2, moff + 2 * D_MODEL:moff + 3 * D_MODEL],
                     mod_ref[0:1, moff + 2 * D_MODEL:moff + 3 * D_MODEL])


def _inproj_kernel(x_ref, mod_ref, g_ref, w_ref, o_ref, h_ref, *, tm):
    i = pl.program_id(0)

    @pl.when(pl.program_id(1) == 0)
    def _():
        h_ref[...] = _norm_mod(x_ref[...], g_ref[...], mod_ref, 0, i * tm).astype(BF16)

    o_ref[...] = jnp.dot(h_ref[...], w_ref[...], preferred_element_type=F32)


def _inproj(xs, mod, g, w):
    t = xs.shape[0]
    tm = _pick(t, (1664, 1280, 640, 256))
    return pl.pallas_call(
        functools.partial(_inproj_kernel, tm=tm),
        grid=(t // tm, IN_COLS_P // IN_TN),
        in_specs=[pl.BlockSpec((tm, D_MODEL), lambda i, j: (i, 0)),
                  pl.BlockSpec((8, 6 * D_MODEL), lambda i, j: (0, 0)),
                  pl.BlockSpec((1, D_MODEL), lambda i, j: (0, 0)),
                  pl.BlockSpec((D_MODEL, IN_TN), lambda i, j: (0, j))],
        out_specs=pl.BlockSpec((tm, IN_TN), lambda i, j: (i, j)),
        out_shape=jax.ShapeDtypeStruct((t, IN_COLS_P), F32),
        scratch_shapes=[pltpu.VMEM((tm, D_MODEL), BF16)],
        compiler_params=_cparams(("arbitrary", "arbitrary")),
        name="inproj",
    )(xs, mod, g, w)


def _tri_blockdiag(n, lower):
    ii = lax.broadcasted_iota(I32, (n, n), 0)
    jj = lax.broadcasted_iota(I32, (n, n), 1)
    same = (ii >> CHUNK_SHIFT) == (jj >> CHUNK_SHIFT)
    tri = (ii >= jj) if lower else (ii <= jj)
    return jnp.where(same, jnp.where(tri, 1.0, 0.0), 0.0).astype(F32)


def _dnprep_kernel(main_ref, prev_ref, next_ref, small_ref, cw_ref, alog_ref, dtb_ref,
                   q_ref, k_ref, v_ref, gcol_ref, ext_ref, *, nb):
    i = pl.program_id(0)
    use_prev = i >= 2
    use_next = jnp.logical_and(i >= 1, i <= nb - 2)
    ext_ref[0:8, :] = jnp.where(use_prev, prev_ref[...], 0.0)
    ext_ref[8:8 + BLK, :] = main_ref[...]
    ext_ref[8 + BLK:16 + BLK, :] = jnp.where(use_next, next_ref[...], 0.0)
    acc = ext_ref[6:6 + BLK, :] * cw_ref[0:1, :]
    for d in range(1, CONV_W):
        acc = acc + ext_ref[6 + d:6 + d + BLK, :] * cw_ref[d:d + 1, :]
    s = _silu(acc)
    for h in range(H_A):
        for part, ref, mul in ((0, q_ref, HD_A ** -0.5), (1, k_ref, 1.0)):
            seg = s[:, part * A_WIDTH + h * HD_A: part * A_WIDTH + (h + 1) * HD_A]
            nrm = seg * lax.rsqrt(jnp.sum(seg * seg, axis=-1, keepdims=True) + EPS)
            ref[:, h * HD_A:(h + 1) * HD_A] = nrm * mul
    v_ref[...] = s[:, 2 * A_WIDTH:3 * A_WIDTH]
    sm = small_ref[...]
    lane = lax.broadcasted_iota(I32, sm.shape, 1)
    g = -jnp.exp(alog_ref[...]) * _softplus(sm + dtb_ref[...])
    gb = jnp.where(lane < 2 * H_A, g, jnp.where(lane < 4 * H_A, _sigmoid(sm), 0.0))
    cf = _fdot(_tri_blockdiag(BLK, True), gb)
    cr = _fdot(_tri_blockdiag(BLK, False), gb)
    gc = jnp.where(lane < H_A, cf, jnp.where(lane < 2 * H_A, cr, gb))
    gcol_ref[...] = gc


def _dnprep(proj, conv_w, alog_vec, dtb_vec):
    t = proj.shape[0]
    nb = t // BLK
    qkv_blk = COL_QKV // (3 * A_WIDTH)
    last8 = t // 8 - 1
    out_sds = jax.ShapeDtypeStruct((t, A_WIDTH), F32)
    return pl.pallas_call(
        functools.partial(_dnprep_kernel, nb=nb),
        grid=(nb,),
        in_specs=[pl.BlockSpec((BLK, 3 * A_WIDTH), lambda i: (i, qkv_blk)),
                  pl.BlockSpec((8, 3 * A_WIDTH), lambda i: (jnp.maximum(i * (BLK // 8) - 1, 0), qkv_blk)),
                  pl.BlockSpec((8, 3 * A_WIDTH), lambda i: (jnp.minimum((i + 1) * (BLK // 8), last8), qkv_blk)),
                  pl.BlockSpec((BLK, 128), lambda i: (i, COL_SMALL // 128)),
                  pl.BlockSpec((CONV_W, 3 * A_WIDTH), lambda i: (0, 0)),
                  pl.BlockSpec((1, 128), lambda i: (0, 0)),
                  pl.BlockSpec((1, 128), lambda i: (0, 0))],
        out_specs=[pl.BlockSpec((BLK, A_WIDTH), lambda i: (i, 0)),
                   pl.BlockSpec((BLK, A_WIDTH), lambda i: (i, 0)),
                   pl.BlockSpec((BLK, A_WIDTH), lambda i: (i, 0)),
                   pl.BlockSpec((BLK, 128), lambda i: (i, 0))],
        out_shape=[out_sds, out_sds, out_sds,
                   jax.ShapeDtypeStruct((t, 128), F32)],
        scratch_shapes=[pltpu.VMEM((BLK + 16, 3 * A_WIDTH), F32)],
        compiler_params=_cparams(("arbitrary",)),
        name="dn_prep",
    )(proj, proj, proj, proj, conv_w, alog_vec, dtb_vec)


def _dot3_all(a_list, b_list):
    d = functools.partial(jnp.dot, preferred_element_type=F32)
    sa = [_split2(a) for a in a_list]
    sb = [_split2(b) for b in b_list]
    hh = [d(a[0], b[0]) for a, b in zip(sa, sb)]
    hl = [d(a[0], b[1]) for a, b in zip(sa, sb)]
    lh = [d(a[1], b[0]) for a, b in zip(sa, sb)]
    return [x + (y + z) for x, y, z in zip(hh, hl, lh)]


def _unit_tri_inverse_all(l_mats, masks):
    eye, m_diag, m_l1, m_l2 = masks
    ld = [l * m_diag for l in l_mats]
    x = [eye - a for a in ld]
    p = _dot3_all(ld, ld)
    for it in range(3):
        xp = _dot3_all(x, p)
        if it < 2:
            p = _dot3_all(p, p)
        x = [a + b for a, b in zip(x, xp)]
    for m in (m_l1, m_l2):
        cx = _dot3_all([l * m for l in l_mats], x)
        xcx = _dot3_all(x, cx)
        x = [a - b for a, b in zip(x, xcx)]
    return x


def _dn_masks():
    ii = lax.broadcasted_iota(I32, (CHUNK, CHUNK), 0)
    jj = lax.broadcasted_iota(I32, (CHUNK, CHUNK), 1)
    one = lambda c: jnp.where(c, 1.0, 0.0).astype(F32)
    eye = one(ii == jj)
    m_diag = one((ii >> SUB_SHIFT) == (jj >> SUB_SHIFT))
    m_l2 = one((ii >> (SUB_SHIFT + 1)) != (jj >> (SUB_SHIFT + 1)))
    m_l1 = 1.0 - m_diag - m_l2
    return ii, jj, (eye, m_diag, m_l1, m_l2)


def _dn_local(items, ii, jj, masks):
    n = len(items)
    dec, lmat, qk, kb, eg = [], [], [], [], []
    for q, k, v, gcol, grow, bcol, fwd in items:
        incl = (ii >= jj) if fwd else (ii <= jj)
        dec.append(jnp.exp(jnp.where(incl, gcol - grow, NEG)))
        kb.append(k * bcol)
        eg.append(jnp.exp(gcol))
    kh = [it[1].astype(BF16) for it in items]
    kk = [_bdot_nt(kb[i], kh[i]) for i in range(n)]
    qkr = [_bdot_nt(items[i][0], kh[i]) for i in range(n)]
    for i in range(n):
        fwd = items[i][6]
        strict = (ii > jj) if fwd else (ii < jj)
        lmat.append(kk[i] * jnp.where(strict, dec[i], 0.0))
        qk.append((qkr[i] * dec[i]).astype(BF16))
    rhs = [jnp.concatenate([items[i][2] * items[i][5], kb[i] * eg[i]], axis=1) for i in range(n)]
    sol = _dot3_all(_unit_tri_inverse_all(lmat, masks), rhs)
    out = []
    for i in range(n):
        q, k, _, gcol, _, _, fwd = items[i]
        glast = gcol[CHUNK - 1:CHUNK, :] if fwd else gcol[0:1, :]
        out.append((sol[i][:, :HD_A], sol[i][:, HD_A:].astype(BF16), qk[i], (q * eg[i]).astype(BF16),
                    (k * jnp.exp(glast - gcol)).astype(BF16), jnp.exp(glast)))
    return out


def _dn_step(local, states):
    n = len(local)
    sb = [s.astype(BF16) for s in states]
    d = functools.partial(jnp.dot, preferred_element_type=F32)
    ws = [d(local[i][1], sb[i]) for i in range(n)]
    qs = [d(local[i][3], sb[i]) for i in range(n)]
    v_new = [(local[i][0] - ws[i]).astype(BF16) for i in range(n)]
    o2 = [d(local[i][2], v_new[i]) for i in range(n)]
    kv = [lax.dot_general(local[i][4], v_new[i], (((0,), (0,)), ((), ())), preferred_element_type=F32)
          for i in range(n)]
    return [qs[i] + o2[i] for i in range(n)], [states[i] * local[i][5] + kv[i] for i in range(n)]


def _dnscan_kernel(qf, kf, vf, gcf, qb, kb, vb, gcb, of_ref, ob_ref, s_ref):
    @pl.when(pl.program_id(0) == 0)
    def _():
        s_ref[...] = jnp.zeros_like(s_ref)

    ii, jj, masks = _dn_masks()
    nch = BLK // CHUNK
    pick = jnp.where(lax.broadcasted_iota(I32, (16, 128), 0) == lax.broadcasted_iota(I32, (16, 128), 1),
                     1.0, 0.0).astype(F32)

    dirs = ((True, (qf, kf, vf, gcf, of_ref)), (False, (qb, kb, vb, gcb, ob_ref)))
    items, sinks = [], []
    for step in range(nch):
        for fwd, (q_r, k_r, v_r, gc_r, o_r) in dirs:
            c = step if fwd else nch - 1 - step
            rows = slice(c * CHUNK, (c + 1) * CHUNK)
            d = 0 if fwd else 1
            gct = gc_r[rows, :]
            grows = lax.dot_general(pick, gct, (((1,), (1,)), ((), ())),
                                    precision=lax.Precision.HIGHEST, preferred_element_type=F32)
            for h in range(H_A):
                lanes = slice(h * HD_A, (h + 1) * HD_A)
                gi = d * H_A + h
                items.append((q_r[rows, lanes], k_r[rows, lanes], v_r[rows, lanes],
                              gct[:, gi:gi + 1], grows[gi:gi + 1, :],
                              gct[:, 2 * H_A + gi:2 * H_A + gi + 1], fwd))
                sinks.append((o_r, rows, lanes))
    local = _dn_local(items, ii, jj, masks)
    nchain = 2 * H_A
    states = [s_ref[gi] for gi in range(nchain)]
    for step in range(nch):
        outs, states = _dn_step(local[step * nchain:(step + 1) * nchain], states)
        for (o_r, rows, lanes), o in zip(sinks[step * nchain:(step + 1) * nchain], outs):
            o_r[rows, lanes] = o
    for gi in range(nchain):
        s_ref[gi] = states[gi]


def _rev_block(nb):
    return lambda i: jnp.where(i == 0, 0, nb - i)


def _dnscan(q, k, v, gcol):
    t = q.shape[0]
    nb = t // BLK
    rev = _rev_block(nb)
    wide = lambda f: pl.BlockSpec((BLK, A_WIDTH), lambda i: (f(i), 0))
    col = lambda f: pl.BlockSpec((BLK, 128), lambda i: (f(i), 0))
    ident = lambda i: i
    out_sds = jax.ShapeDtypeStruct((t, A_WIDTH), F32)
    return pl.pallas_call(
        _dnscan_kernel,
        grid=(nb,),
        in_specs=[wide(ident), wide(ident), wide(ident), col(ident),
                  wide(rev), wide(rev), wide(rev), col(rev)],
        out_specs=[wide(ident), wide(rev)],
        out_shape=[out_sds, out_sds],
        scratch_shapes=[pltpu.VMEM((2 * H_A, HD_A, HD_A), F32)],
        compiler_params=_cparams(("arbitrary",)),
        name="dn_scan",
    )(q, k, v, gcol, q, k, v, gcol)


def _gla_gates(small_ref, w2_ref, b2_ref, b_ref, d, fwd):
    cols = slice(d * B_KWIDTH, (d + 1) * B_KWIDTH)
    pre = _dot3(small_ref[...], w2_ref[:, cols]) + b2_ref[:, cols]
    gk = -_softplus(-pre) * (1.0 / GLA_NORMALIZER)
    b_ref[d] = _cumsum_dot(_tri_blockdiag(BLK, fwd), gk)


def _gla_chunk(qk_ref, v_ref, b_ref, o_ref, st_ref, d, fwd, consts, step):
    sel, headmask_k, st_mask = consts
    nch = BLK // CHUNK
    nsub = CHUNK // SUB
    sub_i = lax.broadcasted_iota(I32, (SUB, 1), 0)
    row_c = lax.broadcasted_iota(I32, (CHUNK, 1), 0)
    if True:
        c = step if fwd else nch - 1 - step
        rows = pl.ds(pl.multiple_of(c * CHUNK, CHUNK), CHUNK)
        q = qk_ref[rows, 0:B_KWIDTH] * (DK_B ** -0.5)
        k = qk_ref[rows, B_KWIDTH:2 * B_KWIDTH]
        v = v_ref[rows, :]
        b = b_ref[d, rows, :]
        vh = v.astype(BF16)
        st = st_ref[d]
        o = _bdot_nt(q * jnp.exp(b), st)
        refs = []
        for sb in range(nsub):
            if fwd:
                r = b[sb * SUB - 1:sb * SUB, :] if sb > 0 else jnp.zeros((1, B_KWIDTH), F32)
            else:
                r = b[(sb + 1) * SUB:(sb + 1) * SUB + 1, :] if sb < nsub - 1 else jnp.zeros((1, B_KWIDTH), F32)
            refs.append(r)
        rfull = jnp.concatenate([jnp.broadcast_to(r, (SUB, B_KWIDTH)) for r in refs], axis=0)
        qs = q * jnp.exp(b - rfull)
        a_off = [None] * H_B
        for sb in (range(1, nsub) if fwd else range(0, nsub - 1)):
            jmask = (row_c < sb * SUB) if fwd else (row_c >= (sb + 1) * SUB)
            ks = (k * jnp.exp(jnp.where(jmask, refs[sb] - b, NEG))).astype(BF16)
            rowmask = jnp.where((row_c >> SUB_SHIFT) == sb, 1.0, 0.0)
            for h in range(H_B):
                a = _bdot_nt(qs * headmask_k[h], ks) * rowmask
                a_off[h] = a if a_off[h] is None else a_off[h] + a
        o = o + jnp.concatenate(
            [_bdot(a_off[h], vh[:, h * DV_B:(h + 1) * DV_B]) for h in range(H_B)], axis=1)
        diag = []
        for sb in range(nsub):
            s0 = sb * SUB
            bs, qsb, ksb = b[s0:s0 + SUB, :], q[s0:s0 + SUB, :], k[s0:s0 + SUB, :]
            tiles = []
            for jl in range(SUB):
                causal = (sub_i >= jl) if fwd else (sub_i <= jl)
                e = jnp.exp(jnp.where(causal, bs - bs[jl:jl + 1, :], NEG))
                tiles.append((qsb * ksb[jl:jl + 1, :] * e).astype(BF16))
            red = jnp.dot(jnp.concatenate(tiles, axis=0), sel, preferred_element_type=F32)
            acc = red[0:SUB, :] * v[s0:s0 + 1, :]
            for jl in range(1, SUB):
                acc = acc + red[jl * SUB:(jl + 1) * SUB, :] * v[s0 + jl:s0 + jl + 1, :]
            diag.append(acc)
        o_ref[rows, :] = o + jnp.concatenate(diag, axis=0)
        blast = b[CHUNK - 1:CHUNK, :] if fwd else b[0:1, :]
        kd = k * jnp.exp(blast - b)
        st_ref[d] = st * jnp.exp(blast) + _bdot_tn(v, kd) * st_mask


def _glascan_kernel(qkf, vf, smf, qkb, vb, smb, w2_ref, b2_ref, of_ref, ob_ref, st_ref, b_ref):
    @pl.when(pl.program_id(0) == 0)
    def _():
        st_ref[...] = jnp.zeros_like(st_ref)

    kk = lax.broadcasted_iota(I32, (B_KWIDTH, B_VWIDTH), 0)
    cc = lax.broadcasted_iota(I32, (B_KWIDTH, B_VWIDTH), 1)
    sel = jnp.where((kk >> DK_SHIFT) == (cc >> DV_SHIFT), 1.0, 0.0).astype(BF16)
    lane = lax.broadcasted_iota(I32, (1, B_KWIDTH), 1)
    headmask_k = [jnp.where((lane >> DK_SHIFT) == h, 1.0, 0.0).astype(F32) for h in range(H_B)]
    rr = lax.broadcasted_iota(I32, (B_VWIDTH, B_KWIDTH), 0)
    kc = lax.broadcasted_iota(I32, (B_VWIDTH, B_KWIDTH), 1)
    st_mask = jnp.where((rr >> DV_SHIFT) == (kc >> DK_SHIFT), 1.0, 0.0).astype(F32)
    consts = (sel, headmask_k, st_mask)
    _gla_gates(smf, w2_ref, b2_ref, b_ref, 0, True)
    _gla_gates(smb, w2_ref, b2_ref, b_ref, 1, False)

    def body(step, carry):
        _gla_chunk(qkf, vf, b_ref, of_ref, st_ref, 0, True, consts, step)
        _gla_chunk(qkb, vb, b_ref, ob_ref, st_ref, 1, False, consts, step)
        return carry

    lax.fori_loop(0, BLK // CHUNK, body, 0)


def _glascan(proj, w2full, b2full):
    t = proj.shape[0]
    nb = t // BLK
    rev = _rev_block(nb)
    ident = lambda i: i
    qk = lambda f: pl.BlockSpec((BLK, 2 * B_KWIDTH), lambda i: (f(i), COL_BQK // (2 * B_KWIDTH)))
    vv = lambda f: pl.BlockSpec((BLK, B_VWIDTH), lambda i: (f(i), COL_BV // B_VWIDTH))
    sm = lambda f: pl.BlockSpec((BLK, 128), lambda i: (f(i), COL_SMALL // 128))
    outs = lambda f: pl.BlockSpec((BLK, B_VWIDTH), lambda i: (f(i), 0))
    out_sds = jax.ShapeDtypeStruct((t, B_VWIDTH), F32)
    return pl.pallas_call(
        _glascan_kernel,
        grid=(nb,),
        in_specs=[qk(ident), vv(ident), sm(ident), qk(rev), vv(rev), sm(rev),
                  pl.BlockSpec((128, 2 * B_KWIDTH), lambda i: (0, 0)),
                  pl.BlockSpec((1, 2 * B_KWIDTH), lambda i: (0, 0))],
        out_specs=[outs(ident), outs(rev)],
        out_shape=[out_sds, out_sds],
        scratch_shapes=[pltpu.VMEM((2, B_VWIDTH, B_KWIDTH), F32), pltpu.VMEM((2, BLK, B_KWIDTH), F32)],
        compiler_params=_cparams(("arbitrary",)),
        name="gla_scan",
    )(proj, proj, proj, proj, proj, proj, w2full, b2full)


def _attnprep_kernel(cq_ref, ckv_ref, qg_ref, kg_ref, cos_ref, sin_ref, q_ref, k_ref, v_ref):
    cos = cos_ref[...]
    sin = sin_ref[...]
    lane = lax.broadcasted_iota(I32, cos.shape, 1)
    first = (lane % 32) < 16

    def norm_rope(x, g):
        y = x * lax.rsqrt(jnp.sum(x * x, axis=-1, keepdims=True) * (1.0 / HD_C) + EPS) * g
        partner = jnp.where(first, pltpu.roll(y, HD_CP - 16, 1), pltpu.roll(y, 16, 1))
        return y * cos + partner * sin

    for h in range(H_C):
        seg = slice(h * HD_CP, (h + 1) * HD_CP)
        q_ref[:, seg] = (norm_rope(cq_ref[:, seg], qg_ref[...]) * (HD_C ** -0.5)).astype(BF16)
    for g in range(KV_C):
        seg = slice(g * HD_CP, (g + 1) * HD_CP)
        k_ref[:, seg] = norm_rope(ckv_ref[:, seg], kg_ref[...]).astype(BF16)
    v_ref[...] = ckv_ref[:, KV_C * HD_CP:2 * KV_C * HD_CP].astype(BF16)


def _attnprep(proj, qg, kg, cos_t, sin_t):
    t = proj.shape[0]
    tm = _pick(t, (640, BLK))
    qw, kw = H_C * HD_CP, KV_C * HD_CP
    return pl.pallas_call(
        _attnprep_kernel,
        grid=(t // tm,),
        in_specs=[pl.BlockSpec((tm, qw), lambda i: (i, COL_CQ // qw)),
                  pl.BlockSpec((tm, 2 * kw), lambda i: (i, COL_CKV // (2 * kw))),
                  pl.BlockSpec((1, HD_CP), lambda i: (0, 0)),
                  pl.BlockSpec((1, HD_CP), lambda i: (0, 0)),
                  pl.BlockSpec((tm, HD_CP), lambda i: (i, 0)),
                  pl.BlockSpec((tm, HD_CP), lambda i: (i, 0))],
        out_specs=[pl.BlockSpec((tm, qw), lambda i: (i, 0)),
                   pl.BlockSpec((tm, kw), lambda i: (i, 0)),
                   pl.BlockSpec((tm, kw), lambda i: (i, 0))],
        out_shape=[jax.ShapeDtypeStruct((t, qw), BF16),
                   jax.ShapeDtypeStruct((t, kw), BF16),
                   jax.ShapeDtypeStruct((t, kw), BF16)],
        compiler_params=_cparams(("arbitrary",)),
        name="attn_prep",
    )(proj, proj, qg, kg, cos_t, sin_t)


def _attn_kernel(sink_ref, q_ref, kp_ref, kc_ref, kn_ref, kx_ref, vp_ref, vc_ref, vn_ref, vx_ref,
                 o_ref, *, nq):
    qi = pl.program_id(0)
    nctx = CTX // ATT_BLOCK
    latent = qi >= nctx
    ql = lax.broadcasted_iota(I32, (ATT_BLOCK, ATT_BLOCK), 0)
    kl = lax.broadcasted_iota(I32, (ATT_BLOCK, ATT_BLOCK), 1)
    ok_prev = jnp.logical_and(qi - 1 >= nctx, kl >= ql)
    ok_next = jnp.logical_and(jnp.logical_and(latent, qi + 1 <= nq - 1), kl <= ql)
    ok_cur = jnp.logical_and(latent, kl >= 0)
    bias = jnp.concatenate([jnp.where(ok_prev, 0.0, NEG), jnp.where(ok_cur, 0.0, NEG),
                            jnp.where(ok_next, 0.0, NEG),
                            jnp.zeros((ATT_BLOCK, CTX), F32)], axis=1)
    for g in range(KV_C):
        seg = slice(g * HD_CP, (g + 1) * HD_CP)
        kcat = jnp.concatenate([kp_ref[:, seg], kc_ref[:, seg], kn_ref[:, seg], kx_ref[:, seg]], axis=0)
        vcat = jnp.concatenate([vp_ref[:, seg], vc_ref[:, seg], vn_ref[:, seg], vx_ref[:, seg]], axis=0)
        heads = [g * REP_C + r for r in range(REP_C)]
        cols = [slice(h * HD_CP, (h + 1) * HD_CP) for h in heads]
        s = [lax.dot_general(q_ref[:, c], kcat, (((1,), (1,)), ((), ())), preferred_element_type=F32) + bias
             for c in cols]
        m = [jnp.maximum(jnp.max(s[i], axis=-1, keepdims=True), sink_ref[heads[i]]) for i in range(REP_C)]
        p = [jnp.exp(s[i] - m[i]) for i in range(REP_C)]
        den = [jnp.sum(p[i], axis=-1, keepdims=True) + jnp.exp(sink_ref[heads[i]] - m[i]) for i in range(REP_C)]
        o = [jnp.dot(p[i].astype(BF16), vcat, preferred_element_type=F32) for i in range(REP_C)]
        for i in range(REP_C):
            o_ref[:, cols[i]] = (o[i] / den[i]).astype(BF16)


def _attn(qr, kr, vr, sink):
    t = qr.shape[0]
    nq = t // ATT_BLOCK
    nctx = CTX // ATT_BLOCK
    qw, kw = H_C * HD_CP, KV_C * HD_CP
    prev = lambda i: (jnp.maximum(i - 1, nctx), 0)
    cur = lambda i: (i, 0)
    nxt = lambda i: (jnp.minimum(jnp.maximum(i + 1, nctx), nq - 1), 0)
    kv = lambda f: pl.BlockSpec((ATT_BLOCK, kw), f)
    ctxs = pl.BlockSpec((CTX, kw), lambda i: (0, 0))
    return pl.pallas_call(
        functools.partial(_attn_kernel, nq=nq),
        grid=(nq,),
        in_specs=[pl.BlockSpec(memory_space=pltpu.SMEM),
                  pl.BlockSpec((ATT_BLOCK, qw), cur),
                  kv(prev), kv(cur), kv(nxt), ctxs, kv(prev), kv(cur), kv(nxt), ctxs],
        out_specs=pl.BlockSpec((ATT_BLOCK, qw), cur),
        out_shape=jax.ShapeDtypeStruct((t, qw), BF16),
        compiler_params=_cparams(("arbitrary",)),
        name="attn",
    )(sink, qr, kr, kr, kr, kr, vr, vr, vr, vr)


def _head_rms_gate(o, z, g):
    outs = []
    for h in range(o.shape[1] // 128):
        seg = o[:, h * 128:(h + 1) * 128]
        nrm = seg * lax.rsqrt(jnp.mean(seg * seg, axis=-1, keepdims=True) + EPS) * g
        outs.append(nrm * _silu(z[:, h * 128:(h + 1) * 128]))
    return jnp.concatenate(outs, axis=1)


def _merge_kernel(oaf, oab, az, obf, obb, bz, yc, xs, g_ref, wgl, dng, glag, wpa, wpb, wpc, wo, mod_ref,
                  o_ref, *, tm):
    i = pl.program_id(0)
    h = _norm_mod(xs[...], g_ref[...], mod_ref, 0, i * tm).astype(BF16)
    gl = jnp.dot(h, wgl[...], preferred_element_type=F32)
    ya = _head_rms_gate(oaf[...] + oab[...], az[...], dng[...])
    yb = _head_rms_gate(obf[...] + obb[...], bz[...], glag[...])
    pa = _bdot(ya, wpa[...])
    pb = _bdot(yb, wpb[...])
    pc = jnp.dot(yc[...], wpc[...], preferred_element_type=F32)
    merged = (_sigmoid(gl[:, 0:D_MODEL]) * pa + _sigmoid(gl[:, D_MODEL:2 * D_MODEL]) * pb
              + _sigmoid(gl[:, 2 * D_MODEL:3 * D_MODEL]) * pc)
    y = _bdot(merged, wo[...])
    o_ref[...] = xs[...] + y * _res_gate(mod_ref, 0, tm, i * tm)


def _merge(oaf, oab, obf, obb, yc, proj, xs, g, wgl, dng, glag, wpa, wpb, wpc, wo, mod):
    t = xs.shape[0]
    tm = _pick(t, (320, BLK))
    row = lambda w, c=0: pl.BlockSpec((tm, w), lambda i: (i, c))
    full = lambda a: pl.BlockSpec(a.shape, lambda i: (0,) * a.ndim)
    return pl.pallas_call(
        functools.partial(_merge_kernel, tm=tm),
        grid=(t // tm,),
        in_specs=[row(A_WIDTH), row(A_WIDTH), row(A_WIDTH, COL_AZ // A_WIDTH),
                  row(B_VWIDTH), row(B_VWIDTH), row(B_VWIDTH, COL_BZ // B_VWIDTH),
                  row(H_C * HD_CP), row(D_MODEL), full(g), full(wgl),
                  full(dng), full(glag), full(wpa), full(wpb), full(wpc), full(wo), full(mod)],
        out_specs=row(D_MODEL),
        out_shape=jax.ShapeDtypeStruct((t, D_MODEL), F32),
        compiler_params=_cparams(("arbitrary",)),
        name="merge",
    )(oaf, oab, proj, obf, obb, proj, yc, xs, g, wgl, dng, glag, wpa, wpb, wpc, wo, mod)


def _router_kernel(x_ref, mod_ref, g_ref, rwt_ref, rb_ref, h_ref, key_ref, gk_ref, cnt_ref, *, tm):
    i = pl.program_id(0)
    h = _norm_mod(x_ref[...], g_ref[...], mod_ref, 3 * D_MODEL, i * tm)
    h_ref[...] = h
    logit = lax.dot_general(rwt_ref[...], h, (((1,), (1,)), ((), ())),
                            precision=lax.Precision.HIGHEST, preferred_element_type=F32) + rb_ref[...]
    erow = lax.broadcasted_iota(I32, (N_EXPERTS, tm), 0)
    vals, idxs, sels = [], [], []
    cur = logit
    for _ in range(TOP_K):
        m = jnp.max(cur, axis=0, keepdims=True)
        idx = jnp.min(jnp.where(cur == m, erow, N_EXPERTS), axis=0, keepdims=True)
        sel = erow == idx
        vals.append(m)
        idxs.append(idx)
        sels.append(sel)
        cur = jnp.where(sel, -jnp.inf, cur)
    ex = [jnp.exp(v - vals[0]) for v in vals]
    den = ex[0] + ex[1] + ex[2] + ex[3]
    onehot = jnp.where(sels[0] | sels[1] | sels[2] | sels[3], 1.0, 0.0).astype(F32)
    ss = lax.broadcasted_iota(I32, (tm, tm), 0)
    tt = lax.broadcasted_iota(I32, (tm, tm), 1)
    before = jnp.where(ss < tt, 1.0, 0.0).astype(BF16)
    cnt = jnp.dot(onehot.astype(BF16), before, preferred_element_type=F32)
    ranks = [jnp.sum(jnp.where(s, cnt, 0.0), axis=0, keepdims=True) for s in sels]
    keys = [idxs[k] * tm + ranks[k].astype(I32) for k in range(TOP_K)]
    key_ref[...] = jnp.concatenate(keys + [jnp.full((8 - TOP_K, tm), -1, I32)], axis=0)
    cols = jnp.concatenate([e / den for e in ex] + [k.astype(F32) for k in keys]
                           + [jnp.zeros((128 - 2 * TOP_K, tm), F32)], axis=0)
    gk_ref[...] = cols.T
    cnt_ref[...] = jnp.broadcast_to(jnp.sum(onehot, axis=1, keepdims=True), (N_EXPERTS, 128))


def _router(xs, mod, g, rwt, rb):
    t = xs.shape[0]
    tm = BLK
    nt = t // tm
    return pl.pallas_call(
        functools.partial(_router_kernel, tm=tm),
        grid=(nt,),
        in_specs=[pl.BlockSpec((tm, D_MODEL), lambda i: (i, 0)),
                  pl.BlockSpec((8, 6 * D_MODEL), lambda i: (0, 0)),
                  pl.BlockSpec((1, D_MODEL), lambda i: (0, 0)),
                  pl.BlockSpec((N_EXPERTS, D_MODEL), lambda i: (0, 0)),
                  pl.BlockSpec((N_EXPERTS, 1), lambda i: (0, 0))],
        out_specs=[pl.BlockSpec((tm, D_MODEL), lambda i: (i, 0)),
                   pl.BlockSpec((8, tm), lambda i: (0, i)),
                   pl.BlockSpec((tm, 128), lambda i: (i, 0)),
                   pl.BlockSpec((None, N_EXPERTS, 128), lambda i: (i, 0, 0))],
        out_shape=[jax.ShapeDtypeStruct((t, D_MODEL), F32),
                   jax.ShapeDtypeStruct((8, t), I32),
                   jax.ShapeDtypeStruct((t, 128), F32),
                   jax.ShapeDtypeStruct((nt, N_EXPERTS, 128), F32)],
        compiler_params=_cparams(("arbitrary",)),
        name="router",
    )(xs, mod, g, rwt, rb)


RUN_CAP = 64
RUN_SHIFT = RUN_CAP.bit_length() - 1
RUN_BITS = (8, 4, 2, 1)
RUN_WAIT_BITS = (N_EXPERTS * RUN_CAP // 8).bit_length()


def _run_pieces(n8, nmax):
    n = jnp.minimum(n8, nmax)
    return [((n & bit) != 0, 8 * bit, pl.multiple_of(8 * (n & ~(2 * bit - 1)), 8)) for bit in RUN_BITS]


def _start_runs(i, base_ref, n8_ref, src_of, dst_of, sem):
    cap8 = RUN_CAP // 8
    total8 = jnp.int32(0)
    for e in range(N_EXPERTS):
        b = pl.multiple_of(base_ref[i * N_EXPERTS + e], 8)
        n = jnp.minimum(n8_ref[i * N_EXPERTS + e], cap8)

        def make_case(j, e=e, b=b):
            def case():
                off = 0
                for bit in RUN_BITS:
                    if j & bit:
                        pltpu.make_async_copy(src_of(e, b, off, 8 * bit), dst_of(e, b, off, 8 * bit), sem).start()
                        off += 8 * bit
                return jnp.int32(0)
            return case

        lax.switch(n, [make_case(j) for j in range(cap8 + 1)])
        total8 = total8 + n
    return total8


def _wait_rows(total8, desc_of, nbits):
    for j in range(nbits):
        @pl.when(((total8 >> j) & 1) == 1)
        def _():
            desc_of(8 << j).wait()


def _dispatch_kernel(base_ref, n8_ref, long_ref, zs_ref, key_ref, h_ref, xs_ref, runs, ovf, zbuf, pend_ref, sem, osem, zsem,
                     *, tm, nzero, nsteps):
    i = pl.program_id(0)

    @pl.when(i == 0)
    def _():
        zbuf[...] = jnp.zeros_like(zbuf)
        for z in range(nzero):
            @pl.when(zs_ref[z] >= 0)
            def _():
                pltpu.make_async_copy(zbuf, xs_ref.at[pl.ds(pl.multiple_of(zs_ref[z], MOE_BM), MOE_BM), :],
                                      zsem).start()
        for z in range(nzero):
            @pl.when(zs_ref[z] >= 0)
            def _():
                pltpu.make_async_copy(zbuf, xs_ref.at[pl.ds(0, MOE_BM), :], zsem).wait()

    hb = h_ref[...].astype(BF16)
    keys = [key_ref[k:k + 1, :] for k in range(TOP_K)]
    nrow = N_EXPERTS * RUN_CAP
    rr = lax.broadcasted_iota(I32, (nrow, tm), 0)
    rowkey = (rr >> RUN_SHIFT) * tm + (rr & (RUN_CAP - 1))
    hit = (rowkey == keys[0]) | (rowkey == keys[1]) | (rowkey == keys[2]) | (rowkey == keys[3])
    slot = i % 2
    runs[slot] = jnp.dot(jnp.where(hit, 1.0, 0.0).astype(BF16), hb, preferred_element_type=F32)
    total8 = _start_runs(i, base_ref, n8_ref,
                         lambda e, b, off, size: runs.at[slot, pl.ds(e * RUN_CAP + off, size), :],
                         lambda e, b, off, size: xs_ref.at[pl.ds(b + off, size), :], sem.at[slot])
    pend_ref[slot] = total8

    def wait_slot(s):
        _wait_rows(pend_ref[s], lambda size: pltpu.make_async_copy(
            runs.at[s, pl.ds(0, size), :], xs_ref.at[pl.ds(0, size), :], sem.at[s]), RUN_WAIT_BITS)

    @pl.when(i > 0)
    def _():
        wait_slot(1 - slot)

    @pl.when(i == nsteps - 1)
    def _():
        wait_slot(slot)

    def ovf_body(e, carry):
        n8 = n8_ref[i * N_EXPERTS + e]
        b = pl.multiple_of(base_ref[i * N_EXPERTS + e], 8)
        for c in range(1, tm // RUN_CAP):
            @pl.when(n8 > c * (RUN_CAP // 8))
            def _():
                ck = lax.broadcasted_iota(I32, (RUN_CAP, tm), 0) + (e * tm + c * RUN_CAP)
                hit_c = (ck == keys[0]) | (ck == keys[1]) | (ck == keys[2]) | (ck == keys[3])
                ovf[...] = jnp.dot(jnp.where(hit_c, 1.0, 0.0).astype(BF16), hb, preferred_element_type=F32)
                for phase in (0, 1):
                    for live, size, off in _run_pieces(n8 - c * (RUN_CAP // 8), RUN_CAP // 8):
                        @pl.when(live)
                        def _():
                            cp = pltpu.make_async_copy(
                                ovf.at[pl.ds(off, size), :],
                                xs_ref.at[pl.ds(b + c * RUN_CAP + off, size), :], osem)
                            if phase == 0:
                                cp.start()
                            else:
                                cp.wait()
        return carry

    @pl.when(long_ref[i] > 0)
    def _():
        lax.fori_loop(0, N_EXPERTS, ovf_body, 0)


def _dispatch(plan, keys, h):
    base, n8, long_run, zstart, p_rows = plan
    t = h.shape[0]
    tm = BLK
    grid_spec = pltpu.PrefetchScalarGridSpec(
        num_scalar_prefetch=4,
        grid=(t // tm,),
        in_specs=[pl.BlockSpec((8, tm), lambda i, b, n, lg, z: (0, i)),
                  pl.BlockSpec((tm, D_MODEL), lambda i, b, n, lg, z: (i, 0))],
        out_specs=pl.BlockSpec(memory_space=pl.ANY),
        scratch_shapes=[pltpu.VMEM((2, N_EXPERTS * RUN_CAP, D_MODEL), F32), pltpu.VMEM((RUN_CAP, D_MODEL), F32),
                        pltpu.VMEM((MOE_BM, D_MODEL), F32), pltpu.SMEM((2,), I32),
                        pltpu.SemaphoreType.DMA((2,)), pltpu.SemaphoreType.DMA(()), pltpu.SemaphoreType.DMA(())],
    )
    return pl.pallas_call(
        functools.partial(_dispatch_kernel, tm=tm, nzero=zstart.shape[0], nsteps=t // tm),
        grid_spec=grid_spec,
        out_shape=jax.ShapeDtypeStruct((p_rows, D_MODEL), F32),
        compiler_params=_cparams(("arbitrary",)),
        name="moe_dispatch",
    )(base, n8, long_run, zstart, keys, h)


def _expert_kernel(be_ref, nused_ref, first_ref, slot_ref, nxt_ref, x_ref, wgu_hbm, bgu_ref, wdn_hbm, bdn_ref,
                   y_ref, gu_stage, dn_stage, wgu_bf, wdn_bf, sem, *, layer):
    b = pl.program_id(0)

    def fetch(e, s):
        return (pltpu.make_async_copy(wgu_hbm.at[layer, e], gu_stage.at[s], sem.at[0, s]),
                pltpu.make_async_copy(wdn_hbm.at[layer, e], dn_stage.at[s], sem.at[1, s]))

    @pl.when(b == 0)
    def _():
        for cp in fetch(be_ref[0], 0):
            cp.start()

    @pl.when(first_ref[b] == 1)
    def _():
        s = slot_ref[b]
        for cp in fetch(be_ref[b], s):
            cp.wait()
        wgu_bf[...] = gu_stage[s].astype(BF16)
        wdn_bf[...] = dn_stage[s].astype(BF16)

        @pl.when(nxt_ref[b] >= 0)
        def _():
            for cp in fetch(nxt_ref[b], 1 - s):
                cp.start()

    @pl.when(b < nused_ref[0])
    def _():
        gu = jnp.dot(x_ref[...].astype(BF16), wgu_bf[...], preferred_element_type=F32) + bgu_ref[...]
        g_ = jnp.minimum(gu[:, :D_FF], SWIGLU_LIMIT)
        u_ = jnp.clip(gu[:, D_FF:], -SWIGLU_LIMIT, SWIGLU_LIMIT)
        act = (u_ + 1.0) * (g_ * _sigmoid(g_ * SWIGLU_ALPHA))
        y_ref[...] = jnp.dot(act.astype(BF16), wdn_bf[...], preferred_element_type=F32) + bdn_ref[...]

    @pl.when(b >= nused_ref[0])
    def _():
        y_ref[...] = jnp.zeros_like(y_ref)


def _experts(plan, xsorted, w_gu, b_gu, w_dn, b_dn, layer):
    blk_e, nused, first, slot, nxt = plan
    p_rows = xsorted.shape[0]
    nblk = p_rows // MOE_BM
    depth = w_gu.shape[0]
    bsel = lambda b, be, nu, fi, sl, nx: (layer, be[b], 0, 0)
    grid_spec = pltpu.PrefetchScalarGridSpec(
        num_scalar_prefetch=5,
        grid=(nblk,),
        in_specs=[pl.BlockSpec((MOE_BM, D_MODEL), lambda b, be, nu, fi, sl, nx: (jnp.minimum(b, nu[0] - 1), 0)),
                  pl.BlockSpec(memory_space=pl.ANY),
                  pl.BlockSpec((None, None, 1, 2 * D_FF), bsel),
                  pl.BlockSpec(memory_space=pl.ANY),
                  pl.BlockSpec((None, None, 1, D_MODEL), bsel)],
        out_specs=pl.BlockSpec((MOE_BM, D_MODEL), lambda b, be, nu, fi, sl, nx: (b, 0)),
        scratch_shapes=[pltpu.VMEM((2, D_MODEL, 2 * D_FF), F32), pltpu.VMEM((2, D_FF, D_MODEL), F32),
                        pltpu.VMEM((D_MODEL, 2 * D_FF), BF16), pltpu.VMEM((D_FF, D_MODEL), BF16),
                        pltpu.SemaphoreType.DMA((2, 2))],
    )
    return pl.pallas_call(
        functools.partial(_expert_kernel, layer=layer),
        grid_spec=grid_spec,
        out_shape=jax.ShapeDtypeStruct((p_rows, D_MODEL), F32),
        compiler_params=_cparams(("arbitrary",)),
        name="moe_experts",
    )(blk_e, nused, first, slot, nxt, xsorted, w_gu, b_gu.reshape(depth, N_EXPERTS, 1, 2 * D_FF), w_dn,
      b_dn.reshape(depth, N_EXPERTS, 1, D_MODEL))


def _combine_kernel(base_ref, n8_ref, long_ref, y_ref, gk_ref, xs_ref, mod_ref, o_ref, runs, ovf, pend_ref, sem, osem,
                    *, tm, nsteps):
    i = pl.program_id(0)

    slot = i % 2

    def gather(tile, s):
        pend_ref[s] = _start_runs(tile, base_ref, n8_ref,
                                  lambda e, b, off, size: y_ref.at[pl.ds(b + off, size), :],
                                  lambda e, b, off, size: runs.at[s, pl.ds(e * RUN_CAP + off, size), :],
                                  sem.at[s])

    @pl.when(i == 0)
    def _():
        runs[...] = jnp.zeros_like(runs)
        ovf[...] = jnp.zeros_like(ovf)
        gather(0, 0)

    @pl.when(i + 1 < nsteps)
    def _():
        gather(i + 1, 1 - slot)

    _wait_rows(pend_ref[slot], lambda size: pltpu.make_async_copy(
        y_ref.at[pl.ds(0, size), :], runs.at[slot, pl.ds(0, size), :], sem.at[slot]), RUN_WAIT_BITS)
    gk = gk_ref[...]
    gates = [gk[:, k:k + 1] for k in range(TOP_K)]
    keys = [gk[:, TOP_K + k:TOP_K + k + 1].astype(I32) for k in range(TOP_K)]
    nrow = N_EXPERTS * RUN_CAP
    cc = lax.broadcasted_iota(I32, (1, nrow), 1)
    colkey = (cc >> RUN_SHIFT) * tm + (cc & (RUN_CAP - 1))
    w = jnp.zeros((tm, nrow), F32)
    for k in reversed(range(TOP_K)):
        w = jnp.where(colkey == keys[k], gates[k], w)
    acc = jnp.dot(w.astype(BF16), runs[slot].astype(BF16), preferred_element_type=F32)
    o_ref[...] = xs_ref[...] + acc * _res_gate(mod_ref, 3 * D_MODEL, tm, i * tm)

    def ovf_body(e, carry):
        n8 = n8_ref[i * N_EXPERTS + e]
        b = pl.multiple_of(base_ref[i * N_EXPERTS + e], 8)
        for c in range(1, tm // RUN_CAP):
            @pl.when(n8 > c * (RUN_CAP // 8))
            def _():
                for phase in (0, 1):
                    for live, size, off in _run_pieces(n8 - c * (RUN_CAP // 8), RUN_CAP // 8):
                        @pl.when(live)
                        def _():
                            cp = pltpu.make_async_copy(
                                y_ref.at[pl.ds(b + c * RUN_CAP + off, size), :],
                                ovf.at[pl.ds(off, size), :], osem)
                            if phase == 0:
                                cp.start()
                            else:
                                cp.wait()
                ck = lax.broadcasted_iota(I32, (1, RUN_CAP), 1) + (e * tm + c * RUN_CAP)
                wc = jnp.where(ck == keys[0], gates[0], 0.0)
                for k in range(1, TOP_K):
                    wc = wc + jnp.where(ck == keys[k], gates[k], 0.0)
                part = jnp.dot(wc.astype(BF16), ovf[...].astype(BF16), preferred_element_type=F32)
                o_ref[...] = o_ref[...] + part * _res_gate(mod_ref, 3 * D_MODEL, tm, i * tm)
        return carry

    @pl.when(long_ref[i] > 0)
    def _():
        lax.fori_loop(0, N_EXPERTS, ovf_body, 0)


def _combine(plan, y, gk_col, xs, mod):
    base, n8, long_run, _, _ = plan
    t = xs.shape[0]
    tm = BLK
    grid_spec = pltpu.PrefetchScalarGridSpec(
        num_scalar_prefetch=3,
        grid=(t // tm,),
        in_specs=[pl.BlockSpec(memory_space=pl.ANY),
                  pl.BlockSpec((tm, 128), lambda i, b, n, lg: (i, 0)),
                  pl.BlockSpec((tm, D_MODEL), lambda i, b, n, lg: (i, 0)),
                  pl.BlockSpec((8, 6 * D_MODEL), lambda i, b, n, lg: (0, 0))],
        out_specs=pl.BlockSpec((tm, D_MODEL), lambda i, b, n, lg: (i, 0)),
        scratch_shapes=[pltpu.VMEM((2, N_EXPERTS * RUN_CAP, D_MODEL), F32), pltpu.VMEM((RUN_CAP, D_MODEL), F32),
                        pltpu.SMEM((2,), I32), pltpu.SemaphoreType.DMA((2,)), pltpu.SemaphoreType.DMA(())],
    )
    return pl.pallas_call(
        functools.partial(_combine_kernel, tm=tm, nsteps=t // tm),
        grid_spec=grid_spec,
        out_shape=jax.ShapeDtypeStruct((t, D_MODEL), F32),
        compiler_params=_cparams(("arbitrary",)),
        name="moe_combine",
    )(base, n8, long_run, y, gk_col, xs, mod)


def _pad_heads(w, nh, hd, hdp):
    d = w.shape[0]
    return jnp.pad(w.reshape(d, nh, hd), ((0, 0), (0, 0), (0, hdp - hd))).reshape(d, nh * hdp)


def _layout_w_in(w):
    w_mix, w_gl = _layout_w_in_f32(w)
    return w_mix.astype(BF16), w_gl.astype(BF16)


def _layout_w_in_f32(w):
    pts = np.cumsum([A_WIDTH] * 4 + [2 * H_A, 2 * H_A, B_KWIDTH, B_KWIDTH, B_VWIDTH, B_VWIDTH,
                                     2 * GLA_RANK, C_WIDTH, KV_C * HD_C, KV_C * HD_C])
    (aq, ak, av, az, aa, ab, bq, bk, bv, bz, bg, cq, ck, cv, gl) = jnp.split(w, pts.tolist(), axis=1)
    small = jnp.concatenate([aa, ab, bg], axis=1)
    small = jnp.pad(small, ((0, 0), (0, 256 - small.shape[1])))
    cols = [aq, ak, av, az, _pad_heads(cq, H_C, HD_C, HD_CP), _pad_heads(ck, KV_C, HD_C, HD_CP),
            _pad_heads(cv, KV_C, HD_C, HD_CP), bq, bk, bv, bz, small]
    out = jnp.concatenate(cols, axis=1)
    assert out.shape[1] == IN_COLS_P
    return out, gl


def _rope_tables(t):
    s_len = t - CTX
    half = HD_C // 2
    inv_freq = ROPE_THETA ** (-jnp.arange(0, half, 2, dtype=F32) / half)
    pos = jnp.arange(s_len)
    rows = (pos // GRID_W).astype(F32)[:, None] * inv_freq[None, :]
    cols = (pos % GRID_W).astype(F32)[:, None] * inv_freq[None, :]
    cr, sr, cc, sc = jnp.cos(rows), jnp.sin(rows), jnp.cos(cols), jnp.sin(cols)
    zpad = jnp.zeros((s_len, HD_CP - HD_C), F32)
    cos_l = jnp.concatenate([cr, cr, cc, cc, zpad], axis=1)
    sin_l = jnp.concatenate([-sr, sr, -sc, sc, zpad], axis=1)
    cos_t = jnp.concatenate([jnp.ones((CTX, HD_CP), F32), cos_l], axis=0)
    sin_t = jnp.concatenate([jnp.zeros((CTX, HD_CP), F32), sin_l], axis=0)
    return cos_t, sin_t


def _lane_vec(v, width=128):
    v = v.reshape(1, -1).astype(F32)
    return jnp.pad(v, ((0, 0), (0, width - v.shape[1])))


def _moe_plan(cnt_tile, tk):
    nt = cnt_tile.shape[0]
    pad8 = (cnt_tile + 7) // 8 * 8
    total = jnp.sum(pad8, axis=0)
    padded = (total + MOE_BM - 1) // MOE_BM * MOE_BM
    pend = jnp.cumsum(padded)
    estart = pend - padded
    base = estart[None, :] + jnp.cumsum(pad8, axis=0) - pad8
    nblk = (tk + nt * N_EXPERTS * 7 + N_EXPERTS * (MOE_BM - 1) + MOE_BM - 1) // MOE_BM
    p_rows = nblk * MOE_BM
    blk = jnp.arange(nblk, dtype=I32)
    blk_e = jnp.minimum(jnp.sum((pend[None, :] <= (blk * MOE_BM)[:, None]).astype(I32), axis=1), N_EXPERTS - 1)
    nused = (pend[-1] // MOE_BM).astype(I32)
    prev_e = jnp.concatenate([jnp.full((1,), -1, I32), blk_e[:-1]])
    first = ((blk_e != prev_e) & (blk < nused)).astype(I32)
    slot = (jnp.cumsum(first) - 1) % 2
    pos = jnp.where(first == 1, blk, nblk)
    nxt_pos = jnp.concatenate([lax.cummin(pos, axis=0, reverse=True)[1:], jnp.full((1,), nblk, I32)])
    nxt = jnp.where(nxt_pos < nblk, blk_e[jnp.minimum(nxt_pos, nblk - 1)], -1)
    last_blk = jnp.where(total > 0, pend - MOE_BM, -1)
    ntail = (nt * N_EXPERTS * 7 + N_EXPERTS * (MOE_BM - 1)) // MOE_BM + 1
    tail = pend[-1] + jnp.arange(ntail, dtype=I32) * MOE_BM
    zstart = jnp.concatenate([last_blk, jnp.where(tail < p_rows, tail, -1)]).astype(I32)
    eplan = (blk_e, nused.reshape(1), first, slot.astype(I32), nxt.astype(I32))
    long_run = (jnp.max(pad8, axis=1) > RUN_CAP).astype(I32)
    mplan = (base.reshape(-1).astype(I32), (pad8 // 8).reshape(-1).astype(I32), long_run, zstart, p_rows)
    return mplan, eplan


def kernel(x, c, ctx, c_ctx, ada_w, ada_b, norm_mix_g, norm_ffn_g, w_in, dn_conv_w, dn_a_log, dn_dt_bias,
           dn_norm_g, gla_w2, gla_b2, gla_norm_g, attn_q_norm_g, attn_k_norm_g, attn_sink, w_branch_a,
           w_branch_b, w_branch_c, w_out, router_w, router_b, w_gate_up, b_gate_up, w_down, b_down):
    assert x.shape[0] == 1 and c.shape[0] == 1 and ctx.shape[1] == CTX
    depth = ada_w.shape[0]
    xs = jnp.concatenate([ctx[0], x[0]], axis=0)
    t = xs.shape[0]
    assert t % BLK == 0 and (t - CTX) % GRID_W == 0
    cc = jnp.concatenate([c, c_ctx[None, :], jnp.zeros((6, D_MODEL), F32)], axis=0)
    mods = _ada_mod(cc, ada_w, ada_b)
    cos_t, sin_t = _rope_tables(t)
    for l in range(depth):
        mod = mods[l]
        w_mix, w_gl = _layout_w_in(w_in[l])
        proj = _inproj(xs, mod, norm_mix_g[l][None, :], w_mix)
        qa, ka, va, gcol = _dnprep(proj, dn_conv_w[l], _lane_vec(dn_a_log[l]), _lane_vec(dn_dt_bias[l]))
        oaf, oab = _dnscan(qa, ka, va, gcol)
        w2 = gla_w2[l].astype(F32)
        w2full = jnp.zeros((128, 2 * B_KWIDTH), F32)
        for d in range(2):
            r0 = 4 * H_A + d * GLA_RANK
            w2full = w2full.at[r0:r0 + GLA_RANK, d * B_KWIDTH:(d + 1) * B_KWIDTH].set(w2[d])
        obf, obb = _glascan(proj, w2full, gla_b2[l].reshape(1, 2 * B_KWIDTH).astype(F32))
        qg = _lane_vec(attn_q_norm_g[l])
        kg = _lane_vec(attn_k_norm_g[l])
        qr, kr, vr = _attnprep(proj, qg, kg, cos_t, sin_t)
        yc = _attn(qr, kr, vr, attn_sink[l].astype(F32))
        wpc = jnp.pad(w_branch_c[l].reshape(H_C, HD_C, D_MODEL),
                      ((0, 0), (0, HD_CP - HD_C), (0, 0))).reshape(H_C * HD_CP, D_MODEL)
        xs = _merge(oaf, oab, obf, obb, yc, proj, xs, norm_mix_g[l][None, :], w_gl,
                    dn_norm_g[l][None, :], gla_norm_g[l][None, :],
                    w_branch_a[l].astype(BF16), w_branch_b[l].astype(BF16), wpc.astype(BF16),
                    w_out[l].astype(BF16), mod)
        h2, keys, gk_col, cnt = _router(xs, mod, norm_ffn_g[l][None, :], router_w[l].T, router_b[l][:, None])
        mplan, eplan = _moe_plan(cnt[:, :, 0].astype(I32), t * TOP_K)
        xsorted = _dispatch(mplan, keys, h2)
        y = _experts(eplan, xsorted, w_gate_up, b_gate_up, w_down, b_down, l)
        xs = _combine(mplan, y, gk_col, xs, mod)
    return xs[CTX:][None]
```

```python
import functools

import jax
import jax.numpy as jnp
import numpy as np
from jax import lax
from jax.experimental import pallas as pl
from jax.experimental.pallas import tpu as pltpu

F32 = jnp.float32
BF16 = jnp.bfloat16
I32 = jnp.int32

D_MODEL = 1024
DEPTH = 4
GRID_W = 64
CTX = 256
H_A = 4
HD_A = 128
A_WIDTH = H_A * HD_A
CONV_W = 5
H_B = 4
DK_B = 64
DV_B = 128
B_KWIDTH = H_B * DK_B
B_VWIDTH = H_B * DV_B
GLA_RANK = 16
GLA_NORMALIZER = 16.0
H_C = 8
KV_C = 2
REP_C = H_C // KV_C
HD_C = 64
HD_CP = 128
C_WIDTH = H_C * HD_C
ATT_BLOCK = 128
ROPE_THETA = 10000.0
CHUNK = 64
SUB = 16
N_EXPERTS = 32
TOP_K = 4
D_FF = 1024
SWIGLU_LIMIT = 7.0
SWIGLU_ALPHA = 1.702
EPS = 1e-6
NEG = -1e30
CHUNK_SHIFT = CHUNK.bit_length() - 1
SUB_SHIFT = SUB.bit_length() - 1
DK_SHIFT = DK_B.bit_length() - 1
DV_SHIFT = DV_B.bit_length() - 1

BLK = 256
MOE_BM = 512
VMEM_LIMIT = 56 * 1024 * 1024

COL_QKV = 0
COL_AZ = 1536
COL_CQ = 2048
COL_CKV = 3072
COL_BQK = 3584
COL_BV = 4096
COL_BZ = 4608
COL_SMALL = 5120
IN_COLS_P = 5376
IN_TN = 768


def _pick(n, cands):
    for c in cands:
        if n % c == 0:
            return c
    raise ValueError(f"no tile for {n}")


def _cparams(sem):
    return pltpu.CompilerParams(dimension_semantics=sem, vmem_limit_bytes=VMEM_LIMIT)


def _bdot(a, b):
    return jnp.dot(a.astype(BF16), b.astype(BF16), preferred_element_type=F32)


def _bdot_nt(a, b):
    return lax.dot_general(a.astype(BF16), b.astype(BF16), (((1,), (1,)), ((), ())),
                           preferred_element_type=F32)


def _bdot_tn(a, b):
    return lax.dot_general(a.astype(BF16), b.astype(BF16), (((0,), (0,)), ((), ())),
                           preferred_element_type=F32)


def _fdot(a, b):
    return jnp.dot(a, b, precision=lax.Precision.HIGHEST, preferred_element_type=F32)


def _split2(a):
    hi = a.astype(BF16)
    lo = (a - hi.astype(F32)).astype(BF16)
    return hi, lo


def _dot3(a, b):
    ah, al = _split2(a)
    bh, bl = _split2(b)
    d = functools.partial(jnp.dot, preferred_element_type=F32)
    return d(ah, bh) + (d(ah, bl) + d(al, bh))


def _cumsum_dot(tri, x):
    t = tri.astype(BF16)
    hi = x.astype(BF16)
    r1 = x - hi.astype(F32)
    mid = r1.astype(BF16)
    lo = (r1 - mid.astype(F32)).astype(BF16)
    d = functools.partial(jnp.dot, preferred_element_type=F32)
    return d(t, hi) + (d(t, mid) + d(t, lo))


def _sigmoid(x):
    return 1.0 / (1.0 + jnp.exp(-x))


def _silu(x):
    return x * _sigmoid(x)


def _softplus(x):
    return jnp.maximum(x, 0.0) + jnp.log(1.0 + jnp.exp(-jnp.abs(x)))


def _ada_kernel(cc_ref, w_ref, b_ref, o_ref):
    o_ref[...] = _fdot(_silu(cc_ref[...]), w_ref[...]) + b_ref[...]


def _ada_mod(cc, ada_w, ada_b):
    depth = ada_w.shape[0]
    tn = 1536
    return pl.pallas_call(
        _ada_kernel,
        grid=(depth, 6 * D_MODEL // tn),
        in_specs=[pl.BlockSpec((8, D_MODEL), lambda l, j: (0, 0)),
                  pl.BlockSpec((None, D_MODEL, tn), lambda l, j: (l, 0, j)),
                  pl.BlockSpec((None, 1, tn), lambda l, j: (l, 0, j))],
        out_specs=pl.BlockSpec((None, 8, tn), lambda l, j: (l, 0, j)),
        out_shape=jax.ShapeDtypeStruct((depth, 8, 6 * D_MODEL), F32),
        compiler_params=_cparams(("arbitrary", "arbitrary")),
        name="ada_mod",
    )(cc, ada_w, ada_b.reshape(depth, 1, 6 * D_MODEL))


def _norm_mod(x, g, mod_ref, moff, row0):
    tm = x.shape[0]
    y = x * lax.rsqrt(jnp.mean(x * x, axis=-1, keepdims=True) + EPS) * g
    isc = (row0 + lax.broadcasted_iota(I32, (tm, 1), 0)) < CTX
    shift = jnp.where(isc, mod_ref[1:2, moff:moff + D_MODEL], mod_ref[0:1, moff:moff + D_MODEL])
    scale = jnp.where(isc, mod_ref[1:2, moff + D_MODEL:moff + 2 * D_MODEL],
                      mod_ref[0:1, moff + D_MODEL:moff + 2 * D_MODEL])
    return y * (1.0 + scale) + shift


def _res_gate(mod_ref, moff, tm, row0):
    isc = (row0 + lax.broadcasted_iota(I32, (tm, 1), 0)) < CTX
    return jnp.where(isc, mod_ref[1:2, moff + 2 * D_MODEL:moff + 3 * D_MODEL],
                     mod_ref[0:1, moff + 2 * D_MODEL:moff + 3 * D_MODEL])


def _inproj_kernel(x_ref, mod_ref, g_ref, w_ref, o_ref, h_ref, *, tm):
    i = pl.program_id(0)

    @pl.when(pl.program_id(1) == 0)
    def _():
        h_ref[...] = _norm_mod(x_ref[...], g_ref[...], mod_ref, 0, i * tm).astype(BF16)

    o_ref[...] = jnp.dot(h_ref[...], w_ref[...], preferred_element_type=F32)


def _inproj(xs, mod, g, w):
    t = xs.shape[0]
    tm = _pick(t, (1664, 1280, 640, 256))
    return pl.pallas_call(
        functools.partial(_inproj_kernel, tm=tm),
        grid=(t // tm, IN_COLS_P // IN_TN),
        in_specs=[pl.BlockSpec((tm, D_MODEL), lambda i, j: (i, 0)),
                  pl.BlockSpec((8, 6 * D_MODEL), lambda i, j: (0, 0)),
                  pl.BlockSpec((1, D_MODEL), lambda i, j: (0, 0)),
                  pl.BlockSpec((D_MODEL, IN_TN), lambda i, j: (0, j))],
        out_specs=pl.BlockSpec((tm, IN_TN), lambda i, j: (i, j)),
        out_shape=jax.ShapeDtypeStruct((t, IN_COLS_P), F32),
        scratch_shapes=[pltpu.VMEM((tm, D_MODEL), BF16)],
        compiler_params=_cparams(("arbitrary", "arbitrary")),
        name="inproj",
    )(xs, mod, g, w)


def _tri_blockdiag(n, lower):
    ii = lax.broadcasted_iota(I32, (n, n), 0)
    jj = lax.broadcasted_iota(I32, (n, n), 1)
    same = (ii >> CHUNK_SHIFT) == (jj >> CHUNK_SHIFT)
    tri = (ii >= jj) if lower else (ii <= jj)
    return jnp.where(same, jnp.where(tri, 1.0, 0.0), 0.0).astype(F32)


def _dnprep_kernel(main_ref, prev_ref, next_ref, small_ref, cw_ref, alog_ref, dtb_ref,
                   q_ref, k_ref, v_ref, gcol_ref, ext_ref, *, nb):
    i = pl.program_id(0)
    use_prev = i >= 2
    use_next = jnp.logical_and(i >= 1, i <= nb - 2)
    ext_ref[0:8, :] = jnp.where(use_prev, prev_ref[...], 0.0)
    ext_ref[8:8 + BLK, :] = main_ref[...]
    ext_ref[8 + BLK:16 + BLK, :] = jnp.where(use_next, next_ref[...], 0.0)
    acc = ext_ref[6:6 + BLK, :] * cw_ref[0:1, :]
    for d in range(1, CONV_W):
        acc = acc + ext_ref[6 + d:6 + d + BLK, :] * cw_ref[d:d + 1, :]
    s = _silu(acc)
    for h in range(H_A):
        for part, ref, mul in ((0, q_ref, HD_A ** -0.5), (1, k_ref, 1.0)):
            seg = s[:, part * A_WIDTH + h * HD_A: part * A_WIDTH + (h + 1) * HD_A]
            nrm = seg * lax.rsqrt(jnp.sum(seg * seg, axis=-1, keepdims=True) + EPS)
            ref[:, h * HD_A:(h + 1) * HD_A] = nrm * mul
    v_ref[...] = s[:, 2 * A_WIDTH:3 * A_WIDTH]
    sm = small_ref[...]
    lane = lax.broadcasted_iota(I32, sm.shape, 1)
    g = -jnp.exp(alog_ref[...]) * _softplus(sm + dtb_ref[...])
    gb = jnp.where(lane < 2 * H_A, g, jnp.where(lane < 4 * H_A, _sigmoid(sm), 0.0))
    cf = _fdot(_tri_blockdiag(BLK, True), gb)
    cr = _fdot(_tri_blockdiag(BLK, False), gb)
    gc = jnp.where(lane < H_A, cf, jnp.where(lane < 2 * H_A, cr, gb))
    gcol_ref[...] = gc


def _dnprep(proj, conv_w, alog_vec, dtb_vec):
    t = proj.shape[0]
    nb = t // BLK
    qkv_blk = COL_QKV // (3 * A_WIDTH)
    last8 = t // 8 - 1
    out_sds = jax.ShapeDtypeStruct((t, A_WIDTH), F32)
    return pl.pallas_call(
        functools.partial(_dnprep_kernel, nb=nb),
        grid=(nb,),
        in_specs=[pl.BlockSpec((BLK, 3 * A_WIDTH), lambda i: (i, qkv_blk)),
                  pl.BlockSpec((8, 3 * A_WIDTH), lambda i: (jnp.maximum(i * (BLK // 8) - 1, 0), qkv_blk)),
                  pl.BlockSpec((8, 3 * A_WIDTH), lambda i: (jnp.minimum((i + 1) * (BLK // 8), last8), qkv_blk)),
                  pl.BlockSpec((BLK, 128), lambda i: (i, COL_SMALL // 128)),
                  pl.BlockSpec((CONV_W, 3 * A_WIDTH), lambda i: (0, 0)),
                  pl.BlockSpec((1, 128), lambda i: (0, 0)),
                  pl.BlockSpec((1, 128), lambda i: (0, 0))],
        out_specs=[pl.BlockSpec((BLK, A_WIDTH), lambda i: (i, 0)),
                   pl.BlockSpec((BLK, A_WIDTH), lambda i: (i, 0)),
                   pl.BlockSpec((BLK, A_WIDTH), lambda i: (i, 0)),
                   pl.BlockSpec((BLK, 128), lambda i: (i, 0))],
        out_shape=[out_sds, out_sds, out_sds,
                   jax.ShapeDtypeStruct((t, 128), F32)],
        scratch_shapes=[pltpu.VMEM((BLK + 16, 3 * A_WIDTH), F32)],
        compiler_params=_cparams(("arbitrary",)),
        name="dn_prep",
    )(proj, proj, proj, proj, conv_w, alog_vec, dtb_vec)


def _dot3_all(a_list, b_list):
    d = functools.partial(jnp.dot, preferred_element_type=F32)
    sa = [_split2(a) for a in a_list]
    sb = [_split2(b) for b in b_list]
    hh = [d(a[0], b[0]) for a, b in zip(sa, sb)]
    hl = [d(a[0], b[1]) for a, b in zip(sa, sb)]
    lh = [d(a[1], b[0]) for a, b in zip(sa, sb)]
    return [x + (y + z) for x, y, z in zip(hh, hl, lh)]


def _unit_tri_inverse_all(l_mats, masks):
    eye, m_diag, m_l1, m_l2 = masks
    ld = [l * m_diag for l in l_mats]
    x = [eye - a for a in ld]
    p = _dot3_all(ld, ld)
    for it in range(3):
        xp = _dot3_all(x, p)
        if it < 2:
            p = _dot3_all(p, p)
        x = [a + b for a, b in zip(x, xp)]
    for m in (m_l1, m_l2):
        cx = _dot3_all([l * m for l in l_mats], x)
        xcx = _dot3_all(x, cx)
        x = [a - b for a, b in zip(x, xcx)]
    return x


def _dn_masks():
    ii = lax.broadcasted_iota(I32, (CHUNK, CHUNK), 0)
    jj = lax.broadcasted_iota(I32, (CHUNK, CHUNK), 1)
    one = lambda c: jnp.where(c, 1.0, 0.0).astype(F32)
    eye = one(ii == jj)
    m_diag = one((ii >> SUB_SHIFT) == (jj >> SUB_SHIFT))
    m_l2 = one((ii >> (SUB_SHIFT + 1)) != (jj >> (SUB_SHIFT + 1)))
    m_l1 = 1.0 - m_diag - m_l2
    return ii, jj, (eye, m_diag, m_l1, m_l2)


def _dn_local(items, ii, jj, masks):
    n = len(items)
    dec, lmat, qk, kb, eg = [], [], [], [], []
    for q, k, v, gcol, grow, bcol, fwd in items:
        incl = (ii >= jj) if fwd else (ii <= jj)
        dec.append(jnp.exp(jnp.where(incl, gcol - grow, NEG)))
        kb.append(k * bcol)
        eg.append(jnp.exp(gcol))
    kh = [it[1].astype(BF16) for it in items]
    kk = [_bdot_nt(kb[i], kh[i]) for i in range(n)]
    qkr = [_bdot_nt(items[i][0], kh[i]) for i in range(n)]
    for i in range(n):
        fwd = items[i][6]
        strict = (ii > jj) if fwd else (ii < jj)
        lmat.append(kk[i] * jnp.where(strict, dec[i], 0.0))
        qk.append((qkr[i] * dec[i]).astype(BF16))
    rhs = [jnp.concatenate([items[i][2] * items[i][5], kb[i] * eg[i]], axis=1) for i in range(n)]
    sol = _dot3_all(_unit_tri_inverse_all(lmat, masks), rhs)
    out = []
    for i in range(n):
        q, k, _, gcol, _, _, fwd = items[i]
        glast = gcol[CHUNK - 1:CHUNK, :] if fwd else gcol[0:1, :]
        out.append((sol[i][:, :HD_A], sol[i][:, HD_A:].astype(BF16), qk[i], (q * eg[i]).astype(BF16),
                    (k * jnp.exp(glast - gcol)).astype(BF16), jnp.exp(glast)))
    return out


def _dn_step(local, states):
    n = len(local)
    sb = [s.astype(BF16) for s in states]
    d = functools.partial(jnp.dot, preferred_element_type=F32)
    ws = [d(local[i][1], sb[i]) for i in range(n)]
    qs = [d(local[i][3], sb[i]) for i in range(n)]
    v_new = [(local[i][0] - ws[i]).astype(BF16) for i in range(n)]
    o2 = [d(local[i][2], v_new[i]) for i in range(n)]
    kv = [lax.dot_general(local[i][4], v_new[i], (((0,), (0,)), ((), ())), preferred_element_type=F32)
          for i in range(n)]
    return [qs[i] + o2[i] for i in range(n)], [states[i] * local[i][5] + kv[i] for i in range(n)]


def _dnscan_kernel(qf, kf, vf, gcf, qb, kb, vb, gcb, of_ref, ob_ref, s_ref):
    @pl.when(pl.program_id(0) == 0)
    def _():
        s_ref[...] = jnp.zeros_like(s_ref)

    ii, jj, masks = _dn_masks()
    nch = BLK // CHUNK
    pick = jnp.where(lax.broadcasted_iota(I32, (16, 128), 0) == lax.broadcasted_iota(I32, (16, 128), 1),
                     1.0, 0.0).astype(F32)

    dirs = ((True, (qf, kf, vf, gcf, of_ref)), (False, (qb, kb, vb, gcb, ob_ref)))
    items, sinks = [], []
    for step in range(nch):
        for fwd, (q_r, k_r, v_r, gc_r, o_r) in dirs:
            c = step if fwd else nch - 1 - step
            rows = slice(c * CHUNK, (c + 1) * CHUNK)
            d = 0 if fwd else 1
            gct = gc_r[rows, :]
            grows = lax.dot_general(pick, gct, (((1,), (1,)), ((), ())),
                                    precision=lax.Precision.HIGHEST, preferred_element_type=F32)
            for h in range(H_A):
                lanes = slice(h * HD_A, (h + 1) * HD_A)
                gi = d * H_A + h
                items.append((q_r[rows, lanes], k_r[rows, lanes], v_r[rows, lanes],
                              gct[:, gi:gi + 1], grows[gi:gi + 1, :],
                              gct[:, 2 * H_A + gi:2 * H_A + gi + 1], fwd))
                sinks.append((o_r, rows, lanes))
    local = _dn_local(items, ii, jj, masks)
    nchain = 2 * H_A
    states = [s_ref[gi] for gi in range(nchain)]
    for step in range(nch):
        outs, states = _dn_step(local[step * nchain:(step + 1) * nchain], states)
        for (o_r, rows, lanes), o in zip(sinks[step * nchain:(step + 1) * nchain], outs):
            o_r[rows, lanes] = o
    for gi in range(nchain):
        s_ref[gi] = states[gi]


def _rev_block(nb):
    return lambda i: jnp.where(i == 0, 0, nb - i)


def _dnscan(q, k, v, gcol):
    t = q.shape[0]
    nb = t // BLK
    rev = _rev_block(nb)
    wide = lambda f: pl.BlockSpec((BLK, A_WIDTH), lambda i: (f(i), 0))
    col = lambda f: pl.BlockSpec((BLK, 128), lambda i: (f(i), 0))
    ident = lambda i: i
    out_sds = jax.ShapeDtypeStruct((t, A_WIDTH), F32)
    return pl.pallas_call(
        _dnscan_kernel,
        grid=(nb,),
        in_specs=[wide(ident), wide(ident), wide(ident), col(ident),
                  wide(rev), wide(rev), wide(rev), col(rev)],
        out_specs=[wide(ident), wide(rev)],
        out_shape=[out_sds, out_sds],
        scratch_shapes=[pltpu.VMEM((2 * H_A, HD_A, HD_A), F32)],
        compiler_params=_cparams(("arbitrary",)),
        name="dn_scan",
    )(q, k, v, gcol, q, k, v, gcol)


def _gla_gates(small_ref, w2_ref, b2_ref, b_ref, d, fwd):
    cols = slice(d * B_KWIDTH, (d + 1) * B_KWIDTH)
    pre = _dot3(small_ref[...], w2_ref[:, cols]) + b2_ref[:, cols]
    gk = -_softplus(-pre) * (1.0 / GLA_NORMALIZER)
    b_ref[d] = _cumsum_dot(_tri_blockdiag(BLK, fwd), gk)


def _gla_chunk(qk_ref, v_ref, b_ref, o_ref, st_ref, d, fwd, consts, step):
    sel, headmask_k, st_mask = consts
    nch = BLK // CHUNK
    nsub = CHUNK // SUB
    sub_i = lax.broadcasted_iota(I32, (SUB, 1), 0)
    row_c = lax.broadcasted_iota(I32, (CHUNK, 1), 0)
    if True:
        c = step if fwd else nch - 1 - step
        rows = pl.ds(pl.multiple_of(c * CHUNK, CHUNK), CHUNK)
        q = qk_ref[rows, 0:B_KWIDTH] * (DK_B ** -0.5)
        k = qk_ref[rows, B_KWIDTH:2 * B_KWIDTH]
        v = v_ref[rows, :]
        b = b_ref[d, rows, :]
        vh = v.astype(BF16)
        st = st_ref[d]
        o = _bdot_nt(q * jnp.exp(b), st)
        refs = []
        for sb in range(nsub):
            if fwd:
                r = b[sb * SUB - 1:sb * SUB, :] if sb > 0 else jnp.zeros((1, B_KWIDTH), F32)
            else:
                r = b[(sb + 1) * SUB:(sb + 1) * SUB + 1, :] if sb < nsub - 1 else jnp.zeros((1, B_KWIDTH), F32)
            refs.append(r)
        rfull = jnp.concatenate([jnp.broadcast_to(r, (SUB, B_KWIDTH)) for r in refs], axis=0)
        qs = q * jnp.exp(b - rfull)
        a_off = [None] * H_B
        for sb in (range(1, nsub) if fwd else range(0, nsub - 1)):
            jmask = (row_c < sb * SUB) if fwd else (row_c >= (sb + 1) * SUB)
            ks = (k * jnp.exp(jnp.where(jmask, refs[sb] - b, NEG))).astype(BF16)
            rowmask = jnp.where((row_c >> SUB_SHIFT) == sb, 1.0, 0.0)
            for h in range(H_B):
                a = _bdot_nt(qs * headmask_k[h], ks) * rowmask
                a_off[h] = a if a_off[h] is None else a_off[h] + a
        o = o + jnp.concatenate(
            [_bdot(a_off[h], vh[:, h * DV_B:(h + 1) * DV_B]) for h in range(H_B)], axis=1)
        diag = []
        for sb in range(nsub):
            s0 = sb * SUB
            bs, qsb, ksb = b[s0:s0 + SUB, :], q[s0:s0 + SUB, :], k[s0:s0 + SUB, :]
            tiles = []
            for jl in range(SUB):
                causal = (sub_i >= jl) if fwd else (sub_i <= jl)
                e = jnp.exp(jnp.where(causal, bs - bs[jl:jl + 1, :], NEG))
                tiles.append((qsb * ksb[jl:jl + 1, :] * e).astype(BF16))
            red = jnp.dot(jnp.concatenate(tiles, axis=0), sel, preferred_element_type=F32)
            acc = red[0:SUB, :] * v[s0:s0 + 1, :]
            for jl in range(1, SUB):
                acc = acc + red[jl * SUB:(jl + 1) * SUB, :] * v[s0 + jl:s0 + jl + 1, :]
            diag.append(acc)
        o_ref[rows, :] = o + jnp.concatenate(diag, axis=0)
        blast = b[CHUNK - 1:CHUNK, :] if fwd else b[0:1, :]
        kd = k * jnp.exp(blast - b)
        st_ref[d] = st * jnp.exp(blast) + _bdot_tn(v, kd) * st_mask


def _glascan_kernel(qkf, vf, smf, qkb, vb, smb, w2_ref, b2_ref, of_ref, ob_ref, st_ref, b_ref):
    @pl.when(pl.program_id(0) == 0)
    def _():
        st_ref[...] = jnp.zeros_like(st_ref)

    kk = lax.broadcasted_iota(I32, (B_KWIDTH, B_VWIDTH), 0)
    cc = lax.broadcasted_iota(I32, (B_KWIDTH, B_VWIDTH), 1)
    sel = jnp.where((kk >> DK_SHIFT) == (cc >> DV_SHIFT), 1.0, 0.0).astype(BF16)
    lane = lax.broadcasted_iota(I32, (1, B_KWIDTH), 1)
    headmask_k = [jnp.where((lane >> DK_SHIFT) == h, 1.0, 0.0).astype(F32) for h in range(H_B)]
    rr = lax.broadcasted_iota(I32, (B_VWIDTH, B_KWIDTH), 0)
    kc = lax.broadcasted_iota(I32, (B_VWIDTH, B_KWIDTH), 1)
    st_mask = jnp.where((rr >> DV_SHIFT) == (kc >> DK_SHIFT), 1.0, 0.0).astype(F32)
    consts = (sel, headmask_k, st_mask)
    _gla_gates(smf, w2_ref, b2_ref, b_ref, 0, True)
    _gla_gates(smb, w2_ref, b2_ref, b_ref, 1, False)

    def body(step, carry):
        _gla_chunk(qkf, vf, b_ref, of_ref, st_ref, 0, True, consts, step)
        _gla_chunk(qkb, vb, b_ref, ob_ref, st_ref, 1, False, consts, step)
        return carry

    lax.fori_loop(0, BLK // CHUNK, body, 0)


def _glascan(proj, w2full, b2full):
    t = proj.shape[0]
    nb = t // BLK
    rev = _rev_block(nb)
    ident = lambda i: i
    qk = lambda f: pl.BlockSpec((BLK, 2 * B_KWIDTH), lambda i: (f(i), COL_BQK // (2 * B_KWIDTH)))
    vv = lambda f: pl.BlockSpec((BLK, B_VWIDTH), lambda i: (f(i), COL_BV // B_VWIDTH))
    sm = lambda f: pl.BlockSpec((BLK, 128), lambda i: (f(i), COL_SMALL // 128))
    outs = lambda f: pl.BlockSpec((BLK, B_VWIDTH), lambda i: (f(i), 0))
    out_sds = jax.ShapeDtypeStruct((t, B_VWIDTH), F32)
    return pl.pallas_call(
        _glascan_kernel,
        grid=(nb,),
        in_specs=[qk(ident), vv(ident), sm(ident), qk(rev), vv(rev), sm(rev),
                  pl.BlockSpec((128, 2 * B_KWIDTH), lambda i: (0, 0)),
                  pl.BlockSpec((1, 2 * B_KWIDTH), lambda i: (0, 0))],
        out_specs=[outs(ident), outs(rev)],
        out_shape=[out_sds, out_sds],
        scratch_shapes=[pltpu.VMEM((2, B_VWIDTH, B_KWIDTH), F32), pltpu.VMEM((2, BLK, B_KWIDTH), F32)],
        compiler_params=_cparams(("arbitrary",)),
        name="gla_scan",
    )(proj, proj, proj, proj, proj, proj, w2full, b2full)


def _attnprep_kernel(cq_ref, ckv_ref, qg_ref, kg_ref, cos_ref, sin_ref, q_ref, k_ref, v_ref):
    cos = cos_ref[...]
    sin = sin_ref[...]
    lane = lax.broadcasted_iota(I32, cos.shape, 1)
    first = (lane % 32) < 16

    def norm_rope(x, g):
        y = x * lax.rsqrt(jnp.sum(x * x, axis=-1, keepdims=True) * (1.0 / HD_C) + EPS) * g
        partner = jnp.where(first, pltpu.roll(y, HD_CP - 16, 1), pltpu.roll(y, 16, 1))
        return y * cos + partner * sin

    for h in range(H_C):
        seg = slice(h * HD_CP, (h + 1) * HD_CP)
        q_ref[:, seg] = (norm_rope(cq_ref[:, seg], qg_ref[...]) * (HD_C ** -0.5)).astype(BF16)
    for g in range(KV_C):
        seg = slice(g * HD_CP, (g + 1) * HD_CP)
        k_ref[:, seg] = norm_rope(ckv_ref[:, seg], kg_ref[...]).astype(BF16)
    v_ref[...] = ckv_ref[:, KV_C * HD_CP:2 * KV_C * HD_CP].astype(BF16)


def _attnprep(proj, qg, kg, cos_t, sin_t):
    t = proj.shape[0]
    tm = _pick(t, (640, BLK))
    qw, kw = H_C * HD_CP, KV_C * HD_CP
    return pl.pallas_call(
        _attnprep_kernel,
        grid=(t // tm,),
        in_specs=[pl.BlockSpec((tm, qw), lambda i: (i, COL_CQ // qw)),
                  pl.BlockSpec((tm, 2 * kw), lambda i: (i, COL_CKV // (2 * kw))),
                  pl.BlockSpec((1, HD_CP), lambda i: (0, 0)),
                  pl.BlockSpec((1, HD_CP), lambda i: (0, 0)),
                  pl.BlockSpec((tm, HD_CP), lambda i: (i, 0)),
                  pl.BlockSpec((tm, HD_CP), lambda i: (i, 0))],
        out_specs=[pl.BlockSpec((tm, qw), lambda i: (i, 0)),
                   pl.BlockSpec((tm, kw), lambda i: (i, 0)),
                   pl.BlockSpec((tm, kw), lambda i: (i, 0))],
        out_shape=[jax.ShapeDtypeStruct((t, qw), BF16),
                   jax.ShapeDtypeStruct((t, kw), BF16),
                   jax.ShapeDtypeStruct((t, kw), BF16)],
        compiler_params=_cparams(("arbitrary",)),
        name="attn_prep",
    )(proj, proj, qg, kg, cos_t, sin_t)


def _attn_kernel(sink_ref, q_ref, kp_ref, kc_ref, kn_ref, kx_ref, vp_ref, vc_ref, vn_ref, vx_ref,
                 o_ref, *, nq):
    qi = pl.program_id(0)
    nctx = CTX // ATT_BLOCK
    latent = qi >= nctx
    ql = lax.broadcasted_iota(I32, (ATT_BLOCK, ATT_BLOCK), 0)
    kl = lax.broadcasted_iota(I32, (ATT_BLOCK, ATT_BLOCK), 1)
    ok_prev = jnp.logical_and(qi - 1 >= nctx, kl >= ql)
    ok_next = jnp.logical_and(jnp.logical_and(latent, qi + 1 <= nq - 1), kl <= ql)
    ok_cur = jnp.logical_and(latent, kl >= 0)
    bias = jnp.concatenate([jnp.where(ok_prev, 0.0, NEG), jnp.where(ok_cur, 0.0, NEG),
                            jnp.where(ok_next, 0.0, NEG),
                            jnp.zeros((ATT_BLOCK, CTX), F32)], axis=1)
    for g in range(KV_C):
        seg = slice(g * HD_CP, (g + 1) * HD_CP)
        kcat = jnp.concatenate([kp_ref[:, seg], kc_ref[:, seg], kn_ref[:, seg], kx_ref[:, seg]], axis=0)
        vcat = jnp.concatenate([vp_ref[:, seg], vc_ref[:, seg], vn_ref[:, seg], vx_ref[:, seg]], axis=0)
        heads = [g * REP_C + r for r in range(REP_C)]
        cols = [slice(h * HD_CP, (h + 1) * HD_CP) for h in heads]
        s = [lax.dot_general(q_ref[:, c], kcat, (((1,), (1,)), ((), ())), preferred_element_type=F32) + bias
             for c in cols]
        m = [jnp.maximum(jnp.max(s[i], axis=-1, keepdims=True), sink_ref[heads[i]]) for i in range(REP_C)]
        p = [jnp.exp(s[i] - m[i]) for i in range(REP_C)]
        den = [jnp.sum(p[i], axis=-1, keepdims=True) + jnp.exp(sink_ref[heads[i]] - m[i]) for i in range(REP_C)]
        o = [jnp.dot(p[i].astype(BF16), vcat, preferred_element_type=F32) for i in range(REP_C)]
        for i in range(REP_C):
            o_ref[:, cols[i]] = (o[i] / den[i]).astype(BF16)


def _attn(qr, kr, vr, sink):
    t = qr.shape[0]
    nq = t // ATT_BLOCK
    nctx = CTX // ATT_BLOCK
    qw, kw = H_C * HD_CP, KV_C * HD_CP
    prev = lambda i: (jnp.maximum(i - 1, nctx), 0)
    cur = lambda i: (i, 0)
    nxt = lambda i: (jnp.minimum(jnp.maximum(i + 1, nctx), nq - 1), 0)
    kv = lambda f: pl.BlockSpec((ATT_BLOCK, kw), f)
    ctxs = pl.BlockSpec((CTX, kw), lambda i: (0, 0))
    return pl.pallas_call(
        functools.partial(_attn_kernel, nq=nq),
        grid=(nq,),
        in_specs=[pl.BlockSpec(memory_space=pltpu.SMEM),
                  pl.BlockSpec((ATT_BLOCK, qw), cur),
                  kv(prev), kv(cur), kv(nxt), ctxs, kv(prev), kv(cur), kv(nxt), ctxs],
        out_specs=pl.BlockSpec((ATT_BLOCK, qw), cur),
        out_shape=jax.ShapeDtypeStruct((t, qw), BF16),
        compiler_params=_cparams(("arbitrary",)),
        name="attn",
    )(sink, qr, kr, kr, kr, kr, vr, vr, vr, vr)


def _head_rms_gate(o, z, g):
    outs = []
    for h in range(o.shape[1] // 128):
        seg = o[:, h * 128:(h + 1) * 128]
        nrm = seg * lax.rsqrt(jnp.mean(seg * seg, axis=-1, keepdims=True) + EPS) * g
        outs.append(nrm * _silu(z[:, h * 128:(h + 1) * 128]))
    return jnp.concatenate(outs, axis=1)


def _merge_kernel(oaf, oab, az, obf, obb, bz, yc, xs, g_ref, wgl, dng, glag, wpa, wpb, wpc, wo, mod_ref,
                  o_ref, *, tm):
    i = pl.program_id(0)
    h = _norm_mod(xs[...], g_ref[...], mod_ref, 0, i * tm).astype(BF16)
    gl = jnp.dot(h, wgl[...], preferred_element_type=F32)
    ya = _head_rms_gate(oaf[...] + oab[...], az[...], dng[...])
    yb = _head_rms_gate(obf[...] + obb[...], bz[...], glag[...])
    pa = _bdot(ya, wpa[...])
    pb = _bdot(yb, wpb[...])
    pc = jnp.dot(yc[...], wpc[...], preferred_element_type=F32)
    merged = (_sigmoid(gl[:, 0:D_MODEL]) * pa + _sigmoid(gl[:, D_MODEL:2 * D_MODEL]) * pb
              + _sigmoid(gl[:, 2 * D_MODEL:3 * D_MODEL]) * pc)
    y = _bdot(merged, wo[...])
    o_ref[...] = xs[...] + y * _res_gate(mod_ref, 0, tm, i * tm)


def _merge(oaf, oab, obf, obb, yc, proj, xs, g, wgl, dng, glag, wpa, wpb, wpc, wo, mod):
    t = xs.shape[0]
    tm = _pick(t, (320, BLK))
    row = lambda w, c=0: pl.BlockSpec((tm, w), lambda i: (i, c))
    full = lambda a: pl.BlockSpec(a.shape, lambda i: (0,) * a.ndim)
    return pl.pallas_call(
        functools.partial(_merge_kernel, tm=tm),
        grid=(t // tm,),
        in_specs=[row(A_WIDTH), row(A_WIDTH), row(A_WIDTH, COL_AZ // A_WIDTH),
                  row(B_VWIDTH), row(B_VWIDTH), row(B_VWIDTH, COL_BZ // B_VWIDTH),
                  row(H_C * HD_CP), row(D_MODEL), full(g), full(wgl),
                  full(dng), full(glag), full(wpa), full(wpb), full(wpc), full(wo), full(mod)],
        out_specs=row(D_MODEL),
        out_shape=jax.ShapeDtypeStruct((t, D_MODEL), F32),
        compiler_params=_cparams(("arbitrary",)),
        name="merge",
    )(oaf, oab, proj, obf, obb, proj, yc, xs, g, wgl, dng, glag, wpa, wpb, wpc, wo, mod)


def _router_kernel(x_ref, mod_ref, g_ref, rwt_ref, rb_ref, h_ref, key_ref, gk_ref, cnt_ref, *, tm):
    i = pl.program_id(0)
    h = _norm_mod(x_ref[...], g_ref[...], mod_ref, 3 * D_MODEL, i * tm)
    h_ref[...] = h
    logit = lax.dot_general(rwt_ref[...], h, (((1,), (1,)), ((), ())),
                            precision=lax.Precision.HIGHEST, preferred_element_type=F32) + rb_ref[...]
    erow = lax.broadcasted_iota(I32, (N_EXPERTS, tm), 0)
    vals, idxs, sels = [], [], []
    cur = logit
    for _ in range(TOP_K):
        m = jnp.max(cur, axis=0, keepdims=True)
        idx = jnp.min(jnp.where(cur == m, erow, N_EXPERTS), axis=0, keepdims=True)
        sel = erow == idx
        vals.append(m)
        idxs.append(idx)
        sels.append(sel)
        cur = jnp.where(sel, -jnp.inf, cur)
    ex = [jnp.exp(v - vals[0]) for v in vals]
    den = ex[0] + ex[1] + ex[2] + ex[3]
    onehot = jnp.where(sels[0] | sels[1] | sels[2] | sels[3], 1.0, 0.0).astype(F32)
    ss = lax.broadcasted_iota(I32, (tm, tm), 0)
    tt = lax.broadcasted_iota(I32, (tm, tm), 1)
    before = jnp.where(ss < tt, 1.0, 0.0).astype(BF16)
    cnt = jnp.dot(onehot.astype(BF16), before, preferred_element_type=F32)
    ranks = [jnp.sum(jnp.where(s, cnt, 0.0), axis=0, keepdims=True) for s in sels]
    keys = [idxs[k] * tm + ranks[k].astype(I32) for k in range(TOP_K)]
    key_ref[...] = jnp.concatenate(keys + [jnp.full((8 - TOP_K, tm), -1, I32)], axis=0)
    cols = jnp.concatenate([e / den for e in ex] + [k.astype(F32) for k in keys]
                           + [jnp.zeros((128 - 2 * TOP_K, tm), F32)], axis=0)
    gk_ref[...] = cols.T
    cnt_ref[...] = jnp.broadcast_to(jnp.sum(onehot, axis=1, keepdims=True), (N_EXPERTS, 128))


def _router(xs, mod, g, rwt, rb):
    t = xs.shape[0]
    tm = BLK
    nt = t // tm
    return pl.pallas_call(
        functools.partial(_router_kernel, tm=tm),
        grid=(nt,),
        in_specs=[pl.BlockSpec((tm, D_MODEL), lambda i: (i, 0)),
                  pl.BlockSpec((8, 6 * D_MODEL), lambda i: (0, 0)),
                  pl.BlockSpec((1, D_MODEL), lambda i: (0, 0)),
                  pl.BlockSpec((N_EXPERTS, D_MODEL), lambda i: (0, 0)),
                  pl.BlockSpec((N_EXPERTS, 1), lambda i: (0, 0))],
        out_specs=[pl.BlockSpec((tm, D_MODEL), lambda i: (i, 0)),
                   pl.BlockSpec((8, tm), lambda i: (0, i)),
                   pl.BlockSpec((tm, 128), lambda i: (i, 0)),
                   pl.BlockSpec((None, N_EXPERTS, 128), lambda i: (i, 0, 0))],
        out_shape=[jax.ShapeDtypeStruct((t, D_MODEL), F32),
                   jax.ShapeDtypeStruct((8, t), I32),
                   jax.ShapeDtypeStruct((t, 128), F32),
                   jax.ShapeDtypeStruct((nt, N_EXPERTS, 128), F32)],
        compiler_params=_cparams(("arbitrary",)),
        name="router",
    )(xs, mod, g, rwt, rb)


RUN_CAP = 64
RUN_SHIFT = RUN_CAP.bit_length() - 1
RUN_BITS = (8, 4, 2, 1)
RUN_WAIT_BITS = (N_EXPERTS * RUN_CAP // 8).bit_length()


def _run_pieces(n8, nmax):
    n = jnp.minimum(n8, nmax)
    return [((n & bit) != 0, 8 * bit, pl.multiple_of(8 * (n & ~(2 * bit - 1)), 8)) for bit in RUN_BITS]


def _start_runs(i, base_ref, n8_ref, src_of, dst_of, sem):
    cap8 = RUN_CAP // 8
    total8 = jnp.int32(0)
    for e in range(N_EXPERTS):
        b = pl.multiple_of(base_ref[i * N_EXPERTS + e], 8)
        n = jnp.minimum(n8_ref[i * N_EXPERTS + e], cap8)

        def make_case(j, e=e, b=b):
            def case():
                off = 0
                for bit in RUN_BITS:
                    if j & bit:
                        pltpu.make_async_copy(src_of(e, b, off, 8 * bit), dst_of(e, b, off, 8 * bit),
                                              sem).start(priority=(e + off // 8) % 2)
                        off += 8 * bit
                return jnp.int32(0)
            return case

        lax.switch(n, [make_case(j) for j in range(cap8 + 1)])
        total8 = total8 + n
    return total8


def _wait_rows(total8, desc_of, nbits):
    for j in range(nbits):
        @pl.when(((total8 >> j) & 1) == 1)
        def _():
            desc_of(8 << j).wait()


def _dispatch_kernel(base_ref, n8_ref, long_ref, zs_ref, key_ref, h_ref, xs_ref, runs, ovf, zbuf, pend_ref, sem, osem, zsem,
                     *, tm, nzero, nsteps):
    i = pl.program_id(0)

    @pl.when(i == 0)
    def _():
        zbuf[...] = jnp.zeros_like(zbuf)
        for z in range(nzero):
            @pl.when(zs_ref[z] >= 0)
            def _():
                pltpu.make_async_copy(zbuf, xs_ref.at[pl.ds(pl.multiple_of(zs_ref[z], MOE_BM), MOE_BM), :],
                                      zsem).start()
        for z in range(nzero):
            @pl.when(zs_ref[z] >= 0)
            def _():
                pltpu.make_async_copy(zbuf, xs_ref.at[pl.ds(0, MOE_BM), :], zsem).wait()

    hb = h_ref[...].astype(BF16)
    keys = [key_ref[k:k + 1, :] for k in range(TOP_K)]
    nrow = N_EXPERTS * RUN_CAP
    rr = lax.broadcasted_iota(I32, (nrow, tm), 0)
    rowkey = (rr >> RUN_SHIFT) * tm + (rr & (RUN_CAP - 1))
    hit = (rowkey == keys[0]) | (rowkey == keys[1]) | (rowkey == keys[2]) | (rowkey == keys[3])
    slot = i % 2
    runs[slot] = jnp.dot(jnp.where(hit, 1.0, 0.0).astype(BF16), hb, preferred_element_type=F32)
    total8 = _start_runs(i, base_ref, n8_ref,
                         lambda e, b, off, size: runs.at[slot, pl.ds(e * RUN_CAP + off, size), :],
                         lambda e, b, off, size: xs_ref.at[pl.ds(b + off, size), :], sem.at[slot])
    pend_ref[slot] = total8

    def wait_slot(s):
        _wait_rows(pend_ref[s], lambda size: pltpu.make_async_copy(
            runs.at[s, pl.ds(0, size), :], xs_ref.at[pl.ds(0, size), :], sem.at[s]), RUN_WAIT_BITS)

    @pl.when(i > 0)
    def _():
        wait_slot(1 - slot)

    @pl.when(i == nsteps - 1)
    def _():
        wait_slot(slot)

    def ovf_body(e, carry):
        n8 = n8_ref[i * N_EXPERTS + e]
        b = pl.multiple_of(base_ref[i * N_EXPERTS + e], 8)
        for c in range(1, tm // RUN_CAP):
            @pl.when(n8 > c * (RUN_CAP // 8))
            def _():
                ck = lax.broadcasted_iota(I32, (RUN_CAP, tm), 0) + (e * tm + c * RUN_CAP)
                hit_c = (ck == keys[0]) | (ck == keys[1]) | (ck == keys[2]) | (ck == keys[3])
                ovf[...] = jnp.dot(jnp.where(hit_c, 1.0, 0.0).astype(BF16), hb, preferred_element_type=F32)
                for phase in (0, 1):
                    for live, size, off in _run_pieces(n8 - c * (RUN_CAP // 8), RUN_CAP // 8):
                        @pl.when(live)
                        def _():
                            cp = pltpu.make_async_copy(
                                ovf.at[pl.ds(off, size), :],
                                xs_ref.at[pl.ds(b + c * RUN_CAP + off, size), :], osem)
                            if phase == 0:
                                cp.start()
                            else:
                                cp.wait()
        return carry

    @pl.when(long_ref[i] > 0)
    def _():
        lax.fori_loop(0, N_EXPERTS, ovf_body, 0)


def _dispatch(plan, keys, h):
    base, n8, long_run, zstart, p_rows = plan
    t = h.shape[0]
    tm = BLK
    grid_spec = pltpu.PrefetchScalarGridSpec(
        num_scalar_prefetch=4,
        grid=(t // tm,),
        in_specs=[pl.BlockSpec((8, tm), lambda i, b, n, lg, z: (0, i)),
                  pl.BlockSpec((tm, D_MODEL), lambda i, b, n, lg, z: (i, 0))],
        out_specs=pl.BlockSpec(memory_space=pl.ANY),
        scratch_shapes=[pltpu.VMEM((2, N_EXPERTS * RUN_CAP, D_MODEL), F32), pltpu.VMEM((RUN_CAP, D_MODEL), F32),
                        pltpu.VMEM((MOE_BM, D_MODEL), F32), pltpu.SMEM((2,), I32),
                        pltpu.SemaphoreType.DMA((2,)), pltpu.SemaphoreType.DMA(()), pltpu.SemaphoreType.DMA(())],
    )
    return pl.pallas_call(
        functools.partial(_dispatch_kernel, tm=tm, nzero=zstart.shape[0], nsteps=t // tm),
        grid_spec=grid_spec,
        out_shape=jax.ShapeDtypeStruct((p_rows, D_MODEL), F32),
        compiler_params=_cparams(("arbitrary",)),
        name="moe_dispatch",
    )(base, n8, long_run, zstart, keys, h)


def _expert_kernel(be_ref, nused_ref, first_ref, slot_ref, nxt_ref, x_ref, wgu_hbm, bgu_ref, wdn_hbm, bdn_ref,
                   y_ref, gu_stage, dn_stage, wgu_bf, wdn_bf, sem, *, layer):
    b = pl.program_id(0)

    def fetch(e, s):
        return (pltpu.make_async_copy(wgu_hbm.at[layer, e], gu_stage.at[s], sem.at[0, s]),
                pltpu.make_async_copy(wdn_hbm.at[layer, e], dn_stage.at[s], sem.at[1, s]))

    @pl.when(b == 0)
    def _():
        for cp in fetch(be_ref[0], 0):
            cp.start()

    @pl.when(first_ref[b] == 1)
    def _():
        s = slot_ref[b]
        for cp in fetch(be_ref[b], s):
            cp.wait()
        wgu_bf[...] = gu_stage[s].astype(BF16)
        wdn_bf[...] = dn_stage[s].astype(BF16)

        @pl.when(nxt_ref[b] >= 0)
        def _():
            for cp in fetch(nxt_ref[b], 1 - s):
                cp.start()

    @pl.when(b < nused_ref[0])
    def _():
        gu = jnp.dot(x_ref[...].astype(BF16), wgu_bf[...], preferred_element_type=F32) + bgu_ref[...]
        g_ = jnp.minimum(gu[:, :D_FF], SWIGLU_LIMIT)
        u_ = jnp.clip(gu[:, D_FF:], -SWIGLU_LIMIT, SWIGLU_LIMIT)
        act = (u_ + 1.0) * (g_ * _sigmoid(g_ * SWIGLU_ALPHA))
        y_ref[...] = jnp.dot(act.astype(BF16), wdn_bf[...], preferred_element_type=F32) + bdn_ref[...]

    @pl.when(b >= nused_ref[0])
    def _():
        y_ref[...] = jnp.zeros_like(y_ref)


def _experts(plan, xsorted, w_gu, b_gu, w_dn, b_dn, layer):
    blk_e, nused, first, slot, nxt = plan
    p_rows = xsorted.shape[0]
    nblk = p_rows // MOE_BM
    depth = w_gu.shape[0]
    bsel = lambda b, be, nu, fi, sl, nx: (layer, be[b], 0, 0)
    grid_spec = pltpu.PrefetchScalarGridSpec(
        num_scalar_prefetch=5,
        grid=(nblk,),
        in_specs=[pl.BlockSpec((MOE_BM, D_MODEL), lambda b, be, nu, fi, sl, nx: (jnp.minimum(b, nu[0] - 1), 0)),
                  pl.BlockSpec(memory_space=pl.ANY),
                  pl.BlockSpec((None, None, 1, 2 * D_FF), bsel),
                  pl.BlockSpec(memory_space=pl.ANY),
                  pl.BlockSpec((None, None, 1, D_MODEL), bsel)],
        out_specs=pl.BlockSpec((MOE_BM, D_MODEL), lambda b, be, nu, fi, sl, nx: (b, 0)),
        scratch_shapes=[pltpu.VMEM((2, D_MODEL, 2 * D_FF), F32), pltpu.VMEM((2, D_FF, D_MODEL), F32),
                        pltpu.VMEM((D_MODEL, 2 * D_FF), BF16), pltpu.VMEM((D_FF, D_MODEL), BF16),
                        pltpu.SemaphoreType.DMA((2, 2))],
    )
    return pl.pallas_call(
        functools.partial(_expert_kernel, layer=layer),
        grid_spec=grid_spec,
        out_shape=jax.ShapeDtypeStruct((p_rows, D_MODEL), F32),
        compiler_params=_cparams(("arbitrary",)),
        name="moe_experts",
    )(blk_e, nused, first, slot, nxt, xsorted, w_gu, b_gu.reshape(depth, N_EXPERTS, 1, 2 * D_FF), w_dn,
      b_dn.reshape(depth, N_EXPERTS, 1, D_MODEL))


def _combine_kernel(base_ref, n8_ref, long_ref, y_ref, gk_ref, xs_ref, mod_ref, o_ref, runs, ovf, pend_ref, sem, osem,
                    *, tm, nsteps):
    i = pl.program_id(0)

    slot = i % 2

    def gather(tile, s):
        pend_ref[s] = _start_runs(tile, base_ref, n8_ref,
                                  lambda e, b, off, size: y_ref.at[pl.ds(b + off, size), :],
                                  lambda e, b, off, size: runs.at[s, pl.ds(e * RUN_CAP + off, size), :],
                                  sem.at[s])

    @pl.when(i == 0)
    def _():
        runs[...] = jnp.zeros_like(runs)
        ovf[...] = jnp.zeros_like(ovf)
        gather(0, 0)

    @pl.when(i + 1 < nsteps)
    def _():
        gather(i + 1, 1 - slot)

    _wait_rows(pend_ref[slot], lambda size: pltpu.make_async_copy(
        y_ref.at[pl.ds(0, size), :], runs.at[slot, pl.ds(0, size), :], sem.at[slot]), RUN_WAIT_BITS)
    gk = gk_ref[...]
    gates = [gk[:, k:k + 1] for k in range(TOP_K)]
    keys = [gk[:, TOP_K + k:TOP_K + k + 1].astype(I32) for k in range(TOP_K)]
    nrow = N_EXPERTS * RUN_CAP
    cc = lax.broadcasted_iota(I32, (1, nrow), 1)
    colkey = (cc >> RUN_SHIFT) * tm + (cc & (RUN_CAP - 1))
    w = jnp.zeros((tm, nrow), F32)
    for k in reversed(range(TOP_K)):
        w = jnp.where(colkey == keys[k], gates[k], w)
    acc = jnp.dot(w.astype(BF16), runs[slot].astype(BF16), preferred_element_type=F32)
    o_ref[...] = xs_ref[...] + acc * _res_gate(mod_ref, 3 * D_MODEL, tm, i * tm)

    def ovf_body(e, carry):
        n8 = n8_ref[i * N_EXPERTS + e]
        b = pl.multiple_of(base_ref[i * N_EXPERTS + e], 8)
        for c in range(1, tm // RUN_CAP):
            @pl.when(n8 > c * (RUN_CAP // 8))
            def _():
                for phase in (0, 1):
                    for live, size, off in _run_pieces(n8 - c * (RUN_CAP // 8), RUN_CAP // 8):
                        @pl.when(live)
                        def _():
                            cp = pltpu.make_async_copy(
                                y_ref.at[pl.ds(b + c * RUN_CAP + off, size), :],
                                ovf.at[pl.ds(off, size), :], osem)
                            if phase == 0:
                                cp.start()
                            else:
                                cp.wait()
                ck = lax.broadcasted_iota(I32, (1, RUN_CAP), 1) + (e * tm + c * RUN_CAP)
                wc = jnp.where(ck == keys[0], gates[0], 0.0)
                for k in range(1, TOP_K):
                    wc = wc + jnp.where(ck == keys[k], gates[k], 0.0)
                part = jnp.dot(wc.astype(BF16), ovf[...].astype(BF16), preferred_element_type=F32)
                o_ref[...] = o_ref[...] + part * _res_gate(mod_ref, 3 * D_MODEL, tm, i * tm)
        return carry

    @pl.when(long_ref[i] > 0)
    def _():
        lax.fori_loop(0, N_EXPERTS, ovf_body, 0)


def _combine(plan, y, gk_col, xs, mod):
    base, n8, long_run, _, _ = plan
    t = xs.shape[0]
    tm = BLK
    grid_spec = pltpu.PrefetchScalarGridSpec(
        num_scalar_prefetch=3,
        grid=(t // tm,),
        in_specs=[pl.BlockSpec(memory_space=pl.ANY),
                  pl.BlockSpec((tm, 128), lambda i, b, n, lg: (i, 0)),
                  pl.BlockSpec((tm, D_MODEL), lambda i, b, n, lg: (i, 0)),
                  pl.BlockSpec((8, 6 * D_MODEL), lambda i, b, n, lg: (0, 0))],
        out_specs=pl.BlockSpec((tm, D_MODEL), lambda i, b, n, lg: (i, 0)),
        scratch_shapes=[pltpu.VMEM((2, N_EXPERTS * RUN_CAP, D_MODEL), F32), pltpu.VMEM((RUN_CAP, D_MODEL), F32),
                        pltpu.SMEM((2,), I32), pltpu.SemaphoreType.DMA((2,)), pltpu.SemaphoreType.DMA(())],
    )
    return pl.pallas_call(
        functools.partial(_combine_kernel, tm=tm, nsteps=t // tm),
        grid_spec=grid_spec,
        out_shape=jax.ShapeDtypeStruct((t, D_MODEL), F32),
        compiler_params=_cparams(("arbitrary",)),
        name="moe_combine",
    )(base, n8, long_run, y, gk_col, xs, mod)


def _pad_heads(w, nh, hd, hdp):
    d = w.shape[0]
    return jnp.pad(w.reshape(d, nh, hd), ((0, 0), (0, 0), (0, hdp - hd))).reshape(d, nh * hdp)


def _layout_w_in(w):
    w_mix, w_gl = _layout_w_in_f32(w)
    return w_mix.astype(BF16), w_gl.astype(BF16)


def _layout_w_in_f32(w):
    pts = np.cumsum([A_WIDTH] * 4 + [2 * H_A, 2 * H_A, B_KWIDTH, B_KWIDTH, B_VWIDTH, B_VWIDTH,
                                     2 * GLA_RANK, C_WIDTH, KV_C * HD_C, KV_C * HD_C])
    (aq, ak, av, az, aa, ab, bq, bk, bv, bz, bg, cq, ck, cv, gl) = jnp.split(w, pts.tolist(), axis=1)
    small = jnp.concatenate([aa, ab, bg], axis=1)
    small = jnp.pad(small, ((0, 0), (0, 256 - small.shape[1])))
    cols = [aq, ak, av, az, _pad_heads(cq, H_C, HD_C, HD_CP), _pad_heads(ck, KV_C, HD_C, HD_CP),
            _pad_heads(cv, KV_C, HD_C, HD_CP), bq, bk, bv, bz, small]
    out = jnp.concatenate(cols, axis=1)
    assert out.shape[1] == IN_COLS_P
    return out, gl


def _rope_tables(t):
    s_len = t - CTX
    half = HD_C // 2
    inv_freq = ROPE_THETA ** (-jnp.arange(0, half, 2, dtype=F32) / half)
    pos = jnp.arange(s_len)
    rows = (pos // GRID_W).astype(F32)[:, None] * inv_freq[None, :]
    cols = (pos % GRID_W).astype(F32)[:, None] * inv_freq[None, :]
    cr, sr, cc, sc = jnp.cos(rows), jnp.sin(rows), jnp.cos(cols), jnp.sin(cols)
    zpad = jnp.zeros((s_len, HD_CP - HD_C), F32)
    cos_l = jnp.concatenate([cr, cr, cc, cc, zpad], axis=1)
    sin_l = jnp.concatenate([-sr, sr, -sc, sc, zpad], axis=1)
    cos_t = jnp.concatenate([jnp.ones((CTX, HD_CP), F32), cos_l], axis=0)
    sin_t = jnp.concatenate([jnp.zeros((CTX, HD_CP), F32), sin_l], axis=0)
    return cos_t, sin_t


def _lane_vec(v, width=128):
    v = v.reshape(1, -1).astype(F32)
    return jnp.pad(v, ((0, 0), (0, width - v.shape[1])))


def _moe_plan(cnt_tile, tk):
    nt = cnt_tile.shape[0]
    pad8 = (cnt_tile + 7) // 8 * 8
    total = jnp.sum(pad8, axis=0)
    padded = (total + MOE_BM - 1) // MOE_BM * MOE_BM
    pend = jnp.cumsum(padded)
    estart = pend - padded
    base = estart[None, :] + jnp.cumsum(pad8, axis=0) - pad8
    nblk = (tk + nt * N_EXPERTS * 7 + N_EXPERTS * (MOE_BM - 1) + MOE_BM - 1) // MOE_BM
    p_rows = nblk * MOE_BM
    blk = jnp.arange(nblk, dtype=I32)
    blk_e = jnp.minimum(jnp.sum((pend[None, :] <= (blk * MOE_BM)[:, None]).astype(I32), axis=1), N_EXPERTS - 1)
    nused = (pend[-1] // MOE_BM).astype(I32)
    prev_e = jnp.concatenate([jnp.full((1,), -1, I32), blk_e[:-1]])
    first = ((blk_e != prev_e) & (blk < nused)).astype(I32)
    slot = (jnp.cumsum(first) - 1) % 2
    pos = jnp.where(first == 1, blk, nblk)
    nxt_pos = jnp.concatenate([lax.cummin(pos, axis=0, reverse=True)[1:], jnp.full((1,), nblk, I32)])
    nxt = jnp.where(nxt_pos < nblk, blk_e[jnp.minimum(nxt_pos, nblk - 1)], -1)
    last_blk = jnp.where(total > 0, pend - MOE_BM, -1)
    ntail = (nt * N_EXPERTS * 7 + N_EXPERTS * (MOE_BM - 1)) // MOE_BM + 1
    tail = pend[-1] + jnp.arange(ntail, dtype=I32) * MOE_BM
    zstart = jnp.concatenate([last_blk, jnp.where(tail < p_rows, tail, -1)]).astype(I32)
    eplan = (blk_e, nused.reshape(1), first, slot.astype(I32), nxt.astype(I32))
    long_run = (jnp.max(pad8, axis=1) > RUN_CAP).astype(I32)
    mplan = (base.reshape(-1).astype(I32), (pad8 // 8).reshape(-1).astype(I32), long_run, zstart, p_rows)
    return mplan, eplan


def kernel(x, c, ctx, c_ctx, ada_w, ada_b, norm_mix_g, norm_ffn_g, w_in, dn_conv_w, dn_a_log, dn_dt_bias,
           dn_norm_g, gla_w2, gla_b2, gla_norm_g, attn_q_norm_g, attn_k_norm_g, attn_sink, w_branch_a,
           w_branch_b, w_branch_c, w_out, router_w, router_b, w_gate_up, b_gate_up, w_down, b_down):
    assert x.shape[0] == 1 and c.shape[0] == 1 and ctx.shape[1] == CTX
    depth = ada_w.shape[0]
    xs = jnp.concatenate([ctx[0], x[0]], axis=0)
    t = xs.shape[0]
    assert t % BLK == 0 and (t - CTX) % GRID_W == 0
    cc = jnp.concatenate([c, c_ctx[None, :], jnp.zeros((6, D_MODEL), F32)], axis=0)
    mods = _ada_mod(cc, ada_w, ada_b)
    cos_t, sin_t = _rope_tables(t)
    for l in range(depth):
        mod = mods[l]
        w_mix, w_gl = _layout_w_in(w_in[l])
        proj = _inproj(xs, mod, norm_mix_g[l][None, :], w_mix)
        qa, ka, va, gcol = _dnprep(proj, dn_conv_w[l], _lane_vec(dn_a_log[l]), _lane_vec(dn_dt_bias[l]))
        oaf, oab = _dnscan(qa, ka, va, gcol)
        w2 = gla_w2[l].astype(F32)
        w2full = jnp.zeros((128, 2 * B_KWIDTH), F32)
        for d in range(2):
            r0 = 4 * H_A + d * GLA_RANK
            w2full = w2full.at[r0:r0 + GLA_RANK, d * B_KWIDTH:(d + 1) * B_KWIDTH].set(w2[d])
        obf, obb = _glascan(proj, w2full, gla_b2[l].reshape(1, 2 * B_KWIDTH).astype(F32))
        qg = _lane_vec(attn_q_norm_g[l])
        kg = _lane_vec(attn_k_norm_g[l])
        qr, kr, vr = _attnprep(proj, qg, kg, cos_t, sin_t)
        yc = _attn(qr, kr, vr, attn_sink[l].astype(F32))
        wpc = jnp.pad(w_branch_c[l].reshape(H_C, HD_C, D_MODEL),
                      ((0, 0), (0, HD_CP - HD_C), (0, 0))).reshape(H_C * HD_CP, D_MODEL)
        xs = _merge(oaf, oab, obf, obb, yc, proj, xs, norm_mix_g[l][None, :], w_gl,
                    dn_norm_g[l][None, :], gla_norm_g[l][None, :],
                    w_branch_a[l].astype(BF16), w_branch_b[l].astype(BF16), wpc.astype(BF16),
                    w_out[l].astype(BF16), mod)
        h2, keys, gk_col, cnt = _router(xs, mod, norm_ffn_g[l][None, :], router_w[l].T, router_b[l][:, None])
        mplan, eplan = _moe_plan(cnt[:, :, 0].astype(I32), t * TOP_K)
        xsorted = _dispatch(mplan, keys, h2)
        y = _experts(eplan, xsorted, w_gate_up, b_gate_up, w_down, b_down, l)
        xs = _combine(mplan, y, gk_col, xs, mod)
    return xs[CTX:][None]
```
